```python
import math
import jax, jax.numpy as jnp
from jax import lax
import numpy as np

D_MODEL = 1024
BATCH = 8
SEQ = 2048
DEPTH = 2
DEC_BATCH = 128
DEC_SEQ = 8
PAST_LEN = 16384
PAGE_SIZE = 128

GROUP_W = D_MODEL // 4
MIX_W = 4 * GROUP_W
RET_HEADS = 4
RET_DK = GROUP_W // RET_HEADS
RET_DV = GROUP_W // RET_HEADS
RET_CHUNK = 64
ROPE_BASE = 10000.0
HG_HEADS = 4
HG_DK = GROUP_W // HG_HEADS
HG_DV = GROUP_W // HG_HEADS
HG_CHUNK = 16
HG_MIN_FORGET = 1e-30
SSD_HEADS = 4
SSD_HEADDIM = GROUP_W // SSD_HEADS
SSD_GROUPS = 2
SSD_STATE = 128
SSD_CONV = 4
SSD_CHUNK = 128
SSD_CONV_DIM = GROUP_W + 2 * SSD_GROUPS * SSD_STATE
S5_CH = 16
S5_GROUPS = GROUP_W // S5_CH
S5_STATE = 64
D_FF = 4 * D_MODEL
IN_COLS = 8 * GROUP_W + GROUP_W + SSD_CONV_DIM + SSD_HEADS + GROUP_W
EPS = 1e-6

kernel_name = 'hybrid_retention_hgrn2_ssd_s5_step'


def rmsnorm(x, w):
    xf = x.astype(jnp.float32)
    y = xf * lax.rsqrt(jnp.mean(xf * xf, axis=-1, keepdims=True) + EPS)
    return (y * w.astype(jnp.float32)).astype(x.dtype)


def _chunk_len(L, chunk):
    return L if L <= chunk else math.gcd(L, chunk)


def chunked_scalar_decay(q, k, v, log_a, s0, chunk):
    f32 = jnp.float32
    B, L, H, K = q.shape
    V = v.shape[-1]
    c = _chunk_len(L, chunk)
    n = L // c
    qc = q.astype(f32).reshape(B, n, c, H, K)
    kc = k.astype(f32).reshape(B, n, c, H, K)
    vc = v.astype(f32).reshape(B, n, c, H, V)
    b = jnp.cumsum(log_a.astype(f32).reshape(B, n, c, H), axis=2).transpose(0, 1, 3, 2)
    causal = jnp.tril(jnp.ones((c, c), dtype=bool))
    diff = b[..., :, None] - b[..., None, :]
    decay = jnp.where(causal, jnp.exp(jnp.where(causal, diff, 0.0)), 0.0)
    scores = jnp.einsum('bnthk,bnshk->bnhts', qc, kc) * decay
    o_intra = jnp.einsum('bnhts,bnshv->bnthv', scores, vc)
    b_last = b[..., -1]
    ds = jnp.einsum('bnhs,bnshk,bnshv->bnhkv', jnp.exp(b_last[..., None] - b), kc, vc)
    a_chunk = jnp.exp(b_last)

    def step(s, inp):
        a, d = inp
        return a[..., None, None] * s + d, s

    s_fin, s_prev = lax.scan(step, s0.astype(f32), (jnp.moveaxis(a_chunk, 1, 0), jnp.moveaxis(ds, 1, 0)))
    s_prev = jnp.moveaxis(s_prev, 0, 1)
    q_dec = qc * jnp.exp(b).transpose(0, 1, 3, 2)[..., None]
    o_cross = jnp.einsum('bnthk,bnhkv->bnthv', q_dec, s_prev)
    return (o_intra + o_cross).reshape(B, L, H, V), s_fin


def chunked_vector_decay(q, k, v, log_f, s0, chunk):
    f32 = jnp.float32
    B, L, H, K = q.shape
    V = v.shape[-1]
    c = _chunk_len(L, chunk)
    n = L // c
    qc = q.astype(f32).reshape(B, n, c, H, K)
    kc = k.astype(f32).reshape(B, n, c, H, K)
    vc = v.astype(f32).reshape(B, n, c, H, V)
    b = jnp.cumsum(log_f.astype(f32).reshape(B, n, c, H, K), axis=2)
    causal = jnp.tril(jnp.ones((c, c), dtype=bool))[None, None, :, :, None, None]
    diff = b[:, :, :, None] - b[:, :, None]
    decay = jnp.where(causal, jnp.exp(jnp.where(causal, diff, 0.0)), 0.0)
    scores = jnp.einsum('bnthk,bnshk,bntshk->bnhts', qc, kc, decay)
    o_intra = jnp.einsum('bnhts,bnshv->bnthv', scores, vc)
    b_last = b[:, :, -1]
    ds = jnp.einsum('bnshk,bnshv->bnhkv', kc * jnp.exp(b_last[:, :, None] - b), vc)
    a_chunk = jnp.exp(b_last)

    def step(s, inp):
        a, d = inp
        return a[..., None] * s + d, s

    s_fin, s_prev = lax.scan(step, s0.astype(f32), (jnp.moveaxis(a_chunk, 1, 0), jnp.moveaxis(ds, 1, 0)))
    s_prev = jnp.moveaxis(s_prev, 0, 1)
    o_cross = jnp.einsum('bnthk,bnhkv->bnthv', qc * jnp.exp(b), s_prev)
    return (o_intra + o_cross).reshape(B, L, H, V), s_fin


def rotate_every_two(x, pos0):
    L, d = x.shape[1], x.shape[-1]
    angle = 1.0 / (ROPE_BASE ** jnp.linspace(0.0, 1.0, d // 2, dtype=jnp.float32))
    pos = (pos0 + jnp.arange(L)).astype(jnp.float32)
    theta = pos[:, None] * angle[None, :]
    cos = jnp.cos(theta)[None, :, None, :]
    sin = jnp.sin(theta)[None, :, None, :]
    xf = x.astype(jnp.float32).reshape(x.shape[:-1] + (d // 2, 2))
    x1, x2 = xf[..., 0], xf[..., 1]
    return jnp.stack([x1 * cos - x2 * sin, x1 * sin + x2 * cos], axis=-1).reshape(x.shape)


def retention(q, k, v, g, s0, pos0):
    B, L, _ = q.shape
    qh = rotate_every_two(q.reshape(B, L, RET_HEADS, RET_DK), pos0)
    kh = rotate_every_two(k.reshape(B, L, RET_HEADS, RET_DK), pos0) * (RET_DK ** -0.5)
    vh = v.reshape(B, L, RET_HEADS, RET_DV)
    log_gamma = jnp.log(1.0 - jnp.exp2(-5.0 - jnp.arange(RET_HEADS, dtype=jnp.float32)))
    log_a = jnp.broadcast_to(log_gamma, (B, L, RET_HEADS))
    o, s = chunked_scalar_decay(qh, kh, vh, log_a, s0, RET_CHUNK)
    mu = jnp.mean(o, axis=-1, keepdims=True)
    var = jnp.mean(jnp.square(o - mu), axis=-1, keepdims=True)
    o = (o - mu) * lax.rsqrt(var + EPS)
    return o.reshape(B, L, GROUP_W) * jax.nn.silu(g.astype(jnp.float32)), s


def hgrn2(q, f, i, g, s0, lower):
    B, L, _ = q.shape
    ff = f.astype(jnp.float32)
    forget = lower + (1.0 - lower) * jax.nn.sigmoid(ff)
    log_forget = jnp.log(jnp.maximum(forget, HG_MIN_FORGET))
    key = (1.0 - lower) * jax.nn.sigmoid(-ff)
    qh = jax.nn.silu(q.astype(jnp.float32)).reshape(B, L, HG_HEADS, HG_DK)
    o, s = chunked_vector_decay(qh, key.reshape(B, L, HG_HEADS, HG_DK),
                                i.reshape(B, L, HG_HEADS, HG_DV),
                                log_forget.reshape(B, L, HG_HEADS, HG_DK), s0, HG_CHUNK)
    o = o * lax.rsqrt(jnp.mean(o * o, axis=-1, keepdims=True) + EPS)
    return o.reshape(B, L, GROUP_W) * jax.nn.silu(g.astype(jnp.float32)), s


def ssd_mixer(z, xbc, dt, conv_buf, s0, conv_w, conv_b, dt_bias, a_log, d_skip, norm_w):
    B, L, _ = xbc.shape
    f32 = jnp.float32
    xpad = jnp.concatenate([conv_buf.astype(xbc.dtype), xbc], axis=1)
    new_buf = xpad[:, -(SSD_CONV - 1):]
    conv = lax.conv_general_dilated(xpad, conv_w[:, None, :].astype(xbc.dtype), window_strides=(1,),
                                    padding='VALID', dimension_numbers=('NWC', 'WIO', 'NWC'),
                                    feature_group_count=SSD_CONV_DIM)
    xbc = jax.nn.silu(conv.astype(f32) + conv_b.astype(f32))
    xs, bm, cm = jnp.split(xbc, [GROUP_W, GROUP_W + SSD_GROUPS * SSD_STATE], axis=-1)
    xs = xs.reshape(B, L, SSD_HEADS, SSD_HEADDIM)
    rep = SSD_HEADS // SSD_GROUPS
    bm = jnp.repeat(bm.reshape(B, L, SSD_GROUPS, SSD_STATE), rep, axis=2)
    cm = jnp.repeat(cm.reshape(B, L, SSD_GROUPS, SSD_STATE), rep, axis=2)
    delta = jax.nn.softplus(dt.astype(f32) + dt_bias.astype(f32))
    log_a = delta * (-jnp.exp(a_log.astype(f32)))
    y, s = chunked_scalar_decay(cm, bm * delta[..., None], xs, log_a, s0, SSD_CHUNK)
    y = y + d_skip.astype(f32)[:, None] * xs
    y = (y.reshape(B, L, GROUP_W) * jax.nn.silu(z.astype(f32))).reshape(B, L, SSD_GROUPS, -1)
    y = y * lax.rsqrt(jnp.mean(y * y, axis=-1, keepdims=True) + EPS)
    return y.reshape(B, L, GROUP_W) * norm_w.astype(f32), s, new_buf


def s5_mixer(u, s0_re, s0_im, a_re, a_im, log_dt, b_re, b_im, c_re, c_im, d_skip, glu_w, glu_b):
    B, L, _ = u.shape
    f32 = jnp.float32
    uf = u.astype(f32).reshape(B, L, S5_GROUPS, S5_CH)
    lam = lax.complex(a_re.astype(f32), a_im.astype(f32))
    step = jnp.exp(log_dt.astype(f32))[:, None]
    a_bar = jnp.exp(lam * step)
    b_bar = ((a_bar - 1.0) / lam)[..., None] * lax.complex(b_re.astype(f32), b_im.astype(f32))
    bu = jnp.einsum('gph,blgh->blgp', b_bar, uf)
    s0 = lax.complex(s0_re.astype(f32), s0_im.astype(f32))
    bu = bu.at[:, 0].add(a_bar * s0)
    a_seq = jnp.broadcast_to(a_bar, bu.shape)

    def combine(e1, e2):
        a1, x1 = e1
        a2, x2 = e2
        return a2 * a1, a2 * x1 + x2

    _, states = lax.associative_scan(combine, (a_seq, bu), axis=1)
    cmat = lax.complex(c_re.astype(f32), c_im.astype(f32))
    y = jnp.einsum('ghp,blgp->blgh', cmat, states).real + d_skip.astype(f32).reshape(S5_GROUPS, S5_CH) * uf
    y = jax.nn.gelu(y.reshape(B, L, GROUP_W))
    y = y * jax.nn.sigmoid(y @ glu_w.astype(f32) + glu_b.astype(f32))
    s_fin = states[:, -1]
    return y, jnp.real(s_fin), jnp.imag(s_fin)


def run_trunk(x, pos0, st_ret, st_hg, st_ssd, st_conv, st_s5_re, st_s5_im,
              norm_mix, w_in, w_out, hg_lb_logits, ssd_conv_w, ssd_conv_b, ssd_dt_bias,
              ssd_a_log, ssd_d, ssd_norm, s5_a_re, s5_a_im, s5_log_dt, s5_b_re, s5_b_im,
              s5_c_re, s5_c_im, s5_d, s5_glu_w, s5_glu_b, norm_mlp, w_up, w_down, norm_final):
    sizes = [GROUP_W] * 8 + [GROUP_W, SSD_CONV_DIM, SSD_HEADS, GROUP_W]
    cuts = [int(c) for c in np.cumsum(sizes)[:-1]]
    sm = jax.nn.softmax(hg_lb_logits.astype(jnp.float32), axis=0)
    lower = jnp.cumsum(sm, axis=0) - sm[0]
    new = [[] for _ in range(6)]
    for l in range(DEPTH):
        h = rmsnorm(x, norm_mix[l])
        (rq, rk, rv, rg, hq, hf, hi, hgate, sz, sxbc, sdt, su) = jnp.split(h @ w_in[l], cuts, axis=-1)
        ya, s_a = retention(rq, rk, rv, rg, st_ret[l], pos0)
        yb, s_b = hgrn2(hq, hf, hi, hgate, st_hg[l], lower[l])
        yc, s_c, c_c = ssd_mixer(sz, sxbc, sdt, st_conv[l], st_ssd[l], ssd_conv_w[l], ssd_conv_b[l],
                                 ssd_dt_bias[l], ssd_a_log[l], ssd_d[l], ssd_norm[l])
        yd, s_re, s_im = s5_mixer(su, st_s5_re[l], st_s5_im[l], s5_a_re[l], s5_a_im[l], s5_log_dt[l],
                                  s5_b_re[l], s5_b_im[l], s5_c_re[l], s5_c_im[l], s5_d[l],
                                  s5_glu_w[l], s5_glu_b[l])
        mix = jnp.concatenate([ya, yb, yc, yd], axis=-1).astype(x.dtype)
        x = x + mix @ w_out[l]
        h = rmsnorm(x, norm_mlp[l])
        x = x + jnp.square(jax.nn.relu(h @ w_up[l])) @ w_down[l]
        for lst, s in zip(new, (s_a, s_b, s_c, c_c, s_re, s_im)):
            lst.append(s)
    y = rmsnorm(x, norm_final)
    return (y, jnp.stack(new[0]), jnp.stack(new[1]), jnp.stack(new[2]),
            jnp.stack(new[3]), jnp.stack(new[4]), jnp.stack(new[5]))


def setup_inputs(seed: int = 0) -> dict:
    key = jax.random.key(seed)
    ks = iter(jax.random.split(key, 48))
    f32 = jnp.float32

    def nrm(shape, scale):
        return scale * jax.random.normal(next(ks), shape, f32)

    def unif(shape, lo, hi):
        return jax.random.uniform(next(ks), shape, f32, lo, hi)

    ssd_dt = jnp.exp(unif((DEPTH, SSD_HEADS), math.log(1e-3), math.log(1e-1)))
    return {
        'x_prompt': nrm((BATCH, SEQ, D_MODEL), 1.0),
        'x_sample': nrm((DEC_BATCH, DEC_SEQ, D_MODEL), 1.0),
        'state_ret': nrm((DEPTH, DEC_BATCH, RET_HEADS, RET_DK, RET_DV), 1.0),
        'state_hgrn': nrm((DEPTH, DEC_BATCH, HG_HEADS, HG_DK, HG_DV), 0.5),
        'state_ssd': nrm((DEPTH, DEC_BATCH, SSD_HEADS, SSD_STATE, SSD_HEADDIM), 0.5),
        'state_ssd_conv': nrm((DEPTH, DEC_BATCH, SSD_CONV - 1, SSD_CONV_DIM), 1.0),
        'state_s5_re': nrm((DEPTH, DEC_BATCH, S5_GROUPS, S5_STATE), 0.5),
        'state_s5_im': nrm((DEPTH, DEC_BATCH, S5_GROUPS, S5_STATE), 0.5),
        'norm_mix': 1.0 + nrm((DEPTH, D_MODEL), 0.02),
        'w_in': nrm((DEPTH, D_MODEL, IN_COLS), D_MODEL ** -0.5),
        'w_out': nrm((DEPTH, MIX_W, D_MODEL), MIX_W ** -0.5),
        'hg_lb_logits': nrm((DEPTH, GROUP_W), 1.0),
        'ssd_conv_w': nrm((DEPTH, SSD_CONV, SSD_CONV_DIM), SSD_CONV ** -0.5),
        'ssd_conv_b': nrm((DEPTH, SSD_CONV_DIM), 0.02),
        'ssd_dt_bias': ssd_dt + jnp.log(-jnp.expm1(-ssd_dt)),
        'ssd_a_log': jnp.log(unif((DEPTH, SSD_HEADS), 1.0, 16.0)),
        'ssd_d': 1.0 + nrm((DEPTH, SSD_HEADS), 0.1),
        'ssd_norm': 1.0 + nrm((DEPTH, GROUP_W), 0.02),
        's5_a_re': -0.5 + nrm((DEPTH, S5_GROUPS, S5_STATE), 0.01),
        's5_a_im': math.pi * jnp.arange(S5_STATE, dtype=f32) + nrm((DEPTH, S5_GROUPS, S5_STATE), 0.01),
        's5_log_dt': unif((DEPTH, S5_GROUPS), math.log(1e-3), math.log(1e-1)),
        's5_b_re': nrm((DEPTH, S5_GROUPS, S5_STATE, S5_CH), (2 * S5_CH) ** -0.5),
        's5_b_im': nrm((DEPTH, S5_GROUPS, S5_STATE, S5_CH), (2 * S5_CH) ** -0.5),
        's5_c_re': nrm((DEPTH, S5_GROUPS, S5_CH, S5_STATE), S5_STATE ** -0.5),
        's5_c_im': nrm((DEPTH, S5_GROUPS, S5_CH, S5_STATE), S5_STATE ** -0.5),
        's5_d': nrm((DEPTH, GROUP_W), 1.0),
        's5_glu_w': nrm((DEPTH, GROUP_W, GROUP_W), GROUP_W ** -0.5),
        's5_glu_b': nrm((DEPTH, GROUP_W), 0.02),
        'norm_mlp': 1.0 + nrm((DEPTH, D_MODEL), 0.02),
        'w_up': nrm((DEPTH, D_MODEL, D_FF), D_MODEL ** -0.5),
        'w_down': nrm((DEPTH, D_FF, D_MODEL), D_FF ** -0.5),
        'norm_final': 1.0 + nrm((D_MODEL,), 0.02),
    }


def reference(x_prompt, x_sample, state_ret, state_hgrn, state_ssd, state_ssd_conv, state_s5_re, state_s5_im,
              norm_mix, w_in, w_out, hg_lb_logits, ssd_conv_w, ssd_conv_b, ssd_dt_bias, ssd_a_log, ssd_d,
              ssd_norm, s5_a_re, s5_a_im, s5_log_dt, s5_b_re, s5_b_im, s5_c_re, s5_c_im, s5_d, s5_glu_w,
              s5_glu_b, norm_mlp, w_up, w_down, norm_final):
    nb = x_prompt.shape[0]

    def fresh(st):
        return jnp.zeros((DEPTH, nb) + st.shape[2:], st.dtype)

    (y_prompt, p_ret, p_hgrn, p_ssd, p_conv, p_s5_re, p_s5_im) = run_trunk(
        x_prompt, 0, fresh(state_ret), fresh(state_hgrn), fresh(state_ssd), fresh(state_ssd_conv),
        fresh(state_s5_re), fresh(state_s5_im),
        norm_mix, w_in, w_out, hg_lb_logits, ssd_conv_w, ssd_conv_b, ssd_dt_bias, ssd_a_log, ssd_d,
        ssd_norm, s5_a_re, s5_a_im, s5_log_dt, s5_b_re, s5_b_im, s5_c_re, s5_c_im, s5_d, s5_glu_w,
        s5_glu_b, norm_mlp, w_up, w_down, norm_final)
    (y_sample, s_ret, s_hgrn, s_ssd, s_conv, s_s5_re, s_s5_im) = run_trunk(
        x_sample, PAST_LEN, state_ret, state_hgrn, state_ssd, state_ssd_conv, state_s5_re, state_s5_im,
        norm_mix, w_in, w_out, hg_lb_logits, ssd_conv_w, ssd_conv_b, ssd_dt_bias, ssd_a_log, ssd_d,
        ssd_norm, s5_a_re, s5_a_im, s5_log_dt, s5_b_re, s5_b_im, s5_c_re, s5_c_im, s5_d, s5_glu_w,
        s5_glu_b, norm_mlp, w_up, w_down, norm_final)
    return (y_prompt, y_sample, p_ret, p_hgrn, p_ssd, p_conv, p_s5_re, p_s5_im,
            s_ret, s_hgrn, s_ssd, s_conv, s_s5_re, s_s5_im)
```

```python
import functools
import math

import jax
import jax.numpy as jnp
from jax import lax
from jax.experimental import pallas as pl
from jax.experimental.pallas import tpu as pltpu

F32 = jnp.float32
BF16 = jnp.bfloat16

D_MODEL = 1024
DEPTH = 2
GROUP_W = 256
HEADS = 4
HEAD_W = 64
ROPE_BASE = 10000.0
HG_MIN_FORGET = 1e-30
SSD_STATE = 128
SSD_CONV = 4
SSD_CONV_DIM = 768
S5_GROUPS = 16
S5_CH = 16
S5_STATE = 64
S5_N = S5_GROUPS * S5_STATE
D_FF = 4096
EPS = 1e-6

COL_RET = 0
COL_HG = 1024
COL_SSD = 2048
COL_S5 = 3072
COL_DT = 3328
IN_COLS_PAD = 3584

VMEM_LIMIT = 56 * 1024 * 1024
SUBLANES = 8


def _cparams(sem):
    return pltpu.CompilerParams(dimension_semantics=sem, vmem_limit_bytes=VMEM_LIMIT)


def _mm(a, b):
    return jnp.dot(a.astype(BF16), b.astype(BF16), preferred_element_type=F32)


def _mm_nt(a, b):
    return lax.dot_general(a.astype(BF16), b.astype(BF16), (((1,), (1,)), ((), ())),
                           preferred_element_type=F32)


def _mm_tn(a, b):
    return lax.dot_general(a.astype(BF16), b.astype(BF16), (((0,), (0,)), ((), ())),
                           preferred_element_type=F32)


def _mm_f32(a, b):
    return jnp.dot(a, b, preferred_element_type=F32, precision=lax.Precision.HIGHEST)


def _silu(x):
    return x * jax.nn.sigmoid(x)


def _rms(x, w):
    return x * lax.rsqrt(jnp.mean(x * x, axis=-1, keepdims=True) + EPS) * w


def _tri(c):
    t = lax.broadcasted_iota(jnp.int32, (c, c), 0)
    s = lax.broadcasted_iota(jnp.int32, (c, c), 1)
    return t >= s


def _inproj_kernel(x_ref, nw_ref, w_ref, o_ref):
    h = _rms(x_ref[...], nw_ref[...])
    o_ref[...] = _mm(h, w_ref[...])


def _inproj(x2d, norm_w, w_bf16, tm):
    t = x2d.shape[0]
    return pl.pallas_call(
        _inproj_kernel,
        grid=(t // tm,),
        in_specs=[pl.BlockSpec((tm, D_MODEL), lambda i: (i, 0)),
                  pl.BlockSpec((1, D_MODEL), lambda i: (0, 0)),
                  pl.BlockSpec((D_MODEL, IN_COLS_PAD), lambda i: (0, 0))],
        out_specs=pl.BlockSpec((tm, IN_COLS_PAD), lambda i: (i, 0)),
        out_shape=jax.ShapeDtypeStruct((t, IN_COLS_PAD), F32),
        compiler_params=_cparams(("parallel",)),
    )(x2d, norm_w, w_bf16)


def _outmlp_kernel(x_ref, ya_ref, yb_ref, yc_ref, yd_ref, wo_ref, nw_ref, wu_ref, wd_ref, nf_ref,
                   o_ref, x1_ref, h_ref, acc_ref, *, final_norm):
    f = pl.program_id(1)

    @pl.when(f == 0)
    def _():
        x1 = x_ref[...]
        for g, y_ref in enumerate((ya_ref, yb_ref, yc_ref, yd_ref)):
            x1 = x1 + _mm(y_ref[...], wo_ref[g * GROUP_W:(g + 1) * GROUP_W, :])
        x1_ref[...] = x1
        h_ref[...] = _rms(x1, nw_ref[...]).astype(BF16)
        acc_ref[...] = jnp.zeros_like(acc_ref)

    u = jnp.dot(h_ref[...], wu_ref[...], preferred_element_type=F32)
    u = jnp.square(jnp.maximum(u, 0.0))
    acc_ref[...] += _mm(u, wd_ref[...])

    @pl.when(f == pl.num_programs(1) - 1)
    def _():
        x2 = x1_ref[...] + acc_ref[...]
        if final_norm:
            x2 = _rms(x2, nf_ref[...])
        o_ref[...] = x2


def _outmlp(x2d, ys, w_out, norm_w, w_up, w_down, norm_f, final_norm, tm, tf):
    t = x2d.shape[0]
    row = lambda i, f: (i, 0)
    const = lambda i, f: (0, 0)
    return pl.pallas_call(
        functools.partial(_outmlp_kernel, final_norm=final_norm),
        grid=(t // tm, D_FF // tf),
        in_specs=[pl.BlockSpec((tm, D_MODEL), row)]
        + [pl.BlockSpec((tm, GROUP_W), row)] * 4
        + [pl.BlockSpec((D_MODEL, D_MODEL), const),
           pl.BlockSpec((1, D_MODEL), const),
           pl.BlockSpec((D_MODEL, tf), lambda i, f: (0, f)),
           pl.BlockSpec((tf, D_MODEL), lambda i, f: (f, 0)),
           pl.BlockSpec((1, D_MODEL), const)],
        out_specs=pl.BlockSpec((tm, D_MODEL), row),
        out_shape=jax.ShapeDtypeStruct((t, D_MODEL), F32),
        scratch_shapes=[pltpu.VMEM((tm, D_MODEL), F32),
                        pltpu.VMEM((tm, D_MODEL), BF16),
                        pltpu.VMEM((tm, D_MODEL), F32)],
        compiler_params=_cparams(("parallel", "arbitrary")),
    )(x2d, *ys, w_out, norm_w, w_up, w_down, norm_f)


def _rope_kernel(angle_ref, cos_ref, sin_ref, *, pos0, tl):
    i = pl.program_id(0)
    pos = (pos0 + i * tl + lax.broadcasted_iota(jnp.int32, (tl, 2 * HEAD_W), 0)).astype(F32)
    theta = pos * angle_ref[...]
    lane = lax.broadcasted_iota(jnp.int32, (tl, 2 * HEAD_W), 1)
    cos_ref[...] = jnp.cos(theta)
    s = jnp.sin(theta)
    sin_ref[...] = jnp.where(lane % 2 == 0, -s, s)


def _rope_tables(angle_row, pos0, seq, tl):
    return pl.pallas_call(
        functools.partial(_rope_kernel, pos0=pos0, tl=tl),
        grid=(seq // tl,),
        in_specs=[pl.BlockSpec((1, 2 * HEAD_W), lambda i: (0, 0))],
        out_specs=[pl.BlockSpec((tl, 2 * HEAD_W), lambda i: (i, 0))] * 2,
        out_shape=[jax.ShapeDtypeStruct((seq, 2 * HEAD_W), F32)] * 2,
        compiler_params=_cparams(("parallel",)),
    )(angle_row)


def _ret_kernel(lg_ref, p_ref, cos_ref, sin_ref, s0_ref, y_ref, s_ref, *, c, nb):
    @pl.when(pl.program_id(1) == 0)
    def _():
        s_ref[...] = s0_ref[...]

    cos = jnp.concatenate([cos_ref[...]] * 2, axis=1)
    sin = jnp.concatenate([sin_ref[...]] * 2, axis=1)
    even = lax.broadcasted_iota(jnp.int32, (c, GROUP_W), 1) % 2 == 0
    t_i = lax.broadcasted_iota(jnp.int32, (c, c), 0)
    s_i = lax.broadcasted_iota(jnp.int32, (c, c), 1)
    dts = (t_i - s_i).astype(F32)
    causal = t_i >= s_i
    tcol = lax.broadcasted_iota(jnp.int32, (c, 1), 0).astype(F32)

    def rot(x):
        swapped = jnp.where(even, pltpu.roll(x, GROUP_W - 1, 1), pltpu.roll(x, 1, 1))
        return x * cos + swapped * sin

    def body(b, carry):
        p = p_ref[b]
        q = rot(p[:, 0:GROUP_W])
        k = rot(p[:, GROUP_W:2 * GROUP_W]) * (HEAD_W ** -0.5)
        v = p[:, 2 * GROUP_W:3 * GROUP_W]
        g = p[:, 3 * GROUP_W:4 * GROUP_W]
        outs = []
        for h in range(HEADS):
            lg = jnp.full((1, 1), lg_ref[h], F32)
            sl = slice(h * HEAD_W, (h + 1) * HEAD_W)
            qh, kh, vh = q[:, sl], k[:, sl], v[:, sl]
            decay = jnp.where(causal, jnp.exp(jnp.where(causal, dts * lg, 0.0)), 0.0)
            s_prev = s_ref[b, h]
            o = _mm(_mm_nt(qh, kh) * decay, vh)
            o = o + _mm(qh * jnp.exp((tcol + 1.0) * lg), s_prev)
            kd = kh * jnp.exp((c - 1.0 - tcol) * lg)
            s_ref[b, h] = jnp.exp(c * lg) * s_prev + _mm_tn(kd, vh)
            mu = jnp.mean(o, axis=-1, keepdims=True)
            var = jnp.mean(jnp.square(o - mu), axis=-1, keepdims=True)
            outs.append((o - mu) * lax.rsqrt(var + EPS))
        y_ref[b] = jnp.concatenate(outs, axis=1) * _silu(g)
        return carry

    lax.fori_loop(0, nb, body, 0)


def _retention(proj, cos, sin, log_gamma, s0, c, nb):
    bsz, seq, _ = proj.shape
    st = pl.BlockSpec((nb, HEADS, HEAD_W, HEAD_W), lambda i, j: (i, 0, 0, 0))
    return pl.pallas_call(
        functools.partial(_ret_kernel, c=c, nb=nb),
        grid=(bsz // nb, seq // c),
        in_specs=[pl.BlockSpec(memory_space=pltpu.SMEM),
                  pl.BlockSpec((nb, c, 4 * GROUP_W), lambda i, j: (i, j, COL_RET // (4 * GROUP_W))),
                  pl.BlockSpec((c, 2 * HEAD_W), lambda i, j: (j, 0)),
                  pl.BlockSpec((c, 2 * HEAD_W), lambda i, j: (j, 0)),
                  st],
        out_specs=[pl.BlockSpec((nb, c, GROUP_W), lambda i, j: (i, j, 0)), st],
        out_shape=[jax.ShapeDtypeStruct((bsz, seq, GROUP_W), F32),
                   jax.ShapeDtypeStruct(s0.shape, F32)],
        compiler_params=_cparams(("parallel", "arbitrary")),
    )(log_gamma, proj, cos, sin, s0)


def _hg_kernel(p_ref, lb_ref, s0_ref, y_ref, s_ref, st_ref, *, c, nb):
    j = pl.program_id(1)

    @pl.when(j == 0)
    def _():
        for b in range(nb):
            for h in range(HEADS):
                st_ref[b, h] = s0_ref[b, h].T

    logits = lb_ref[...]
    lg = logits[1:DEPTH + 1, :]
    e = jnp.exp(lg - jnp.max(lg, axis=0, keepdims=True))
    sm = e / jnp.sum(e, axis=0, keepdims=True)
    layer = logits[0:1, :]
    csum = sm[0:1, :]
    lower = jnp.zeros_like(csum)
    for l in range(DEPTH):
        if l > 0:
            csum = csum + sm[l:l + 1, :]
        lower = jnp.where(layer == float(l), csum - sm[0:1, :], lower)

    tri = _tri(c)
    trif = tri.astype(F32)
    mid = c // 2 - 1

    def body(b, carry):
        p = p_ref[b]
        qq = _silu(p[:, 0:GROUP_W])
        ff = p[:, GROUP_W:2 * GROUP_W]
        v = p[:, 2 * GROUP_W:3 * GROUP_W]
        g = p[:, 3 * GROUP_W:4 * GROUP_W]
        forget = lower + (1.0 - lower) * jax.nn.sigmoid(ff)
        log_f = jnp.log(jnp.maximum(forget, HG_MIN_FORGET))
        key = (1.0 - lower) * jax.nn.sigmoid(-ff)
        bcum = _mm_f32(trif, log_f)
        b_mid = bcum[mid:mid + 1, :]
        b_last = bcum[c - 1:c, :]
        q_in = qq * jnp.exp(bcum - b_mid)
        k_in = key * jnp.exp(b_mid - bcum)
        q_x = qq * jnp.exp(bcum)
        k_x = key * jnp.exp(b_last - bcum)
        a_c = jnp.exp(b_last)
        outs = []
        for h in range(HEADS):
            sl = slice(h * HEAD_W, (h + 1) * HEAD_W)
            vh = v[:, sl]
            st_prev = st_ref[b, h]
            scores = jnp.where(tri, _mm_nt(q_in[:, sl], k_in[:, sl]), 0.0)
            o = _mm(scores, vh) + _mm_nt(q_x[:, sl], st_prev)
            st_ref[b, h] = st_prev * a_c[:, sl] + _mm_tn(vh, k_x[:, sl])
            outs.append(o * lax.rsqrt(jnp.mean(o * o, axis=-1, keepdims=True) + EPS))
        y_ref[b] = jnp.concatenate(outs, axis=1) * _silu(g)
        return carry

    lax.fori_loop(0, nb, body, 0)

    @pl.when(j == pl.num_programs(1) - 1)
    def _():
        for b in range(nb):
            for h in range(HEADS):
                s_ref[b, h] = st_ref[b, h].T


def _hgrn2(proj, lb_rows, s0, c, nb):
    bsz, seq, _ = proj.shape
    st = pl.BlockSpec((nb, HEADS, HEAD_W, HEAD_W), lambda i, j: (i, 0, 0, 0))
    return pl.pallas_call(
        functools.partial(_hg_kernel, c=c, nb=nb),
        grid=(bsz // nb, seq // c),
        in_specs=[pl.BlockSpec((nb, c, 4 * GROUP_W), lambda i, j: (i, j, COL_HG // (4 * GROUP_W))),
                  pl.BlockSpec((DEPTH + 1, GROUP_W), lambda i, j: (0, 0)),
                  st],
        out_specs=[pl.BlockSpec((nb, c, GROUP_W), lambda i, j: (i, j, 0)), st],
        out_shape=[jax.ShapeDtypeStruct((bsz, seq, GROUP_W), F32),
                   jax.ShapeDtypeStruct(s0.shape, F32)],
        scratch_shapes=[pltpu.VMEM((nb, HEADS, HEAD_W, HEAD_W), F32)],
        compiler_params=_cparams(("parallel", "arbitrary")),
    )(proj, lb_rows, s0)


def _ssd_kernel(p_ref, dt_ref, cw_ref, cb_ref, dtb_ref, alog_ref, dsk_ref, nw_ref, cv0_ref, s0_ref,
                y_ref, s_ref, cv_ref, xpad_ref, *, c, nb):
    @pl.when(pl.program_id(1) == 0)
    def _():
        s_ref[...] = s0_ref[...]
        xpad_ref[:, SUBLANES - (SSD_CONV - 1):SUBLANES, :] = cv0_ref[...]

    tri = _tri(c)
    trif = tri.astype(F32)
    cw = cw_ref[...]
    n_tail = SSD_CONV - 1
    lo = SUBLANES - n_tail

    def body(b, carry):
        p = p_ref[b]
        z = p[:, 0:GROUP_W]
        xpad_ref[b, SUBLANES:SUBLANES + c, :] = p[:, GROUP_W:GROUP_W + SSD_CONV_DIM]
        conv = cb_ref[...] + cw[n_tail:n_tail + 1, :] * p[:, GROUP_W:GROUP_W + SSD_CONV_DIM]
        for w in range(n_tail):
            conv = conv + cw[w:w + 1, :] * xpad_ref[b, lo + w:lo + w + c, :]
        tail = xpad_ref[b, c + lo:c + SUBLANES, :]
        cv_ref[b] = tail
        xpad_ref[b, lo:SUBLANES, :] = tail
        xc = _silu(conv)
        xs = xc[:, 0:GROUP_W]
        bm = xc[:, GROUP_W:2 * GROUP_W]
        cm = xc[:, 2 * GROUP_W:3 * GROUP_W]

        delta = jax.nn.softplus(dt_ref[b][:, 0:128] + dtb_ref[...])
        log_a = -delta * jnp.exp(alog_ref[...])
        bcol = _mm_f32(trif, log_a)
        brow = bcol.T
        drow = delta.T
        outs = []
        for h in range(HEADS):
            grp = h // 2
            gs = slice(grp * SSD_STATE, (grp + 1) * SSD_STATE)
            sl = slice(h * HEAD_W, (h + 1) * HEAD_W)
            xh = xs[:, sl]
            b_t = bcol[:, h:h + 1]
            b_s = brow[h:h + 1, :]
            b_last = bcol[c - 1:c, h:h + 1]
            w = jnp.where(tri, jnp.exp(jnp.where(tri, b_t - b_s, 0.0)) * drow[h:h + 1, :], 0.0)
            s_prev = s_ref[b, h]
            o = _mm(_mm_nt(cm[:, gs], bm[:, gs]) * w, xh)
            o = o + _mm(cm[:, gs] * jnp.exp(b_t), s_prev)
            kd = bm[:, gs] * (delta[:, h:h + 1] * jnp.exp(b_last - b_t))
            s_ref[b, h] = jnp.exp(b_last) * s_prev + _mm_tn(kd, xh)
            outs.append(o)
        y = (jnp.concatenate(outs, axis=1) + dsk_ref[...] * xs) * _silu(z)
        halves = []
        for grp in range(2):
            yg = y[:, grp * 128:(grp + 1) * 128]
            halves.append(yg * lax.rsqrt(jnp.mean(yg * yg, axis=-1, keepdims=True) + EPS))
        y_ref[b] = jnp.concatenate(halves, axis=1) * nw_ref[...]
        return carry

    lax.fori_loop(0, nb, body, 0)


def _ssd(proj, conv_w, conv_b, dt_bias, a_log, d_skip, norm_w, cv0, s0, c, nb):
    bsz, seq, _ = proj.shape
    st = pl.BlockSpec((nb, HEADS, SSD_STATE, HEAD_W), lambda i, j: (i, 0, 0, 0))
    cv = pl.BlockSpec((nb, SSD_CONV - 1, SSD_CONV_DIM), lambda i, j: (i, 0, 0))
    const = lambda i, j: (0, 0)
    return pl.pallas_call(
        functools.partial(_ssd_kernel, c=c, nb=nb),
        grid=(bsz // nb, seq // c),
        in_specs=[pl.BlockSpec((nb, c, 4 * GROUP_W), lambda i, j: (i, j, COL_SSD // (4 * GROUP_W))),
                  pl.BlockSpec((nb, c, GROUP_W), lambda i, j: (i, j, COL_DT // GROUP_W)),
                  pl.BlockSpec((SSD_CONV, SSD_CONV_DIM), const),
                  pl.BlockSpec((1, SSD_CONV_DIM), const),
                  pl.BlockSpec((1, 128), const),
                  pl.BlockSpec((1, 128), const),
                  pl.BlockSpec((1, GROUP_W), const),
                  pl.BlockSpec((1, GROUP_W), const),
                  cv, st],
        out_specs=[pl.BlockSpec((nb, c, GROUP_W), lambda i, j: (i, j, 0)), st, cv],
        out_shape=[jax.ShapeDtypeStruct((bsz, seq, GROUP_W), F32),
                   jax.ShapeDtypeStruct(s0.shape, F32),
                   jax.ShapeDtypeStruct(cv0.shape, F32)],
        scratch_shapes=[pltpu.VMEM((nb, c + SUBLANES, SSD_CONV_DIM), F32)],
        compiler_params=_cparams(("parallel", "arbitrary")),
    )(proj, proj, conv_w, conv_b, dt_bias, a_log, d_skip, norm_w, cv0, s0)


def _s5_kernel(u_ref, are_ref, aim_ref, step_ref, bre_ref, bim_ref, cw_ref, dsk_ref, gw_ref, gb_ref,
               s0r_ref, s0i_ref, y_ref, sr_ref, si_ref, wb_ref, ab_ref, bu_ref, *, tl):
    nbt = SUBLANES * tl

    @pl.when(pl.program_id(1) == 0)
    def _():
        sr_ref[...] = s0r_ref[...]
        si_ref[...] = s0i_ref[...]
        lr, li, st = are_ref[...], aim_ref[...], jnp.exp(step_ref[...])
        mag = jnp.exp(lr * st)
        abr = mag * jnp.cos(li * st)
        abi = mag * jnp.sin(li * st)
        ab_ref[0:1, :] = abr
        ab_ref[1:2, :] = abi
        den = lr * lr + li * li
        cr = ((abr - 1.0) * lr + abi * li) / den
        ci = (abi * lr - (abr - 1.0) * li) / den
        wb_ref[:, 0:S5_N] = (cr * bre_ref[...] - ci * bim_ref[...]).astype(BF16)
        wb_ref[:, S5_N:2 * S5_N] = (cr * bim_ref[...] + ci * bre_ref[...]).astype(BF16)

    u = u_ref[...].reshape(nbt, GROUP_W)
    bu_ref[...] = jnp.dot(u.astype(BF16), wb_ref[...], preferred_element_type=F32)
    ar = ab_ref[0:1, :]
    ai = ab_ref[1:2, :]

    def step(t, carry):
        xr, xi = carry
        rows = pl.ds(pl.multiple_of(t * SUBLANES, SUBLANES), SUBLANES)
        nr = ar * xr - ai * xi + bu_ref[rows, 0:S5_N]
        ni = ar * xi + ai * xr + bu_ref[rows, S5_N:2 * S5_N]
        bu_ref[rows, 0:S5_N] = nr
        bu_ref[rows, S5_N:2 * S5_N] = ni
        return nr, ni

    xr, xi = lax.fori_loop(0, tl, step, (sr_ref[...], si_ref[...]))
    sr_ref[...] = xr
    si_ref[...] = xi

    y = _mm(bu_ref[...], cw_ref[...]) + dsk_ref[...] * u
    y = jax.nn.gelu(y)
    y = y * jax.nn.sigmoid(_mm(y, gw_ref[...]) + gb_ref[...])
    y_ref[...] = y.reshape(tl, SUBLANES, GROUP_W)


def _s5(u_tb, a_re, a_im, step, b_re_blk, b_im_blk, c_blk, d_skip, glu_w, glu_b, s0r, s0i, tl):
    seq, bsz, _ = u_tb.shape
    const = lambda i, j: (0, 0)
    st = pl.BlockSpec((SUBLANES, S5_N), lambda i, j: (i, 0))
    return pl.pallas_call(
        functools.partial(_s5_kernel, tl=tl),
        grid=(bsz // SUBLANES, seq // tl),
        in_specs=[pl.BlockSpec((tl, SUBLANES, GROUP_W), lambda i, j: (j, i, 0)),
                  pl.BlockSpec((1, S5_N), const), pl.BlockSpec((1, S5_N), const),
                  pl.BlockSpec((1, S5_N), const),
                  pl.BlockSpec((GROUP_W, S5_N), const), pl.BlockSpec((GROUP_W, S5_N), const),
                  pl.BlockSpec((2 * S5_N, GROUP_W), const),
                  pl.BlockSpec((1, GROUP_W), const),
                  pl.BlockSpec((GROUP_W, GROUP_W), const),
                  pl.BlockSpec((1, GROUP_W), const),
                  st, st],
        out_specs=[pl.BlockSpec((tl, SUBLANES, GROUP_W), lambda i, j: (j, i, 0)), st, st],
        out_shape=[jax.ShapeDtypeStruct((seq, bsz, GROUP_W), F32),
                   jax.ShapeDtypeStruct(s0r.shape, F32),
                   jax.ShapeDtypeStruct(s0i.shape, F32)],
        scratch_shapes=[pltpu.VMEM((GROUP_W, 2 * S5_N), BF16),
                        pltpu.VMEM((SUBLANES, S5_N), F32),
                        pltpu.VMEM((SUBLANES * tl, 2 * S5_N), F32)],
        compiler_params=_cparams(("parallel", "arbitrary")),
    )(u_tb, a_re, a_im, step, b_re_blk, b_im_blk, c_blk, d_skip, glu_w, glu_b, s0r, s0i)


def _block_diag(blocks):
    g, r, c = blocks.shape
    eye = jnp.eye(g, dtype=blocks.dtype)
    return (eye[:, None, :, None] * blocks[:, :, None, :]).reshape(g * r, g * c)


def _prep_layer(l, w_in, w_out, hg_lb_logits, ssd_conv_w, ssd_conv_b, ssd_dt_bias, ssd_a_log, ssd_d,
                ssd_norm, s5_a_re, s5_a_im, s5_log_dt, s5_b_re, s5_b_im, s5_c_re, s5_c_im, s5_d,
                s5_glu_w, s5_glu_b, w_up, w_down):
    wi = w_in[l]
    pad = jnp.zeros((D_MODEL, IN_COLS_PAD - COL_DT - HEADS), F32)
    wi = jnp.concatenate([wi[:, :3072], wi[:, 3076:3332], wi[:, 3072:3076], pad], axis=1).astype(BF16)
    lane_pad = lambda v: jnp.concatenate([v, jnp.zeros((128 - HEADS,), F32)])[None, :]
    return dict(
        w_in=wi, w_out=w_out[l].astype(BF16), w_up=w_up[l].astype(BF16), w_down=w_down[l].astype(BF16),
        lb_rows=jnp.concatenate([jnp.full((1, GROUP_W), float(l), F32), hg_lb_logits], axis=0),
        conv_w=ssd_conv_w[l], conv_b=ssd_conv_b[l][None, :],
        dt_bias=lane_pad(ssd_dt_bias[l]), a_log=lane_pad(ssd_a_log[l]),
        ssd_d=jnp.repeat(ssd_d[l], HEAD_W)[None, :], ssd_norm=ssd_norm[l][None, :],
        a_re=s5_a_re[l].reshape(1, S5_N), a_im=s5_a_im[l].reshape(1, S5_N),
        step=jnp.repeat(s5_log_dt[l], S5_STATE)[None, :],
        b_re=_block_diag(jnp.swapaxes(s5_b_re[l], 1, 2)), b_im=_block_diag(jnp.swapaxes(s5_b_im[l], 1, 2)),
        c_blk=jnp.concatenate([_block_diag(jnp.swapaxes(s5_c_re[l], 1, 2)),
                               -_block_diag(jnp.swapaxes(s5_c_im[l], 1, 2))], axis=0).astype(BF16),
        s5_d=s5_d[l][None, :], glu_w=s5_glu_w[l].astype(BF16), glu_b=s5_glu_b[l][None, :],
    )


def _trunk(x, pos0, states, layers, norm_mix, norm_mlp, norm_final, angle_row, log_gamma, cfg):
    bsz, seq, _ = x.shape
    st_ret, st_hg, st_ssd, st_conv, st_re, st_im = states
    cos, sin = _rope_tables(angle_row, pos0, seq, cfg["rope_tl"])
    x2d = x.reshape(bsz * seq, D_MODEL)
    new = [[] for _ in range(6)]
    for l, p in enumerate(layers):
        proj = _inproj(x2d, norm_mix[l][None, :], p["w_in"], cfg["tm_in"]).reshape(bsz, seq, IN_COLS_PAD)
        ya, s_a = _retention(proj, cos, sin, log_gamma, st_ret[l], cfg["c_ret"], cfg["nb"])
        yb, s_b = _hgrn2(proj, p["lb_rows"], st_hg[l], cfg["c_hg"], cfg["nb"])
        yc, s_c, c_c = _ssd(proj, p["conv_w"], p["conv_b"], p["dt_bias"], p["a_log"], p["ssd_d"],
                            p["ssd_norm"], st_conv[l], st_ssd[l], cfg["c_ssd"], cfg["nb"])
        u_tb = jnp.swapaxes(proj[:, :, COL_S5:COL_S5 + GROUP_W], 0, 1)
        yd, s_re, s_im = _s5(u_tb, p["a_re"], p["a_im"], p["step"], p["b_re"], p["b_im"], p["c_blk"],
                             p["s5_d"], p["glu_w"], p["glu_b"],
                             st_re[l].reshape(bsz, S5_N), st_im[l].reshape(bsz, S5_N), cfg["tl_s5"])
        yd = jnp.swapaxes(yd, 0, 1)
        ys = [y.reshape(bsz * seq, GROUP_W) for y in (ya, yb, yc, yd)]
        x2d = _outmlp(x2d, ys, p["w_out"], norm_mlp[l][None, :], p["w_up"], p["w_down"],
                      norm_final[None, :], l == DEPTH - 1, cfg["tm_out"], cfg["tf"])
        for lst, s in zip(new, (s_a, s_b, s_c, c_c, s_re.reshape(bsz, S5_GROUPS, S5_STATE),
                                s_im.reshape(bsz, S5_GROUPS, S5_STATE))):
            lst.append(s)
    return (x2d.reshape(bsz, seq, D_MODEL),) + tuple(jnp.stack(n) for n in new)


PROMPT_CFG = dict(rope_tl=256, tm_in=256, c_ret=128, c_hg=64, c_ssd=128, tl_s5=64, nb=1, tm_out=512, tf=1024)
SAMPLE_CFG = dict(rope_tl=8, tm_in=256, c_ret=8, c_hg=8, c_ssd=8, tl_s5=8, nb=16, tm_out=512, tf=1024)


def kernel(x_prompt, x_sample, state_ret, state_hgrn, state_ssd, state_ssd_conv, state_s5_re, state_s5_im, norm_mix, w_in, w_out, hg_lb_logits, ssd_conv_w, ssd_conv_b, ssd_dt_bias, ssd_a_log, ssd_d, ssd_norm, s5_a_re, s5_a_im, s5_log_dt, s5_b_re, s5_b_im, s5_c_re, s5_c_im, s5_d, s5_glu_w, s5_glu_b, norm_mlp, w_up, w_down, norm_final):
    layers = [_prep_layer(l, w_in, w_out, hg_lb_logits, ssd_conv_w, ssd_conv_b, ssd_dt_bias, ssd_a_log,
                          ssd_d, ssd_norm, s5_a_re, s5_a_im, s5_log_dt, s5_b_re, s5_b_im, s5_c_re,
                          s5_c_im, s5_d, s5_glu_w, s5_glu_b, w_up, w_down) for l in range(DEPTH)]
    angle = 1.0 / (ROPE_BASE ** jnp.linspace(0.0, 1.0, HEAD_W // 2, dtype=F32))
    angle_row = jnp.tile(jnp.repeat(angle, 2), 2)[None, :]
    log_gamma = jnp.log(1.0 - jnp.exp2(-5.0 - jnp.arange(HEADS, dtype=F32)))

    nb = x_prompt.shape[0]
    states = (state_ret, state_hgrn, state_ssd, state_ssd_conv, state_s5_re, state_s5_im)
    fresh = tuple(jnp.zeros((DEPTH, nb) + s.shape[2:], s.dtype) for s in states)
    seq_p = x_prompt.shape[1]
    past_len = 16384
    out_p = _trunk(x_prompt, 0, fresh, layers, norm_mix, norm_mlp, norm_final, angle_row, log_gamma, PROMPT_CFG)
    out_s = _trunk(x_sample, past_len, states, layers, norm_mix, norm_mlp, norm_final, angle_row, log_gamma,
                   SAMPLE_CFG)
    del seq_p
    return (out_p[0], out_s[0]) + out_p[1:] + out_s[1:]
```

```python
import functools

import jax
import jax.numpy as jnp
from jax import lax
from jax.experimental import pallas as pl
from jax.experimental.pallas import tpu as pltpu

F32 = jnp.float32
BF16 = jnp.bfloat16

D_MODEL = 1024
DEPTH = 2
PAST_LEN = 16384
GROUP_W = 256
HEADS = 4
HEAD_W = 64
ROPE_BASE = 10000.0
HG_MIN_FORGET = 1e-30
SSD_STATE = 128
SSD_CONV = 4
SSD_CONV_DIM = 768
S5_GROUPS = 16
S5_STATE = 64
S5_N = S5_GROUPS * S5_STATE
D_FF = 4096
IN_COLS = 3332
EPS = 1e-6

COL_RET = 0
COL_HG = 1024
COL_SSD = 2048
COL_S5 = 3072
COL_DT = 3328
IN_COLS_PAD = 3584

LANES = 128
SUBLANES = 8
VMEM_LIMIT = 56 * 1024 * 1024


def _cparams(sem):
    return pltpu.CompilerParams(dimension_semantics=sem, vmem_limit_bytes=VMEM_LIMIT)


def _mm(a, b):
    return jnp.dot(a.astype(BF16), b.astype(BF16), preferred_element_type=F32)


def _mm_nt(a, b):
    return lax.dot_general(a.astype(BF16), b.astype(BF16), (((1,), (1,)), ((), ())),
                           preferred_element_type=F32)


def _mm_tn(a, b):
    return lax.dot_general(a.astype(BF16), b.astype(BF16), (((0,), (0,)), ((), ())),
                           preferred_element_type=F32)


def _mm_f32(a, b):
    return jnp.dot(a, b, preferred_element_type=F32, precision=lax.Precision.HIGHEST)


def _silu(x):
    return x * jax.nn.sigmoid(x)


def _rms(x, w):
    return x * lax.rsqrt(jnp.mean(x * x, axis=-1, keepdims=True) + EPS) * w


def _tri(c):
    t = lax.broadcasted_iota(jnp.int32, (c, c), 0)
    s = lax.broadcasted_iota(jnp.int32, (c, c), 1)
    return t >= s


def _layer_spec(l, tail, n_grid):
    zeros = (0,) * len(tail)
    if n_grid == 1:
        return pl.BlockSpec((None,) + tail, lambda i: (l,) + zeros)
    return pl.BlockSpec((None,) + tail, lambda i, j: (l,) + zeros)


def _state_spec(l, nb, tail):
    zeros = (0,) * len(tail)
    return pl.BlockSpec((None, nb) + tail, lambda i, j: (l, i) + zeros)


def _state_io(l, nb, tail, bsz, init, prev):
    spec = _state_spec(l, nb, tail)
    in_specs, operands = [], []
    if init is not None:
        in_specs.append(spec)
        operands.append(init)
    alias_pos = None
    if prev is not None:
        alias_pos = len(in_specs)
        in_specs.append(pl.BlockSpec(memory_space=pl.ANY))
        operands.append(prev)
    return in_specs, operands, spec, jax.ShapeDtypeStruct((DEPTH, bsz) + tail, F32), alias_pos


def _cast_kernel(w_ref, o_ref):
    o_ref[...] = w_ref[...].astype(BF16)


def _cast_bf16(w, tr):
    d, r, c = w.shape
    spec = pl.BlockSpec((None, tr, c), lambda l, i: (l, i, 0))
    return pl.pallas_call(
        _cast_kernel, grid=(d, r // tr), in_specs=[spec], out_specs=spec,
        out_shape=jax.ShapeDtypeStruct(w.shape, BF16),
        compiler_params=_cparams(("parallel", "parallel")),
    )(w)


def _inproj_kernel(x_ref, nw_ref, w_ref, o_ref, wb_ref):
    @pl.when(pl.program_id(0) == 0)
    def _():
        rows = 256
        lane = lax.broadcasted_iota(jnp.int32, (rows, GROUP_W), 1)
        for r in range(0, D_MODEL, rows):
            rs = slice(r, r + rows)
            wb_ref[rs, 0:COL_S5] = w_ref[rs, 0:COL_S5].astype(BF16)
            tail = w_ref[rs, COL_S5:IN_COLS]
            wb_ref[rs, COL_S5:COL_DT] = tail[:, HEADS:HEADS + GROUP_W].astype(BF16)
            wb_ref[rs, COL_DT:IN_COLS_PAD] = jnp.where(lane < HEADS, tail[:, 0:GROUP_W], 0.0).astype(BF16)

    h = _rms(x_ref[...], nw_ref[...])
    o_ref[...] = jnp.dot(h.astype(BF16), wb_ref[...], preferred_element_type=F32)


def _inproj(x2d, norm_w, w_in, l, tm):
    t = x2d.shape[0]
    return pl.pallas_call(
        _inproj_kernel,
        grid=(t // tm,),
        in_specs=[pl.BlockSpec((tm, D_MODEL), lambda i: (i, 0)),
                  _layer_spec(l, (1, D_MODEL), 1),
                  pl.BlockSpec((None, D_MODEL, IN_COLS), lambda i: (l, 0, 0), pipeline_mode=pl.Buffered(1))],
        out_specs=pl.BlockSpec((tm, IN_COLS_PAD), lambda i: (i, 0)),
        out_shape=jax.ShapeDtypeStruct((t, IN_COLS_PAD), F32),
        scratch_shapes=[pltpu.VMEM((D_MODEL, IN_COLS_PAD), BF16)],
        compiler_params=_cparams(("arbitrary",)),
    )(x2d, norm_w, w_in)


def _outmlp_kernel(x_ref, ya_ref, yb_ref, yc_ref, yd_ref, wo_ref, nw_ref, wu_ref, wd_ref, nf_ref,
                   o_ref, x1_ref, h_ref, acc_ref, *, final_norm):
    f = pl.program_id(1)

    @pl.when(f == 0)
    def _():
        x1 = x_ref[...]
        for g, y_ref in enumerate((ya_ref, yb_ref, yc_ref, yd_ref)):
            x1 = x1 + _mm(y_ref[...], wo_ref[g * GROUP_W:(g + 1) * GROUP_W, :])
        x1_ref[...] = x1
        h_ref[...] = _rms(x1, nw_ref[...]).astype(BF16)
        acc_ref[...] = jnp.zeros_like(acc_ref)

    u = jnp.dot(h_ref[...], wu_ref[...], preferred_element_type=F32)
    u = jnp.square(jnp.maximum(u, 0.0))
    acc_ref[...] += _mm(u, wd_ref[...])

    @pl.when(f == pl.num_programs(1) - 1)
    def _():
        x2 = x1_ref[...] + acc_ref[...]
        if final_norm:
            x2 = _rms(x2, nf_ref[...])
        o_ref[...] = x2


def _outmlp(x2d, ys, w_out, norm_w, w_up, w_down, norm_f, l, tm, tf):
    t = x2d.shape[0]
    row = lambda i, f: (i, 0)
    return pl.pallas_call(
        functools.partial(_outmlp_kernel, final_norm=(l == DEPTH - 1)),
        grid=(t // tm, D_FF // tf),
        in_specs=[pl.BlockSpec((tm, D_MODEL), row)]
        + [pl.BlockSpec((tm, GROUP_W), row)] * 4
        + [_layer_spec(l, (D_MODEL, D_MODEL), 2),
           _layer_spec(l, (1, D_MODEL), 2),
           pl.BlockSpec((None, D_MODEL, tf), lambda i, f: (l, 0, f)),
           pl.BlockSpec((None, tf, D_MODEL), lambda i, f: (l, f, 0)),
           pl.BlockSpec((1, D_MODEL), lambda i, f: (0, 0))],
        out_specs=pl.BlockSpec((tm, D_MODEL), row),
        out_shape=jax.ShapeDtypeStruct((t, D_MODEL), F32),
        scratch_shapes=[pltpu.VMEM((tm, D_MODEL), F32),
                        pltpu.VMEM((tm, D_MODEL), BF16),
                        pltpu.VMEM((tm, D_MODEL), F32)],
        compiler_params=_cparams(("parallel", "arbitrary")),
    )(x2d, *ys, w_out, norm_w, w_up, w_down, norm_f)


def _rope_kernel(angle_ref, cos_ref, sin_ref, *, pos0, tl):
    i = pl.program_id(0)
    pos = (pos0 + i * tl + lax.broadcasted_iota(jnp.int32, (tl, 2 * HEAD_W), 0)).astype(F32)
    theta = pos * angle_ref[...]
    lane = lax.broadcasted_iota(jnp.int32, (tl, 2 * HEAD_W), 1)
    cos_ref[...] = jnp.cos(theta)
    s = jnp.sin(theta)
    sin_ref[...] = jnp.where(lane % 2 == 0, -s, s)


def _rope_tables(angle_row, pos0, seq, tl):
    return pl.pallas_call(
        functools.partial(_rope_kernel, pos0=pos0, tl=tl),
        grid=(seq // tl,),
        in_specs=[pl.BlockSpec((1, 2 * HEAD_W), lambda i: (0, 0))],
        out_specs=[pl.BlockSpec((tl, 2 * HEAD_W), lambda i: (i, 0))] * 2,
        out_shape=[jax.ShapeDtypeStruct((seq, 2 * HEAD_W), F32)] * 2,
        compiler_params=_cparams(("parallel",)),
    )(angle_row)


def _ret_kernel(*refs, c, nchunk, nb, has_init, has_prev):
    lg_ref, p_ref, cos_ref, sin_ref = refs[:4]
    s0_ref = refs[4] if has_init else None
    y_ref, s_ref = refs[4 + has_init + has_prev:]

    @pl.when(pl.program_id(1) == 0)
    def _():
        s_ref[...] = s0_ref[...] if has_init else jnp.zeros_like(s_ref)

    even = lax.broadcasted_iota(jnp.int32, (c, GROUP_W), 1) % 2 == 0
    t_i = lax.broadcasted_iota(jnp.int32, (c, c), 0)
    s_i = lax.broadcasted_iota(jnp.int32, (c, c), 1)
    dts = (t_i - s_i).astype(F32)
    causal = t_i >= s_i
    tcol = lax.broadcasted_iota(jnp.int32, (c, 1), 0).astype(F32)
    decay, e_q, e_k, a_c = [], [], [], []
    for h in range(HEADS):
        lg = jnp.full((1, 1), lg_ref[h], F32)
        decay.append(jnp.where(causal, jnp.exp(jnp.where(causal, dts * lg, 0.0)), 0.0))
        e_q.append(jnp.exp((tcol + 1.0) * lg))
        e_k.append(jnp.exp((c - 1.0 - tcol) * lg))
        a_c.append(jnp.exp(c * lg))

    def rot(x, cos, sin):
        swapped = jnp.where(even, pltpu.roll(x, GROUP_W - 1, 1), pltpu.roll(x, 1, 1))
        return x * cos + swapped * sin

    for b in range(nb):
        for ci in range(nchunk):
            rows = slice(ci * c, (ci + 1) * c)
            cos = jnp.concatenate([cos_ref[rows, :]] * 2, axis=1)
            sin = jnp.concatenate([sin_ref[rows, :]] * 2, axis=1)
            q = rot(p_ref[b, rows, 0:GROUP_W], cos, sin)
            k = rot(p_ref[b, rows, GROUP_W:2 * GROUP_W], cos, sin) * (HEAD_W ** -0.5)
            v = p_ref[b, rows, 2 * GROUP_W:3 * GROUP_W]
            g = p_ref[b, rows, 3 * GROUP_W:4 * GROUP_W]
            outs = []
            for h in range(HEADS):
                sl = slice(h * HEAD_W, (h + 1) * HEAD_W)
                qh, kh, vh = q[:, sl], k[:, sl], v[:, sl]
                s_prev = s_ref[b, h]
                o = _mm(_mm_nt(qh, kh) * decay[h], vh) + _mm(qh * e_q[h], s_prev)
                s_ref[b, h] = a_c[h] * s_prev + _mm_tn(kh * e_k[h], vh)
                mu = jnp.mean(o, axis=-1, keepdims=True)
                var = jnp.mean(jnp.square(o - mu), axis=-1, keepdims=True)
                outs.append((o - mu) * lax.rsqrt(var + EPS))
            y_ref[b, rows, :] = jnp.concatenate(outs, axis=1) * _silu(g)


def _retention(proj, cos, sin, log_gamma, init, prev, l, c, nchunk, nb):
    bsz, seq, _ = proj.shape
    tb = c * nchunk
    st_in, st_ops, st_out, st_shape, alias = _state_io(l, nb, (HEADS, HEAD_W, HEAD_W), bsz, init, prev)
    n_fixed = 4
    return pl.pallas_call(
        functools.partial(_ret_kernel, c=c, nchunk=nchunk, nb=nb, has_init=init is not None,
                          has_prev=prev is not None),
        grid=(bsz // nb, seq // tb),
        in_specs=[pl.BlockSpec(memory_space=pltpu.SMEM),
                  pl.BlockSpec((nb, tb, 4 * GROUP_W), lambda i, j: (i, j, COL_RET // (4 * GROUP_W))),
                  pl.BlockSpec((tb, 2 * HEAD_W), lambda i, j: (j, 0)),
                  pl.BlockSpec((tb, 2 * HEAD_W), lambda i, j: (j, 0))] + st_in,
        out_specs=[pl.BlockSpec((nb, tb, GROUP_W), lambda i, j: (i, j, 0)), st_out],
        out_shape=[jax.ShapeDtypeStruct((bsz, seq, GROUP_W), F32), st_shape],
        input_output_aliases={} if alias is None else {n_fixed + alias: 1},
        compiler_params=_cparams(("parallel", "arbitrary")),
    )(log_gamma, proj, cos, sin, *st_ops)


def _hg_kernel(*refs, c, nchunk, nb, layer, has_init, has_prev):
    p_ref, lb_ref = refs[:2]
    s0_ref = refs[2] if has_init else None
    y_ref, s_ref, st_ref = refs[2 + has_init + has_prev:]
    j = pl.program_id(1)

    @pl.when(j == 0)
    def _():
        if has_init:
            for b in range(nb):
                for h in range(HEADS):
                    st_ref[b, h] = s0_ref[b, h].T
        else:
            st_ref[...] = jnp.zeros_like(st_ref)

    logits = lb_ref[...]
    e = jnp.exp(logits - jnp.max(logits, axis=0, keepdims=True))
    sm = e / jnp.sum(e, axis=0, keepdims=True)
    csum = sm[0:1, :]
    for i in range(1, layer + 1):
        csum = csum + sm[i:i + 1, :]
    lower = csum - sm[0:1, :]

    tri = _tri(c)
    trif = tri.astype(F32)
    mid = c // 2 - 1

    for b in range(nb):
        for ci in range(nchunk):
            rows = slice(ci * c, (ci + 1) * c)
            qq = _silu(p_ref[b, rows, 0:GROUP_W])
            ff = p_ref[b, rows, GROUP_W:2 * GROUP_W]
            v = p_ref[b, rows, 2 * GROUP_W:3 * GROUP_W]
            g = p_ref[b, rows, 3 * GROUP_W:4 * GROUP_W]
            forget = lower + (1.0 - lower) * jax.nn.sigmoid(ff)
            log_f = jnp.log(jnp.maximum(forget, HG_MIN_FORGET))
            key = (1.0 - lower) * jax.nn.sigmoid(-ff)
            bcum = _mm_f32(trif, log_f)
            b_mid = bcum[mid:mid + 1, :]
            b_last = bcum[c - 1:c, :]
            q_in = qq * jnp.exp(bcum - b_mid)
            k_in = key * jnp.exp(b_mid - bcum)
            q_x = qq * jnp.exp(bcum)
            k_x = key * jnp.exp(b_last - bcum)
            a_c = jnp.exp(b_last)
            outs = []
            for h in range(HEADS):
                sl = slice(h * HEAD_W, (h + 1) * HEAD_W)
                vh = v[:, sl]
                st_prev = st_ref[b, h]
                scores = jnp.where(tri, _mm_nt(q_in[:, sl], k_in[:, sl]), 0.0)
                o = _mm(scores, vh) + _mm_nt(q_x[:, sl], st_prev)
                st_ref[b, h] = st_prev * a_c[:, sl] + _mm_tn(vh, k_x[:, sl])
                outs.append(o * lax.rsqrt(jnp.mean(o * o, axis=-1, keepdims=True) + EPS))
            y_ref[b, rows, :] = jnp.concatenate(outs, axis=1) * _silu(g)

    @pl.when(j == pl.num_programs(1) - 1)
    def _():
        for b in range(nb):
            for h in range(HEADS):
                s_ref[b, h] = st_ref[b, h].T


def _hgrn2(proj, lb_logits, init, prev, l, c, nchunk, nb):
    bsz, seq, _ = proj.shape
    tb = c * nchunk
    st_in, st_ops, st_out, st_shape, alias = _state_io(l, nb, (HEADS, HEAD_W, HEAD_W), bsz, init, prev)
    n_fixed = 2
    return pl.pallas_call(
        functools.partial(_hg_kernel, c=c, nchunk=nchunk, nb=nb, layer=l, has_init=init is not None,
                          has_prev=prev is not None),
        grid=(bsz // nb, seq // tb),
        in_specs=[pl.BlockSpec((nb, tb, 4 * GROUP_W), lambda i, j: (i, j, COL_HG // (4 * GROUP_W))),
                  pl.BlockSpec((DEPTH, GROUP_W), lambda i, j: (0, 0))] + st_in,
        out_specs=[pl.BlockSpec((nb, tb, GROUP_W), lambda i, j: (i, j, 0)), st_out],
        out_shape=[jax.ShapeDtypeStruct((bsz, seq, GROUP_W), F32), st_shape],
        scratch_shapes=[pltpu.VMEM((nb, HEADS, HEAD_W, HEAD_W), F32)],
        input_output_aliases={} if alias is None else {n_fixed + alias: 1},
        compiler_params=_cparams(("parallel", "arbitrary")),
    )(proj, lb_logits, *st_ops)


def _ssd_kernel(*refs, c, nchunk, nb, has_init, has_prev):
    p_ref, dt_ref, cw_ref, cb_ref, dtb_ref, alog_ref, dsk_ref, nw_ref = refs[:8]
    n_in = 8
    cv0_ref = s0_ref = None
    if has_init:
        cv0_ref, s0_ref = refs[n_in], refs[n_in + 1 + has_prev]
    y_ref, s_ref, cv_ref, xpad_ref = refs[n_in + 2 * (has_init + has_prev):]
    n_tail = SSD_CONV - 1
    lo = SUBLANES - n_tail

    @pl.when(pl.program_id(1) == 0)
    def _():
        if has_init:
            s_ref[...] = s0_ref[...]
            xpad_ref[:, lo:SUBLANES, :] = cv0_ref[...]
        else:
            s_ref[...] = jnp.zeros_like(s_ref)
            xpad_ref[:, lo:SUBLANES, :] = jnp.zeros((nb, n_tail, SSD_CONV_DIM), F32)

    tri = _tri(c)
    trif = tri.astype(F32)
    cw = cw_ref[...]
    tb = c * nchunk

    for b in range(nb):
        xpad_ref[b, SUBLANES:SUBLANES + tb, :] = p_ref[b, :, GROUP_W:GROUP_W + SSD_CONV_DIM]
        for ci in range(nchunk):
            r0 = ci * c
            rows = slice(r0, r0 + c)
            z = p_ref[b, rows, 0:GROUP_W]
            conv = cb_ref[...] + cw[n_tail:n_tail + 1, :] * p_ref[b, rows, GROUP_W:GROUP_W + SSD_CONV_DIM]
            for w in range(n_tail):
                conv = conv + cw[w:w + 1, :] * xpad_ref[b, r0 + lo + w:r0 + lo + w + c, :]
            xc = _silu(conv)
            xs = xc[:, 0:GROUP_W]
            bm = xc[:, GROUP_W:2 * GROUP_W]
            cm = xc[:, 2 * GROUP_W:3 * GROUP_W]

            delta = jax.nn.softplus(dt_ref[b, rows, 0:LANES] + dtb_ref[...])
            log_a = -delta * jnp.exp(alog_ref[...])
            bcol = _mm_f32(trif, log_a)
            brow = bcol.T
            drow = delta.T
            outs = []
            for h in range(HEADS):
                grp = h // 2
                gs = slice(grp * SSD_STATE, (grp + 1) * SSD_STATE)
                sl = slice(h * HEAD_W, (h + 1) * HEAD_W)
                xh = xs[:, sl]
                b_t = bcol[:, h:h + 1]
                b_s = brow[h:h + 1, :]
                b_last = bcol[c - 1:c, h:h + 1]
                w_ts = jnp.where(tri, jnp.exp(jnp.where(tri, b_t - b_s, 0.0)) * drow[h:h + 1, :], 0.0)
                s_prev = s_ref[b, h]
                o = _mm(_mm_nt(cm[:, gs], bm[:, gs]) * w_ts, xh)
                o = o + _mm(cm[:, gs] * jnp.exp(b_t), s_prev)
                kd = bm[:, gs] * (delta[:, h:h + 1] * jnp.exp(b_last - b_t))
                s_ref[b, h] = jnp.exp(b_last) * s_prev + _mm_tn(kd, xh)
                outs.append(o)
            y = (jnp.concatenate(outs, axis=1) + dsk_ref[...] * xs) * _silu(z)
            halves = []
            for grp in range(2):
                yg = y[:, grp * LANES:(grp + 1) * LANES]
                halves.append(yg * lax.rsqrt(jnp.mean(yg * yg, axis=-1, keepdims=True) + EPS))
            y_ref[b, rows, :] = jnp.concatenate(halves, axis=1) * nw_ref[...]
        tail = xpad_ref[b, tb + lo:tb + SUBLANES, :]
        cv_ref[b] = tail
        xpad_ref[b, lo:SUBLANES, :] = tail


def _ssd(proj, prm, init_cv, init_s, prev_cv, prev_s, l, c, nchunk, nb):
    bsz, seq, _ = proj.shape
    tb = c * nchunk
    cv_in, cv_ops, cv_out, cv_shape, cv_alias = _state_io(l, nb, (SSD_CONV - 1, SSD_CONV_DIM), bsz,
                                                          init_cv, prev_cv)
    s_in, s_ops, s_out, s_shape, s_alias = _state_io(l, nb, (HEADS, SSD_STATE, HEAD_W), bsz, init_s, prev_s)
    n_fixed = 8
    aliases = {}
    if cv_alias is not None:
        aliases[n_fixed + cv_alias] = 2
        aliases[n_fixed + len(cv_in) + s_alias] = 1
    return pl.pallas_call(
        functools.partial(_ssd_kernel, c=c, nchunk=nchunk, nb=nb, has_init=init_s is not None,
                          has_prev=prev_s is not None),
        grid=(bsz // nb, seq // tb),
        in_specs=[pl.BlockSpec((nb, tb, 4 * GROUP_W), lambda i, j: (i, j, COL_SSD // (4 * GROUP_W))),
                  pl.BlockSpec((nb, tb, GROUP_W), lambda i, j: (i, j, COL_DT // GROUP_W)),
                  _layer_spec(l, (SSD_CONV, SSD_CONV_DIM), 2),
                  _layer_spec(l, (1, SSD_CONV_DIM), 2),
                  _layer_spec(l, (1, LANES), 2),
                  _layer_spec(l, (1, LANES), 2),
                  _layer_spec(l, (1, GROUP_W), 2),
                  _layer_spec(l, (1, GROUP_W), 2)] + cv_in + s_in,
        out_specs=[pl.BlockSpec((nb, tb, GROUP_W), lambda i, j: (i, j, 0)), s_out, cv_out],
        out_shape=[jax.ShapeDtypeStruct((bsz, seq, GROUP_W), F32), s_shape, cv_shape],
        scratch_shapes=[pltpu.VMEM((nb, tb + SUBLANES, SSD_CONV_DIM), F32)],
        input_output_aliases=aliases,
        compiler_params=_cparams(("parallel", "arbitrary")),
    )(proj, proj, prm["conv_w"], prm["conv_b"], prm["dt_bias"], prm["a_log"], prm["ssd_d"], prm["ssd_norm"],
      *cv_ops, *s_ops)


def _s5_kernel(*refs, tl, unroll, has_init, has_prev):
    (u0_ref, u1_ref, are_ref, aim_ref, ldt_ref, bre_ref, bim_ref, cw_ref, dsk_ref, gw_ref,
     gb_ref) = refs[:11]
    n_in = 11
    s0r_ref = s0i_ref = None
    if has_init:
        s0r_ref, s0i_ref = refs[n_in], refs[n_in + 1 + has_prev]
    (y_ref, sr_ref, si_ref, wb_ref, ab_ref, us0_ref, us1_ref, ut_ref, bu_ref, yt_ref, ys0_ref,
     ys1_ref) = refs[n_in + 2 * (has_init + has_prev):]
    nbt = SUBLANES * tl

    @pl.when(pl.program_id(1) == 0)
    def _():
        if has_init:
            sr_ref[...] = s0r_ref[...]
            si_ref[...] = s0i_ref[...]
        else:
            sr_ref[...] = jnp.zeros_like(sr_ref)
            si_ref[...] = jnp.zeros_like(si_ref)
        lr, li, st = are_ref[...], aim_ref[...], jnp.exp(ldt_ref[...])
        mag = jnp.exp(lr * st)
        abr = mag * jnp.cos(li * st)
        abi = mag * jnp.sin(li * st)
        ab_ref[0:1, :] = abr
        ab_ref[1:2, :] = abi
        den = lr * lr + li * li
        cr = ((abr - 1.0) * lr + abi * li) / den
        ci = (abi * lr - (abr - 1.0) * li) / den
        wb_ref[:, 0:S5_N] = (cr * bre_ref[...] - ci * bim_ref[...]).astype(BF16)
        wb_ref[:, S5_N:2 * S5_N] = (cr * bim_ref[...] + ci * bre_ref[...]).astype(BF16)

    us0_ref[...] = u0_ref[...].reshape(nbt, LANES)
    us1_ref[...] = u1_ref[...].reshape(nbt, LANES)

    def gather(t, carry):
        dst = pl.ds(pl.multiple_of(t * SUBLANES, SUBLANES), SUBLANES)
        src = pl.ds(t, SUBLANES, stride=tl)
        ut_ref[dst, 0:LANES] = us0_ref[src, :]
        ut_ref[dst, LANES:2 * LANES] = us1_ref[src, :]
        return carry

    lax.fori_loop(0, tl, gather, 0, unroll=unroll)
    u = ut_ref[...]
    bu_ref[...] = jnp.dot(u.astype(BF16), wb_ref[...], preferred_element_type=F32)
    ar = ab_ref[0:1, :]
    ai = ab_ref[1:2, :]

    def step(t, carry):
        xr, xi = carry
        rows = pl.ds(pl.multiple_of(t * SUBLANES, SUBLANES), SUBLANES)
        nr = ar * xr - ai * xi + bu_ref[rows, 0:S5_N]
        ni = ar * xi + ai * xr + bu_ref[rows, S5_N:2 * S5_N]
        bu_ref[rows, 0:S5_N] = nr
        bu_ref[rows, S5_N:2 * S5_N] = ni
        return nr, ni

    xr, xi = lax.fori_loop(0, tl, step, (sr_ref[...], si_ref[...]), unroll=unroll)
    sr_ref[...] = xr
    si_ref[...] = xi

    y = _mm(bu_ref[...], cw_ref[...]) + dsk_ref[...] * u
    y = jax.nn.gelu(y)
    yt_ref[...] = y * jax.nn.sigmoid(_mm(y, gw_ref[...]) + gb_ref[...])

    def scatter(t, carry):
        src = pl.ds(pl.multiple_of(t * SUBLANES, SUBLANES), SUBLANES)
        dst = pl.ds(t, SUBLANES, stride=tl)
        ys0_ref[dst, :] = yt_ref[src, 0:LANES]
        ys1_ref[dst, :] = yt_ref[src, LANES:2 * LANES]
        return carry

    lax.fori_loop(0, tl, scatter, 0, unroll=unroll)
    y_ref[:, :, 0:LANES] = ys0_ref[...].reshape(SUBLANES, tl, LANES)
    y_ref[:, :, LANES:2 * LANES] = ys1_ref[...].reshape(SUBLANES, tl, LANES)


def _s5(proj, prm, init_re, init_im, prev_re, prev_im, l, tl, unroll):
    bsz, seq, _ = proj.shape
    re_in, re_ops, re_out, re_shape, re_alias = _state_io(l, SUBLANES, (S5_N,), bsz, init_re, prev_re)
    im_in, im_ops, im_out, im_shape, im_alias = _state_io(l, SUBLANES, (S5_N,), bsz, init_im, prev_im)
    n_fixed = 11
    aliases = {}
    if re_alias is not None:
        aliases[n_fixed + re_alias] = 1
        aliases[n_fixed + len(re_in) + im_alias] = 2
    nbt = SUBLANES * tl
    return pl.pallas_call(
        functools.partial(_s5_kernel, tl=tl, unroll=unroll, has_init=init_re is not None,
                          has_prev=prev_re is not None),
        grid=(bsz // SUBLANES, seq // tl),
        in_specs=[pl.BlockSpec((SUBLANES, tl, LANES), lambda i, j: (i, j, COL_S5 // LANES)),
                  pl.BlockSpec((SUBLANES, tl, LANES), lambda i, j: (i, j, COL_S5 // LANES + 1)),
                  _layer_spec(l, (1, S5_N), 2), _layer_spec(l, (1, S5_N), 2), _layer_spec(l, (1, S5_N), 2),
                  _layer_spec(l, (GROUP_W, S5_N), 2), _layer_spec(l, (GROUP_W, S5_N), 2),
                  _layer_spec(l, (2 * S5_N, GROUP_W), 2),
                  _layer_spec(l, (1, GROUP_W), 2),
                  _layer_spec(l, (GROUP_W, GROUP_W), 2),
                  _layer_spec(l, (1, GROUP_W), 2)] + re_in + im_in,
        out_specs=[pl.BlockSpec((SUBLANES, tl, GROUP_W), lambda i, j: (i, j, 0)), re_out, im_out],
        out_shape=[jax.ShapeDtypeStruct((bsz, seq, GROUP_W), F32), re_shape, im_shape],
        scratch_shapes=[pltpu.VMEM((GROUP_W, 2 * S5_N), BF16),
                        pltpu.VMEM((SUBLANES, S5_N), F32),
                        pltpu.VMEM((nbt, LANES), F32), pltpu.VMEM((nbt, LANES), F32),
                        pltpu.VMEM((nbt, GROUP_W), F32),
                        pltpu.VMEM((nbt, 2 * S5_N), F32),
                        pltpu.VMEM((nbt, GROUP_W), F32),
                        pltpu.VMEM((nbt, LANES), F32), pltpu.VMEM((nbt, LANES), F32)],
        input_output_aliases=aliases,
        compiler_params=_cparams(("parallel", "arbitrary")),
    )(proj, proj, prm["a_re"], prm["a_im"], prm["log_dt"], prm["b_re"], prm["b_im"], prm["c_blk"],
      prm["s5_d"], prm["glu_w"], prm["glu_b"], *re_ops, *im_ops)


def _block_diag(blocks):
    d, g, r, c = blocks.shape
    eye = jnp.eye(g, dtype=blocks.dtype)
    return (eye[None, :, None, :, None] * blocks[:, :, :, None, :]).reshape(d, g * r, g * c)


def _prep(ssd_conv_w, ssd_conv_b, ssd_dt_bias, ssd_a_log, ssd_d, ssd_norm, s5_a_re, s5_a_im, s5_log_dt,
          s5_b_re, s5_b_im, s5_c_re, s5_c_im, s5_d, s5_glu_w, s5_glu_b):
    lane_pad = lambda v: jnp.pad(v, ((0, 0), (0, LANES - HEADS)))[:, None, :]
    c_blk = jnp.concatenate([_block_diag(jnp.swapaxes(s5_c_re, 2, 3)),
                             -_block_diag(jnp.swapaxes(s5_c_im, 2, 3))], axis=1)
    return dict(
        conv_w=ssd_conv_w, conv_b=ssd_conv_b[:, None, :],
        dt_bias=lane_pad(ssd_dt_bias), a_log=lane_pad(ssd_a_log),
        ssd_d=jnp.repeat(ssd_d, HEAD_W, axis=1)[:, None, :], ssd_norm=ssd_norm[:, None, :],
        a_re=s5_a_re.reshape(DEPTH, 1, S5_N), a_im=s5_a_im.reshape(DEPTH, 1, S5_N),
        log_dt=jnp.repeat(s5_log_dt, S5_STATE, axis=1)[:, None, :],
        b_re=_block_diag(jnp.swapaxes(s5_b_re, 2, 3)), b_im=_block_diag(jnp.swapaxes(s5_b_im, 2, 3)),
        c_blk=c_blk.astype(BF16),
        s5_d=s5_d[:, None, :], glu_w=s5_glu_w.astype(BF16), glu_b=s5_glu_b[:, None, :],
    )


def _trunk(x, pos0, init, prm, w_in, w_out, w_up, w_down, norm_mix, norm_mlp, norm_final, lb_logits,
           angle_row, log_gamma, cfg):
    bsz, seq, _ = x.shape
    cos, sin = _rope_tables(angle_row, pos0, seq, cfg["rope_tl"])
    x2d = x.reshape(bsz * seq, D_MODEL)
    i_ret, i_hg, i_ssd, i_cv, i_re, i_im = init if init is not None else (None,) * 6
    if i_re is not None:
        i_re = i_re.reshape(DEPTH, bsz, S5_N)
        i_im = i_im.reshape(DEPTH, bsz, S5_N)
    s_ret = s_hg = s_ssd = s_cv = s_re = s_im = None
    for l in range(DEPTH):
        proj = _inproj(x2d, norm_mix, w_in, l, cfg["tm_in"]).reshape(bsz, seq, IN_COLS_PAD)
        ya, s_ret = _retention(proj, cos, sin, log_gamma, i_ret, s_ret, l, *cfg["ret"])
        yb, s_hg = _hgrn2(proj, lb_logits, i_hg, s_hg, l, *cfg["hg"])
        yc, s_ssd, s_cv = _ssd(proj, prm, i_cv, i_ssd, s_cv, s_ssd, l, *cfg["ssd"])
        yd, s_re, s_im = _s5(proj, prm, i_re, i_im, s_re, s_im, l, *cfg["s5"])
        ys = [y.reshape(bsz * seq, GROUP_W) for y in (ya, yb, yc, yd)]
        x2d = _outmlp(x2d, ys, w_out, norm_mlp, w_up, w_down, norm_final, l, cfg["tm_out"], cfg["tf"])
    return (x2d.reshape(bsz, seq, D_MODEL), s_ret, s_hg, s_ssd, s_cv,
            s_re.reshape(DEPTH, bsz, S5_GROUPS, S5_STATE), s_im.reshape(DEPTH, bsz, S5_GROUPS, S5_STATE))


PROMPT_CFG = dict(rope_tl=256, tm_in=256, ret=(128, 4, 1), hg=(64, 4, 1), ssd=(128, 4, 1), s5=(128, 8),
                  tm_out=512, tf=1024)
SAMPLE_CFG = dict(rope_tl=8, tm_in=256, ret=(8, 1, 8), hg=(8, 1, 8), ssd=(8, 1, 8), s5=(8, 8),
                  tm_out=512, tf=1024)


def kernel(x_prompt, x_sample, state_ret, state_hgrn, state_ssd, state_ssd_conv, state_s5_re, state_s5_im, norm_mix, w_in, w_out, hg_lb_logits, ssd_conv_w, ssd_conv_b, ssd_dt_bias, ssd_a_log, ssd_d, ssd_norm, s5_a_re, s5_a_im, s5_log_dt, s5_b_re, s5_b_im, s5_c_re, s5_c_im, s5_d, s5_glu_w, s5_glu_b, norm_mlp, w_up, w_down, norm_final):
    prm = _prep(ssd_conv_w, ssd_conv_b, ssd_dt_bias, ssd_a_log, ssd_d, ssd_norm, s5_a_re, s5_a_im, s5_log_dt,
                s5_b_re, s5_b_im, s5_c_re, s5_c_im, s5_d, s5_glu_w, s5_glu_b)
    w_out_b = _cast_bf16(w_out, 512)
    w_up_b = _cast_bf16(w_up, 256)
    w_down_b = _cast_bf16(w_down, 1024)
    angle = 1.0 / (ROPE_BASE ** jnp.linspace(0.0, 1.0, HEAD_W // 2, dtype=F32))
    angle_row = jnp.tile(jnp.repeat(angle, 2), 2)[None, :]
    log_gamma = jnp.log(1.0 - jnp.exp2(-5.0 - jnp.arange(HEADS, dtype=F32)))
    shared = (prm, w_in, w_out_b, w_up_b, w_down_b, norm_mix[:, None, :], norm_mlp[:, None, :],
              norm_final[None, :], hg_lb_logits, angle_row, log_gamma)
    states = (state_ret, state_hgrn, state_ssd, state_ssd_conv, state_s5_re, state_s5_im)
    out_p = _trunk(x_prompt, 0, None, *shared, PROMPT_CFG)
    out_s = _trunk(x_sample, PAST_LEN, states, *shared, SAMPLE_CFG)
    return (out_p[0], out_s[0]) + out_p[1:] + out_s[1:]
```

```python
import functools

import jax
import jax.numpy as jnp
from jax import lax
from jax.experimental import pallas as pl
from jax.experimental.pallas import tpu as pltpu

F32 = jnp.float32
BF16 = jnp.bfloat16

D_MODEL = 1024
DEPTH = 2
PAST_LEN = 16384
GROUP_W = 256
HEADS = 4
HEAD_W = 64
ROPE_BASE = 10000.0
HG_MIN_FORGET = 1e-30
SSD_STATE = 128
SSD_GROUPS = 2
SSD_CONV = 4
SSD_CONV_DIM = 768
S5_GROUPS = 16
S5_STATE = 64
S5_N = S5_GROUPS * S5_STATE
D_FF = 4096
IN_COLS = 3332
EPS = 1e-6

COL_RET = 0
COL_HG = 1024
COL_SSD = 2048
COL_S5 = 3072
COL_DT = 3328
IN_COLS_PAD = 3584

LANES = 128
SUBLANES = 8
VMEM_LIMIT = 56 * 1024 * 1024


def _cparams(sem):
    return pltpu.CompilerParams(dimension_semantics=sem, vmem_limit_bytes=VMEM_LIMIT)


def _mm(a, b):
    return jnp.dot(a.astype(BF16), b.astype(BF16), preferred_element_type=F32)


def _mm_nt(a, b):
    return lax.dot_general(a.astype(BF16), b.astype(BF16), (((1,), (1,)), ((), ())),
                           preferred_element_type=F32)


def _mm_tn(a, b):
    return lax.dot_general(a.astype(BF16), b.astype(BF16), (((0,), (0,)), ((), ())),
                           preferred_element_type=F32)


def _mm_f32(a, b):
    return jnp.dot(a, b, preferred_element_type=F32, precision=lax.Precision.HIGHEST)


def _silu(x):
    return x * jax.nn.sigmoid(x)


def _rms(x, w):
    return x * lax.rsqrt(jnp.mean(x * x, axis=-1, keepdims=True) + EPS) * w


def _tri(c):
    t = lax.broadcasted_iota(jnp.int32, (c, c), 0)
    s = lax.broadcasted_iota(jnp.int32, (c, c), 1)
    return t >= s


def _head_of_lane(rows):
    return lax.broadcasted_iota(jnp.int32, (rows, GROUP_W), 1) // HEAD_W


def _expand_heads(x, head):
    return jnp.concatenate([jnp.where(head == h, x, 0.0) for h in range(HEADS)], axis=0).astype(BF16)


def _block_mask(seg):
    r = lax.broadcasted_iota(jnp.int32, (GROUP_W, GROUP_W), 0) // seg
    c = lax.broadcasted_iota(jnp.int32, (GROUP_W, GROUP_W), 1) // seg
    return r == c


def _seg_mean(x, m):
    hi = x.astype(BF16)
    lo = (x - hi.astype(F32)).astype(BF16)
    return jnp.dot(hi, m, preferred_element_type=F32) + jnp.dot(lo, m, preferred_element_type=F32)


def _layer_spec(l, tail, n_grid, single=False):
    zeros = (0,) * len(tail)
    kw = dict(pipeline_mode=pl.Buffered(1)) if single else {}
    if n_grid == 1:
        return pl.BlockSpec((None,) + tail, lambda i: (l,) + zeros, **kw)
    return pl.BlockSpec((None,) + tail, lambda i, j: (l,) + zeros, **kw)


def _state_spec(l, nb, tail):
    zeros = (0,) * len(tail)
    return pl.BlockSpec((None, nb) + tail, lambda i, j: (l, i) + zeros)


def _state_io(l, nb, tail, bsz, init, prev):
    spec = _state_spec(l, nb, tail)
    in_specs, operands = [], []
    if init is not None:
        in_specs.append(spec)
        operands.append(init)
    alias_pos = None
    if prev is not None:
        alias_pos = len(in_specs)
        in_specs.append(pl.BlockSpec(memory_space=pl.ANY))
        operands.append(prev)
    return in_specs, operands, spec, jax.ShapeDtypeStruct((DEPTH, bsz) + tail, F32), alias_pos


def _cast_kernel(w_ref, o_ref):
    o_ref[...] = w_ref[...].astype(BF16)


def _cast_bf16(w, tr):
    d, r, c = w.shape
    spec = pl.BlockSpec((None, tr, c), lambda l, i: (l, i, 0))
    return pl.pallas_call(
        _cast_kernel, grid=(d, r // tr), in_specs=[spec], out_specs=spec,
        out_shape=jax.ShapeDtypeStruct(w.shape, BF16),
        compiler_params=_cparams(("parallel", "parallel")),
    )(w)


def _inproj_kernel(x_ref, nw_ref, w_ref, o_ref, wb_ref):
    @pl.when(pl.program_id(0) == 0)
    def _():
        rows = 256
        head = _head_of_lane(rows)
        for r in range(0, D_MODEL, rows):
            rs = slice(r, r + rows)
            wb_ref[rs, 0:COL_S5] = w_ref[rs, 0:COL_S5].astype(BF16)
            tail = w_ref[rs, COL_S5:IN_COLS]
            wb_ref[rs, COL_S5:COL_DT] = tail[:, HEADS:HEADS + GROUP_W].astype(BF16)
            dtw = tail[:, HEADS - 1:HEADS]
            for h in range(HEADS - 2, -1, -1):
                dtw = jnp.where(head == h, tail[:, h:h + 1], dtw)
            wb_ref[rs, COL_DT:IN_COLS_PAD] = dtw.astype(BF16)

    h = _rms(x_ref[...], nw_ref[...])
    o_ref[...] = jnp.dot(h.astype(BF16), wb_ref[...], preferred_element_type=F32)


def _inproj(x2d, norm_w, w_in, l, tm):
    t = x2d.shape[0]
    return pl.pallas_call(
        _inproj_kernel,
        grid=(t // tm,),
        in_specs=[pl.BlockSpec((tm, D_MODEL), lambda i: (i, 0)),
                  _layer_spec(l, (1, D_MODEL), 1),
                  _layer_spec(l, (D_MODEL, IN_COLS), 1, single=True)],
        out_specs=pl.BlockSpec((tm, IN_COLS_PAD), lambda i: (i, 0)),
        out_shape=jax.ShapeDtypeStruct((t, IN_COLS_PAD), F32),
        scratch_shapes=[pltpu.VMEM((D_MODEL, IN_COLS_PAD), BF16)],
        compiler_params=_cparams(("arbitrary",)),
    )(x2d, norm_w, w_in)


def _outmlp_kernel(x_ref, ya_ref, yb_ref, yc_ref, yd_ref, wo_ref, nw_ref, wu_ref, wd_ref, nf_ref,
                   o_ref, h_ref, u_ref, *, final_norm, tn_up, tn_down):
    x1 = x_ref[...]
    for g, y_ref in enumerate((ya_ref, yb_ref, yc_ref, yd_ref)):
        x1 = x1 + _mm(y_ref[...], wo_ref[g * GROUP_W:(g + 1) * GROUP_W, :])
    o_ref[...] = x1
    h_ref[...] = _rms(x1, nw_ref[...]).astype(BF16)
    for n in range(0, D_FF, tn_up):
        u = jnp.dot(h_ref[...], wu_ref[:, n:n + tn_up], preferred_element_type=F32)
        u_ref[:, n:n + tn_up] = jnp.square(jnp.maximum(u, 0.0)).astype(BF16)
    for n in range(0, D_MODEL, tn_down):
        o_ref[:, n:n + tn_down] += jnp.dot(u_ref[...], wd_ref[:, n:n + tn_down], preferred_element_type=F32)
    if final_norm:
        o_ref[...] = _rms(o_ref[...], nf_ref[...])


def _outmlp(x2d, ys, w_out, norm_w, w_up, w_down, norm_f, l, tm):
    t = x2d.shape[0]
    row = lambda i: (i, 0)
    return pl.pallas_call(
        functools.partial(_outmlp_kernel, final_norm=(l == DEPTH - 1), tn_up=512, tn_down=256),
        grid=(t // tm,),
        in_specs=[pl.BlockSpec((tm, D_MODEL), row)]
        + [pl.BlockSpec((tm, GROUP_W), row)] * 4
        + [_layer_spec(l, (D_MODEL, D_MODEL), 1, single=True),
           _layer_spec(l, (1, D_MODEL), 1),
           _layer_spec(l, (D_MODEL, D_FF), 1, single=True),
           _layer_spec(l, (D_FF, D_MODEL), 1, single=True),
           pl.BlockSpec((1, D_MODEL), lambda i: (0, 0))],
        out_specs=pl.BlockSpec((tm, D_MODEL), row),
        out_shape=jax.ShapeDtypeStruct((t, D_MODEL), F32),
        scratch_shapes=[pltpu.VMEM((tm, D_MODEL), BF16),
                        pltpu.VMEM((tm, D_FF), BF16)],
        compiler_params=_cparams(("parallel",)),
    )(x2d, *ys, w_out, norm_w, w_up, w_down, norm_f)


def _rope_kernel(angle_ref, cos_ref, sin_ref, *, pos0, tl):
    i = pl.program_id(0)
    pos = (pos0 + i * tl + lax.broadcasted_iota(jnp.int32, (tl, 2 * HEAD_W), 0)).astype(F32)
    theta = pos * angle_ref[...]
    lane = lax.broadcasted_iota(jnp.int32, (tl, 2 * HEAD_W), 1)
    cos_ref[...] = jnp.cos(theta)
    s = jnp.sin(theta)
    sin_ref[...] = jnp.where(lane % 2 == 0, -s, s)


def _rope_tables(angle_row, pos0, seq, tl):
    return pl.pallas_call(
        functools.partial(_rope_kernel, pos0=pos0, tl=tl),
        grid=(seq // tl,),
        in_specs=[pl.BlockSpec((1, 2 * HEAD_W), lambda i: (0, 0))],
        out_specs=[pl.BlockSpec((tl, 2 * HEAD_W), lambda i: (i, 0))] * 2,
        out_shape=[jax.ShapeDtypeStruct((seq, 2 * HEAD_W), F32)] * 2,
        compiler_params=_cparams(("parallel",)),
    )(angle_row)


def _ret_kernel(*refs, c, nchunk, nb, has_init, has_prev):
    lg_ref, p_ref, cos_ref, sin_ref = refs[:4]
    s0_ref = refs[4] if has_init else None
    y_ref, s_ref, sb_ref = refs[4 + has_init + has_prev:]
    j = pl.program_id(1)

    @pl.when(j == 0)
    def _():
        sb_ref[...] = jnp.zeros_like(sb_ref)
        if has_init:
            for b in range(nb):
                for h in range(HEADS):
                    hs = slice(h * HEAD_W, (h + 1) * HEAD_W)
                    sb_ref[b, hs, hs] = s0_ref[b, h]

    def per_head(idx, shape):
        out = jnp.full(shape, lg_ref[HEADS - 1], F32)
        for h in range(HEADS - 2, -1, -1):
            out = jnp.where(idx == h, lg_ref[h], out)
        return out

    head = _head_of_lane(c)
    bd = _block_mask(HEAD_W)
    m_seg = jnp.where(bd, 1.0 / HEAD_W, 0.0).astype(BF16)
    even = lax.broadcasted_iota(jnp.int32, (c, GROUP_W), 1) % 2 == 0
    lg_lane = per_head(head[0:1, :], (1, GROUP_W))
    tcol = lax.broadcasted_iota(jnp.int32, (c, 1), 0).astype(F32)
    e_q = jnp.exp((tcol + 1.0) * lg_lane)
    e_k = jnp.exp((c - 1.0 - tcol) * lg_lane)
    a_c = jnp.exp(c * lg_lane)
    t_i = lax.broadcasted_iota(jnp.int32, (c, HEADS * c), 0)
    j_i = lax.broadcasted_iota(jnp.int32, (c, HEADS * c), 1)
    h_i = j_i // c
    s_i = j_i - h_i * c
    causal = t_i >= s_i
    dist = jnp.where(causal, (t_i - s_i).astype(F32), 0.0)
    decay = jnp.where(causal, jnp.exp(dist * per_head(h_i, (c, HEADS * c))), 0.0)

    def rot(x, cos, sin):
        swapped = jnp.where(even, pltpu.roll(x, GROUP_W - 1, 1), pltpu.roll(x, 1, 1))
        return x * cos + swapped * sin

    for b in range(nb):
        for ci in range(nchunk):
            rows = slice(ci * c, (ci + 1) * c)
            cos = jnp.concatenate([cos_ref[rows, :]] * 2, axis=1)
            sin = jnp.concatenate([sin_ref[rows, :]] * 2, axis=1)
            q = rot(p_ref[b, rows, 0:GROUP_W], cos, sin)
            k = rot(p_ref[b, rows, GROUP_W:2 * GROUP_W], cos, sin) * (HEAD_W ** -0.5)
            v = p_ref[b, rows, 2 * GROUP_W:3 * GROUP_W]
            g = p_ref[b, rows, 3 * GROUP_W:4 * GROUP_W]
            scores = _mm_nt(q, _expand_heads(k, head)) * decay
            s_prev = sb_ref[b]
            o = _mm(scores, _expand_heads(v, head)) + _mm(q * e_q, s_prev)
            sb_ref[b] = a_c * s_prev + jnp.where(bd, _mm_tn(k * e_k, v), 0.0)
            mu = _seg_mean(o, m_seg)
            d = o - mu
            var = _seg_mean(d * d, m_seg)
            y_ref[b, rows, :] = d * lax.rsqrt(var + EPS) * _silu(g)

    @pl.when(j == pl.num_programs(1) - 1)
    def _():
        for b in range(nb):
            for h in range(HEADS):
                hs = slice(h * HEAD_W, (h + 1) * HEAD_W)
                s_ref[b, h] = sb_ref[b, hs, hs]


def _retention(proj, cos, sin, log_gamma, init, prev, l, c, nchunk, nb):
    bsz, seq, _ = proj.shape
    tb = c * nchunk
    st_in, st_ops, st_out, st_shape, alias = _state_io(l, nb, (HEADS, HEAD_W, HEAD_W), bsz, init, prev)
    n_fixed = 4
    return pl.pallas_call(
        functools.partial(_ret_kernel, c=c, nchunk=nchunk, nb=nb, has_init=init is not None,
                          has_prev=prev is not None),
        grid=(bsz // nb, seq // tb),
        in_specs=[pl.BlockSpec(memory_space=pltpu.SMEM),
                  pl.BlockSpec((nb, tb, 4 * GROUP_W), lambda i, j: (i, j, COL_RET // (4 * GROUP_W))),
                  pl.BlockSpec((tb, 2 * HEAD_W), lambda i, j: (j, 0)),
                  pl.BlockSpec((tb, 2 * HEAD_W), lambda i, j: (j, 0))] + st_in,
        out_specs=[pl.BlockSpec((nb, tb, GROUP_W), lambda i, j: (i, j, 0)), st_out],
        out_shape=[jax.ShapeDtypeStruct((bsz, seq, GROUP_W), F32), st_shape],
        scratch_shapes=[pltpu.VMEM((nb, GROUP_W, GROUP_W), F32)],
        input_output_aliases={} if alias is None else {n_fixed + alias: 1},
        compiler_params=_cparams(("parallel", "arbitrary")),
    )(log_gamma, proj, cos, sin, *st_ops)


def _hg_kernel(*refs, c, nchunk, nb, layer, has_init, has_prev):
    p_ref, lb_ref = refs[:2]
    s0_ref = refs[2] if has_init else None
    y_ref, s_ref, st_ref = refs[2 + has_init + has_prev:]
    j = pl.program_id(1)

    @pl.when(j == 0)
    def _():
        st_ref[...] = jnp.zeros_like(st_ref)
        if has_init:
            for b in range(nb):
                for h in range(HEADS):
                    hs = slice(h * HEAD_W, (h + 1) * HEAD_W)
                    st_ref[b, hs, hs] = s0_ref[b, h].T

    logits = lb_ref[...]
    e = jnp.exp(logits - jnp.max(logits, axis=0, keepdims=True))
    sm = e / jnp.sum(e, axis=0, keepdims=True)
    csum = sm[0:1, :]
    for i in range(1, layer + 1):
        csum = csum + sm[i:i + 1, :]
    lower = csum - sm[0:1, :]

    head = _head_of_lane(c)
    bd = _block_mask(HEAD_W)
    m_seg = jnp.where(bd, 1.0 / HEAD_W, 0.0).astype(BF16)
    trif = _tri(c).astype(F32)
    t_i = lax.broadcasted_iota(jnp.int32, (c, HEADS * c), 0)
    j_i = lax.broadcasted_iota(jnp.int32, (c, HEADS * c), 1)
    causal = t_i >= j_i - (j_i // c) * c
    mid = c // 2 - 1

    for b in range(nb):
        for ci in range(nchunk):
            rows = slice(ci * c, (ci + 1) * c)
            qq = _silu(p_ref[b, rows, 0:GROUP_W])
            ff = p_ref[b, rows, GROUP_W:2 * GROUP_W]
            v = p_ref[b, rows, 2 * GROUP_W:3 * GROUP_W]
            g = p_ref[b, rows, 3 * GROUP_W:4 * GROUP_W]
            forget = lower + (1.0 - lower) * jax.nn.sigmoid(ff)
            log_f = jnp.log(jnp.maximum(forget, HG_MIN_FORGET))
            key = (1.0 - lower) * jax.nn.sigmoid(-ff)
            bcum = _mm_f32(trif, log_f)
            b_mid = bcum[mid:mid + 1, :]
            b_last = bcum[c - 1:c, :]
            q_in = qq * jnp.exp(bcum - b_mid)
            k_in = key * jnp.exp(b_mid - bcum)
            q_x = qq * jnp.exp(bcum)
            k_x = key * jnp.exp(b_last - bcum)
            scores = jnp.where(causal, _mm_nt(q_in, _expand_heads(k_in, head)), 0.0)
            st_prev = st_ref[b]
            o = _mm(scores, _expand_heads(v, head)) + _mm_nt(q_x, st_prev)
            st_ref[b] = st_prev * jnp.exp(b_last) + jnp.where(bd, _mm_tn(v, k_x), 0.0)
            y_ref[b, rows, :] = o * lax.rsqrt(_seg_mean(o * o, m_seg) + EPS) * _silu(g)

    @pl.when(j == pl.num_programs(1) - 1)
    def _():
        for b in range(nb):
            for h in range(HEADS):
                hs = slice(h * HEAD_W, (h + 1) * HEAD_W)
                s_ref[b, h] = st_ref[b, hs, hs].T


def _hgrn2(proj, lb_logits, init, prev, l, c, nchunk, nb):
    bsz, seq, _ = proj.shape
    tb = c * nchunk
    st_in, st_ops, st_out, st_shape, alias = _state_io(l, nb, (HEADS, HEAD_W, HEAD_W), bsz, init, prev)
    n_fixed = 2
    return pl.pallas_call(
        functools.partial(_hg_kernel, c=c, nchunk=nchunk, nb=nb, layer=l, has_init=init is not None,
                          has_prev=prev is not None),
        grid=(bsz // nb, seq // tb),
        in_specs=[pl.BlockSpec((nb, tb, 4 * GROUP_W), lambda i, j: (i, j, COL_HG // (4 * GROUP_W))),
                  pl.BlockSpec((DEPTH, GROUP_W), lambda i, j: (0, 0))] + st_in,
        out_specs=[pl.BlockSpec((nb, tb, GROUP_W), lambda i, j: (i, j, 0)), st_out],
        out_shape=[jax.ShapeDtypeStruct((bsz, seq, GROUP_W), F32), st_shape],
        scratch_shapes=[pltpu.VMEM((nb, GROUP_W, GROUP_W), F32)],
        input_output_aliases={} if alias is None else {n_fixed + alias: 1},
        compiler_params=_cparams(("parallel", "arbitrary")),
    )(proj, lb_logits, *st_ops)


def _ssd_kernel(*refs, c, nchunk, nb, has_init, has_prev):
    p_ref, dt_ref, cw_ref, cb_ref, dtb_ref, alog_ref, dsk_ref, nw_ref = refs[:8]
    n_in = 8
    cv0_ref = s0_ref = None
    if has_init:
        cv0_ref, s0_ref = refs[n_in], refs[n_in + 1 + has_prev]
    y_ref, s_ref, cv_ref, xpad_ref = refs[n_in + 2 * (has_init + has_prev):]
    n_tail = SSD_CONV - 1
    lo = SUBLANES - n_tail

    @pl.when(pl.program_id(1) == 0)
    def _():
        if has_init:
            s_ref[...] = s0_ref[...]
            xpad_ref[:, lo:SUBLANES, :] = cv0_ref[...]
        else:
            s_ref[...] = jnp.zeros_like(s_ref)
            xpad_ref[:, lo:SUBLANES, :] = jnp.zeros((nb, n_tail, SSD_CONV_DIM), F32)

    tri = _tri(c)
    trif = tri.astype(F32)
    head = _head_of_lane(c)
    m_grp = jnp.where(_block_mask(LANES), 1.0 / LANES, 0.0).astype(BF16)
    cw = cw_ref[...]
    tb = c * nchunk

    for b in range(nb):
        xpad_ref[b, SUBLANES:SUBLANES + tb, :] = p_ref[b, :, GROUP_W:GROUP_W + SSD_CONV_DIM]
        for ci in range(nchunk):
            r0 = ci * c
            rows = slice(r0, r0 + c)
            z = p_ref[b, rows, 0:GROUP_W]
            conv = cb_ref[...] + cw[n_tail:n_tail + 1, :] * p_ref[b, rows, GROUP_W:GROUP_W + SSD_CONV_DIM]
            for w in range(n_tail):
                conv = conv + cw[w:w + 1, :] * xpad_ref[b, r0 + lo + w:r0 + lo + w + c, :]
            xc = _silu(conv)
            xs = xc[:, 0:GROUP_W]
            bm = xc[:, GROUP_W:2 * GROUP_W]
            cm = xc[:, 2 * GROUP_W:3 * GROUP_W]

            delta = jax.nn.softplus(dt_ref[b, rows, :] + dtb_ref[...])
            log_a = -delta * jnp.exp(alog_ref[...])
            bcum = _mm_f32(trif, log_a)
            b_last = bcum[c - 1:c, :]
            xd = xs * delta
            xr = xd * jnp.exp(b_last - bcum)
            brow = [bcum[:, g * LANES:(g + 1) * LANES].T for g in range(SSD_GROUPS)]
            gram = [_mm_nt(cm[:, g * LANES:(g + 1) * LANES], bm[:, g * LANES:(g + 1) * LANES])
                    for g in range(SSD_GROUPS)]
            w_parts = []
            for h in range(HEADS):
                g, jh = divmod(h, HEADS // SSD_GROUPS)
                b_t = bcum[:, h * HEAD_W:h * HEAD_W + 1]
                b_s = brow[g][jh * HEAD_W:jh * HEAD_W + 1, :]
                w_parts.append(gram[g] * jnp.where(tri, jnp.exp(jnp.where(tri, b_t - b_s, 0.0)), 0.0))
            if c % LANES == 0:
                o = _mm(jnp.concatenate(w_parts, axis=1), _expand_heads(xd, head))
            else:
                o = _mm(w_parts[0], jnp.where(head == 0, xd, 0.0))
                for h in range(1, HEADS):
                    o = o + _mm(w_parts[h], jnp.where(head == h, xd, 0.0))
            cross = jnp.concatenate([_mm_nt(cm[:, g * LANES:(g + 1) * LANES], s_ref[b, g])
                                     for g in range(SSD_GROUPS)], axis=1)
            o = o + jnp.exp(bcum) * cross
            for g in range(SSD_GROUPS):
                gl = slice(g * LANES, (g + 1) * LANES)
                a_col = jnp.exp(brow[g][:, c - 1:c])
                s_ref[b, g] = s_ref[b, g] * a_col + _mm_tn(xr[:, gl], bm[:, gl])
            y = (o + dsk_ref[...] * xs) * _silu(z)
            y_ref[b, rows, :] = y * lax.rsqrt(_seg_mean(y * y, m_grp) + EPS) * nw_ref[...]
        tail = xpad_ref[b, tb + lo:tb + SUBLANES, :]
        cv_ref[b] = tail
        xpad_ref[b, lo:SUBLANES, :] = tail


def _ssd(proj, prm, init_cv, init_s, prev_cv, prev_s, l, c, nchunk, nb):
    bsz, seq, _ = proj.shape
    tb = c * nchunk
    cv_in, cv_ops, cv_out, cv_shape, cv_alias = _state_io(l, nb, (SSD_CONV - 1, SSD_CONV_DIM), bsz,
                                                          init_cv, prev_cv)
    s_in, s_ops, s_out, s_shape, s_alias = _state_io(l, nb, (SSD_GROUPS, LANES, SSD_STATE), bsz,
                                                     init_s, prev_s)
    n_fixed = 8
    aliases = {}
    if cv_alias is not None:
        aliases[n_fixed + cv_alias] = 2
        aliases[n_fixed + len(cv_in) + s_alias] = 1
    return pl.pallas_call(
        functools.partial(_ssd_kernel, c=c, nchunk=nchunk, nb=nb, has_init=init_s is not None,
                          has_prev=prev_s is not None),
        grid=(bsz // nb, seq // tb),
        in_specs=[pl.BlockSpec((nb, tb, 4 * GROUP_W), lambda i, j: (i, j, COL_SSD // (4 * GROUP_W))),
                  pl.BlockSpec((nb, tb, GROUP_W), lambda i, j: (i, j, COL_DT // GROUP_W)),
                  _layer_spec(l, (SSD_CONV, SSD_CONV_DIM), 2),
                  _layer_spec(l, (1, SSD_CONV_DIM), 2),
                  _layer_spec(l, (1, GROUP_W), 2),
                  _layer_spec(l, (1, GROUP_W), 2),
                  _layer_spec(l, (1, GROUP_W), 2),
                  _layer_spec(l, (1, GROUP_W), 2)] + cv_in + s_in,
        out_specs=[pl.BlockSpec((nb, tb, GROUP_W), lambda i, j: (i, j, 0)), s_out, cv_out],
        out_shape=[jax.ShapeDtypeStruct((bsz, seq, GROUP_W), F32), s_shape, cv_shape],
        scratch_shapes=[pltpu.VMEM((nb, tb + SUBLANES, SSD_CONV_DIM), F32)],
        input_output_aliases=aliases,
        compiler_params=_cparams(("parallel", "arbitrary")),
    )(proj, proj, prm["conv_w"], prm["conv_b"], prm["dt_bias"], prm["a_log"], prm["ssd_d"], prm["ssd_norm"],
      *cv_ops, *s_ops)


def _ssd_state_in(s):
    d, bsz = s.shape[:2]
    return jnp.swapaxes(s, 3, 4).reshape(d, bsz, SSD_GROUPS, LANES, SSD_STATE)


def _ssd_state_out(s):
    d, bsz = s.shape[:2]
    return jnp.swapaxes(s.reshape(d, bsz, HEADS, HEAD_W, SSD_STATE), 3, 4)


def _s5_kernel(*refs, tl, unroll, has_init, has_prev):
    (u0_ref, u1_ref, are_ref, aim_ref, ldt_ref, bre_ref, bim_ref, cw_ref, dsk_ref, gw_ref,
     gb_ref) = refs[:11]
    n_in = 11
    s0r_ref = s0i_ref = None
    if has_init:
        s0r_ref, s0i_ref = refs[n_in], refs[n_in + 1 + has_prev]
    (y_ref, sr_ref, si_ref, wb_ref, ab_ref, us0_ref, us1_ref, ut_ref, bu_ref, yt_ref, ys0_ref,
     ys1_ref) = refs[n_in + 2 * (has_init + has_prev):]
    nbt = SUBLANES * tl

    @pl.when(pl.program_id(1) == 0)
    def _():
        if has_init:
            sr_ref[...] = s0r_ref[...]
            si_ref[...] = s0i_ref[...]
        else:
            sr_ref[...] = jnp.zeros_like(sr_ref)
            si_ref[...] = jnp.zeros_like(si_ref)
        lr, li, st = are_ref[...], aim_ref[...], jnp.exp(ldt_ref[...])
        mag = jnp.exp(lr * st)
        abr = mag * jnp.cos(li * st)
        abi = mag * jnp.sin(li * st)
        ab_ref[0:1, :] = abr
        ab_ref[1:2, :] = abi
        den = lr * lr + li * li
        cr = ((abr - 1.0) * lr + abi * li) / den
        ci = (abi * lr - (abr - 1.0) * li) / den
        wb_ref[:, 0:S5_N] = (cr * bre_ref[...] - ci * bim_ref[...]).astype(BF16)
        wb_ref[:, S5_N:2 * S5_N] = (cr * bim_ref[...] + ci * bre_ref[...]).astype(BF16)

    us0_ref[...] = u0_ref[...].reshape(nbt, LANES)
    us1_ref[...] = u1_ref[...].reshape(nbt, LANES)

    def gather(t, carry):
        dst = pl.ds(pl.multiple_of(t * SUBLANES, SUBLANES), SUBLANES)
        src = pl.ds(t, SUBLANES, stride=tl)
        ut_ref[dst, 0:LANES] = us0_ref[src, :]
        ut_ref[dst, LANES:2 * LANES] = us1_ref[src, :]
        return carry

    lax.fori_loop(0, tl, gather, 0, unroll=unroll)
    u = ut_ref[...]
    bu_ref[...] = jnp.dot(u.astype(BF16), wb_ref[...], preferred_element_type=F32)
    ar = ab_ref[0:1, :]
    ai = ab_ref[1:2, :]

    def step(t, carry):
        xr, xi = carry
        rows = pl.ds(pl.multiple_of(t * SUBLANES, SUBLANES), SUBLANES)
        nr = ar * xr - ai * xi + bu_ref[rows, 0:S5_N]
        ni = ar * xi + ai * xr + bu_ref[rows, S5_N:2 * S5_N]
        bu_ref[rows, 0:S5_N] = nr
        bu_ref[rows, S5_N:2 * S5_N] = ni
        return nr, ni

    xr, xi = lax.fori_loop(0, tl, step, (sr_ref[...], si_ref[...]), unroll=unroll)
    sr_ref[...] = xr
    si_ref[...] = xi

    y = _mm(bu_ref[...], cw_ref[...]) + dsk_ref[...] * u
    y = jax.nn.gelu(y)
    yt_ref[...] = y * jax.nn.sigmoid(_mm(y, gw_ref[...]) + gb_ref[...])

    def scatter(t, carry):
        src = pl.ds(pl.multiple_of(t * SUBLANES, SUBLANES), SUBLANES)
        dst = pl.ds(t, SUBLANES, stride=tl)
        ys0_ref[dst, :] = yt_ref[src, 0:LANES]
        ys1_ref[dst, :] = yt_ref[src, LANES:2 * LANES]
        return carry

    lax.fori_loop(0, tl, scatter, 0, unroll=unroll)
    y_ref[:, :, 0:LANES] = ys0_ref[...].reshape(SUBLANES, tl, LANES)
    y_ref[:, :, LANES:2 * LANES] = ys1_ref[...].reshape(SUBLANES, tl, LANES)


def _s5(proj, prm, init_re, init_im, prev_re, prev_im, l, tl, unroll):
    bsz, seq, _ = proj.shape
    re_in, re_ops, re_out, re_shape, re_alias = _state_io(l, SUBLANES, (S5_N,), bsz, init_re, prev_re)
    im_in, im_ops, im_out, im_shape, im_alias = _state_io(l, SUBLANES, (S5_N,), bsz, init_im, prev_im)
    n_fixed = 11
    aliases = {}
    if re_alias is not None:
        aliases[n_fixed + re_alias] = 1
        aliases[n_fixed + len(re_in) + im_alias] = 2
    nbt = SUBLANES * tl
    return pl.pallas_call(
        functools.partial(_s5_kernel, tl=tl, unroll=unroll, has_init=init_re is not None,
                          has_prev=prev_re is not None),
        grid=(bsz // SUBLANES, seq // tl),
        in_specs=[pl.BlockSpec((SUBLANES, tl, LANES), lambda i, j: (i, j, COL_S5 // LANES)),
                  pl.BlockSpec((SUBLANES, tl, LANES), lambda i, j: (i, j, COL_S5 // LANES + 1)),
                  _layer_spec(l, (1, S5_N), 2), _layer_spec(l, (1, S5_N), 2), _layer_spec(l, (1, S5_N), 2),
                  _layer_spec(l, (GROUP_W, S5_N), 2), _layer_spec(l, (GROUP_W, S5_N), 2),
                  _layer_spec(l, (2 * S5_N, GROUP_W), 2),
                  _layer_spec(l, (1, GROUP_W), 2),
                  _layer_spec(l, (GROUP_W, GROUP_W), 2),
                  _layer_spec(l, (1, GROUP_W), 2)] + re_in + im_in,
        out_specs=[pl.BlockSpec((SUBLANES, tl, GROUP_W), lambda i, j: (i, j, 0)), re_out, im_out],
        out_shape=[jax.ShapeDtypeStruct((bsz, seq, GROUP_W), F32), re_shape, im_shape],
        scratch_shapes=[pltpu.VMEM((GROUP_W, 2 * S5_N), BF16),
                        pltpu.VMEM((SUBLANES, S5_N), F32),
                        pltpu.VMEM((nbt, LANES), F32), pltpu.VMEM((nbt, LANES), F32),
                        pltpu.VMEM((nbt, GROUP_W), F32),
                        pltpu.VMEM((nbt, 2 * S5_N), F32),
                        pltpu.VMEM((nbt, GROUP_W), F32),
                        pltpu.VMEM((nbt, LANES), F32), pltpu.VMEM((nbt, LANES), F32)],
        input_output_aliases=aliases,
        compiler_params=_cparams(("parallel", "arbitrary")),
    )(proj, proj, prm["a_re"], prm["a_im"], prm["log_dt"], prm["b_re"], prm["b_im"], prm["c_blk"],
      prm["s5_d"], prm["glu_w"], prm["glu_b"], *re_ops, *im_ops)


def _block_diag(blocks):
    d, g, r, c = blocks.shape
    eye = jnp.eye(g, dtype=blocks.dtype)
    return (eye[None, :, None, :, None] * blocks[:, :, :, None, :]).reshape(d, g * r, g * c)


def _prep(ssd_conv_w, ssd_conv_b, ssd_dt_bias, ssd_a_log, ssd_d, ssd_norm, s5_a_re, s5_a_im, s5_log_dt,
          s5_b_re, s5_b_im, s5_c_re, s5_c_im, s5_d, s5_glu_w, s5_glu_b):
    per_head = lambda v: jnp.repeat(v, HEAD_W, axis=1)[:, None, :]
    c_blk = jnp.concatenate([_block_diag(jnp.swapaxes(s5_c_re, 2, 3)),
                             -_block_diag(jnp.swapaxes(s5_c_im, 2, 3))], axis=1)
    return dict(
        conv_w=ssd_conv_w, conv_b=ssd_conv_b[:, None, :],
        dt_bias=per_head(ssd_dt_bias), a_log=per_head(ssd_a_log), ssd_d=per_head(ssd_d),
        ssd_norm=ssd_norm[:, None, :],
        a_re=s5_a_re.reshape(DEPTH, 1, S5_N), a_im=s5_a_im.reshape(DEPTH, 1, S5_N),
        log_dt=jnp.repeat(s5_log_dt, S5_STATE, axis=1)[:, None, :],
        b_re=_block_diag(jnp.swapaxes(s5_b_re, 2, 3)), b_im=_block_diag(jnp.swapaxes(s5_b_im, 2, 3)),
        c_blk=c_blk.astype(BF16),
        s5_d=s5_d[:, None, :], glu_w=s5_glu_w.astype(BF16), glu_b=s5_glu_b[:, None, :],
    )


def _trunk(x, pos0, init, prm, w_in, w_out, w_up, w_down, norm_mix, norm_mlp, norm_final, lb_logits,
           angle_row, log_gamma, cfg):
    bsz, seq, _ = x.shape
    cos, sin = _rope_tables(angle_row, pos0, seq, cfg["rope_tl"])
    x2d = x.reshape(bsz * seq, D_MODEL)
    i_ret, i_hg, i_ssd, i_cv, i_re, i_im = init if init is not None else (None,) * 6
    if init is not None:
        i_ssd = _ssd_state_in(i_ssd)
        i_re = i_re.reshape(DEPTH, bsz, S5_N)
        i_im = i_im.reshape(DEPTH, bsz, S5_N)
    s_ret = s_hg = s_ssd = s_cv = s_re = s_im = None
    for l in range(DEPTH):
        proj = _inproj(x2d, norm_mix, w_in, l, cfg["tm_in"]).reshape(bsz, seq, IN_COLS_PAD)
        ya, s_ret = _retention(proj, cos, sin, log_gamma, i_ret, s_ret, l, *cfg["ret"])
        yb, s_hg = _hgrn2(proj, lb_logits, i_hg, s_hg, l, *cfg["hg"])
        yc, s_ssd, s_cv = _ssd(proj, prm, i_cv, i_ssd, s_cv, s_ssd, l, *cfg["ssd"])
        yd, s_re, s_im = _s5(proj, prm, i_re, i_im, s_re, s_im, l, *cfg["s5"])
        ys = [y.reshape(bsz * seq, GROUP_W) for y in (ya, yb, yc, yd)]
        x2d = _outmlp(x2d, ys, w_out, norm_mlp, w_up, w_down, norm_final, l, cfg["tm_out"])
    return (x2d.reshape(bsz, seq, D_MODEL), s_ret, s_hg, _ssd_state_out(s_ssd), s_cv,
            s_re.reshape(DEPTH, bsz, S5_GROUPS, S5_STATE), s_im.reshape(DEPTH, bsz, S5_GROUPS, S5_STATE))


PROMPT_CFG = dict(rope_tl=256, tm_in=256, ret=(128, 4, 1), hg=(64, 4, 1), ssd=(128, 4, 1), s5=(128, 8),
                  tm_out=512)
SAMPLE_CFG = dict(rope_tl=8, tm_in=256, ret=(8, 1, 8), hg=(8, 1, 8), ssd=(8, 1, 8), s5=(8, 8),
                  tm_out=512)


def kernel(x_prompt, x_sample, state_ret, state_hgrn, state_ssd, state_ssd_conv, state_s5_re, state_s5_im, norm_mix, w_in, w_out, hg_lb_logits, ssd_conv_w, ssd_conv_b, ssd_dt_bias, ssd_a_log, ssd_d, ssd_norm, s5_a_re, s5_a_im, s5_log_dt, s5_b_re, s5_b_im, s5_c_re, s5_c_im, s5_d, s5_glu_w, s5_glu_b, norm_mlp, w_up, w_down, norm_final):
    prm = _prep(ssd_conv_w, ssd_conv_b, ssd_dt_bias, ssd_a_log, ssd_d, ssd_norm, s5_a_re, s5_a_im, s5_log_dt,
                s5_b_re, s5_b_im, s5_c_re, s5_c_im, s5_d, s5_glu_w, s5_glu_b)
    w_out_b = _cast_bf16(w_out, 512)
    w_up_b = _cast_bf16(w_up, 256)
    w_down_b = _cast_bf16(w_down, 1024)
    angle = 1.0 / (ROPE_BASE ** jnp.linspace(0.0, 1.0, HEAD_W // 2, dtype=F32))
    angle_row = jnp.tile(jnp.repeat(angle, 2), 2)[None, :]
    log_gamma = jnp.log(1.0 - jnp.exp2(-5.0 - jnp.arange(HEADS, dtype=F32)))
    shared = (prm, w_in, w_out_b, w_up_b, w_down_b, norm_mix[:, None, :], norm_mlp[:, None, :],
              norm_final[None, :], hg_lb_logits, angle_row, log_gamma)
    states = (state_ret, state_hgrn, state_ssd, state_ssd_conv, state_s5_re, state_s5_im)
    out_p = _trunk(x_prompt, 0, None, *shared, PROMPT_CFG)
    out_s = _trunk(x_sample, PAST_LEN, states, *shared, SAMPLE_CFG)
    return (out_p[0], out_s[0]) + out_p[1:] + out_s[1:]
```

```python
import functools

import jax
import jax.numpy as jnp
from jax import lax
from jax.experimental import pallas as pl
from jax.experimental.pallas import tpu as pltpu

F32 = jnp.float32
BF16 = jnp.bfloat16

D_MODEL = 1024
DEPTH = 2
PAST_LEN = 16384
GROUP_W = 256
HEADS = 4
HEAD_W = 64
ROPE_BASE = 10000.0
HG_MIN_FORGET = 1e-30
SSD_STATE = 128
SSD_GROUPS = 2
SSD_CONV = 4
SSD_CONV_DIM = 768
S5_GROUPS = 16
S5_STATE = 64
S5_N = S5_GROUPS * S5_STATE
D_FF = 4096
IN_COLS = 3332
EPS = 1e-6

COL_RET = 0
COL_HG = 1024
COL_SSD = 2048
COL_S5 = 3072
COL_DT = 3328
IN_COLS_PAD = 3584

LANES = 128
SUBLANES = 8
VMEM_LIMIT = 56 * 1024 * 1024


def _cparams(sem):
    return pltpu.CompilerParams(dimension_semantics=sem, vmem_limit_bytes=VMEM_LIMIT)


def _mm(a, b):
    return jnp.dot(a.astype(BF16), b.astype(BF16), preferred_element_type=F32)


def _mm_nt(a, b):
    return lax.dot_general(a.astype(BF16), b.astype(BF16), (((1,), (1,)), ((), ())),
                           preferred_element_type=F32)


def _mm_tn(a, b):
    return lax.dot_general(a.astype(BF16), b.astype(BF16), (((0,), (0,)), ((), ())),
                           preferred_element_type=F32)


def _mm_f32(a, b):
    return jnp.dot(a, b, preferred_element_type=F32, precision=lax.Precision.HIGHEST)


def _silu(x):
    return x * jax.nn.sigmoid(x)


def _rms(x, w):
    return x * lax.rsqrt(jnp.mean(x * x, axis=-1, keepdims=True) + EPS) * w


def _tri(c):
    t = lax.broadcasted_iota(jnp.int32, (c, c), 0)
    s = lax.broadcasted_iota(jnp.int32, (c, c), 1)
    return t >= s


def _head_of_lane(rows):
    return lax.broadcasted_iota(jnp.int32, (rows, GROUP_W), 1) // HEAD_W


def _expand_heads(x, head):
    return jnp.concatenate([jnp.where(head == h, x, 0.0) for h in range(HEADS)], axis=0).astype(BF16)


def _block_mask(seg):
    r = lax.broadcasted_iota(jnp.int32, (GROUP_W, GROUP_W), 0) // seg
    c = lax.broadcasted_iota(jnp.int32, (GROUP_W, GROUP_W), 1) // seg
    return r == c


def _seg_mean(x, m):
    hi = x.astype(BF16)
    lo = (x - hi.astype(F32)).astype(BF16)
    return jnp.dot(hi, m, preferred_element_type=F32) + jnp.dot(lo, m, preferred_element_type=F32)


def _row_blocks(nseq, c_seq, nblk):
    if nseq == 1:
        return [(slice(0, 1), slice(k * c_seq, (k + 1) * c_seq), [0]) for k in range(nblk)]
    return [(slice(k * nseq, (k + 1) * nseq), slice(0, c_seq), list(range(k * nseq, (k + 1) * nseq)))
            for k in range(nblk)]


def _seq_grid(rows, c_seq, copies):
    r = lax.broadcasted_iota(jnp.int32, (rows, copies * rows), 0)
    j = lax.broadcasted_iota(jnp.int32, (rows, copies * rows), 1)
    h = j // rows
    r2 = j - h * rows
    causal = r >= r2
    if rows > c_seq:
        causal = jnp.logical_and(r // c_seq == r2 // c_seq, causal)
    return r, r2, h, causal


def _stack_rows(pieces):
    return pieces[0] if len(pieces) == 1 else jnp.concatenate(pieces, axis=0)


def _mixer_grid(bsz, seq, c_seq, nseq, nblk):
    nb = 1 if nseq == 1 else nseq * nblk
    tb = c_seq * nblk if nseq == 1 else c_seq
    assert nseq == 1 or seq == c_seq
    return nb, tb, (bsz // nb, seq // tb)


def _layer_spec(l, tail, n_grid, single=False):
    zeros = (0,) * len(tail)
    kw = dict(pipeline_mode=pl.Buffered(1)) if single else {}
    if n_grid == 1:
        return pl.BlockSpec((None,) + tail, lambda i: (l,) + zeros, **kw)
    return pl.BlockSpec((None,) + tail, lambda i, j: (l,) + zeros, **kw)


def _state_spec(l, nb, tail):
    zeros = (0,) * len(tail)
    return pl.BlockSpec((None, nb) + tail, lambda i, j: (l, i) + zeros)


def _state_io(l, nb, tail, bsz, init, prev):
    spec = _state_spec(l, nb, tail)
    in_specs, operands = [], []
    if init is not None:
        in_specs.append(spec)
        operands.append(init)
    alias_pos = None
    if prev is not None:
        alias_pos = len(in_specs)
        in_specs.append(pl.BlockSpec(memory_space=pl.ANY))
        operands.append(prev)
    return in_specs, operands, spec, jax.ShapeDtypeStruct((DEPTH, bsz) + tail, F32), alias_pos


def _cast_kernel(w_ref, o_ref):
    o_ref[...] = w_ref[...].astype(BF16)


def _cast_bf16(w, tr):
    d, r, c = w.shape
    spec = pl.BlockSpec((None, tr, c), lambda l, i: (l, i, 0))
    return pl.pallas_call(
        _cast_kernel, grid=(d, r // tr), in_specs=[spec], out_specs=spec,
        out_shape=jax.ShapeDtypeStruct(w.shape, BF16),
        compiler_params=_cparams(("parallel", "parallel")),
    )(w)


def _inproj_kernel(x_ref, nw_ref, w_ref, o_ref, wb_ref):
    @pl.when(pl.program_id(0) == 0)
    def _():
        rows = 256
        head = _head_of_lane(rows)
        for r in range(0, D_MODEL, rows):
            rs = slice(r, r + rows)
            wb_ref[rs, 0:COL_S5] = w_ref[rs, 0:COL_S5].astype(BF16)
            tail = w_ref[rs, COL_S5:IN_COLS]
            wb_ref[rs, COL_S5:COL_DT] = tail[:, HEADS:HEADS + GROUP_W].astype(BF16)
            dtw = tail[:, HEADS - 1:HEADS]
            for h in range(HEADS - 2, -1, -1):
                dtw = jnp.where(head == h, tail[:, h:h + 1], dtw)
            wb_ref[rs, COL_DT:IN_COLS_PAD] = dtw.astype(BF16)

    h = _rms(x_ref[...], nw_ref[...])
    o_ref[...] = jnp.dot(h.astype(BF16), wb_ref[...], preferred_element_type=F32)


def _inproj(x2d, norm_w, w_in, l, tm):
    t = x2d.shape[0]
    return pl.pallas_call(
        _inproj_kernel,
        grid=(t // tm,),
        in_specs=[pl.BlockSpec((tm, D_MODEL), lambda i: (i, 0)),
                  _layer_spec(l, (1, D_MODEL), 1),
                  _layer_spec(l, (D_MODEL, IN_COLS), 1, single=True)],
        out_specs=pl.BlockSpec((tm, IN_COLS_PAD), lambda i: (i, 0)),
        out_shape=jax.ShapeDtypeStruct((t, IN_COLS_PAD), F32),
        scratch_shapes=[pltpu.VMEM((D_MODEL, IN_COLS_PAD), BF16)],
        compiler_params=_cparams(("arbitrary",)),
    )(x2d, norm_w, w_in)


def _outmlp_kernel(x_ref, ya_ref, yb_ref, yc_ref, yd_ref, wo_ref, nw_ref, wu_ref, wd_ref, nf_ref,
                   o_ref, h_ref, u_ref, *, final_norm, tn_up, tn_down):
    x1 = x_ref[...]
    for g, y_ref in enumerate((ya_ref, yb_ref, yc_ref, yd_ref)):
        x1 = x1 + _mm(y_ref[...], wo_ref[g * GROUP_W:(g + 1) * GROUP_W, :])
    o_ref[...] = x1
    h_ref[...] = _rms(x1, nw_ref[...]).astype(BF16)
    for n in range(0, D_FF, tn_up):
        u = jnp.dot(h_ref[...], wu_ref[:, n:n + tn_up], preferred_element_type=F32)
        u_ref[:, n:n + tn_up] = jnp.square(jnp.maximum(u, 0.0)).astype(BF16)
    for n in range(0, D_MODEL, tn_down):
        o_ref[:, n:n + tn_down] += jnp.dot(u_ref[...], wd_ref[:, n:n + tn_down], preferred_element_type=F32)
    if final_norm:
        o_ref[...] = _rms(o_ref[...], nf_ref[...])


def _outmlp(x2d, ys, w_out, norm_w, w_up, w_down, norm_f, l, tm):
    t = x2d.shape[0]
    row = lambda i: (i, 0)
    return pl.pallas_call(
        functools.partial(_outmlp_kernel, final_norm=(l == DEPTH - 1), tn_up=512, tn_down=256),
        grid=(t // tm,),
        in_specs=[pl.BlockSpec((tm, D_MODEL), row)]
        + [pl.BlockSpec((tm, GROUP_W), row)] * 4
        + [_layer_spec(l, (D_MODEL, D_MODEL), 1, single=True),
           _layer_spec(l, (1, D_MODEL), 1),
           _layer_spec(l, (D_MODEL, D_FF), 1, single=True),
           _layer_spec(l, (D_FF, D_MODEL), 1, single=True),
           pl.BlockSpec((1, D_MODEL), lambda i: (0, 0))],
        out_specs=pl.BlockSpec((tm, D_MODEL), row),
        out_shape=jax.ShapeDtypeStruct((t, D_MODEL), F32),
        scratch_shapes=[pltpu.VMEM((tm, D_MODEL), BF16),
                        pltpu.VMEM((tm, D_FF), BF16)],
        compiler_params=_cparams(("parallel",)),
    )(x2d, *ys, w_out, norm_w, w_up, w_down, norm_f)


def _rope_kernel(angle_ref, cos_ref, sin_ref, *, pos0, tl, period):
    i = pl.program_id(0)
    row = i * tl + lax.broadcasted_iota(jnp.int32, (tl, 2 * HEAD_W), 0)
    pos = (pos0 + row % period).astype(F32)
    theta = pos * angle_ref[...]
    lane = lax.broadcasted_iota(jnp.int32, (tl, 2 * HEAD_W), 1)
    cos_ref[...] = jnp.cos(theta)
    s = jnp.sin(theta)
    sin_ref[...] = jnp.where(lane % 2 == 0, -s, s)


def _rope_tables(angle_row, pos0, rows, period, tl):
    seq = rows
    return pl.pallas_call(
        functools.partial(_rope_kernel, pos0=pos0, tl=tl, period=period),
        grid=(seq // tl,),
        in_specs=[pl.BlockSpec((1, 2 * HEAD_W), lambda i: (0, 0))],
        out_specs=[pl.BlockSpec((tl, 2 * HEAD_W), lambda i: (i, 0))] * 2,
        out_shape=[jax.ShapeDtypeStruct((seq, 2 * HEAD_W), F32)] * 2,
        compiler_params=_cparams(("parallel",)),
    )(angle_row)


def _ret_kernel(*refs, c_seq, nseq, nblk, has_init, has_prev):
    lg_ref, p_ref, cos_ref, sin_ref = refs[:4]
    s0_ref = refs[4] if has_init else None
    y_ref, s_ref, sb_ref = refs[4 + has_init + has_prev:]
    nb = s_ref.shape[0]
    c = nseq * c_seq
    j = pl.program_id(1)

    @pl.when(j == 0)
    def _():
        sb_ref[...] = jnp.zeros_like(sb_ref)
        if has_init:
            for b in range(nb):
                for h in range(HEADS):
                    hs = slice(h * HEAD_W, (h + 1) * HEAD_W)
                    sb_ref[b, hs, hs] = s0_ref[b, h]

    def per_head(idx, shape):
        out = jnp.full(shape, lg_ref[HEADS - 1], F32)
        for h in range(HEADS - 2, -1, -1):
            out = jnp.where(idx == h, lg_ref[h], out)
        return out

    head = _head_of_lane(c)
    bd = _block_mask(HEAD_W)
    m_seg = jnp.where(bd, 1.0 / HEAD_W, 0.0).astype(BF16)
    even = lax.broadcasted_iota(jnp.int32, (c, GROUP_W), 1) % 2 == 0
    lg_lane = per_head(head[0:1, :], (1, GROUP_W))
    t_loc = (lax.broadcasted_iota(jnp.int32, (c, 1), 0) % c_seq).astype(F32)
    e_q = jnp.exp((t_loc + 1.0) * lg_lane)
    e_k = jnp.exp((c_seq - 1.0 - t_loc) * lg_lane)
    a_c = jnp.exp(c_seq * lg_lane)
    r_i, r2_i, h_i, causal = _seq_grid(c, c_seq, HEADS)
    dist = jnp.where(causal, (r_i - r2_i).astype(F32), 0.0)
    decay = jnp.where(causal, jnp.exp(dist * per_head(h_i, (c, HEADS * c))), 0.0)

    def rot(x, cos, sin):
        swapped = jnp.where(even, pltpu.roll(x, GROUP_W - 1, 1), pltpu.roll(x, 1, 1))
        return x * cos + swapped * sin

    for bs, ts, seqs in _row_blocks(nseq, c_seq, nblk):
        load = lambda lo: p_ref[bs, ts, lo:lo + GROUP_W].reshape(c, GROUP_W)
        crow = ts if nseq == 1 else slice(0, c)
        cos = jnp.concatenate([cos_ref[crow, :]] * 2, axis=1)
        sin = jnp.concatenate([sin_ref[crow, :]] * 2, axis=1)
        q = rot(load(0), cos, sin)
        k = rot(load(GROUP_W), cos, sin) * (HEAD_W ** -0.5)
        v = load(2 * GROUP_W)
        g = load(3 * GROUP_W)
        scores = _mm_nt(q, _expand_heads(k, head)) * decay
        o = _mm(scores, _expand_heads(v, head))
        qe = q * e_q
        ke = k * e_k
        cross = []
        for i, b in enumerate(seqs):
            sl = slice(i * c_seq, (i + 1) * c_seq)
            s_prev = sb_ref[b]
            cross.append(_mm(qe[sl], s_prev))
            sb_ref[b] = a_c * s_prev + jnp.where(bd, _mm_tn(ke[sl], v[sl]), 0.0)
        o = o + _stack_rows(cross)
        mu = _seg_mean(o, m_seg)
        d = o - mu
        var = _seg_mean(d * d, m_seg)
        y_ref[bs, ts, :] = (d * lax.rsqrt(var + EPS) * _silu(g)).reshape(len(seqs), c_seq, GROUP_W)

    @pl.when(j == pl.num_programs(1) - 1)
    def _():
        for b in range(nb):
            for h in range(HEADS):
                hs = slice(h * HEAD_W, (h + 1) * HEAD_W)
                s_ref[b, h] = sb_ref[b, hs, hs]


def _retention(proj, cos, sin, log_gamma, init, prev, l, c_seq, nseq, nblk):
    bsz, seq, _ = proj.shape
    nb, tb, grid = _mixer_grid(bsz, seq, c_seq, nseq, nblk)
    st_in, st_ops, st_out, st_shape, alias = _state_io(l, nb, (HEADS, HEAD_W, HEAD_W), bsz, init, prev)
    n_fixed = 4
    rope_spec = (pl.BlockSpec((tb, 2 * HEAD_W), lambda i, j: (j, 0)) if nseq == 1 else
                 pl.BlockSpec((nseq * c_seq, 2 * HEAD_W), lambda i, j: (0, 0)))
    return pl.pallas_call(
        functools.partial(_ret_kernel, c_seq=c_seq, nseq=nseq, nblk=nblk, has_init=init is not None,
                          has_prev=prev is not None),
        grid=grid,
        in_specs=[pl.BlockSpec(memory_space=pltpu.SMEM),
                  pl.BlockSpec((nb, tb, 4 * GROUP_W), lambda i, j: (i, j, COL_RET // (4 * GROUP_W))),
                  rope_spec, rope_spec] + st_in,
        out_specs=[pl.BlockSpec((nb, tb, GROUP_W), lambda i, j: (i, j, 0)), st_out],
        out_shape=[jax.ShapeDtypeStruct((bsz, seq, GROUP_W), F32), st_shape],
        scratch_shapes=[pltpu.VMEM((nb, GROUP_W, GROUP_W), F32)],
        input_output_aliases={} if alias is None else {n_fixed + alias: 1},
        compiler_params=_cparams(("parallel", "arbitrary")),
    )(log_gamma, proj, cos, sin, *st_ops)


def _hg_kernel(*refs, c_seq, nseq, nblk, layer, has_init, has_prev):
    p_ref, lb_ref = refs[:2]
    s0_ref = refs[2] if has_init else None
    y_ref, s_ref, st_ref = refs[2 + has_init + has_prev:]
    nb = s_ref.shape[0]
    c = nseq * c_seq
    j = pl.program_id(1)

    @pl.when(j == 0)
    def _():
        st_ref[...] = jnp.zeros_like(st_ref)
        if has_init:
            for b in range(nb):
                for h in range(HEADS):
                    hs = slice(h * HEAD_W, (h + 1) * HEAD_W)
                    st_ref[b, hs, hs] = s0_ref[b, h].T

    logits = lb_ref[...]
    e = jnp.exp(logits - jnp.max(logits, axis=0, keepdims=True))
    sm = e / jnp.sum(e, axis=0, keepdims=True)
    csum = sm[0:1, :]
    for i in range(1, layer + 1):
        csum = csum + sm[i:i + 1, :]
    lower = csum - sm[0:1, :]

    head = _head_of_lane(c)
    bd = _block_mask(HEAD_W)
    m_seg = jnp.where(bd, 1.0 / HEAD_W, 0.0).astype(BF16)
    r_i, r2_i, _, tril = _seq_grid(c, c_seq, 1)
    trif = tril.astype(F32)
    later = r2_i > r_i
    if nseq > 1:
        later = jnp.logical_and(r_i // c_seq == r2_i // c_seq, later)
    laterf = later.astype(F32)
    causal = _seq_grid(c, c_seq, HEADS)[3]
    mid = c // 2 - 1

    for bs, ts, seqs in _row_blocks(nseq, c_seq, nblk):
        load = lambda lo: p_ref[bs, ts, lo:lo + GROUP_W].reshape(c, GROUP_W)
        qq = _silu(load(0))
        ff = load(GROUP_W)
        v = load(2 * GROUP_W)
        g = load(3 * GROUP_W)
        forget = lower + (1.0 - lower) * jax.nn.sigmoid(ff)
        log_f = jnp.log(jnp.maximum(forget, HG_MIN_FORGET))
        key = (1.0 - lower) * jax.nn.sigmoid(-ff)
        bcum = _mm_f32(trif, log_f)
        brem = bcum[c - 1:c, :] - bcum if nseq == 1 else _mm_f32(laterf, log_f)
        b_ref = bcum[mid:mid + 1, :] if nseq == 1 else 0.0
        q_in = qq * jnp.exp(bcum - b_ref)
        k_in = key * jnp.exp(b_ref - bcum)
        q_x = qq * jnp.exp(bcum)
        k_x = key * jnp.exp(brem)
        scores = jnp.where(causal, _mm_nt(q_in, _expand_heads(k_in, head)), 0.0)
        o = _mm(scores, _expand_heads(v, head))
        cross = []
        for i, b in enumerate(seqs):
            sl = slice(i * c_seq, (i + 1) * c_seq)
            st_prev = st_ref[b]
            cross.append(_mm_nt(q_x[sl], st_prev))
            a_seq = jnp.exp(bcum[(i + 1) * c_seq - 1:(i + 1) * c_seq, :])
            st_ref[b] = st_prev * a_seq + jnp.where(bd, _mm_tn(v[sl], k_x[sl]), 0.0)
        o = o + _stack_rows(cross)
        y = o * lax.rsqrt(_seg_mean(o * o, m_seg) + EPS) * _silu(g)
        y_ref[bs, ts, :] = y.reshape(len(seqs), c_seq, GROUP_W)

    @pl.when(j == pl.num_programs(1) - 1)
    def _():
        for b in range(nb):
            for h in range(HEADS):
                hs = slice(h * HEAD_W, (h + 1) * HEAD_W)
                s_ref[b, h] = st_ref[b, hs, hs].T


def _hgrn2(proj, lb_logits, init, prev, l, c_seq, nseq, nblk):
    bsz, seq, _ = proj.shape
    nb, tb, grid = _mixer_grid(bsz, seq, c_seq, nseq, nblk)
    st_in, st_ops, st_out, st_shape, alias = _state_io(l, nb, (HEADS, HEAD_W, HEAD_W), bsz, init, prev)
    n_fixed = 2
    return pl.pallas_call(
        functools.partial(_hg_kernel, c_seq=c_seq, nseq=nseq, nblk=nblk, layer=l, has_init=init is not None,
                          has_prev=prev is not None),
        grid=grid,
        in_specs=[pl.BlockSpec((nb, tb, 4 * GROUP_W), lambda i, j: (i, j, COL_HG // (4 * GROUP_W))),
                  pl.BlockSpec((DEPTH, GROUP_W), lambda i, j: (0, 0))] + st_in,
        out_specs=[pl.BlockSpec((nb, tb, GROUP_W), lambda i, j: (i, j, 0)), st_out],
        out_shape=[jax.ShapeDtypeStruct((bsz, seq, GROUP_W), F32), st_shape],
        scratch_shapes=[pltpu.VMEM((nb, GROUP_W, GROUP_W), F32)],
        input_output_aliases={} if alias is None else {n_fixed + alias: 1},
        compiler_params=_cparams(("parallel", "arbitrary")),
    )(proj, lb_logits, *st_ops)


def _ssd_kernel(*refs, c_seq, nseq, nblk, has_init, has_prev):
    p_ref, dt_ref, cw_ref, cb_ref, dtb_ref, alog_ref, dsk_ref, nw_ref = refs[:8]
    n_in = 8
    cv0_ref = s0_ref = None
    if has_init:
        cv0_ref, s0_ref = refs[n_in], refs[n_in + 1 + has_prev]
    y_ref, s_ref, cv_ref, xpad_ref = refs[n_in + 2 * (has_init + has_prev):]
    nb = s_ref.shape[0]
    tb = p_ref.shape[1]
    c = nseq * c_seq
    n_tail = SSD_CONV - 1
    lo = SUBLANES - n_tail

    @pl.when(pl.program_id(1) == 0)
    def _():
        if has_init:
            s_ref[...] = s0_ref[...]
            xpad_ref[:, lo:SUBLANES, :] = cv0_ref[...]
        else:
            s_ref[...] = jnp.zeros_like(s_ref)
            xpad_ref[:, lo:SUBLANES, :] = jnp.zeros((nb, n_tail, SSD_CONV_DIM), F32)

    r_i, r2_i, _, tri = _seq_grid(c, c_seq, 1)
    trif = tri.astype(F32)
    later = r2_i > r_i
    if nseq > 1:
        later = jnp.logical_and(r_i // c_seq == r2_i // c_seq, later)
    laterf = later.astype(F32)
    head = _head_of_lane(c)
    m_grp = jnp.where(_block_mask(LANES), 1.0 / LANES, 0.0).astype(BF16)
    cw = cw_ref[...]
    groups = [slice(g * LANES, (g + 1) * LANES) for g in range(SSD_GROUPS)]

    xpad_ref[:, SUBLANES:SUBLANES + tb, :] = p_ref[:, :, GROUP_W:GROUP_W + SSD_CONV_DIM]
    for bs, ts, seqs in _row_blocks(nseq, c_seq, nblk):
        r0 = ts.start
        load = lambda ref, col, w: ref[bs, ts, col:col + w].reshape(c, w)
        z = load(p_ref, 0, GROUP_W)
        conv = cb_ref[...] + cw[n_tail:n_tail + 1, :] * load(p_ref, GROUP_W, SSD_CONV_DIM)
        for w in range(n_tail):
            win = xpad_ref[bs, r0 + lo + w:r0 + lo + w + c_seq, :]
            conv = conv + cw[w:w + 1, :] * win.reshape(c, SSD_CONV_DIM)
        xc = _silu(conv)
        xs = xc[:, 0:GROUP_W]
        bm = xc[:, GROUP_W:2 * GROUP_W]
        cm = xc[:, 2 * GROUP_W:3 * GROUP_W]

        delta = jax.nn.softplus(load(dt_ref, 0, GROUP_W) + dtb_ref[...])
        log_a = -delta * jnp.exp(alog_ref[...])
        bcum = _mm_f32(trif, log_a)
        brem = bcum[c - 1:c, :] - bcum if nseq == 1 else _mm_f32(laterf, log_a)
        xd = xs * delta
        xr = xd * jnp.exp(brem)
        brow = [bcum[:, gl].T for gl in groups]
        gram = [_mm_nt(cm[:, gl], bm[:, gl]) for gl in groups]
        w_parts = []
        for h in range(HEADS):
            g, jh = divmod(h, HEADS // SSD_GROUPS)
            b_t = bcum[:, h * HEAD_W:h * HEAD_W + 1]
            b_s = brow[g][jh * HEAD_W:jh * HEAD_W + 1, :]
            w_parts.append(gram[g] * jnp.where(tri, jnp.exp(jnp.where(tri, b_t - b_s, 0.0)), 0.0))
        o = _mm(jnp.concatenate(w_parts, axis=1), _expand_heads(xd, head))
        cross = []
        for i, b in enumerate(seqs):
            sl = slice(i * c_seq, (i + 1) * c_seq)
            last = (i + 1) * c_seq - 1
            parts = []
            for g, gl in enumerate(groups):
                s_prev = s_ref[b, g]
                parts.append(_mm_nt(cm[sl, gl], s_prev))
                a_col = jnp.exp(brow[g][:, last:last + 1])
                s_ref[b, g] = s_prev * a_col + _mm_tn(xr[sl, gl], bm[sl, gl])
            cross.append(jnp.concatenate(parts, axis=1))
        o = o + jnp.exp(bcum) * _stack_rows(cross)
        y = (o + dsk_ref[...] * xs) * _silu(z)
        y = y * lax.rsqrt(_seg_mean(y * y, m_grp) + EPS) * nw_ref[...]
        y_ref[bs, ts, :] = y.reshape(len(seqs), c_seq, GROUP_W)
    tail = xpad_ref[:, tb + lo:tb + SUBLANES, :]
    cv_ref[...] = tail
    xpad_ref[:, lo:SUBLANES, :] = tail


def _ssd(proj, prm, init_cv, init_s, prev_cv, prev_s, l, c_seq, nseq, nblk):
    bsz, seq, _ = proj.shape
    nb, tb, grid = _mixer_grid(bsz, seq, c_seq, nseq, nblk)
    cv_in, cv_ops, cv_out, cv_shape, cv_alias = _state_io(l, nb, (SSD_CONV - 1, SSD_CONV_DIM), bsz,
                                                          init_cv, prev_cv)
    s_in, s_ops, s_out, s_shape, s_alias = _state_io(l, nb, (SSD_GROUPS, LANES, SSD_STATE), bsz,
                                                     init_s, prev_s)
    n_fixed = 8
    aliases = {}
    if cv_alias is not None:
        aliases[n_fixed + cv_alias] = 2
        aliases[n_fixed + len(cv_in) + s_alias] = 1
    return pl.pallas_call(
        functools.partial(_ssd_kernel, c_seq=c_seq, nseq=nseq, nblk=nblk, has_init=init_s is not None,
                          has_prev=prev_s is not None),
        grid=grid,
        in_specs=[pl.BlockSpec((nb, tb, 4 * GROUP_W), lambda i, j: (i, j, COL_SSD // (4 * GROUP_W))),
                  pl.BlockSpec((nb, tb, GROUP_W), lambda i, j: (i, j, COL_DT // GROUP_W)),
                  _layer_spec(l, (SSD_CONV, SSD_CONV_DIM), 2),
                  _layer_spec(l, (1, SSD_CONV_DIM), 2),
                  _layer_spec(l, (1, GROUP_W), 2),
                  _layer_spec(l, (1, GROUP_W), 2),
                  _layer_spec(l, (1, GROUP_W), 2),
                  _layer_spec(l, (1, GROUP_W), 2)] + cv_in + s_in,
        out_specs=[pl.BlockSpec((nb, tb, GROUP_W), lambda i, j: (i, j, 0)), s_out, cv_out],
        out_shape=[jax.ShapeDtypeStruct((bsz, seq, GROUP_W), F32), s_shape, cv_shape],
        scratch_shapes=[pltpu.VMEM((nb, tb + SUBLANES, SSD_CONV_DIM), F32)],
        input_output_aliases=aliases,
        compiler_params=_cparams(("parallel", "arbitrary")),
    )(proj, proj, prm["conv_w"], prm["conv_b"], prm["dt_bias"], prm["a_log"], prm["ssd_d"], prm["ssd_norm"],
      *cv_ops, *s_ops)


def _ssd_state_in(s):
    d, bsz = s.shape[:2]
    return jnp.swapaxes(s, 3, 4).reshape(d, bsz, SSD_GROUPS, LANES, SSD_STATE)


def _ssd_state_out(s):
    d, bsz = s.shape[:2]
    return jnp.swapaxes(s.reshape(d, bsz, HEADS, HEAD_W, SSD_STATE), 3, 4)


def _s5_kernel(*refs, tl, unroll, has_init, has_prev):
    (u0_ref, u1_ref, are_ref, aim_ref, ldt_ref, bre_ref, bim_ref, cw_ref, dsk_ref, gw_ref,
     gb_ref) = refs[:11]
    n_in = 11
    s0r_ref = s0i_ref = None
    if has_init:
        s0r_ref, s0i_ref = refs[n_in], refs[n_in + 1 + has_prev]
    (y_ref, sr_ref, si_ref, wb_ref, ab_ref, us0_ref, us1_ref, ut_ref, bu_ref, yt_ref, ys0_ref,
     ys1_ref) = refs[n_in + 2 * (has_init + has_prev):]
    nbt = SUBLANES * tl

    @pl.when(pl.program_id(1) == 0)
    def _():
        if has_init:
            sr_ref[...] = s0r_ref[...]
            si_ref[...] = s0i_ref[...]
        else:
            sr_ref[...] = jnp.zeros_like(sr_ref)
            si_ref[...] = jnp.zeros_like(si_ref)
        lr, li, st = are_ref[...], aim_ref[...], jnp.exp(ldt_ref[...])
        mag = jnp.exp(lr * st)
        abr = mag * jnp.cos(li * st)
        abi = mag * jnp.sin(li * st)
        ab_ref[0:1, :] = abr
        ab_ref[1:2, :] = abi
        den = lr * lr + li * li
        cr = ((abr - 1.0) * lr + abi * li) / den
        ci = (abi * lr - (abr - 1.0) * li) / den
        wb_ref[:, 0:S5_N] = (cr * bre_ref[...] - ci * bim_ref[...]).astype(BF16)
        wb_ref[:, S5_N:2 * S5_N] = (cr * bim_ref[...] + ci * bre_ref[...]).astype(BF16)

    us0_ref[...] = u0_ref[...].reshape(nbt, LANES)
    us1_ref[...] = u1_ref[...].reshape(nbt, LANES)

    def gather(t, carry):
        dst = pl.ds(pl.multiple_of(t * SUBLANES, SUBLANES), SUBLANES)
        src = pl.ds(t, SUBLANES, stride=tl)
        ut_ref[dst, 0:LANES] = us0_ref[src, :]
        ut_ref[dst, LANES:2 * LANES] = us1_ref[src, :]
        return carry

    lax.fori_loop(0, tl, gather, 0, unroll=unroll)
    u = ut_ref[...]
    bu_ref[...] = jnp.dot(u.astype(BF16), wb_ref[...], preferred_element_type=F32)
    ar = ab_ref[0:1, :]
    ai = ab_ref[1:2, :]

    def step(t, carry):
        xr, xi = carry
        rows = pl.ds(pl.multiple_of(t * SUBLANES, SUBLANES), SUBLANES)
        nr = ar * xr - ai * xi + bu_ref[rows, 0:S5_N]
        ni = ar * xi + ai * xr + bu_ref[rows, S5_N:2 * S5_N]
        bu_ref[rows, 0:S5_N] = nr
        bu_ref[rows, S5_N:2 * S5_N] = ni
        return nr, ni

    xr, xi = lax.fori_loop(0, tl, step, (sr_ref[...], si_ref[...]), unroll=unroll)
    sr_ref[...] = xr
    si_ref[...] = xi

    y = _mm(bu_ref[...], cw_ref[...]) + dsk_ref[...] * u
    y = jax.nn.gelu(y)
    yt_ref[...] = y * jax.nn.sigmoid(_mm(y, gw_ref[...]) + gb_ref[...])

    def scatter(t, carry):
        src = pl.ds(pl.multiple_of(t * SUBLANES, SUBLANES), SUBLANES)
        dst = pl.ds(t, SUBLANES, stride=tl)
        ys0_ref[dst, :] = yt_ref[src, 0:LANES]
        ys1_ref[dst, :] = yt_ref[src, LANES:2 * LANES]
        return carry

    lax.fori_loop(0, tl, scatter, 0, unroll=unroll)
    y_ref[:, :, 0:LANES] = ys0_ref[...].reshape(SUBLANES, tl, LANES)
    y_ref[:, :, LANES:2 * LANES] = ys1_ref[...].reshape(SUBLANES, tl, LANES)


def _s5(proj, prm, init_re, init_im, prev_re, prev_im, l, tl, unroll):
    bsz, seq, _ = proj.shape
    re_in, re_ops, re_out, re_shape, re_alias = _state_io(l, SUBLANES, (S5_N,), bsz, init_re, prev_re)
    im_in, im_ops, im_out, im_shape, im_alias = _state_io(l, SUBLANES, (S5_N,), bsz, init_im, prev_im)
    n_fixed = 11
    aliases = {}
    if re_alias is not None:
        aliases[n_fixed + re_alias] = 1
        aliases[n_fixed + len(re_in) + im_alias] = 2
    nbt = SUBLANES * tl
    return pl.pallas_call(
        functools.partial(_s5_kernel, tl=tl, unroll=unroll, has_init=init_re is not None,
                          has_prev=prev_re is not None),
        grid=(bsz // SUBLANES, seq // tl),
        in_specs=[pl.BlockSpec((SUBLANES, tl, LANES), lambda i, j: (i, j, COL_S5 // LANES)),
                  pl.BlockSpec((SUBLANES, tl, LANES), lambda i, j: (i, j, COL_S5 // LANES + 1)),
                  _layer_spec(l, (1, S5_N), 2), _layer_spec(l, (1, S5_N), 2), _layer_spec(l, (1, S5_N), 2),
                  _layer_spec(l, (GROUP_W, S5_N), 2), _layer_spec(l, (GROUP_W, S5_N), 2),
                  _layer_spec(l, (2 * S5_N, GROUP_W), 2),
                  _layer_spec(l, (1, GROUP_W), 2),
                  _layer_spec(l, (GROUP_W, GROUP_W), 2),
                  _layer_spec(l, (1, GROUP_W), 2)] + re_in + im_in,
        out_specs=[pl.BlockSpec((SUBLANES, tl, GROUP_W), lambda i, j: (i, j, 0)), re_out, im_out],
        out_shape=[jax.ShapeDtypeStruct((bsz, seq, GROUP_W), F32), re_shape, im_shape],
        scratch_shapes=[pltpu.VMEM((GROUP_W, 2 * S5_N), BF16),
                        pltpu.VMEM((SUBLANES, S5_N), F32),
                        pltpu.VMEM((nbt, LANES), F32), pltpu.VMEM((nbt, LANES), F32),
                        pltpu.VMEM((nbt, GROUP_W), F32),
                        pltpu.VMEM((nbt, 2 * S5_N), F32),
                        pltpu.VMEM((nbt, GROUP_W), F32),
                        pltpu.VMEM((nbt, LANES), F32), pltpu.VMEM((nbt, LANES), F32)],
        input_output_aliases=aliases,
        compiler_params=_cparams(("parallel", "arbitrary")),
    )(proj, proj, prm["a_re"], prm["a_im"], prm["log_dt"], prm["b_re"], prm["b_im"], prm["c_blk"],
      prm["s5_d"], prm["glu_w"], prm["glu_b"], *re_ops, *im_ops)


def _block_diag(blocks):
    d, g, r, c = blocks.shape
    eye = jnp.eye(g, dtype=blocks.dtype)
    return (eye[None, :, None, :, None] * blocks[:, :, :, None, :]).reshape(d, g * r, g * c)


def _prep(ssd_conv_w, ssd_conv_b, ssd_dt_bias, ssd_a_log, ssd_d, ssd_norm, s5_a_re, s5_a_im, s5_log_dt,
          s5_b_re, s5_b_im, s5_c_re, s5_c_im, s5_d, s5_glu_w, s5_glu_b):
    per_head = lambda v: jnp.repeat(v, HEAD_W, axis=1)[:, None, :]
    c_blk = jnp.concatenate([_block_diag(jnp.swapaxes(s5_c_re, 2, 3)),
                             -_block_diag(jnp.swapaxes(s5_c_im, 2, 3))], axis=1)
    return dict(
        conv_w=ssd_conv_w, conv_b=ssd_conv_b[:, None, :],
        dt_bias=per_head(ssd_dt_bias), a_log=per_head(ssd_a_log), ssd_d=per_head(ssd_d),
        ssd_norm=ssd_norm[:, None, :],
        a_re=s5_a_re.reshape(DEPTH, 1, S5_N), a_im=s5_a_im.reshape(DEPTH, 1, S5_N),
        log_dt=jnp.repeat(s5_log_dt, S5_STATE, axis=1)[:, None, :],
        b_re=_block_diag(jnp.swapaxes(s5_b_re, 2, 3)), b_im=_block_diag(jnp.swapaxes(s5_b_im, 2, 3)),
        c_blk=c_blk.astype(BF16),
        s5_d=s5_d[:, None, :], glu_w=s5_glu_w.astype(BF16), glu_b=s5_glu_b[:, None, :],
    )


def _trunk(x, pos0, init, prm, w_in, w_out, w_up, w_down, norm_mix, norm_mlp, norm_final, lb_logits,
           angle_row, log_gamma, cfg):
    bsz, seq, _ = x.shape
    c_ret, nseq_ret, _ = cfg["ret"]
    cos, sin = _rope_tables(angle_row, pos0, seq if nseq_ret == 1 else nseq_ret * c_ret, seq, cfg["rope_tl"])
    x2d = x.reshape(bsz * seq, D_MODEL)
    i_ret, i_hg, i_ssd, i_cv, i_re, i_im = init if init is not None else (None,) * 6
    if init is not None:
        i_ssd = _ssd_state_in(i_ssd)
        i_re = i_re.reshape(DEPTH, bsz, S5_N)
        i_im = i_im.reshape(DEPTH, bsz, S5_N)
    s_ret = s_hg = s_ssd = s_cv = s_re = s_im = None
    for l in range(DEPTH):
        proj = _inproj(x2d, norm_mix, w_in, l, cfg["tm_in"]).reshape(bsz, seq, IN_COLS_PAD)
        ya, s_ret = _retention(proj, cos, sin, log_gamma, i_ret, s_ret, l, *cfg["ret"])
        yb, s_hg = _hgrn2(proj, lb_logits, i_hg, s_hg, l, *cfg["hg"])
        yc, s_ssd, s_cv = _ssd(proj, prm, i_cv, i_ssd, s_cv, s_ssd, l, *cfg["ssd"])
        yd, s_re, s_im = _s5(proj, prm, i_re, i_im, s_re, s_im, l, *cfg["s5"])
        ys = [y.reshape(bsz * seq, GROUP_W) for y in (ya, yb, yc, yd)]
        x2d = _outmlp(x2d, ys, w_out, norm_mlp, w_up, w_down, norm_final, l, cfg["tm_out"])
    return (x2d.reshape(bsz, seq, D_MODEL), s_ret, s_hg, _ssd_state_out(s_ssd), s_cv,
            s_re.reshape(DEPTH, bsz, S5_GROUPS, S5_STATE), s_im.reshape(DEPTH, bsz, S5_GROUPS, S5_STATE))


PROMPT_CFG = dict(rope_tl=256, tm_in=512, ret=(128, 1, 8), hg=(64, 1, 8), ssd=(128, 1, 8), s5=(128, 8),
                  tm_out=512)
SAMPLE_CFG = dict(rope_tl=128, tm_in=512, ret=(8, 16, 2), hg=(8, 16, 2), ssd=(8, 16, 2), s5=(8, 8),
                  tm_out=512)


def kernel(x_prompt, x_sample, state_ret, state_hgrn, state_ssd, state_ssd_conv, state_s5_re, state_s5_im, norm_mix, w_in, w_out, hg_lb_logits, ssd_conv_w, ssd_conv_b, ssd_dt_bias, ssd_a_log, ssd_d, ssd_norm, s5_a_re, s5_a_im, s5_log_dt, s5_b_re, s5_b_im, s5_c_re, s5_c_im, s5_d, s5_glu_w, s5_glu_b, norm_mlp, w_up, w_down, norm_final):
    prm = _prep(ssd_conv_w, ssd_conv_b, ssd_dt_bias, ssd_a_log, ssd_d, ssd_norm, s5_a_re, s5_a_im, s5_log_dt,
                s5_b_re, s5_b_im, s5_c_re, s5_c_im, s5_d, s5_glu_w, s5_glu_b)
    w_out_b = _cast_bf16(w_out, 512)
    w_up_b = _cast_bf16(w_up, 256)
    w_down_b = _cast_bf16(w_down, 1024)
    angle = 1.0 / (ROPE_BASE ** jnp.linspace(0.0, 1.0, HEAD_W // 2, dtype=F32))
    angle_row = jnp.tile(jnp.repeat(angle, 2), 2)[None, :]
    log_gamma = jnp.log(1.0 - jnp.exp2(-5.0 - jnp.arange(HEADS, dtype=F32)))
    shared = (prm, w_in, w_out_b, w_up_b, w_down_b, norm_mix[:, None, :], norm_mlp[:, None, :],
              norm_final[None, :], hg_lb_logits, angle_row, log_gamma)
    states = (state_ret, state_hgrn, state_ssd, state_ssd_conv, state_s5_re, state_s5_im)
    out_p = _trunk(x_prompt, 0, None, *shared, PROMPT_CFG)
    out_s = _trunk(x_sample, PAST_LEN, states, *shared, SAMPLE_CFG)
    return (out_p[0], out_s[0]) + out_p[1:] + out_s[1:]
```

```python
import functools
import itertools

import jax
import jax.numpy as jnp
from jax import lax
from jax.experimental import pallas as pl
from jax.experimental.pallas import tpu as pltpu

F32 = jnp.float32
BF16 = jnp.bfloat16

D_MODEL = 1024
DEPTH = 2
PAST_LEN = 16384
GROUP_W = 256
HEADS = 4
HEAD_W = 64
PAIR_HEADS = 2
ROPE_BASE = 10000.0
HG_MIN_FORGET = 1e-30
SSD_STATE = 128
SSD_GROUPS = 2
SSD_CONV = 4
SSD_CONV_DIM = 768
S5_GROUPS = 16
S5_STATE = 64
S5_N = S5_GROUPS * S5_STATE
D_FF = 4096
IN_COLS = 3332
EPS = 1e-6

COL_RET = 0
COL_HG = 1024
COL_SSD = 2048
COL_S5 = 3072
COL_DT = 3328
IN_COLS_PAD = 3584

LANES = 128
SUBLANES = 8
VMEM_LIMIT = 56 * 1024 * 1024


def _cparams(sem):
    return pltpu.CompilerParams(dimension_semantics=sem, vmem_limit_bytes=VMEM_LIMIT)


def _mm(a, b):
    return jnp.dot(a.astype(BF16), b.astype(BF16), preferred_element_type=F32)


def _mm_nt(a, b):
    return lax.dot_general(a.astype(BF16), b.astype(BF16), (((1,), (1,)), ((), ())),
                           preferred_element_type=F32)


def _mm_tn(a, b):
    return lax.dot_general(a.astype(BF16), b.astype(BF16), (((0,), (0,)), ((), ())),
                           preferred_element_type=F32)


def _mm_f32(a, b):
    return jnp.dot(a, b, preferred_element_type=F32, precision=lax.Precision.HIGHEST)


def _silu(x):
    return x * jax.nn.sigmoid(x)


def _rms(x, w):
    return x * lax.rsqrt(jnp.mean(x * x, axis=-1, keepdims=True) + EPS) * w


def _tri(c):
    t = lax.broadcasted_iota(jnp.int32, (c, c), 0)
    s = lax.broadcasted_iota(jnp.int32, (c, c), 1)
    return t >= s


def _head_of_lane(rows, width=GROUP_W):
    return lax.broadcasted_iota(jnp.int32, (rows, width), 1) // HEAD_W


def _expand_heads(x, head):
    nheads = x.shape[1] // HEAD_W
    return jnp.concatenate([jnp.where(head == h, x, 0.0) for h in range(nheads)], axis=0).astype(BF16)


def _block_mask(seg, width=GROUP_W):
    r = lax.broadcasted_iota(jnp.int32, (width, width), 0) // seg
    c = lax.broadcasted_iota(jnp.int32, (width, width), 1) // seg
    return r == c


def _seg_mean(x, m):
    hi = x.astype(BF16)
    lo = (x - hi.astype(F32)).astype(BF16)
    return jnp.dot(hi, m, preferred_element_type=F32) + jnp.dot(lo, m, preferred_element_type=F32)


def _row_blocks(nseq, c_seq, nblk):
    if nseq == 1:
        return [(slice(0, 1), slice(k * c_seq, (k + 1) * c_seq), [0]) for k in range(nblk)]
    return [(slice(k * nseq, (k + 1) * nseq), slice(0, c_seq), list(range(k * nseq, (k + 1) * nseq)))
            for k in range(nblk)]


def _seq_grid(rows, c_seq, copies):
    r = lax.broadcasted_iota(jnp.int32, (rows, copies * rows), 0)
    j = lax.broadcasted_iota(jnp.int32, (rows, copies * rows), 1)
    h = j // rows
    r2 = j - h * rows
    causal = r >= r2
    if rows > c_seq:
        causal = jnp.logical_and(r // c_seq == r2 // c_seq, causal)
    return r, r2, h, causal


def _stack_rows(pieces):
    return pieces[0] if len(pieces) == 1 else jnp.concatenate(pieces, axis=0)


def _mixer_grid(bsz, seq, c_seq, nseq, nblk):
    nb = 1 if nseq == 1 else nseq * nblk
    tb = c_seq * nblk if nseq == 1 else c_seq
    assert nseq == 1 or seq == c_seq
    return nb, tb, (bsz // nb, seq // tb)


def _layer_spec(l, tail, n_grid, single=False):
    zeros = (0,) * len(tail)
    kw = dict(pipeline_mode=pl.Buffered(1)) if single else {}
    if n_grid == 1:
        return pl.BlockSpec((None,) + tail, lambda i: (l,) + zeros, **kw)
    return pl.BlockSpec((None,) + tail, lambda i, j: (l,) + zeros, **kw)


def _state_spec(l, nb, tail):
    zeros = (0,) * len(tail)
    return pl.BlockSpec((None, nb) + tail, lambda i, j: (l, i) + zeros)


def _state_io(l, nb, tail, bsz, init, prev):
    spec = _state_spec(l, nb, tail)
    in_specs, operands = [], []
    if init is not None:
        in_specs.append(spec)
        operands.append(init)
    alias_pos = None
    if prev is not None:
        alias_pos = len(in_specs)
        in_specs.append(pl.BlockSpec(memory_space=pl.ANY))
        operands.append(prev)
    return in_specs, operands, spec, jax.ShapeDtypeStruct((DEPTH, bsz) + tail, F32), alias_pos


def _cast_kernel(w_ref, o_ref):
    o_ref[...] = w_ref[...].astype(BF16)


def _cast_bf16(w, tr):
    d, r, c = w.shape
    spec = pl.BlockSpec((None, tr, c), lambda l, i: (l, i, 0))
    return pl.pallas_call(
        _cast_kernel, grid=(d, r // tr), in_specs=[spec], out_specs=spec,
        out_shape=jax.ShapeDtypeStruct(w.shape, BF16),
        compiler_params=_cparams(("parallel", "parallel")),
    )(w)


def _inproj_kernel(x_ref, nw_ref, w_ref, o_ref, wb_ref):
    @pl.when(pl.program_id(0) == 0)
    def _():
        rows = 256
        head = _head_of_lane(rows)
        for r in range(0, D_MODEL, rows):
            rs = slice(r, r + rows)
            wb_ref[rs, 0:COL_S5] = w_ref[rs, 0:COL_S5].astype(BF16)
            tail = w_ref[rs, COL_S5:IN_COLS]
            wb_ref[rs, COL_S5:COL_DT] = tail[:, HEADS:HEADS + GROUP_W].astype(BF16)
            dtw = tail[:, HEADS - 1:HEADS]
            for h in range(HEADS - 2, -1, -1):
                dtw = jnp.where(head == h, tail[:, h:h + 1], dtw)
            wb_ref[rs, COL_DT:IN_COLS_PAD] = dtw.astype(BF16)

    h = _rms(x_ref[...], nw_ref[...])
    o_ref[...] = jnp.dot(h.astype(BF16), wb_ref[...], preferred_element_type=F32)


def _inproj(x2d, norm_w, w_in, l, tm):
    t = x2d.shape[0]
    return pl.pallas_call(
        _inproj_kernel,
        grid=(t // tm,),
        in_specs=[pl.BlockSpec((tm, D_MODEL), lambda i: (i, 0)),
                  _layer_spec(l, (1, D_MODEL), 1),
                  _layer_spec(l, (D_MODEL, IN_COLS), 1, single=True)],
        out_specs=pl.BlockSpec((tm, IN_COLS_PAD), lambda i: (i, 0)),
        out_shape=jax.ShapeDtypeStruct((t, IN_COLS_PAD), F32),
        scratch_shapes=[pltpu.VMEM((D_MODEL, IN_COLS_PAD), BF16)],
        compiler_params=_cparams(("arbitrary",)),
    )(x2d, norm_w, w_in)


def _outmlp_kernel(x_ref, ya_ref, yb_ref, yc_ref, yd_ref, wo_ref, nw_ref, wu_ref, wd_ref, nf_ref,
                   o_ref, h_ref, u_ref, *, final_norm, tn_up, tn_down):
    x1 = x_ref[...]
    for g, y_ref in enumerate((ya_ref, yb_ref, yc_ref, yd_ref)):
        x1 = x1 + _mm(y_ref[...], wo_ref[g * GROUP_W:(g + 1) * GROUP_W, :])
    o_ref[...] = x1
    h_ref[...] = _rms(x1, nw_ref[...]).astype(BF16)
    for n in range(0, D_FF, tn_up):
        u = jnp.dot(h_ref[...], wu_ref[:, n:n + tn_up], preferred_element_type=F32)
        u_ref[:, n:n + tn_up] = jnp.square(jnp.maximum(u, 0.0)).astype(BF16)
    for n in range(0, D_MODEL, tn_down):
        o_ref[:, n:n + tn_down] += jnp.dot(u_ref[...], wd_ref[:, n:n + tn_down], preferred_element_type=F32)
    if final_norm:
        o_ref[...] = _rms(o_ref[...], nf_ref[...])


def _outmlp(x2d, ys, w_out, norm_w, w_up, w_down, norm_f, l, tm):
    t = x2d.shape[0]
    row = lambda i: (i, 0)
    return pl.pallas_call(
        functools.partial(_outmlp_kernel, final_norm=(l == DEPTH - 1), tn_up=512, tn_down=256),
        grid=(t // tm,),
        in_specs=[pl.BlockSpec((tm, D_MODEL), row)]
        + [pl.BlockSpec((tm, GROUP_W), row)] * 4
        + [_layer_spec(l, (D_MODEL, D_MODEL), 1, single=True),
           _layer_spec(l, (1, D_MODEL), 1),
           _layer_spec(l, (D_MODEL, D_FF), 1, single=True),
           _layer_spec(l, (D_FF, D_MODEL), 1, single=True),
           pl.BlockSpec((1, D_MODEL), lambda i: (0, 0))],
        out_specs=pl.BlockSpec((tm, D_MODEL), row),
        out_shape=jax.ShapeDtypeStruct((t, D_MODEL), F32),
        scratch_shapes=[pltpu.VMEM((tm, D_MODEL), BF16),
                        pltpu.VMEM((tm, D_FF), BF16)],
        compiler_params=_cparams(("parallel",)),
    )(x2d, *ys, w_out, norm_w, w_up, w_down, norm_f)


def _rope_kernel(angle_ref, cos_ref, sin_ref, *, pos0, tl, period):
    i = pl.program_id(0)
    row = i * tl + lax.broadcasted_iota(jnp.int32, (tl, 2 * HEAD_W), 0)
    pos = (pos0 + row % period).astype(F32)
    theta = pos * angle_ref[...]
    lane = lax.broadcasted_iota(jnp.int32, (tl, 2 * HEAD_W), 1)
    cos_ref[...] = jnp.cos(theta)
    s = jnp.sin(theta)
    sin_ref[...] = jnp.where(lane % 2 == 0, -s, s)


def _rope_tables(angle_row, pos0, rows, period, tl):
    seq = rows
    return pl.pallas_call(
        functools.partial(_rope_kernel, pos0=pos0, tl=tl, period=period),
        grid=(seq // tl,),
        in_specs=[pl.BlockSpec((1, 2 * HEAD_W), lambda i: (0, 0))],
        out_specs=[pl.BlockSpec((tl, 2 * HEAD_W), lambda i: (i, 0))] * 2,
        out_shape=[jax.ShapeDtypeStruct((seq, 2 * HEAD_W), F32)] * 2,
        compiler_params=_cparams(("parallel",)),
    )(angle_row)


def _ret_kernel(*refs, c_seq, nseq, nblk, has_init, has_prev):
    lg_ref, p_ref, cos_ref, sin_ref = refs[:4]
    s0_ref = refs[4] if has_init else None
    y_ref, s_ref, sb_ref = refs[4 + has_init + has_prev:]
    nb = s_ref.shape[0]
    c = nseq * c_seq
    j = pl.program_id(1)

    @pl.when(j == 0)
    def _():
        sb_ref[...] = jnp.zeros_like(sb_ref)
        if has_init:
            for b in range(nb):
                for h in range(HEADS):
                    hs = slice(h * HEAD_W, (h + 1) * HEAD_W)
                    sb_ref[b, hs, hs] = s0_ref[b, h]

    def per_head(idx, shape):
        out = jnp.full(shape, lg_ref[HEADS - 1], F32)
        for h in range(HEADS - 2, -1, -1):
            out = jnp.where(idx == h, lg_ref[h], out)
        return out

    head = _head_of_lane(c)
    bd = _block_mask(HEAD_W)
    m_seg = jnp.where(bd, 1.0 / HEAD_W, 0.0).astype(BF16)
    even = lax.broadcasted_iota(jnp.int32, (c, GROUP_W), 1) % 2 == 0
    lg_lane = per_head(head[0:1, :], (1, GROUP_W))
    t_loc = (lax.broadcasted_iota(jnp.int32, (c, 1), 0) % c_seq).astype(F32)
    e_q = jnp.exp((t_loc + 1.0) * lg_lane)
    e_k = jnp.exp((c_seq - 1.0 - t_loc) * lg_lane)
    a_c = jnp.exp(c_seq * lg_lane)
    r_i, r2_i, h_i, causal = _seq_grid(c, c_seq, HEADS)
    dist = jnp.where(causal, (r_i - r2_i).astype(F32), 0.0)
    decay = jnp.where(causal, jnp.exp(dist * per_head(h_i, (c, HEADS * c))), 0.0)

    def rot(x, cos, sin):
        swapped = jnp.where(even, pltpu.roll(x, GROUP_W - 1, 1), pltpu.roll(x, 1, 1))
        return x * cos + swapped * sin

    for bs, ts, seqs in _row_blocks(nseq, c_seq, nblk):
        load = lambda lo: p_ref[bs, ts, lo:lo + GROUP_W].reshape(c, GROUP_W)
        crow = ts if nseq == 1 else slice(0, c)
        cos = jnp.concatenate([cos_ref[crow, :]] * 2, axis=1)
        sin = jnp.concatenate([sin_ref[crow, :]] * 2, axis=1)
        q = rot(load(0), cos, sin)
        k = rot(load(GROUP_W), cos, sin) * (HEAD_W ** -0.5)
        v = load(2 * GROUP_W)
        g = load(3 * GROUP_W)
        scores = _mm_nt(q, _expand_heads(k, head)) * decay
        o = _mm(scores, _expand_heads(v, head))
        qe = q * e_q
        ke = k * e_k
        cross = []
        for i, b in enumerate(seqs):
            sl = slice(i * c_seq, (i + 1) * c_seq)
            s_prev = sb_ref[b]
            cross.append(_mm(qe[sl], s_prev))
            sb_ref[b] = a_c * s_prev + jnp.where(bd, _mm_tn(ke[sl], v[sl]), 0.0)
        o = o + _stack_rows(cross)
        mu = _seg_mean(o, m_seg)
        d = o - mu
        var = _seg_mean(d * d, m_seg)
        y_ref[bs, ts, :] = (d * lax.rsqrt(var + EPS) * _silu(g)).reshape(len(seqs), c_seq, GROUP_W)

    @pl.when(j == pl.num_programs(1) - 1)
    def _():
        for b in range(nb):
            for h in range(HEADS):
                hs = slice(h * HEAD_W, (h + 1) * HEAD_W)
                s_ref[b, h] = sb_ref[b, hs, hs]


def _retention(proj, cos, sin, log_gamma, init, prev, l, c_seq, nseq, nblk):
    bsz, seq, _ = proj.shape
    nb, tb, grid = _mixer_grid(bsz, seq, c_seq, nseq, nblk)
    st_in, st_ops, st_out, st_shape, alias = _state_io(l, nb, (HEADS, HEAD_W, HEAD_W), bsz, init, prev)
    n_fixed = 4
    rope_spec = (pl.BlockSpec((tb, 2 * HEAD_W), lambda i, j: (j, 0)) if nseq == 1 else
                 pl.BlockSpec((nseq * c_seq, 2 * HEAD_W), lambda i, j: (0, 0)))
    return pl.pallas_call(
        functools.partial(_ret_kernel, c_seq=c_seq, nseq=nseq, nblk=nblk, has_init=init is not None,
                          has_prev=prev is not None),
        grid=grid,
        in_specs=[pl.BlockSpec(memory_space=pltpu.SMEM),
                  pl.BlockSpec((nb, tb, 4 * GROUP_W), lambda i, j: (i, j, COL_RET // (4 * GROUP_W))),
                  rope_spec, rope_spec] + st_in,
        out_specs=[pl.BlockSpec((nb, tb, GROUP_W), lambda i, j: (i, j, 0)), st_out],
        out_shape=[jax.ShapeDtypeStruct((bsz, seq, GROUP_W), F32), st_shape],
        scratch_shapes=[pltpu.VMEM((nb, GROUP_W, GROUP_W), F32)],
        input_output_aliases={} if alias is None else {n_fixed + alias: 1},
        compiler_params=_cparams(("parallel", "arbitrary")),
    )(log_gamma, proj, cos, sin, *st_ops)


def _hg_kernel(*refs, c_seq, nseq, nblk, layer, has_init, has_prev):
    p_ref, lb_ref = refs[:2]
    s0_ref = refs[2] if has_init else None
    y_ref, s_ref, st_ref = refs[2 + has_init + has_prev:]
    nb = s_ref.shape[0]
    c = nseq * c_seq
    j = pl.program_id(1)

    @pl.when(j == 0)
    def _():
        st_ref[...] = jnp.zeros_like(st_ref)
        if has_init:
            for b in range(nb):
                for h in range(HEADS):
                    hs = slice(h * HEAD_W, (h + 1) * HEAD_W)
                    st_ref[b, hs, hs] = s0_ref[b, h].T

    logits = lb_ref[...]
    e = jnp.exp(logits - jnp.max(logits, axis=0, keepdims=True))
    sm = e / jnp.sum(e, axis=0, keepdims=True)
    csum = sm[0:1, :]
    for i in range(1, layer + 1):
        csum = csum + sm[i:i + 1, :]
    lower = csum - sm[0:1, :]

    head = _head_of_lane(c)
    bd = _block_mask(HEAD_W)
    m_seg = jnp.where(bd, 1.0 / HEAD_W, 0.0).astype(BF16)
    r_i, r2_i, _, tril = _seq_grid(c, c_seq, 1)
    trif = tril.astype(F32)
    later = r2_i > r_i
    if nseq > 1:
        later = jnp.logical_and(r_i // c_seq == r2_i // c_seq, later)
    laterf = later.astype(F32)
    causal = _seq_grid(c, c_seq, HEADS)[3]
    mid = c // 2 - 1

    staged = []
    for bs, ts, seqs in _row_blocks(nseq, c_seq, nblk):
        load = lambda lo: p_ref[bs, ts, lo:lo + GROUP_W].reshape(c, GROUP_W)
        qq = _silu(load(0))
        ff = load(GROUP_W)
        v = load(2 * GROUP_W)
        g = load(3 * GROUP_W)
        sig = jax.nn.sigmoid(ff)
        forget = lower + (1.0 - lower) * sig
        log_f = jnp.log(jnp.maximum(forget, HG_MIN_FORGET))
        key = (1.0 - lower) * (1.0 - sig)
        bcum = _mm_f32(trif, log_f)
        if nseq == 1:
            b_ref = bcum[mid:mid + 1, :]
            q_in = qq * jnp.exp(bcum - b_ref)
            k_in = key * jnp.exp(b_ref - bcum)
            q_x = q_in * jnp.exp(b_ref)
            k_x = k_in * jnp.exp(bcum[c - 1:c, :] - b_ref)
        else:
            q_in = q_x = qq * jnp.exp(bcum)
            k_in = key * jnp.exp(-bcum)
            k_x = key * jnp.exp(_mm_f32(laterf, log_f))
        scores = jnp.where(causal, _mm_nt(q_in, _expand_heads(k_in, head)), 0.0)
        o = _mm(scores, _expand_heads(v, head))
        upd = []
        for i in range(len(seqs)):
            sl = slice(i * c_seq, (i + 1) * c_seq)
            a_seq = jnp.exp(bcum[(i + 1) * c_seq - 1:(i + 1) * c_seq, :])
            upd.append((q_x[sl], a_seq, jnp.where(bd, _mm_tn(v[sl], k_x[sl]), 0.0)))
        staged.append((bs, ts, seqs, o, upd, _silu(g)))

    for bs, ts, seqs, o, upd, gate in staged:
        cross = []
        for b, (q_b, a_seq, ds) in zip(seqs, upd):
            st_prev = st_ref[b]
            cross.append(_mm_nt(q_b, st_prev))
            st_ref[b] = st_prev * a_seq + ds
        o = o + _stack_rows(cross)
        y = o * lax.rsqrt(_seg_mean(o * o, m_seg) + EPS) * gate
        y_ref[bs, ts, :] = y.reshape(len(seqs), c_seq, GROUP_W)

    @pl.when(j == pl.num_programs(1) - 1)
    def _():
        for b in range(nb):
            for h in range(HEADS):
                hs = slice(h * HEAD_W, (h + 1) * HEAD_W)
                s_ref[b, h] = st_ref[b, hs, hs].T


def _hgrn2(proj, lb_logits, init, prev, l, c_seq, nseq, nblk):
    bsz, seq, _ = proj.shape
    nb, tb, grid = _mixer_grid(bsz, seq, c_seq, nseq, nblk)
    st_in, st_ops, st_out, st_shape, alias = _state_io(l, nb, (HEADS, HEAD_W, HEAD_W), bsz, init, prev)
    n_fixed = 2
    return pl.pallas_call(
        functools.partial(_hg_kernel, c_seq=c_seq, nseq=nseq, nblk=nblk, layer=l, has_init=init is not None,
                          has_prev=prev is not None),
        grid=grid,
        in_specs=[pl.BlockSpec((nb, tb, 4 * GROUP_W), lambda i, j: (i, j, COL_HG // (4 * GROUP_W))),
                  pl.BlockSpec((DEPTH, GROUP_W), lambda i, j: (0, 0))] + st_in,
        out_specs=[pl.BlockSpec((nb, tb, GROUP_W), lambda i, j: (i, j, 0)), st_out],
        out_shape=[jax.ShapeDtypeStruct((bsz, seq, GROUP_W), F32), st_shape],
        scratch_shapes=[pltpu.VMEM((nb, GROUP_W, GROUP_W), F32)],
        input_output_aliases={} if alias is None else {n_fixed + alias: 1},
        compiler_params=_cparams(("parallel", "arbitrary")),
    )(proj, lb_logits, *st_ops)


def _ssd_kernel(*refs, c_seq, nseq, nblk, has_init, has_prev):
    p_ref, dt_ref, cw_ref, cb_ref, dtb_ref, alog_ref, dsk_ref, nw_ref = refs[:8]
    n_in = 8
    cv0_ref = s0_ref = None
    if has_init:
        cv0_ref, s0_ref = refs[n_in], refs[n_in + 1 + has_prev]
    y_ref, s_ref, cv_ref, xpad_ref = refs[n_in + 2 * (has_init + has_prev):]
    nb = s_ref.shape[0]
    tb = p_ref.shape[1]
    c = nseq * c_seq
    n_tail = SSD_CONV - 1
    lo = SUBLANES - n_tail

    @pl.when(pl.program_id(1) == 0)
    def _():
        if has_init:
            s_ref[...] = s0_ref[...]
            xpad_ref[:, lo:SUBLANES, :] = cv0_ref[...]
        else:
            s_ref[...] = jnp.zeros_like(s_ref)
            xpad_ref[:, lo:SUBLANES, :] = jnp.zeros((nb, n_tail, SSD_CONV_DIM), F32)

    r_i, r2_i, _, tri = _seq_grid(c, c_seq, 1)
    trif = tri.astype(F32)
    later = r2_i > r_i
    if nseq > 1:
        later = jnp.logical_and(r_i // c_seq == r2_i // c_seq, later)
    laterf = later.astype(F32)
    head = _head_of_lane(c)
    m_grp = jnp.where(_block_mask(LANES), 1.0 / LANES, 0.0).astype(BF16)
    cw = cw_ref[...]
    groups = [slice(g * LANES, (g + 1) * LANES) for g in range(SSD_GROUPS)]

    xpad_ref[:, SUBLANES:SUBLANES + tb, :] = p_ref[:, :, GROUP_W:GROUP_W + SSD_CONV_DIM]
    for bs, ts, seqs in _row_blocks(nseq, c_seq, nblk):
        r0 = ts.start
        load = lambda ref, col, w: ref[bs, ts, col:col + w].reshape(c, w)
        z = load(p_ref, 0, GROUP_W)
        conv = cb_ref[...] + cw[n_tail:n_tail + 1, :] * load(p_ref, GROUP_W, SSD_CONV_DIM)
        for w in range(n_tail):
            win = xpad_ref[bs, r0 + lo + w:r0 + lo + w + c_seq, :]
            conv = conv + cw[w:w + 1, :] * win.reshape(c, SSD_CONV_DIM)
        xc = _silu(conv)
        xs = xc[:, 0:GROUP_W]
        bm = xc[:, GROUP_W:2 * GROUP_W]
        cm = xc[:, 2 * GROUP_W:3 * GROUP_W]

        delta = jax.nn.softplus(load(dt_ref, 0, GROUP_W) + dtb_ref[...])
        log_a = -delta * jnp.exp(alog_ref[...])
        bcum = _mm_f32(trif, log_a)
        brem = bcum[c - 1:c, :] - bcum if nseq == 1 else _mm_f32(laterf, log_a)
        xd = xs * delta
        xr = xd * jnp.exp(brem)
        brow = [bcum[:, gl].T for gl in groups]
        gram = [_mm_nt(cm[:, gl], bm[:, gl]) for gl in groups]
        w_parts = []
        for h in range(HEADS):
            g, jh = divmod(h, HEADS // SSD_GROUPS)
            b_t = bcum[:, h * HEAD_W:h * HEAD_W + 1]
            b_s = brow[g][jh * HEAD_W:jh * HEAD_W + 1, :]
            w_parts.append(gram[g] * jnp.where(tri, jnp.exp(jnp.where(tri, b_t - b_s, 0.0)), 0.0))
        o = _mm(jnp.concatenate(w_parts, axis=1), _expand_heads(xd, head))
        cross = []
        for i, b in enumerate(seqs):
            sl = slice(i * c_seq, (i + 1) * c_seq)
            last = (i + 1) * c_seq - 1
            parts = []
            for g, gl in enumerate(groups):
                s_prev = s_ref[b, g]
                parts.append(_mm_nt(cm[sl, gl], s_prev))
                a_col = jnp.exp(brow[g][:, last:last + 1])
                s_ref[b, g] = s_prev * a_col + _mm_tn(xr[sl, gl], bm[sl, gl])
            cross.append(jnp.concatenate(parts, axis=1))
        o = o + jnp.exp(bcum) * _stack_rows(cross)
        y = (o + dsk_ref[...] * xs) * _silu(z)
        y = y * lax.rsqrt(_seg_mean(y * y, m_grp) + EPS) * nw_ref[...]
        y_ref[bs, ts, :] = y.reshape(len(seqs), c_seq, GROUP_W)
    tail = xpad_ref[:, tb + lo:tb + SUBLANES, :]
    cv_ref[...] = tail
    xpad_ref[:, lo:SUBLANES, :] = tail


def _ssd(proj, prm, init_cv, init_s, prev_cv, prev_s, l, c_seq, nseq, nblk):
    bsz, seq, _ = proj.shape
    nb, tb, grid = _mixer_grid(bsz, seq, c_seq, nseq, nblk)
    cv_in, cv_ops, cv_out, cv_shape, cv_alias = _state_io(l, nb, (SSD_CONV - 1, SSD_CONV_DIM), bsz,
                                                          init_cv, prev_cv)
    s_in, s_ops, s_out, s_shape, s_alias = _state_io(l, nb, (SSD_GROUPS, LANES, SSD_STATE), bsz,
                                                     init_s, prev_s)
    n_fixed = 8
    aliases = {}
    if cv_alias is not None:
        aliases[n_fixed + cv_alias] = 2
        aliases[n_fixed + len(cv_in) + s_alias] = 1
    return pl.pallas_call(
        functools.partial(_ssd_kernel, c_seq=c_seq, nseq=nseq, nblk=nblk, has_init=init_s is not None,
                          has_prev=prev_s is not None),
        grid=grid,
        in_specs=[pl.BlockSpec((nb, tb, 4 * GROUP_W), lambda i, j: (i, j, COL_SSD // (4 * GROUP_W))),
                  pl.BlockSpec((nb, tb, GROUP_W), lambda i, j: (i, j, COL_DT // GROUP_W)),
                  _layer_spec(l, (SSD_CONV, SSD_CONV_DIM), 2),
                  _layer_spec(l, (1, SSD_CONV_DIM), 2),
                  _layer_spec(l, (1, GROUP_W), 2),
                  _layer_spec(l, (1, GROUP_W), 2),
                  _layer_spec(l, (1, GROUP_W), 2),
                  _layer_spec(l, (1, GROUP_W), 2)] + cv_in + s_in,
        out_specs=[pl.BlockSpec((nb, tb, GROUP_W), lambda i, j: (i, j, 0)), s_out, cv_out],
        out_shape=[jax.ShapeDtypeStruct((bsz, seq, GROUP_W), F32), s_shape, cv_shape],
        scratch_shapes=[pltpu.VMEM((nb, tb + SUBLANES, SSD_CONV_DIM), F32)],
        input_output_aliases=aliases,
        compiler_params=_cparams(("parallel", "arbitrary")),
    )(proj, proj, prm["conv_w"], prm["conv_b"], prm["dt_bias"], prm["a_log"], prm["ssd_d"], prm["ssd_norm"],
      *cv_ops, *s_ops)


def _ssd_state_in(s):
    d, bsz = s.shape[:2]
    return jnp.swapaxes(s, 3, 4).reshape(d, bsz, SSD_GROUPS, LANES, SSD_STATE)


def _ssd_state_out(s):
    d, bsz = s.shape[:2]
    return jnp.swapaxes(s.reshape(d, bsz, HEADS, HEAD_W, SSD_STATE), 3, 4)


def _s5_kernel(*refs, tl, unroll, has_init, has_prev):
    (u0_ref, u1_ref, are_ref, aim_ref, ldt_ref, bre_ref, bim_ref, cw_ref, dsk_ref, gw_ref,
     gb_ref) = refs[:11]
    n_in = 11
    s0r_ref = s0i_ref = None
    if has_init:
        s0r_ref, s0i_ref = refs[n_in], refs[n_in + 1 + has_prev]
    (y_ref, sr_ref, si_ref, wb_ref, ab_ref, us0_ref, us1_ref, ut_ref, bu_ref, yt_ref, ys0_ref,
     ys1_ref) = refs[n_in + 2 * (has_init + has_prev):]
    nbt = SUBLANES * tl

    @pl.when(pl.program_id(1) == 0)
    def _():
        if has_init:
            sr_ref[...] = s0r_ref[...]
            si_ref[...] = s0i_ref[...]
        else:
            sr_ref[...] = jnp.zeros_like(sr_ref)
            si_ref[...] = jnp.zeros_like(si_ref)
        lr, li, st = are_ref[...], aim_ref[...], jnp.exp(ldt_ref[...])
        mag = jnp.exp(lr * st)
        abr = mag * jnp.cos(li * st)
        abi = mag * jnp.sin(li * st)
        ab_ref[0:1, :] = abr
        ab_ref[1:2, :] = abi
        den = lr * lr + li * li
        cr = ((abr - 1.0) * lr + abi * li) / den
        ci = (abi * lr - (abr - 1.0) * li) / den
        wb_ref[:, 0:S5_N] = (cr * bre_ref[...] - ci * bim_ref[...]).astype(BF16)
        wb_ref[:, S5_N:2 * S5_N] = (cr * bim_ref[...] + ci * bre_ref[...]).astype(BF16)

    us0_ref[...] = u0_ref[...].reshape(nbt, LANES)
    us1_ref[...] = u1_ref[...].reshape(nbt, LANES)

    def gather(t, carry):
        dst = pl.ds(pl.multiple_of(t * SUBLANES, SUBLANES), SUBLANES)
        src = pl.ds(t, SUBLANES, stride=tl)
        ut_ref[dst, 0:LANES] = us0_ref[src, :]
        ut_ref[dst, LANES:2 * LANES] = us1_ref[src, :]
        return carry

    lax.fori_loop(0, tl, gather, 0, unroll=unroll)
    u = ut_ref[...]
    bu_ref[...] = jnp.dot(u.astype(BF16), wb_ref[...], preferred_element_type=F32)
    ar = ab_ref[0:1, :]
    ai = ab_ref[1:2, :]

    def step(t, carry):
        xr, xi = carry
        rows = pl.ds(pl.multiple_of(t * SUBLANES, SUBLANES), SUBLANES)
        nr = ar * xr - ai * xi + bu_ref[rows, 0:S5_N]
        ni = ar * xi + ai * xr + bu_ref[rows, S5_N:2 * S5_N]
        bu_ref[rows, 0:S5_N] = nr
        bu_ref[rows, S5_N:2 * S5_N] = ni
        return nr, ni

    xr, xi = lax.fori_loop(0, tl, step, (sr_ref[...], si_ref[...]), unroll=unroll)
    sr_ref[...] = xr
    si_ref[...] = xi

    y = _mm(bu_ref[...], cw_ref[...]) + dsk_ref[...] * u
    y = jax.nn.gelu(y)
    yt_ref[...] = y * jax.nn.sigmoid(_mm(y, gw_ref[...]) + gb_ref[...])

    def scatter(t, carry):
        src = pl.ds(pl.multiple_of(t * SUBLANES, SUBLANES), SUBLANES)
        dst = pl.ds(t, SUBLANES, stride=tl)
        ys0_ref[dst, :] = yt_ref[src, 0:LANES]
        ys1_ref[dst, :] = yt_ref[src, LANES:2 * LANES]
        return carry

    lax.fori_loop(0, tl, scatter, 0, unroll=unroll)
    y_ref[:, :, 0:LANES] = ys0_ref[...].reshape(SUBLANES, tl, LANES)
    y_ref[:, :, LANES:2 * LANES] = ys1_ref[...].reshape(SUBLANES, tl, LANES)


def _s5(proj, prm, init_re, init_im, prev_re, prev_im, l, tl, unroll):
    bsz, seq, _ = proj.shape
    re_in, re_ops, re_out, re_shape, re_alias = _state_io(l, SUBLANES, (S5_N,), bsz, init_re, prev_re)
    im_in, im_ops, im_out, im_shape, im_alias = _state_io(l, SUBLANES, (S5_N,), bsz, init_im, prev_im)
    n_fixed = 11
    aliases = {}
    if re_alias is not None:
        aliases[n_fixed + re_alias] = 1
        aliases[n_fixed + len(re_in) + im_alias] = 2
    nbt = SUBLANES * tl
    return pl.pallas_call(
        functools.partial(_s5_kernel, tl=tl, unroll=unroll, has_init=init_re is not None,
                          has_prev=prev_re is not None),
        grid=(bsz // SUBLANES, seq // tl),
        in_specs=[pl.BlockSpec((SUBLANES, tl, LANES), lambda i, j: (i, j, COL_S5 // LANES)),
                  pl.BlockSpec((SUBLANES, tl, LANES), lambda i, j: (i, j, COL_S5 // LANES + 1)),
                  _layer_spec(l, (1, S5_N), 2), _layer_spec(l, (1, S5_N), 2), _layer_spec(l, (1, S5_N), 2),
                  _layer_spec(l, (GROUP_W, S5_N), 2), _layer_spec(l, (GROUP_W, S5_N), 2),
                  _layer_spec(l, (2 * S5_N, GROUP_W), 2),
                  _layer_spec(l, (1, GROUP_W), 2),
                  _layer_spec(l, (GROUP_W, GROUP_W), 2),
                  _layer_spec(l, (1, GROUP_W), 2)] + re_in + im_in,
        out_specs=[pl.BlockSpec((SUBLANES, tl, GROUP_W), lambda i, j: (i, j, 0)), re_out, im_out],
        out_shape=[jax.ShapeDtypeStruct((bsz, seq, GROUP_W), F32), re_shape, im_shape],
        scratch_shapes=[pltpu.VMEM((GROUP_W, 2 * S5_N), BF16),
                        pltpu.VMEM((SUBLANES, S5_N), F32),
                        pltpu.VMEM((nbt, LANES), F32), pltpu.VMEM((nbt, LANES), F32),
                        pltpu.VMEM((nbt, GROUP_W), F32),
                        pltpu.VMEM((nbt, 2 * S5_N), F32),
                        pltpu.VMEM((nbt, GROUP_W), F32),
                        pltpu.VMEM((nbt, LANES), F32), pltpu.VMEM((nbt, LANES), F32)],
        input_output_aliases=aliases,
        compiler_params=_cparams(("parallel", "arbitrary")),
    )(proj, proj, prm["a_re"], prm["a_im"], prm["log_dt"], prm["b_re"], prm["b_im"], prm["c_blk"],
      prm["s5_d"], prm["glu_w"], prm["glu_b"], *re_ops, *im_ops)


def _block_diag(blocks):
    d, g, r, c = blocks.shape
    eye = jnp.eye(g, dtype=blocks.dtype)
    return (eye[None, :, None, :, None] * blocks[:, :, :, None, :]).reshape(d, g * r, g * c)


def _prep(ssd_conv_w, ssd_conv_b, ssd_dt_bias, ssd_a_log, ssd_d, ssd_norm, s5_a_re, s5_a_im, s5_log_dt,
          s5_b_re, s5_b_im, s5_c_re, s5_c_im, s5_d, s5_glu_w, s5_glu_b):
    per_head = lambda v: jnp.repeat(v, HEAD_W, axis=1)[:, None, :]
    c_blk = jnp.concatenate([_block_diag(jnp.swapaxes(s5_c_re, 2, 3)),
                             -_block_diag(jnp.swapaxes(s5_c_im, 2, 3))], axis=1)
    return dict(
        conv_w=ssd_conv_w, conv_b=ssd_conv_b[:, None, :],
        dt_bias=per_head(ssd_dt_bias), a_log=per_head(ssd_a_log), ssd_d=per_head(ssd_d),
        ssd_norm=ssd_norm[:, None, :],
        a_re=s5_a_re.reshape(DEPTH, 1, S5_N), a_im=s5_a_im.reshape(DEPTH, 1, S5_N),
        log_dt=jnp.repeat(s5_log_dt, S5_STATE, axis=1)[:, None, :],
        b_re=_block_diag(jnp.swapaxes(s5_b_re, 2, 3)), b_im=_block_diag(jnp.swapaxes(s5_b_im, 2, 3)),
        c_blk=c_blk.astype(BF16),
        s5_d=s5_d[:, None, :], glu_w=s5_glu_w.astype(BF16), glu_b=s5_glu_b[:, None, :],
    )


def _trunk(x, pos0, init, prm, w_in, w_out, w_up, w_down, norm_mix, norm_mlp, norm_final, lb_logits,
           angle_row, log_gamma, cfg):
    bsz, seq, _ = x.shape
    c_ret, nseq_ret, _ = cfg["ret"]
    cos, sin = _rope_tables(angle_row, pos0, seq if nseq_ret == 1 else nseq_ret * c_ret, seq, cfg["rope_tl"])
    x2d = x.reshape(bsz * seq, D_MODEL)
    i_ret, i_hg, i_ssd, i_cv, i_re, i_im = init if init is not None else (None,) * 6
    if init is not None:
        i_ssd = _ssd_state_in(i_ssd)
        i_re = i_re.reshape(DEPTH, bsz, S5_N)
        i_im = i_im.reshape(DEPTH, bsz, S5_N)
    s_ret = s_hg = s_ssd = s_cv = s_re = s_im = None
    for l in range(DEPTH):
        proj = _inproj(x2d, norm_mix, w_in, l, cfg["tm_in"]).reshape(bsz, seq, IN_COLS_PAD)
        ya, s_ret = _retention(proj, cos, sin, log_gamma, i_ret, s_ret, l, *cfg["ret"])
        yb, s_hg = _hgrn2(proj, lb_logits, i_hg, s_hg, l, *cfg["hg"])
        yc, s_ssd, s_cv = _ssd(proj, prm, i_cv, i_ssd, s_cv, s_ssd, l, *cfg["ssd"])
        yd, s_re, s_im = _s5(proj, prm, i_re, i_im, s_re, s_im, l, *cfg["s5"])
        ys = [y.reshape(bsz * seq, GROUP_W) for y in (ya, yb, yc, yd)]
        x2d = _outmlp(x2d, ys, w_out, norm_mlp, w_up, w_down, norm_final, l, cfg["tm_out"])
    return (x2d.reshape(bsz, seq, D_MODEL), s_ret, s_hg, _ssd_state_out(s_ssd), s_cv,
            s_re.reshape(DEPTH, bsz, S5_GROUPS, S5_STATE), s_im.reshape(DEPTH, bsz, S5_GROUPS, S5_STATE))


PROMPT_CFG = dict(rope_tl=256, tm_in=512, ret=(256, 1, 4), hg=(64, 1, 8), ssd=(128, 1, 8), s5=(128, 8),
                  tm_out=512)
SAMPLE_CFG = dict(rope_tl=128, tm_in=512, ret=(8, 16, 2), hg=(8, 16, 2), ssd=(8, 16, 2), s5=(8, 8),
                  tm_out=512)


def kernel(x_prompt, x_sample, state_ret, state_hgrn, state_ssd, state_ssd_conv, state_s5_re, state_s5_im, norm_mix, w_in, w_out, hg_lb_logits, ssd_conv_w, ssd_conv_b, ssd_dt_bias, ssd_a_log, ssd_d, ssd_norm, s5_a_re, s5_a_im, s5_log_dt, s5_b_re, s5_b_im, s5_c_re, s5_c_im, s5_d, s5_glu_w, s5_glu_b, norm_mlp, w_up, w_down, norm_final):
    prm = _prep(ssd_conv_w, ssd_conv_b, ssd_dt_bias, ssd_a_log, ssd_d, ssd_norm, s5_a_re, s5_a_im, s5_log_dt,
                s5_b_re, s5_b_im, s5_c_re, s5_c_im, s5_d, s5_glu_w, s5_glu_b)
    w_out_b = _cast_bf16(w_out, 512)
    w_up_b = _cast_bf16(w_up, 256)
    w_down_b = _cast_bf16(w_down, 1024)
    angle = 1.0 / (ROPE_BASE ** jnp.linspace(0.0, 1.0, HEAD_W // 2, dtype=F32))
    angle_row = jnp.tile(jnp.repeat(angle, 2), 2)[None, :]
    log_gamma = jnp.log(1.0 - jnp.exp2(-5.0 - jnp.arange(HEADS, dtype=F32)))
    shared = (prm, w_in, w_out_b, w_up_b, w_down_b, norm_mix[:, None, :], norm_mlp[:, None, :],
              norm_final[None, :], hg_lb_logits, angle_row, log_gamma)
    states = (state_ret, state_hgrn, state_ssd, state_ssd_conv, state_s5_re, state_s5_im)
    out_p = _trunk(x_prompt, 0, None, *shared, PROMPT_CFG)
    out_s = _trunk(x_sample, PAST_LEN, states, *shared, SAMPLE_CFG)
    return (out_p[0], out_s[0]) + out_p[1:] + out_s[1:]
```

```python
import functools
import itertools

import jax
import jax.numpy as jnp
from jax import lax
from jax.experimental import pallas as pl
from jax.experimental.pallas import tpu as pltpu

F32 = jnp.float32
BF16 = jnp.bfloat16

D_MODEL = 1024
DEPTH = 2
PAST_LEN = 16384
GROUP_W = 256
HEADS = 4
HEAD_W = 64
PAIR_HEADS = 2
ROPE_BASE = 10000.0
HG_MIN_FORGET = 1e-30
HG_SAFE_SPREAD = 80.0
SSD_STATE = 128
SSD_GROUPS = 2
SSD_CONV = 4
SSD_CONV_DIM = 768
S5_GROUPS = 16
S5_STATE = 64
S5_N = S5_GROUPS * S5_STATE
D_FF = 4096
IN_COLS = 3332
EPS = 1e-6

COL_RET = 0
COL_HG = 1024
COL_SSD = 2048
COL_S5 = 3072
COL_DT = 3328
IN_COLS_PAD = 3584

LANES = 128
SUBLANES = 8
VMEM_LIMIT = 56 * 1024 * 1024


def _cparams(sem):
    return pltpu.CompilerParams(dimension_semantics=sem, vmem_limit_bytes=VMEM_LIMIT)


def _mm(a, b):
    return jnp.dot(a.astype(BF16), b.astype(BF16), preferred_element_type=F32)


def _mm_nt(a, b):
    return lax.dot_general(a.astype(BF16), b.astype(BF16), (((1,), (1,)), ((), ())),
                           preferred_element_type=F32)


def _mm_tn(a, b):
    return lax.dot_general(a.astype(BF16), b.astype(BF16), (((0,), (0,)), ((), ())),
                           preferred_element_type=F32)


def _mm_f32(a, b):
    return jnp.dot(a, b, preferred_element_type=F32, precision=lax.Precision.HIGHEST)


def _silu(x):
    return x * jax.nn.sigmoid(x)


def _rms(x, w):
    return x * lax.rsqrt(jnp.mean(x * x, axis=-1, keepdims=True) + EPS) * w


def _tri(c):
    t = lax.broadcasted_iota(jnp.int32, (c, c), 0)
    s = lax.broadcasted_iota(jnp.int32, (c, c), 1)
    return t >= s


def _head_of_lane(rows, width=GROUP_W):
    return lax.broadcasted_iota(jnp.int32, (rows, width), 1) // HEAD_W


def _expand_heads(x, head):
    nheads = x.shape[1] // HEAD_W
    return jnp.concatenate([jnp.where(head == h, x, 0.0) for h in range(nheads)], axis=0).astype(BF16)


def _block_mask(seg, width=GROUP_W):
    r = lax.broadcasted_iota(jnp.int32, (width, width), 0) // seg
    c = lax.broadcasted_iota(jnp.int32, (width, width), 1) // seg
    return r == c


def _seg_mean(x, m):
    hi = x.astype(BF16)
    lo = (x - hi.astype(F32)).astype(BF16)
    return jnp.dot(hi, m, preferred_element_type=F32) + jnp.dot(lo, m, preferred_element_type=F32)


def _row_blocks(nseq, c_seq, nblk):
    if nseq == 1:
        return [(slice(0, 1), slice(k * c_seq, (k + 1) * c_seq), [0]) for k in range(nblk)]
    return [(slice(k * nseq, (k + 1) * nseq), slice(0, c_seq), list(range(k * nseq, (k + 1) * nseq)))
            for k in range(nblk)]


def _seq_grid(rows, c_seq, copies):
    r = lax.broadcasted_iota(jnp.int32, (rows, copies * rows), 0)
    j = lax.broadcasted_iota(jnp.int32, (rows, copies * rows), 1)
    h = j // rows
    r2 = j - h * rows
    causal = r >= r2
    if rows > c_seq:
        causal = jnp.logical_and(r // c_seq == r2 // c_seq, causal)
    return r, r2, h, causal


def _stack_rows(pieces):
    return pieces[0] if len(pieces) == 1 else jnp.concatenate(pieces, axis=0)


def _mixer_grid(bsz, seq, c_seq, nseq, nblk):
    nb = 1 if nseq == 1 else nseq * nblk
    tb = c_seq * nblk if nseq == 1 else c_seq
    assert nseq == 1 or seq == c_seq
    return nb, tb, (bsz // nb, seq // tb)


def _layer_spec(l, tail, n_grid, single=False):
    zeros = (0,) * len(tail)
    kw = dict(pipeline_mode=pl.Buffered(1)) if single else {}
    if n_grid == 1:
        return pl.BlockSpec((None,) + tail, lambda i: (l,) + zeros, **kw)
    return pl.BlockSpec((None,) + tail, lambda i, j: (l,) + zeros, **kw)


def _state_spec(l, nb, tail):
    zeros = (0,) * len(tail)
    return pl.BlockSpec((None, nb) + tail, lambda i, j: (l, i) + zeros)


def _state_io(l, nb, tail, bsz, init, prev):
    spec = _state_spec(l, nb, tail)
    in_specs, operands = [], []
    if init is not None:
        in_specs.append(spec)
        operands.append(init)
    alias_pos = None
    if prev is not None:
        alias_pos = len(in_specs)
        in_specs.append(pl.BlockSpec(memory_space=pl.ANY))
        operands.append(prev)
    return in_specs, operands, spec, jax.ShapeDtypeStruct((DEPTH, bsz) + tail, F32), alias_pos


def _cast_kernel(w_ref, o_ref):
    o_ref[...] = w_ref[...].astype(BF16)


def _cast_bf16(w, tr):
    d, r, c = w.shape
    spec = pl.BlockSpec((None, tr, c), lambda l, i: (l, i, 0))
    return pl.pallas_call(
        _cast_kernel, grid=(d, r // tr), in_specs=[spec], out_specs=spec,
        out_shape=jax.ShapeDtypeStruct(w.shape, BF16),
        compiler_params=_cparams(("parallel", "parallel")),
    )(w)


def _inproj_kernel(x_ref, nw_ref, w_ref, o_ref, wb_ref):
    @pl.when(pl.program_id(0) == 0)
    def _():
        rows = 256
        head = _head_of_lane(rows)
        for r in range(0, D_MODEL, rows):
            rs = slice(r, r + rows)
            wb_ref[rs, 0:COL_S5] = w_ref[rs, 0:COL_S5].astype(BF16)
            tail = w_ref[rs, COL_S5:IN_COLS]
            wb_ref[rs, COL_S5:COL_DT] = tail[:, HEADS:HEADS + GROUP_W].astype(BF16)
            dtw = tail[:, HEADS - 1:HEADS]
            for h in range(HEADS - 2, -1, -1):
                dtw = jnp.where(head == h, tail[:, h:h + 1], dtw)
            wb_ref[rs, COL_DT:IN_COLS_PAD] = dtw.astype(BF16)

    h = _rms(x_ref[...], nw_ref[...])
    o_ref[...] = jnp.dot(h.astype(BF16), wb_ref[...], preferred_element_type=F32)


def _inproj(x2d, norm_w, w_in, l, tm):
    t = x2d.shape[0]
    return pl.pallas_call(
        _inproj_kernel,
        grid=(t // tm,),
        in_specs=[pl.BlockSpec((tm, D_MODEL), lambda i: (i, 0)),
                  _layer_spec(l, (1, D_MODEL), 1),
                  _layer_spec(l, (D_MODEL, IN_COLS), 1, single=True)],
        out_specs=pl.BlockSpec((tm, IN_COLS_PAD), lambda i: (i, 0)),
        out_shape=jax.ShapeDtypeStruct((t, IN_COLS_PAD), F32),
        scratch_shapes=[pltpu.VMEM((D_MODEL, IN_COLS_PAD), BF16)],
        compiler_params=_cparams(("arbitrary",)),
    )(x2d, norm_w, w_in)


def _outmlp_kernel(x_ref, ya_ref, yb_ref, yc_ref, yd_ref, wo_ref, nw_ref, wu_ref, wd_ref, nf_ref,
                   o_ref, h_ref, u_ref, *, final_norm, tn_up, tn_down):
    x1 = x_ref[...]
    for g, y_ref in enumerate((ya_ref, yb_ref, yc_ref, yd_ref)):
        x1 = x1 + _mm(y_ref[...], wo_ref[g * GROUP_W:(g + 1) * GROUP_W, :])
    o_ref[...] = x1
    h_ref[...] = _rms(x1, nw_ref[...]).astype(BF16)
    for n in range(0, D_FF, tn_up):
        u = jnp.dot(h_ref[...], wu_ref[:, n:n + tn_up], preferred_element_type=F32)
        u_ref[:, n:n + tn_up] = jnp.square(jnp.maximum(u, 0.0)).astype(BF16)
    for n in range(0, D_MODEL, tn_down):
        o_ref[:, n:n + tn_down] += jnp.dot(u_ref[...], wd_ref[:, n:n + tn_down], preferred_element_type=F32)
    if final_norm:
        o_ref[...] = _rms(o_ref[...], nf_ref[...])


def _outmlp(x2d, ys, w_out, norm_w, w_up, w_down, norm_f, l, tm):
    t = x2d.shape[0]
    row = lambda i: (i, 0)
    return pl.pallas_call(
        functools.partial(_outmlp_kernel, final_norm=(l == DEPTH - 1), tn_up=512, tn_down=256),
        grid=(t // tm,),
        in_specs=[pl.BlockSpec((tm, D_MODEL), row)]
        + [pl.BlockSpec((tm, GROUP_W), row)] * 4
        + [_layer_spec(l, (D_MODEL, D_MODEL), 1, single=True),
           _layer_spec(l, (1, D_MODEL), 1),
           _layer_spec(l, (D_MODEL, D_FF), 1, single=True),
           _layer_spec(l, (D_FF, D_MODEL), 1, single=True),
           pl.BlockSpec((1, D_MODEL), lambda i: (0, 0))],
        out_specs=pl.BlockSpec((tm, D_MODEL), row),
        out_shape=jax.ShapeDtypeStruct((t, D_MODEL), F32),
        scratch_shapes=[pltpu.VMEM((tm, D_MODEL), BF16),
                        pltpu.VMEM((tm, D_FF), BF16)],
        compiler_params=_cparams(("parallel",)),
    )(x2d, *ys, w_out, norm_w, w_up, w_down, norm_f)


def _rope_kernel(angle_ref, cos_ref, sin_ref, *, pos0, tl, period):
    i = pl.program_id(0)
    row = i * tl + lax.broadcasted_iota(jnp.int32, (tl, 2 * HEAD_W), 0)
    pos = (pos0 + row % period).astype(F32)
    theta = pos * angle_ref[...]
    lane = lax.broadcasted_iota(jnp.int32, (tl, 2 * HEAD_W), 1)
    cos_ref[...] = jnp.cos(theta)
    s = jnp.sin(theta)
    sin_ref[...] = jnp.where(lane % 2 == 0, -s, s)


def _rope_tables(angle_row, pos0, rows, period, tl):
    seq = rows
    return pl.pallas_call(
        functools.partial(_rope_kernel, pos0=pos0, tl=tl, period=period),
        grid=(seq // tl,),
        in_specs=[pl.BlockSpec((1, 2 * HEAD_W), lambda i: (0, 0))],
        out_specs=[pl.BlockSpec((tl, 2 * HEAD_W), lambda i: (i, 0))] * 2,
        out_shape=[jax.ShapeDtypeStruct((seq, 2 * HEAD_W), F32)] * 2,
        compiler_params=_cparams(("parallel",)),
    )(angle_row)


def _ret_kernel(*refs, c_seq, nseq, nblk, has_init, has_prev):
    lg_ref, p_ref, cos_ref, sin_ref = refs[:4]
    s0_ref = refs[4] if has_init else None
    y_ref, s_ref, sb_ref = refs[4 + has_init + has_prev:]
    nb = s_ref.shape[0]
    c = nseq * c_seq
    j = pl.program_id(1)

    @pl.when(j == 0)
    def _():
        sb_ref[...] = jnp.zeros_like(sb_ref)
        if has_init:
            for b in range(nb):
                for h in range(HEADS):
                    hs = slice(h * HEAD_W, (h + 1) * HEAD_W)
                    sb_ref[b, hs, hs] = s0_ref[b, h]

    def per_head(idx, shape):
        out = jnp.full(shape, lg_ref[HEADS - 1], F32)
        for h in range(HEADS - 2, -1, -1):
            out = jnp.where(idx == h, lg_ref[h], out)
        return out

    head = _head_of_lane(c)
    bd = _block_mask(HEAD_W)
    m_seg = jnp.where(bd, 1.0 / HEAD_W, 0.0).astype(BF16)
    even = lax.broadcasted_iota(jnp.int32, (c, GROUP_W), 1) % 2 == 0
    lg_lane = per_head(head[0:1, :], (1, GROUP_W))
    t_loc = (lax.broadcasted_iota(jnp.int32, (c, 1), 0) % c_seq).astype(F32)
    e_q = jnp.exp((t_loc + 1.0) * lg_lane)
    e_k = jnp.exp((c_seq - 1.0 - t_loc) * lg_lane)
    a_c = jnp.exp(c_seq * lg_lane)
    r_i, r2_i, h_i, causal = _seq_grid(c, c_seq, HEADS)
    dist = jnp.where(causal, (r_i - r2_i).astype(F32), 0.0)
    decay = jnp.where(causal, jnp.exp(dist * per_head(h_i, (c, HEADS * c))), 0.0)

    def rot(x, cos, sin):
        swapped = jnp.where(even, pltpu.roll(x, GROUP_W - 1, 1), pltpu.roll(x, 1, 1))
        return x * cos + swapped * sin

    for bs, ts, seqs in _row_blocks(nseq, c_seq, nblk):
        load = lambda lo: p_ref[bs, ts, lo:lo + GROUP_W].reshape(c, GROUP_W)
        crow = ts if nseq == 1 else slice(0, c)
        cos = jnp.concatenate([cos_ref[crow, :]] * 2, axis=1)
        sin = jnp.concatenate([sin_ref[crow, :]] * 2, axis=1)
        q = rot(load(0), cos, sin)
        k = rot(load(GROUP_W), cos, sin) * (HEAD_W ** -0.5)
        v = load(2 * GROUP_W)
        g = load(3 * GROUP_W)
        scores = _mm_nt(q, _expand_heads(k, head)) * decay
        o = _mm(scores, _expand_heads(v, head))
        qe = q * e_q
        ke = k * e_k
        cross = []
        for i, b in enumerate(seqs):
            sl = slice(i * c_seq, (i + 1) * c_seq)
            s_prev = sb_ref[b]
            cross.append(_mm(qe[sl], s_prev))
            sb_ref[b] = a_c * s_prev + jnp.where(bd, _mm_tn(ke[sl], v[sl]), 0.0)
        o = o + _stack_rows(cross)
        mu = _seg_mean(o, m_seg)
        d = o - mu
        var = _seg_mean(d * d, m_seg)
        y_ref[bs, ts, :] = (d * lax.rsqrt(var + EPS) * _silu(g)).reshape(len(seqs), c_seq, GROUP_W)

    @pl.when(j == pl.num_programs(1) - 1)
    def _():
        for b in range(nb):
            for h in range(HEADS):
                hs = slice(h * HEAD_W, (h + 1) * HEAD_W)
                s_ref[b, h] = sb_ref[b, hs, hs]


def _retention(proj, cos, sin, log_gamma, init, prev, l, c_seq, nseq, nblk):
    bsz, seq, _ = proj.shape
    nb, tb, grid = _mixer_grid(bsz, seq, c_seq, nseq, nblk)
    st_in, st_ops, st_out, st_shape, alias = _state_io(l, nb, (HEADS, HEAD_W, HEAD_W), bsz, init, prev)
    n_fixed = 4
    rope_spec = (pl.BlockSpec((tb, 2 * HEAD_W), lambda i, j: (j, 0)) if nseq == 1 else
                 pl.BlockSpec((nseq * c_seq, 2 * HEAD_W), lambda i, j: (0, 0)))
    return pl.pallas_call(
        functools.partial(_ret_kernel, c_seq=c_seq, nseq=nseq, nblk=nblk, has_init=init is not None,
                          has_prev=prev is not None),
        grid=grid,
        in_specs=[pl.BlockSpec(memory_space=pltpu.SMEM),
                  pl.BlockSpec((nb, tb, 4 * GROUP_W), lambda i, j: (i, j, COL_RET // (4 * GROUP_W))),
                  rope_spec, rope_spec] + st_in,
        out_specs=[pl.BlockSpec((nb, tb, GROUP_W), lambda i, j: (i, j, 0)), st_out],
        out_shape=[jax.ShapeDtypeStruct((bsz, seq, GROUP_W), F32), st_shape],
        scratch_shapes=[pltpu.VMEM((nb, GROUP_W, GROUP_W), F32)],
        input_output_aliases={} if alias is None else {n_fixed + alias: 1},
        compiler_params=_cparams(("parallel", "arbitrary")),
    )(log_gamma, proj, cos, sin, *st_ops)


def _hg_kernel(*refs, c_seq, nseq, nblk, layer, has_init, has_prev):
    p_ref, lb_ref = refs[:2]
    s0_ref = refs[2] if has_init else None
    y_ref, s_ref, st_ref, keep_ref = refs[2 + has_init + has_prev:]
    nb = s_ref.shape[0]
    c = nseq * c_seq
    j = pl.program_id(1)

    @pl.when(j == 0)
    def _():
        st_ref[...] = jnp.zeros_like(st_ref)
        if has_init:
            for b in range(nb):
                for h in range(HEADS):
                    hs = slice(h * HEAD_W, (h + 1) * HEAD_W)
                    st_ref[b, hs, hs] = s0_ref[b, h].T

    logits = lb_ref[...]
    e = jnp.exp(logits - jnp.max(logits, axis=0, keepdims=True))
    sm = e / jnp.sum(e, axis=0, keepdims=True)
    csum = sm[0:1, :]
    for i in range(1, layer + 1):
        csum = csum + sm[i:i + 1, :]
    lower = csum - sm[0:1, :]

    head = _head_of_lane(c)
    bd = _block_mask(HEAD_W)
    m_seg = jnp.where(bd, 1.0 / HEAD_W, 0.0).astype(BF16)
    r_i, r2_i, _, tril = _seq_grid(c, c_seq, 1)
    trif = tril.astype(F32)
    later = r2_i > r_i
    if nseq > 1:
        later = jnp.logical_and(r_i // c_seq == r2_i // c_seq, later)
    laterf = later.astype(F32)
    causal = _seq_grid(c, c_seq, HEADS)[3]
    mid = c // 2 - 1

    staged = []
    spreads = []
    for bs, ts, seqs in _row_blocks(nseq, c_seq, nblk):
        load = lambda lo: p_ref[bs, ts, lo:lo + GROUP_W].reshape(c, GROUP_W)
        qq = _silu(load(0))
        ff = load(GROUP_W)
        v = load(2 * GROUP_W)
        g = load(3 * GROUP_W)
        sig = jax.nn.sigmoid(ff)
        forget = lower + (1.0 - lower) * sig
        log_f = jnp.log(jnp.maximum(forget, HG_MIN_FORGET))
        key = (1.0 - lower) * (1.0 - sig)
        bcum = _mm_f32(trif, log_f)
        if nseq == 1:
            b_ref = bcum[mid:mid + 1, :]
            spread = bcum - b_ref
            q_in = qq * jnp.exp(spread)
            k_in = key * jnp.exp(-spread)
            q_x = q_in * jnp.exp(b_ref)
            k_x = k_in * jnp.exp(bcum[c - 1:c, :] - b_ref)
        else:
            spread = bcum
            q_in = q_x = qq * jnp.exp(bcum)
            k_in = key * jnp.exp(-bcum)
            k_x = key * jnp.exp(_mm_f32(laterf, log_f))
        spreads.append(jnp.max(jnp.abs(spread)))
        scores = jnp.where(causal, _mm_nt(q_in, _expand_heads(k_in, head)), 0.0)
        o = _mm(scores, _expand_heads(v, head))
        upd = []
        for i in range(len(seqs)):
            sl = slice(i * c_seq, (i + 1) * c_seq)
            a_seq = jnp.exp(bcum[(i + 1) * c_seq - 1:(i + 1) * c_seq, :])
            upd.append((q_x[sl], a_seq, jnp.where(bd, _mm_tn(v[sl], k_x[sl]), 0.0)))
        staged.append((bs, ts, seqs, o, upd, _silu(g)))

    unsafe = functools.reduce(jnp.maximum, spreads) > HG_SAFE_SPREAD
    if nb == 1:
        keep_ref[...] = st_ref[0]

    for bs, ts, seqs, o, upd, gate in staged:
        cross = []
        for b, (q_b, a_seq, ds) in zip(seqs, upd):
            st_prev = st_ref[b]
            cross.append(_mm_nt(q_b, st_prev))
            st_ref[b] = st_prev * a_seq + ds
        o = o + _stack_rows(cross)
        y = o * lax.rsqrt(_seg_mean(o * o, m_seg) + EPS) * gate
        y_ref[bs, ts, :] = y.reshape(len(seqs), c_seq, GROUP_W)

    @pl.when(unsafe)
    def _():
        tb = p_ref.shape[1]
        assert nb == 1 or (tb == SUBLANES and pl.num_programs(1) == 1)
        rowid = lax.broadcasted_iota(jnp.int32, (SUBLANES, GROUP_W), 0)
        if nb == 1:
            st_ref[0] = keep_ref[...]

        def slab(s, carry):
            b, t0 = (0, pl.multiple_of(s * SUBLANES, SUBLANES)) if nb == 1 else (s, 0)
            rows = pl.ds(t0, SUBLANES)
            if nb > 1:
                st_ref[b] = jnp.zeros((GROUP_W, GROUP_W), F32)
                if has_init:
                    for h in range(HEADS):
                        hs = slice(h * HEAD_W, (h + 1) * HEAD_W)
                        st_ref[b, hs, hs] = s0_ref[b, h].T
            q8 = _silu(p_ref[b, rows, 0:GROUP_W])
            sig = jax.nn.sigmoid(p_ref[b, rows, GROUP_W:2 * GROUP_W])
            v8 = p_ref[b, rows, 2 * GROUP_W:3 * GROUP_W]
            g8 = p_ref[b, rows, 3 * GROUP_W:4 * GROUP_W]
            f8 = jnp.maximum(lower + (1.0 - lower) * sig, HG_MIN_FORGET)
            k8 = (1.0 - lower) * (1.0 - sig)
            o8 = jnp.zeros((SUBLANES, GROUP_W), F32)
            for i in range(SUBLANES):
                outer = _mm_tn(jnp.where(rowid == i, v8, 0.0), k8)
                st = st_ref[b] * f8[i:i + 1, :] + jnp.where(bd, outer, 0.0)
                st_ref[b] = st
                o8 = jnp.where(rowid == i, _mm_nt(q8, st), o8)
            y_ref[b, rows, :] = o8 * lax.rsqrt(_seg_mean(o8 * o8, m_seg) + EPS) * _silu(g8)
            return carry

        lax.fori_loop(0, nb * tb // SUBLANES, slab, 0)

    @pl.when(j == pl.num_programs(1) - 1)
    def _():
        for b in range(nb):
            for h in range(HEADS):
                hs = slice(h * HEAD_W, (h + 1) * HEAD_W)
                s_ref[b, h] = st_ref[b, hs, hs].T


def _hgrn2(proj, lb_logits, init, prev, l, c_seq, nseq, nblk):
    bsz, seq, _ = proj.shape
    nb, tb, grid = _mixer_grid(bsz, seq, c_seq, nseq, nblk)
    st_in, st_ops, st_out, st_shape, alias = _state_io(l, nb, (HEADS, HEAD_W, HEAD_W), bsz, init, prev)
    n_fixed = 2
    return pl.pallas_call(
        functools.partial(_hg_kernel, c_seq=c_seq, nseq=nseq, nblk=nblk, layer=l, has_init=init is not None,
                          has_prev=prev is not None),
        grid=grid,
        in_specs=[pl.BlockSpec((nb, tb, 4 * GROUP_W), lambda i, j: (i, j, COL_HG // (4 * GROUP_W))),
                  pl.BlockSpec((DEPTH, GROUP_W), lambda i, j: (0, 0))] + st_in,
        out_specs=[pl.BlockSpec((nb, tb, GROUP_W), lambda i, j: (i, j, 0)), st_out],
        out_shape=[jax.ShapeDtypeStruct((bsz, seq, GROUP_W), F32), st_shape],
        scratch_shapes=[pltpu.VMEM((nb, GROUP_W, GROUP_W), F32), pltpu.VMEM((GROUP_W, GROUP_W), F32)],
        input_output_aliases={} if alias is None else {n_fixed + alias: 1},
        compiler_params=_cparams(("parallel", "arbitrary")),
    )(proj, lb_logits, *st_ops)


def _ssd_kernel(*refs, c_seq, nseq, nblk, has_init, has_prev):
    p_ref, dt_ref, cw_ref, cb_ref, dtb_ref, alog_ref, dsk_ref, nw_ref = refs[:8]
    n_in = 8
    cv0_ref = s0_ref = None
    if has_init:
        cv0_ref, s0_ref = refs[n_in], refs[n_in + 1 + has_prev]
    y_ref, s_ref, cv_ref, xpad_ref = refs[n_in + 2 * (has_init + has_prev):]
    nb = s_ref.shape[0]
    tb = p_ref.shape[1]
    c = nseq * c_seq
    n_tail = SSD_CONV - 1
    lo = SUBLANES - n_tail

    @pl.when(pl.program_id(1) == 0)
    def _():
        if has_init:
            s_ref[...] = s0_ref[...]
            xpad_ref[:, lo:SUBLANES, :] = cv0_ref[...]
        else:
            s_ref[...] = jnp.zeros_like(s_ref)
            xpad_ref[:, lo:SUBLANES, :] = jnp.zeros((nb, n_tail, SSD_CONV_DIM), F32)

    r_i, r2_i, _, tri = _seq_grid(c, c_seq, 1)
    trif = tri.astype(F32)
    later = r2_i > r_i
    if nseq > 1:
        later = jnp.logical_and(r_i // c_seq == r2_i // c_seq, later)
    laterf = later.astype(F32)
    head = _head_of_lane(c)
    m_grp = jnp.where(_block_mask(LANES), 1.0 / LANES, 0.0).astype(BF16)
    cw = cw_ref[...]
    groups = [slice(g * LANES, (g + 1) * LANES) for g in range(SSD_GROUPS)]

    xpad_ref[:, SUBLANES:SUBLANES + tb, :] = p_ref[:, :, GROUP_W:GROUP_W + SSD_CONV_DIM]
    for bs, ts, seqs in _row_blocks(nseq, c_seq, nblk):
        r0 = ts.start
        load = lambda ref, col, w: ref[bs, ts, col:col + w].reshape(c, w)
        z = load(p_ref, 0, GROUP_W)
        conv = cb_ref[...] + cw[n_tail:n_tail + 1, :] * load(p_ref, GROUP_W, SSD_CONV_DIM)
        for w in range(n_tail):
            win = xpad_ref[bs, r0 + lo + w:r0 + lo + w + c_seq, :]
            conv = conv + cw[w:w + 1, :] * win.reshape(c, SSD_CONV_DIM)
        xc = _silu(conv)
        xs = xc[:, 0:GROUP_W]
        bm = xc[:, GROUP_W:2 * GROUP_W]
        cm = xc[:, 2 * GROUP_W:3 * GROUP_W]

        delta = jax.nn.softplus(load(dt_ref, 0, GROUP_W) + dtb_ref[...])
        log_a = -delta * jnp.exp(alog_ref[...])
        bcum = _mm_f32(trif, log_a)
        brem = bcum[c - 1:c, :] - bcum if nseq == 1 else _mm_f32(laterf, log_a)
        xd = xs * delta
        xr = xd * jnp.exp(brem)
        brow = [bcum[:, gl].T for gl in groups]
        gram = [_mm_nt(cm[:, gl], bm[:, gl]) for gl in groups]
        w_parts = []
        for h in range(HEADS):
            g, jh = divmod(h, HEADS // SSD_GROUPS)
            b_t = bcum[:, h * HEAD_W:h * HEAD_W + 1]
            b_s = brow[g][jh * HEAD_W:jh * HEAD_W + 1, :]
            w_parts.append(gram[g] * jnp.where(tri, jnp.exp(jnp.where(tri, b_t - b_s, 0.0)), 0.0))
        o = _mm(jnp.concatenate(w_parts, axis=1), _expand_heads(xd, head))
        cross = []
        for i, b in enumerate(seqs):
            sl = slice(i * c_seq, (i + 1) * c_seq)
            last = (i + 1) * c_seq - 1
            parts = []
            for g, gl in enumerate(groups):
                s_prev = s_ref[b, g]
                parts.append(_mm_nt(cm[sl, gl], s_prev))
                a_col = jnp.exp(brow[g][:, last:last + 1])
                s_ref[b, g] = s_prev * a_col + _mm_tn(xr[sl, gl], bm[sl, gl])
            cross.append(jnp.concatenate(parts, axis=1))
        o = o + jnp.exp(bcum) * _stack_rows(cross)
        y = (o + dsk_ref[...] * xs) * _silu(z)
        y = y * lax.rsqrt(_seg_mean(y * y, m_grp) + EPS) * nw_ref[...]
        y_ref[bs, ts, :] = y.reshape(len(seqs), c_seq, GROUP_W)
    tail = xpad_ref[:, tb + lo:tb + SUBLANES, :]
    cv_ref[...] = tail
    xpad_ref[:, lo:SUBLANES, :] = tail


def _ssd(proj, prm, init_cv, init_s, prev_cv, prev_s, l, c_seq, nseq, nblk):
    bsz, seq, _ = proj.shape
    nb, tb, grid = _mixer_grid(bsz, seq, c_seq, nseq, nblk)
    cv_in, cv_ops, cv_out, cv_shape, cv_alias = _state_io(l, nb, (SSD_CONV - 1, SSD_CONV_DIM), bsz,
                                                          init_cv, prev_cv)
    s_in, s_ops, s_out, s_shape, s_alias = _state_io(l, nb, (SSD_GROUPS, LANES, SSD_STATE), bsz,
                                                     init_s, prev_s)
    n_fixed = 8
    aliases = {}
    if cv_alias is not None:
        aliases[n_fixed + cv_alias] = 2
        aliases[n_fixed + len(cv_in) + s_alias] = 1
    return pl.pallas_call(
        functools.partial(_ssd_kernel, c_seq=c_seq, nseq=nseq, nblk=nblk, has_init=init_s is not None,
                          has_prev=prev_s is not None),
        grid=grid,
        in_specs=[pl.BlockSpec((nb, tb, 4 * GROUP_W), lambda i, j: (i, j, COL_SSD // (4 * GROUP_W))),
                  pl.BlockSpec((nb, tb, GROUP_W), lambda i, j: (i, j, COL_DT // GROUP_W)),
                  _layer_spec(l, (SSD_CONV, SSD_CONV_DIM), 2),
                  _layer_spec(l, (1, SSD_CONV_DIM), 2),
                  _layer_spec(l, (1, GROUP_W), 2),
                  _layer_spec(l, (1, GROUP_W), 2),
                  _layer_spec(l, (1, GROUP_W), 2),
                  _layer_spec(l, (1, GROUP_W), 2)] + cv_in + s_in,
        out_specs=[pl.BlockSpec((nb, tb, GROUP_W), lambda i, j: (i, j, 0)), s_out, cv_out],
        out_shape=[jax.ShapeDtypeStruct((bsz, seq, GROUP_W), F32), s_shape, cv_shape],
        scratch_shapes=[pltpu.VMEM((nb, tb + SUBLANES, SSD_CONV_DIM), F32)],
        input_output_aliases=aliases,
        compiler_params=_cparams(("parallel", "arbitrary")),
    )(proj, proj, prm["conv_w"], prm["conv_b"], prm["dt_bias"], prm["a_log"], prm["ssd_d"], prm["ssd_norm"],
      *cv_ops, *s_ops)


def _ssd_state_in(s):
    d, bsz = s.shape[:2]
    return jnp.swapaxes(s, 3, 4).reshape(d, bsz, SSD_GROUPS, LANES, SSD_STATE)


def _ssd_state_out(s):
    d, bsz = s.shape[:2]
    return jnp.swapaxes(s.reshape(d, bsz, HEADS, HEAD_W, SSD_STATE), 3, 4)


def _s5_kernel(*refs, tl, unroll, has_init, has_prev):
    (u0_ref, u1_ref, are_ref, aim_ref, ldt_ref, bre_ref, bim_ref, cw_ref, dsk_ref, gw_ref,
     gb_ref) = refs[:11]
    n_in = 11
    s0r_ref = s0i_ref = None
    if has_init:
        s0r_ref, s0i_ref = refs[n_in], refs[n_in + 1 + has_prev]
    (y_ref, sr_ref, si_ref, wb_ref, ab_ref, us0_ref, us1_ref, ut_ref, bu_ref, yt_ref, ys0_ref,
     ys1_ref) = refs[n_in + 2 * (has_init + has_prev):]
    nbt = SUBLANES * tl

    @pl.when(pl.program_id(1) == 0)
    def _():
        if has_init:
            sr_ref[...] = s0r_ref[...]
            si_ref[...] = s0i_ref[...]
        else:
            sr_ref[...] = jnp.zeros_like(sr_ref)
            si_ref[...] = jnp.zeros_like(si_ref)
        lr, li, st = are_ref[...], aim_ref[...], jnp.exp(ldt_ref[...])
        mag = jnp.exp(lr * st)
        abr = mag * jnp.cos(li * st)
        abi = mag * jnp.sin(li * st)
        ab_ref[0:1, :] = abr
        ab_ref[1:2, :] = abi
        den = lr * lr + li * li
        cr = ((abr - 1.0) * lr + abi * li) / den
        ci = (abi * lr - (abr - 1.0) * li) / den
        wb_ref[:, 0:S5_N] = (cr * bre_ref[...] - ci * bim_ref[...]).astype(BF16)
        wb_ref[:, S5_N:2 * S5_N] = (cr * bim_ref[...] + ci * bre_ref[...]).astype(BF16)

    us0_ref[...] = u0_ref[...].reshape(nbt, LANES)
    us1_ref[...] = u1_ref[...].reshape(nbt, LANES)

    def gather(t, carry):
        dst = pl.ds(pl.multiple_of(t * SUBLANES, SUBLANES), SUBLANES)
        src = pl.ds(t, SUBLANES, stride=tl)
        ut_ref[dst, 0:LANES] = us0_ref[src, :]
        ut_ref[dst, LANES:2 * LANES] = us1_ref[src, :]
        return carry

    lax.fori_loop(0, tl, gather, 0, unroll=unroll)
    u = ut_ref[...]
    bu_ref[...] = jnp.dot(u.astype(BF16), wb_ref[...], preferred_element_type=F32)
    ar = ab_ref[0:1, :]
    ai = ab_ref[1:2, :]

    def step(t, carry):
        xr, xi = carry
        rows = pl.ds(pl.multiple_of(t * SUBLANES, SUBLANES), SUBLANES)
        nr = ar * xr - ai * xi + bu_ref[rows, 0:S5_N]
        ni = ar * xi + ai * xr + bu_ref[rows, S5_N:2 * S5_N]
        bu_ref[rows, 0:S5_N] = nr
        bu_ref[rows, S5_N:2 * S5_N] = ni
        return nr, ni

    xr, xi = lax.fori_loop(0, tl, step, (sr_ref[...], si_ref[...]), unroll=unroll)
    sr_ref[...] = xr
    si_ref[...] = xi

    y = _mm(bu_ref[...], cw_ref[...]) + dsk_ref[...] * u
    y = jax.nn.gelu(y)
    yt_ref[...] = y * jax.nn.sigmoid(_mm(y, gw_ref[...]) + gb_ref[...])

    def scatter(t, carry):
        src = pl.ds(pl.multiple_of(t * SUBLANES, SUBLANES), SUBLANES)
        dst = pl.ds(t, SUBLANES, stride=tl)
        ys0_ref[dst, :] = yt_ref[src, 0:LANES]
        ys1_ref[dst, :] = yt_ref[src, LANES:2 * LANES]
        return carry

    lax.fori_loop(0, tl, scatter, 0, unroll=unroll)
    y_ref[:, :, 0:LANES] = ys0_ref[...].reshape(SUBLANES, tl, LANES)
    y_ref[:, :, LANES:2 * LANES] = ys1_ref[...].reshape(SUBLANES, tl, LANES)


def _s5(proj, prm, init_re, init_im, prev_re, prev_im, l, tl, unroll):
    bsz, seq, _ = proj.shape
    re_in, re_ops, re_out, re_shape, re_alias = _state_io(l, SUBLANES, (S5_N,), bsz, init_re, prev_re)
    im_in, im_ops, im_out, im_shape, im_alias = _state_io(l, SUBLANES, (S5_N,), bsz, init_im, prev_im)
    n_fixed = 11
    aliases = {}
    if re_alias is not None:
        aliases[n_fixed + re_alias] = 1
        aliases[n_fixed + len(re_in) + im_alias] = 2
    nbt = SUBLANES * tl
    return pl.pallas_call(
        functools.partial(_s5_kernel, tl=tl, unroll=unroll, has_init=init_re is not None,
                          has_prev=prev_re is not None),
        grid=(bsz // SUBLANES, seq // tl),
        in_specs=[pl.BlockSpec((SUBLANES, tl, LANES), lambda i, j: (i, j, COL_S5 // LANES)),
                  pl.BlockSpec((SUBLANES, tl, LANES), lambda i, j: (i, j, COL_S5 // LANES + 1)),
                  _layer_spec(l, (1, S5_N), 2), _layer_spec(l, (1, S5_N), 2), _layer_spec(l, (1, S5_N), 2),
                  _layer_spec(l, (GROUP_W, S5_N), 2), _layer_spec(l, (GROUP_W, S5_N), 2),
                  _layer_spec(l, (2 * S5_N, GROUP_W), 2),
                  _layer_spec(l, (1, GROUP_W), 2),
                  _layer_spec(l, (GROUP_W, GROUP_W), 2),
                  _layer_spec(l, (1, GROUP_W), 2)] + re_in + im_in,
        out_specs=[pl.BlockSpec((SUBLANES, tl, GROUP_W), lambda i, j: (i, j, 0)), re_out, im_out],
        out_shape=[jax.ShapeDtypeStruct((bsz, seq, GROUP_W), F32), re_shape, im_shape],
        scratch_shapes=[pltpu.VMEM((GROUP_W, 2 * S5_N), BF16),
                        pltpu.VMEM((SUBLANES, S5_N), F32),
                        pltpu.VMEM((nbt, LANES), F32), pltpu.VMEM((nbt, LANES), F32),
                        pltpu.VMEM((nbt, GROUP_W), F32),
                        pltpu.VMEM((nbt, 2 * S5_N), F32),
                        pltpu.VMEM((nbt, GROUP_W), F32),
                        pltpu.VMEM((nbt, LANES), F32), pltpu.VMEM((nbt, LANES), F32)],
        input_output_aliases=aliases,
        compiler_params=_cparams(("parallel", "arbitrary")),
    )(proj, proj, prm["a_re"], prm["a_im"], prm["log_dt"], prm["b_re"], prm["b_im"], prm["c_blk"],
      prm["s5_d"], prm["glu_w"], prm["glu_b"], *re_ops, *im_ops)


def _block_diag(blocks):
    d, g, r, c = blocks.shape
    eye = jnp.eye(g, dtype=blocks.dtype)
    return (eye[None, :, None, :, None] * blocks[:, :, :, None, :]).reshape(d, g * r, g * c)


def _prep(ssd_conv_w, ssd_conv_b, ssd_dt_bias, ssd_a_log, ssd_d, ssd_norm, s5_a_re, s5_a_im, s5_log_dt,
          s5_b_re, s5_b_im, s5_c_re, s5_c_im, s5_d, s5_glu_w, s5_glu_b):
    per_head = lambda v: jnp.repeat(v, HEAD_W, axis=1)[:, None, :]
    c_blk = jnp.concatenate([_block_diag(jnp.swapaxes(s5_c_re, 2, 3)),
                             -_block_diag(jnp.swapaxes(s5_c_im, 2, 3))], axis=1)
    return dict(
        conv_w=ssd_conv_w, conv_b=ssd_conv_b[:, None, :],
        dt_bias=per_head(ssd_dt_bias), a_log=per_head(ssd_a_log), ssd_d=per_head(ssd_d),
        ssd_norm=ssd_norm[:, None, :],
        a_re=s5_a_re.reshape(DEPTH, 1, S5_N), a_im=s5_a_im.reshape(DEPTH, 1, S5_N),
        log_dt=jnp.repeat(s5_log_dt, S5_STATE, axis=1)[:, None, :],
        b_re=_block_diag(jnp.swapaxes(s5_b_re, 2, 3)), b_im=_block_diag(jnp.swapaxes(s5_b_im, 2, 3)),
        c_blk=c_blk.astype(BF16),
        s5_d=s5_d[:, None, :], glu_w=s5_glu_w.astype(BF16), glu_b=s5_glu_b[:, None, :],
    )


def _trunk(x, pos0, init, prm, w_in, w_out, w_up, w_down, norm_mix, norm_mlp, norm_final, lb_logits,
           angle_row, log_gamma, cfg):
    bsz, seq, _ = x.shape
    c_ret, nseq_ret, _ = cfg["ret"]
    cos, sin = _rope_tables(angle_row, pos0, seq if nseq_ret == 1 else nseq_ret * c_ret, seq, cfg["rope_tl"])
    x2d = x.reshape(bsz * seq, D_MODEL)
    i_ret, i_hg, i_ssd, i_cv, i_re, i_im = init if init is not None else (None,) * 6
    if init is not None:
        i_ssd = _ssd_state_in(i_ssd)
        i_re = i_re.reshape(DEPTH, bsz, S5_N)
        i_im = i_im.reshape(DEPTH, bsz, S5_N)
    s_ret = s_hg = s_ssd = s_cv = s_re = s_im = None
    for l in range(DEPTH):
        proj = _inproj(x2d, norm_mix, w_in, l, cfg["tm_in"]).reshape(bsz, seq, IN_COLS_PAD)
        ya, s_ret = _retention(proj, cos, sin, log_gamma, i_ret, s_ret, l, *cfg["ret"])
        yb, s_hg = _hgrn2(proj, lb_logits, i_hg, s_hg, l, *cfg["hg"])
        yc, s_ssd, s_cv = _ssd(proj, prm, i_cv, i_ssd, s_cv, s_ssd, l, *cfg["ssd"])
        yd, s_re, s_im = _s5(proj, prm, i_re, i_im, s_re, s_im, l, *cfg["s5"])
        ys = [y.reshape(bsz * seq, GROUP_W) for y in (ya, yb, yc, yd)]
        x2d = _outmlp(x2d, ys, w_out, norm_mlp, w_up, w_down, norm_final, l, cfg["tm_out"])
    return (x2d.reshape(bsz, seq, D_MODEL), s_ret, s_hg, _ssd_state_out(s_ssd), s_cv,
            s_re.reshape(DEPTH, bsz, S5_GROUPS, S5_STATE), s_im.reshape(DEPTH, bsz, S5_GROUPS, S5_STATE))


PROMPT_CFG = dict(rope_tl=256, tm_in=512, ret=(256, 1, 4), hg=(64, 1, 8), ssd=(128, 1, 8), s5=(128, 8),
                  tm_out=512)
SAMPLE_CFG = dict(rope_tl=128, tm_in=512, ret=(8, 16, 2), hg=(8, 16, 2), ssd=(8, 16, 2), s5=(8, 8),
                  tm_out=512)


def kernel(x_prompt, x_sample, state_ret, state_hgrn, state_ssd, state_ssd_conv, state_s5_re, state_s5_im, norm_mix, w_in, w_out, hg_lb_logits, ssd_conv_w, ssd_conv_b, ssd_dt_bias, ssd_a_log, ssd_d, ssd_norm, s5_a_re, s5_a_im, s5_log_dt, s5_b_re, s5_b_im, s5_c_re, s5_c_im, s5_d, s5_glu_w, s5_glu_b, norm_mlp, w_up, w_down, norm_final):
    prm = _prep(ssd_conv_w, ssd_conv_b, ssd_dt_bias, ssd_a_log, ssd_d, ssd_norm, s5_a_re, s5_a_im, s5_log_dt,
                s5_b_re, s5_b_im, s5_c_re, s5_c_im, s5_d, s5_glu_w, s5_glu_b)
    w_out_b = _cast_bf16(w_out, 512)
    w_up_b = _cast_bf16(w_up, 256)
    w_down_b = _cast_bf16(w_down, 1024)
    angle = 1.0 / (ROPE_BASE ** jnp.linspace(0.0, 1.0, HEAD_W // 2, dtype=F32))
    angle_row = jnp.tile(jnp.repeat(angle, 2), 2)[None, :]
    log_gamma = jnp.log(1.0 - jnp.exp2(-5.0 - jnp.arange(HEADS, dtype=F32)))
    shared = (prm, w_in, w_out_b, w_up_b, w_down_b, norm_mix[:, None, :], norm_mlp[:, None, :],
              norm_final[None, :], hg_lb_logits, angle_row, log_gamma)
    states = (state_ret, state_hgrn, state_ssd, state_ssd_conv, state_s5_re, state_s5_im)
    out_p = _trunk(x_prompt, 0, None, *shared, PROMPT_CFG)
    out_s = _trunk(x_sample, PAST_LEN, states, *shared, SAMPLE_CFG)
    return (out_p[0], out_s[0]) + out_p[1:] + out_s[1:]
```

```python
import functools
import itertools

import jax
import jax.numpy as jnp
from jax import lax
from jax.experimental import pallas as pl
from jax.experimental.pallas import tpu as pltpu

F32 = jnp.float32
BF16 = jnp.bfloat16

D_MODEL = 1024
DEPTH = 2
PAST_LEN = 16384
GROUP_W = 256
HEADS = 4
HEAD_W = 64
PAIR_HEADS = 2
ROPE_BASE = 10000.0
HG_MIN_FORGET = 1e-30
HG_SAFE_SPREAD = 80.0
SSD_STATE = 128
SSD_GROUPS = 2
SSD_CONV = 4
SSD_CONV_DIM = 768
S5_GROUPS = 16
S5_STATE = 64
S5_N = S5_GROUPS * S5_STATE
D_FF = 4096
IN_COLS = 3332
EPS = 1e-6

COL_RET = 0
COL_HG = 1024
COL_SSD = 2048
COL_S5 = 3072
COL_DT = 3328
IN_COLS_PAD = 3584

LANES = 128
SUBLANES = 8
VMEM_LIMIT = 56 * 1024 * 1024


def _cparams(sem):
    return pltpu.CompilerParams(dimension_semantics=sem, vmem_limit_bytes=VMEM_LIMIT)


def _mm(a, b):
    return jnp.dot(a.astype(BF16), b.astype(BF16), preferred_element_type=F32)


def _mm_nt(a, b):
    return lax.dot_general(a.astype(BF16), b.astype(BF16), (((1,), (1,)), ((), ())),
                           preferred_element_type=F32)


def _mm_tn(a, b):
    return lax.dot_general(a.astype(BF16), b.astype(BF16), (((0,), (0,)), ((), ())),
                           preferred_element_type=F32)


def _mm_f32(a, b):
    return jnp.dot(a, b, preferred_element_type=F32, precision=lax.Precision.HIGHEST)


def _silu(x):
    return x * jax.nn.sigmoid(x)


def _rms(x, w):
    return x * lax.rsqrt(jnp.mean(x * x, axis=-1, keepdims=True) + EPS) * w


def _tri(c):
    t = lax.broadcasted_iota(jnp.int32, (c, c), 0)
    s = lax.broadcasted_iota(jnp.int32, (c, c), 1)
    return t >= s


def _head_of_lane(rows, width=GROUP_W):
    return lax.broadcasted_iota(jnp.int32, (rows, width), 1) // HEAD_W


def _expand_heads(x, head):
    nheads = x.shape[1] // HEAD_W
    return jnp.concatenate([jnp.where(head == h, x, 0.0) for h in range(nheads)], axis=0).astype(BF16)


def _block_mask(seg, width=GROUP_W):
    r = lax.broadcasted_iota(jnp.int32, (width, width), 0) // seg
    c = lax.broadcasted_iota(jnp.int32, (width, width), 1) // seg
    return r == c


def _seg_mean(x, m):
    hi = x.astype(BF16)
    lo = (x - hi.astype(F32)).astype(BF16)
    return jnp.dot(hi, m, preferred_element_type=F32) + jnp.dot(lo, m, preferred_element_type=F32)


def _row_blocks(nseq, c_seq, nblk):
    if nseq == 1:
        return [(slice(0, 1), slice(k * c_seq, (k + 1) * c_seq), [0]) for k in range(nblk)]
    return [(slice(k * nseq, (k + 1) * nseq), slice(0, c_seq), list(range(k * nseq, (k + 1) * nseq)))
            for k in range(nblk)]


def _seq_grid(rows, c_seq, copies):
    r = lax.broadcasted_iota(jnp.int32, (rows, copies * rows), 0)
    j = lax.broadcasted_iota(jnp.int32, (rows, copies * rows), 1)
    h = j // rows
    r2 = j - h * rows
    causal = r >= r2
    if rows > c_seq:
        causal = jnp.logical_and(r // c_seq == r2 // c_seq, causal)
    return r, r2, h, causal


def _stack_rows(pieces):
    return pieces[0] if len(pieces) == 1 else jnp.concatenate(pieces, axis=0)


def _mixer_grid(bsz, seq, c_seq, nseq, nblk):
    nb = 1 if nseq == 1 else nseq * nblk
    tb = c_seq * nblk if nseq == 1 else c_seq
    assert nseq == 1 or seq == c_seq
    return nb, tb, (bsz // nb, seq // tb)


def _layer_spec(l, tail, n_grid, single=False):
    zeros = (0,) * len(tail)
    kw = dict(pipeline_mode=pl.Buffered(1)) if single else {}
    if n_grid == 1:
        return pl.BlockSpec((None,) + tail, lambda i: (l,) + zeros, **kw)
    return pl.BlockSpec((None,) + tail, lambda i, j: (l,) + zeros, **kw)


def _state_spec(l, nb, tail):
    zeros = (0,) * len(tail)
    return pl.BlockSpec((None, nb) + tail, lambda i, j: (l, i) + zeros)


def _state_io(l, nb, tail, bsz, init, prev):
    spec = _state_spec(l, nb, tail)
    in_specs, operands = [], []
    if init is not None:
        in_specs.append(spec)
        operands.append(init)
    alias_pos = None
    if prev is not None:
        alias_pos = len(in_specs)
        in_specs.append(pl.BlockSpec(memory_space=pl.ANY))
        operands.append(prev)
    return in_specs, operands, spec, jax.ShapeDtypeStruct((DEPTH, bsz) + tail, F32), alias_pos


def _cast_kernel(w_ref, o_ref):
    o_ref[...] = w_ref[...].astype(BF16)


def _cast_bf16(w, tr):
    d, r, c = w.shape
    spec = pl.BlockSpec((None, tr, c), lambda l, i: (l, i, 0))
    return pl.pallas_call(
        _cast_kernel, grid=(d, r // tr), in_specs=[spec], out_specs=spec,
        out_shape=jax.ShapeDtypeStruct(w.shape, BF16),
        compiler_params=_cparams(("parallel", "parallel")),
    )(w)


def _inproj_kernel(x_ref, nw_ref, w_ref, o_ref, wb_ref):
    @pl.when(pl.program_id(0) == 0)
    def _():
        rows = 256
        head = _head_of_lane(rows)
        for r in range(0, D_MODEL, rows):
            rs = slice(r, r + rows)
            wb_ref[rs, 0:COL_S5] = w_ref[rs, 0:COL_S5].astype(BF16)
            tail = w_ref[rs, COL_S5:IN_COLS]
            wb_ref[rs, COL_S5:COL_DT] = tail[:, HEADS:HEADS + GROUP_W].astype(BF16)
            dtw = tail[:, HEADS - 1:HEADS]
            for h in range(HEADS - 2, -1, -1):
                dtw = jnp.where(head == h, tail[:, h:h + 1], dtw)
            wb_ref[rs, COL_DT:IN_COLS_PAD] = dtw.astype(BF16)

    h = _rms(x_ref[...], nw_ref[...])
    o_ref[...] = jnp.dot(h.astype(BF16), wb_ref[...], preferred_element_type=F32)


def _inproj(x2d, norm_w, w_in, l, tm):
    t = x2d.shape[0]
    return pl.pallas_call(
        _inproj_kernel,
        grid=(t // tm,),
        in_specs=[pl.BlockSpec((tm, D_MODEL), lambda i: (i, 0)),
                  _layer_spec(l, (1, D_MODEL), 1),
                  _layer_spec(l, (D_MODEL, IN_COLS), 1, single=True)],
        out_specs=pl.BlockSpec((tm, IN_COLS_PAD), lambda i: (i, 0)),
        out_shape=jax.ShapeDtypeStruct((t, IN_COLS_PAD), F32),
        scratch_shapes=[pltpu.VMEM((D_MODEL, IN_COLS_PAD), BF16)],
        compiler_params=_cparams(("arbitrary",)),
    )(x2d, norm_w, w_in)


def _outmlp_kernel(x_ref, ya_ref, yb_ref, yc_ref, yd_ref, wo_ref, nw_ref, wu_ref, wd_ref, nf_ref,
                   o_ref, h_ref, u_ref, *, final_norm, tn_up, tn_down):
    x1 = x_ref[...]
    for g, y_ref in enumerate((ya_ref, yb_ref, yc_ref, yd_ref)):
        x1 = x1 + _mm(y_ref[...], wo_ref[g * GROUP_W:(g + 1) * GROUP_W, :])
    o_ref[...] = x1
    h_ref[...] = _rms(x1, nw_ref[...]).astype(BF16)
    for n in range(0, D_FF, tn_up):
        u = jnp.dot(h_ref[...], wu_ref[:, n:n + tn_up], preferred_element_type=F32)
        u_ref[:, n:n + tn_up] = jnp.square(jnp.maximum(u, 0.0)).astype(BF16)
    for n in range(0, D_MODEL, tn_down):
        o_ref[:, n:n + tn_down] += jnp.dot(u_ref[...], wd_ref[:, n:n + tn_down], preferred_element_type=F32)
    if final_norm:
        o_ref[...] = _rms(o_ref[...], nf_ref[...])


def _outmlp(x2d, ys, w_out, norm_w, w_up, w_down, norm_f, l, tm):
    t = x2d.shape[0]
    row = lambda i: (i, 0)
    return pl.pallas_call(
        functools.partial(_outmlp_kernel, final_norm=(l == DEPTH - 1), tn_up=512, tn_down=256),
        grid=(t // tm,),
        in_specs=[pl.BlockSpec((tm, D_MODEL), row)]
        + [pl.BlockSpec((tm, GROUP_W), row)] * 4
        + [_layer_spec(l, (D_MODEL, D_MODEL), 1, single=True),
           _layer_spec(l, (1, D_MODEL), 1),
           _layer_spec(l, (D_MODEL, D_FF), 1, single=True),
           _layer_spec(l, (D_FF, D_MODEL), 1, single=True),
           pl.BlockSpec((1, D_MODEL), lambda i: (0, 0))],
        out_specs=pl.BlockSpec((tm, D_MODEL), row),
        out_shape=jax.ShapeDtypeStruct((t, D_MODEL), F32),
        scratch_shapes=[pltpu.VMEM((tm, D_MODEL), BF16),
                        pltpu.VMEM((tm, D_FF), BF16)],
        compiler_params=_cparams(("parallel",)),
    )(x2d, *ys, w_out, norm_w, w_up, w_down, norm_f)


def _rope_kernel(angle_ref, cos_ref, sin_ref, *, pos0, tl, period):
    i = pl.program_id(0)
    row = i * tl + lax.broadcasted_iota(jnp.int32, (tl, 2 * HEAD_W), 0)
    pos = (pos0 + row % period).astype(F32)
    theta = pos * angle_ref[...]
    lane = lax.broadcasted_iota(jnp.int32, (tl, 2 * HEAD_W), 1)
    cos_ref[...] = jnp.cos(theta)
    s = jnp.sin(theta)
    sin_ref[...] = jnp.where(lane % 2 == 0, -s, s)


def _rope_tables(angle_row, pos0, rows, period, tl):
    seq = rows
    return pl.pallas_call(
        functools.partial(_rope_kernel, pos0=pos0, tl=tl, period=period),
        grid=(seq // tl,),
        in_specs=[pl.BlockSpec((1, 2 * HEAD_W), lambda i: (0, 0))],
        out_specs=[pl.BlockSpec((tl, 2 * HEAD_W), lambda i: (i, 0))] * 2,
        out_shape=[jax.ShapeDtypeStruct((seq, 2 * HEAD_W), F32)] * 2,
        compiler_params=_cparams(("parallel",)),
    )(angle_row)


def _ret_kernel(*refs, c_seq, nseq, nblk, has_init, has_prev):
    lg_ref, p_ref, cos_ref, sin_ref = refs[:4]
    s0_ref = refs[4] if has_init else None
    y_ref, s_ref, sb_ref = refs[4 + has_init + has_prev:]
    nb = s_ref.shape[0]
    c = nseq * c_seq
    j = pl.program_id(1)

    @pl.when(j == 0)
    def _():
        sb_ref[...] = jnp.zeros_like(sb_ref)
        if has_init:
            for b in range(nb):
                for h in range(HEADS):
                    hs = slice(h * HEAD_W, (h + 1) * HEAD_W)
                    sb_ref[b, hs, hs] = s0_ref[b, h]

    def per_head(idx, shape):
        out = jnp.full(shape, lg_ref[HEADS - 1], F32)
        for h in range(HEADS - 2, -1, -1):
            out = jnp.where(idx == h, lg_ref[h], out)
        return out

    head = _head_of_lane(c)
    bd = _block_mask(HEAD_W)
    m_seg = jnp.where(bd, 1.0 / HEAD_W, 0.0).astype(BF16)
    even = lax.broadcasted_iota(jnp.int32, (c, GROUP_W), 1) % 2 == 0
    lg_lane = per_head(head[0:1, :], (1, GROUP_W))
    t_loc = (lax.broadcasted_iota(jnp.int32, (c, 1), 0) % c_seq).astype(F32)
    e_q = jnp.exp((t_loc + 1.0) * lg_lane)
    e_k = jnp.exp((c_seq - 1.0 - t_loc) * lg_lane)
    a_c = jnp.exp(c_seq * lg_lane)
    r_i, r2_i, h_i, causal = _seq_grid(c, c_seq, HEADS)
    dist = jnp.where(causal, (r_i - r2_i).astype(F32), 0.0)
    decay = jnp.where(causal, jnp.exp(dist * per_head(h_i, (c, HEADS * c))), 0.0)

    def rot(x, cos, sin):
        swapped = jnp.where(even, pltpu.roll(x, GROUP_W - 1, 1), pltpu.roll(x, 1, 1))
        return x * cos + swapped * sin

    for bs, ts, seqs in _row_blocks(nseq, c_seq, nblk):
        load = lambda lo: p_ref[bs, ts, lo:lo + GROUP_W].reshape(c, GROUP_W)
        crow = ts if nseq == 1 else slice(0, c)
        cos = jnp.concatenate([cos_ref[crow, :]] * 2, axis=1)
        sin = jnp.concatenate([sin_ref[crow, :]] * 2, axis=1)
        q = rot(load(0), cos, sin)
        k = rot(load(GROUP_W), cos, sin) * (HEAD_W ** -0.5)
        v = load(2 * GROUP_W)
        g = load(3 * GROUP_W)
        scores = _mm_nt(q, _expand_heads(k, head)) * decay
        o = _mm(scores, _expand_heads(v, head))
        qe = q * e_q
        ke = k * e_k
        cross = []
        for i, b in enumerate(seqs):
            sl = slice(i * c_seq, (i + 1) * c_seq)
            s_prev = sb_ref[b]
            cross.append(_mm(qe[sl], s_prev))
            sb_ref[b] = a_c * s_prev + jnp.where(bd, _mm_tn(ke[sl], v[sl]), 0.0)
        o = o + _stack_rows(cross)
        mu = _seg_mean(o, m_seg)
        d = o - mu
        var = _seg_mean(d * d, m_seg)
        y_ref[bs, ts, :] = (d * lax.rsqrt(var + EPS) * _silu(g)).reshape(len(seqs), c_seq, GROUP_W)

    @pl.when(j == pl.num_programs(1) - 1)
    def _():
        for b in range(nb):
            for h in range(HEADS):
                hs = slice(h * HEAD_W, (h + 1) * HEAD_W)
                s_ref[b, h] = sb_ref[b, hs, hs]


def _retention(proj, cos, sin, log_gamma, init, prev, l, c_seq, nseq, nblk):
    bsz, seq, _ = proj.shape
    nb, tb, grid = _mixer_grid(bsz, seq, c_seq, nseq, nblk)
    st_in, st_ops, st_out, st_shape, alias = _state_io(l, nb, (HEADS, HEAD_W, HEAD_W), bsz, init, prev)
    n_fixed = 4
    rope_spec = (pl.BlockSpec((tb, 2 * HEAD_W), lambda i, j: (j, 0)) if nseq == 1 else
                 pl.BlockSpec((nseq * c_seq, 2 * HEAD_W), lambda i, j: (0, 0)))
    return pl.pallas_call(
        functools.partial(_ret_kernel, c_seq=c_seq, nseq=nseq, nblk=nblk, has_init=init is not None,
                          has_prev=prev is not None),
        grid=grid,
        in_specs=[pl.BlockSpec(memory_space=pltpu.SMEM),
                  pl.BlockSpec((nb, tb, 4 * GROUP_W), lambda i, j: (i, j, COL_RET // (4 * GROUP_W))),
                  rope_spec, rope_spec] + st_in,
        out_specs=[pl.BlockSpec((nb, tb, GROUP_W), lambda i, j: (i, j, 0)), st_out],
        out_shape=[jax.ShapeDtypeStruct((bsz, seq, GROUP_W), F32), st_shape],
        scratch_shapes=[pltpu.VMEM((nb, GROUP_W, GROUP_W), F32)],
        input_output_aliases={} if alias is None else {n_fixed + alias: 1},
        compiler_params=_cparams(("parallel", "arbitrary")),
    )(log_gamma, proj, cos, sin, *st_ops)


def _hg_kernel(*refs, c_seq, nseq, nblk, layer, has_init, has_prev):
    p_ref, lb_ref = refs[:2]
    s0_ref = refs[2] if has_init else None
    y_ref, s_ref, st_ref, keep_ref = refs[2 + has_init + has_prev:]
    nb = s_ref.shape[0]
    c = nseq * c_seq
    j = pl.program_id(1)

    @pl.when(j == 0)
    def _():
        st_ref[...] = jnp.zeros_like(st_ref)
        if has_init:
            for b in range(nb):
                for h in range(HEADS):
                    hs = slice(h * HEAD_W, (h + 1) * HEAD_W)
                    st_ref[b, hs, hs] = s0_ref[b, h].T

    logits = lb_ref[...]
    e = jnp.exp(logits - jnp.max(logits, axis=0, keepdims=True))
    sm = e / jnp.sum(e, axis=0, keepdims=True)
    csum = sm[0:1, :]
    for i in range(1, layer + 1):
        csum = csum + sm[i:i + 1, :]
    lower = csum - sm[0:1, :]

    head = _head_of_lane(c)
    bd = _block_mask(HEAD_W)
    m_seg = jnp.where(bd, 1.0 / HEAD_W, 0.0).astype(BF16)
    r_i, r2_i, _, tril = _seq_grid(c, c_seq, 1)
    trif = tril.astype(F32)
    later = r2_i > r_i
    if nseq > 1:
        later = jnp.logical_and(r_i // c_seq == r2_i // c_seq, later)
    laterf = later.astype(F32)
    causal = _seq_grid(c, c_seq, HEADS)[3]
    mid = c // 2 - 1

    staged = []
    spreads = []
    for bs, ts, seqs in _row_blocks(nseq, c_seq, nblk):
        load = lambda lo: p_ref[bs, ts, lo:lo + GROUP_W].reshape(c, GROUP_W)
        qq = _silu(load(0))
        ff = load(GROUP_W)
        v = load(2 * GROUP_W)
        g = load(3 * GROUP_W)
        sig = jax.nn.sigmoid(ff)
        forget = lower + (1.0 - lower) * sig
        log_f = jnp.log(jnp.maximum(forget, HG_MIN_FORGET))
        key = (1.0 - lower) * (1.0 - sig)
        bcum = _mm_f32(trif, log_f)
        if nseq == 1:
            b_ref = bcum[mid:mid + 1, :]
            spread = bcum - b_ref
            q_in = qq * jnp.exp(spread)
            k_in = key * jnp.exp(-spread)
            q_x = q_in * jnp.exp(b_ref)
            k_x = k_in * jnp.exp(bcum[c - 1:c, :] - b_ref)
        else:
            spread = bcum
            q_in = q_x = qq * jnp.exp(bcum)
            k_in = key * jnp.exp(-bcum)
            k_x = key * jnp.exp(_mm_f32(laterf, log_f))
        spreads.append(jnp.max(jnp.abs(spread)))
        scores = jnp.where(causal, _mm_nt(q_in, _expand_heads(k_in, head)), 0.0)
        o = _mm(scores, _expand_heads(v, head))
        upd = []
        for i in range(len(seqs)):
            sl = slice(i * c_seq, (i + 1) * c_seq)
            a_seq = jnp.exp(bcum[(i + 1) * c_seq - 1:(i + 1) * c_seq, :])
            upd.append((q_x[sl], a_seq, jnp.where(bd, _mm_tn(v[sl], k_x[sl]), 0.0)))
        staged.append((bs, ts, seqs, o, upd, _silu(g)))

    unsafe = functools.reduce(jnp.maximum, spreads) > HG_SAFE_SPREAD
    if nb == 1:
        keep_ref[...] = st_ref[0]

    for bs, ts, seqs, o, upd, gate in staged:
        cross = []
        for b, (q_b, a_seq, ds) in zip(seqs, upd):
            st_prev = st_ref[b]
            cross.append(_mm_nt(q_b, st_prev))
            st_ref[b] = st_prev * a_seq + ds
        o = o + _stack_rows(cross)
        y = o * lax.rsqrt(_seg_mean(o * o, m_seg) + EPS) * gate
        y_ref[bs, ts, :] = y.reshape(len(seqs), c_seq, GROUP_W)

    @pl.when(unsafe)
    def _():
        tb = p_ref.shape[1]
        assert nb == 1 or (tb == SUBLANES and pl.num_programs(1) == 1)
        rowid = lax.broadcasted_iota(jnp.int32, (SUBLANES, GROUP_W), 0)
        if nb == 1:
            st_ref[0] = keep_ref[...]

        def slab(s, carry):
            b, t0 = (0, pl.multiple_of(s * SUBLANES, SUBLANES)) if nb == 1 else (s, 0)
            rows = pl.ds(t0, SUBLANES)
            if nb > 1:
                st_ref[b] = jnp.zeros((GROUP_W, GROUP_W), F32)
                if has_init:
                    for h in range(HEADS):
                        hs = slice(h * HEAD_W, (h + 1) * HEAD_W)
                        st_ref[b, hs, hs] = s0_ref[b, h].T
            q8 = _silu(p_ref[b, rows, 0:GROUP_W])
            sig = jax.nn.sigmoid(p_ref[b, rows, GROUP_W:2 * GROUP_W])
            v8 = p_ref[b, rows, 2 * GROUP_W:3 * GROUP_W]
            g8 = p_ref[b, rows, 3 * GROUP_W:4 * GROUP_W]
            f8 = jnp.maximum(lower + (1.0 - lower) * sig, HG_MIN_FORGET)
            k8 = (1.0 - lower) * (1.0 - sig)
            o8 = jnp.zeros((SUBLANES, GROUP_W), F32)
            for i in range(SUBLANES):
                outer = _mm_tn(jnp.where(rowid == i, v8, 0.0), k8)
                st = st_ref[b] * f8[i:i + 1, :] + jnp.where(bd, outer, 0.0)
                st_ref[b] = st
                o8 = jnp.where(rowid == i, _mm_nt(q8, st), o8)
            y_ref[b, rows, :] = o8 * lax.rsqrt(_seg_mean(o8 * o8, m_seg) + EPS) * _silu(g8)
            return carry

        lax.fori_loop(0, nb * tb // SUBLANES, slab, 0)

    @pl.when(j == pl.num_programs(1) - 1)
    def _():
        for b in range(nb):
            for h in range(HEADS):
                hs = slice(h * HEAD_W, (h + 1) * HEAD_W)
                s_ref[b, h] = st_ref[b, hs, hs].T


def _hgrn2(proj, lb_logits, init, prev, l, c_seq, nseq, nblk):
    bsz, seq, _ = proj.shape
    nb, tb, grid = _mixer_grid(bsz, seq, c_seq, nseq, nblk)
    st_in, st_ops, st_out, st_shape, alias = _state_io(l, nb, (HEADS, HEAD_W, HEAD_W), bsz, init, prev)
    n_fixed = 2
    return pl.pallas_call(
        functools.partial(_hg_kernel, c_seq=c_seq, nseq=nseq, nblk=nblk, layer=l, has_init=init is not None,
                          has_prev=prev is not None),
        grid=grid,
        in_specs=[pl.BlockSpec((nb, tb, 4 * GROUP_W), lambda i, j: (i, j, COL_HG // (4 * GROUP_W))),
                  pl.BlockSpec((DEPTH, GROUP_W), lambda i, j: (0, 0))] + st_in,
        out_specs=[pl.BlockSpec((nb, tb, GROUP_W), lambda i, j: (i, j, 0)), st_out],
        out_shape=[jax.ShapeDtypeStruct((bsz, seq, GROUP_W), F32), st_shape],
        scratch_shapes=[pltpu.VMEM((nb, GROUP_W, GROUP_W), F32), pltpu.VMEM((GROUP_W, GROUP_W), F32)],
        input_output_aliases={} if alias is None else {n_fixed + alias: 1},
        compiler_params=_cparams(("parallel", "arbitrary")),
    )(proj, lb_logits, *st_ops)


def _ssd_kernel(*refs, c_seq, nseq, nblk, has_init, has_prev):
    p_ref, dt_ref, cw_ref, cb_ref, dtb_ref, alog_ref, dsk_ref, nw_ref = refs[:8]
    n_in = 8
    cv0_ref = s0_ref = None
    if has_init:
        cv0_ref, s0_ref = refs[n_in], refs[n_in + 1 + has_prev]
    y_ref, s_ref, cv_ref, xpad_ref = refs[n_in + 2 * (has_init + has_prev):]
    nb = s_ref.shape[0]
    tb = p_ref.shape[1]
    c = nseq * c_seq
    n_tail = SSD_CONV - 1
    lo = SUBLANES - n_tail

    @pl.when(pl.program_id(1) == 0)
    def _():
        if has_init:
            s_ref[...] = s0_ref[...]
            xpad_ref[:, lo:SUBLANES, :] = cv0_ref[...]
        else:
            s_ref[...] = jnp.zeros_like(s_ref)
            xpad_ref[:, lo:SUBLANES, :] = jnp.zeros((nb, n_tail, SSD_CONV_DIM), F32)

    r_i, r2_i, _, tri = _seq_grid(c, c_seq, 1)
    trif = tri.astype(F32)
    later = r2_i > r_i
    if nseq > 1:
        later = jnp.logical_and(r_i // c_seq == r2_i // c_seq, later)
    laterf = later.astype(F32)
    head = _head_of_lane(c)
    m_grp = jnp.where(_block_mask(LANES), 1.0 / LANES, 0.0).astype(BF16)
    cw = cw_ref[...]
    groups = [slice(g * LANES, (g + 1) * LANES) for g in range(SSD_GROUPS)]

    xpad_ref[:, SUBLANES:SUBLANES + tb, :] = p_ref[:, :, GROUP_W:GROUP_W + SSD_CONV_DIM]
    for bs, ts, seqs in _row_blocks(nseq, c_seq, nblk):
        r0 = ts.start
        load = lambda ref, col, w: ref[bs, ts, col:col + w].reshape(c, w)
        z = load(p_ref, 0, GROUP_W)
        conv = cb_ref[...] + cw[n_tail:n_tail + 1, :] * load(p_ref, GROUP_W, SSD_CONV_DIM)
        for w in range(n_tail):
            win = xpad_ref[bs, r0 + lo + w:r0 + lo + w + c_seq, :]
            conv = conv + cw[w:w + 1, :] * win.reshape(c, SSD_CONV_DIM)
        xc = _silu(conv)
        xs = xc[:, 0:GROUP_W]
        bm = xc[:, GROUP_W:2 * GROUP_W]
        cm = xc[:, 2 * GROUP_W:3 * GROUP_W]

        delta = jax.nn.softplus(load(dt_ref, 0, GROUP_W) + dtb_ref[...])
        log_a = -delta * jnp.exp(alog_ref[...])
        bcum = _mm_f32(trif, log_a)
        brem = bcum[c - 1:c, :] - bcum if nseq == 1 else _mm_f32(laterf, log_a)
        xd = xs * delta
        xr = xd * jnp.exp(brem)
        brow = [bcum[:, gl].T for gl in groups]
        gram = [_mm_nt(cm[:, gl], bm[:, gl]) for gl in groups]
        w_parts = []
        for h in range(HEADS):
            g, jh = divmod(h, HEADS // SSD_GROUPS)
            b_t = bcum[:, h * HEAD_W:h * HEAD_W + 1]
            b_s = brow[g][jh * HEAD_W:jh * HEAD_W + 1, :]
            w_parts.append(gram[g] * jnp.where(tri, jnp.exp(jnp.where(tri, b_t - b_s, 0.0)), 0.0))
        o = _mm(jnp.concatenate(w_parts, axis=1), _expand_heads(xd, head))
        cross = []
        for i, b in enumerate(seqs):
            sl = slice(i * c_seq, (i + 1) * c_seq)
            last = (i + 1) * c_seq - 1
            parts = []
            for g, gl in enumerate(groups):
                s_prev = s_ref[b, g]
                parts.append(_mm_nt(cm[sl, gl], s_prev))
                a_col = jnp.exp(brow[g][:, last:last + 1])
                s_ref[b, g] = s_prev * a_col + _mm_tn(xr[sl, gl], bm[sl, gl])
            cross.append(jnp.concatenate(parts, axis=1))
        o = o + jnp.exp(bcum) * _stack_rows(cross)
        y = (o + dsk_ref[...] * xs) * _silu(z)
        y = y * lax.rsqrt(_seg_mean(y * y, m_grp) + EPS) * nw_ref[...]
        y_ref[bs, ts, :] = y.reshape(len(seqs), c_seq, GROUP_W)
    tail = xpad_ref[:, tb + lo:tb + SUBLANES, :]
    cv_ref[...] = tail
    xpad_ref[:, lo:SUBLANES, :] = tail


def _ssd(proj, prm, init_cv, init_s, prev_cv, prev_s, l, c_seq, nseq, nblk):
    bsz, seq, _ = proj.shape
    nb, tb, grid = _mixer_grid(bsz, seq, c_seq, nseq, nblk)
    cv_in, cv_ops, cv_out, cv_shape, cv_alias = _state_io(l, nb, (SSD_CONV - 1, SSD_CONV_DIM), bsz,
                                                          init_cv, prev_cv)
    s_in, s_ops, s_out, s_shape, s_alias = _state_io(l, nb, (SSD_GROUPS, LANES, SSD_STATE), bsz,
                                                     init_s, prev_s)
    n_fixed = 8
    aliases = {}
    if cv_alias is not None:
        aliases[n_fixed + cv_alias] = 2
        aliases[n_fixed + len(cv_in) + s_alias] = 1
    return pl.pallas_call(
        functools.partial(_ssd_kernel, c_seq=c_seq, nseq=nseq, nblk=nblk, has_init=init_s is not None,
                          has_prev=prev_s is not None),
        grid=grid,
        in_specs=[pl.BlockSpec((nb, tb, 4 * GROUP_W), lambda i, j: (i, j, COL_SSD // (4 * GROUP_W))),
                  pl.BlockSpec((nb, tb, GROUP_W), lambda i, j: (i, j, COL_DT // GROUP_W)),
                  _layer_spec(l, (SSD_CONV, SSD_CONV_DIM), 2),
                  _layer_spec(l, (1, SSD_CONV_DIM), 2),
                  _layer_spec(l, (1, GROUP_W), 2),
                  _layer_spec(l, (1, GROUP_W), 2),
                  _layer_spec(l, (1, GROUP_W), 2),
                  _layer_spec(l, (1, GROUP_W), 2)] + cv_in + s_in,
        out_specs=[pl.BlockSpec((nb, tb, GROUP_W), lambda i, j: (i, j, 0)), s_out, cv_out],
        out_shape=[jax.ShapeDtypeStruct((bsz, seq, GROUP_W), F32), s_shape, cv_shape],
        scratch_shapes=[pltpu.VMEM((nb, tb + SUBLANES, SSD_CONV_DIM), F32)],
        input_output_aliases=aliases,
        compiler_params=_cparams(("parallel", "arbitrary")),
    )(proj, proj, prm["conv_w"], prm["conv_b"], prm["dt_bias"], prm["a_log"], prm["ssd_d"], prm["ssd_norm"],
      *cv_ops, *s_ops)


def _ssd_state_in(s):
    d, bsz = s.shape[:2]
    return jnp.swapaxes(s, 3, 4).reshape(d, bsz, SSD_GROUPS, LANES, SSD_STATE)


def _ssd_state_out(s):
    d, bsz = s.shape[:2]
    return jnp.swapaxes(s.reshape(d, bsz, HEADS, HEAD_W, SSD_STATE), 3, 4)


def _s5_kernel(*refs, tl, unroll, has_init, has_prev):
    (u0_ref, u1_ref, are_ref, aim_ref, ldt_ref, bre_ref, bim_ref, cw_ref, dsk_ref, gw_ref,
     gb_ref) = refs[:11]
    n_in = 11
    s0r_ref = s0i_ref = None
    if has_init:
        s0r_ref, s0i_ref = refs[n_in], refs[n_in + 1 + has_prev]
    (y_ref, sr_ref, si_ref, wb_ref, ab_ref, us0_ref, us1_ref, ut_ref, bu_ref, yt_ref, ys0_ref,
     ys1_ref) = refs[n_in + 2 * (has_init + has_prev):]
    nbt = SUBLANES * tl

    @pl.when(pl.program_id(1) == 0)
    def _():
        if has_init:
            sr_ref[...] = s0r_ref[...]
            si_ref[...] = s0i_ref[...]
        else:
            sr_ref[...] = jnp.zeros_like(sr_ref)
            si_ref[...] = jnp.zeros_like(si_ref)
        lr, li, st = are_ref[...], aim_ref[...], jnp.exp(ldt_ref[...])
        mag = jnp.exp(lr * st)
        abr = mag * jnp.cos(li * st)
        abi = mag * jnp.sin(li * st)
        ab_ref[0:1, :] = abr
        ab_ref[1:2, :] = abi
        den = lr * lr + li * li
        cr = ((abr - 1.0) * lr + abi * li) / den
        ci = (abi * lr - (abr - 1.0) * li) / den
        wb_ref[:, 0:S5_N] = (cr * bre_ref[...] - ci * bim_ref[...]).astype(BF16)
        wb_ref[:, S5_N:2 * S5_N] = (cr * bim_ref[...] + ci * bre_ref[...]).astype(BF16)

    us0_ref[...] = u0_ref[...].reshape(nbt, LANES)
    us1_ref[...] = u1_ref[...].reshape(nbt, LANES)

    nparts = bu_ref.shape[0]
    prow = nbt // nparts
    psteps = tl // nparts
    for h in range(nparts):
        for t in range(h * psteps, (h + 1) * psteps):
            dst = slice(t * SUBLANES, (t + 1) * SUBLANES)
            src = pl.ds(t, SUBLANES, stride=tl)
            ut_ref[dst, 0:LANES] = us0_ref[src, :]
            ut_ref[dst, LANES:2 * LANES] = us1_ref[src, :]
        u_h = ut_ref[h * prow:(h + 1) * prow, :]
        bu_ref[h] = jnp.dot(u_h.astype(BF16), wb_ref[...], preferred_element_type=F32)
    ar = jnp.broadcast_to(ab_ref[0:1, :], (SUBLANES, S5_N))
    ai = jnp.broadcast_to(ab_ref[1:2, :], (SUBLANES, S5_N))
    xr, xi = sr_ref[...], si_ref[...]
    for h in range(nparts):
        for t in range(tl // nparts):
            rows = slice(t * SUBLANES, (t + 1) * SUBLANES)
            nr = ar * xr - ai * xi + bu_ref[h, rows, 0:S5_N]
            ni = ar * xi + ai * xr + bu_ref[h, rows, S5_N:2 * S5_N]
            bu_ref[h, rows, 0:S5_N] = nr
            bu_ref[h, rows, S5_N:2 * S5_N] = ni
            xr, xi = nr, ni
        y = _mm(bu_ref[h], cw_ref[...]) + dsk_ref[...] * ut_ref[h * prow:(h + 1) * prow, :]
        y = jax.nn.gelu(y)
        yt_ref[h * prow:(h + 1) * prow, :] = y * jax.nn.sigmoid(_mm(y, gw_ref[...]) + gb_ref[...])
        for t in range(h * psteps, (h + 1) * psteps):
            src = slice(t * SUBLANES, (t + 1) * SUBLANES)
            dst = pl.ds(t, SUBLANES, stride=tl)
            ys0_ref[dst, :] = yt_ref[src, 0:LANES]
            ys1_ref[dst, :] = yt_ref[src, LANES:2 * LANES]
    sr_ref[...] = xr
    si_ref[...] = xi
    y_ref[:, :, 0:LANES] = ys0_ref[...].reshape(SUBLANES, tl, LANES)
    y_ref[:, :, LANES:2 * LANES] = ys1_ref[...].reshape(SUBLANES, tl, LANES)


def _s5(proj, prm, init_re, init_im, prev_re, prev_im, l, tl, unroll, nparts):
    bsz, seq, _ = proj.shape
    re_in, re_ops, re_out, re_shape, re_alias = _state_io(l, SUBLANES, (S5_N,), bsz, init_re, prev_re)
    im_in, im_ops, im_out, im_shape, im_alias = _state_io(l, SUBLANES, (S5_N,), bsz, init_im, prev_im)
    n_fixed = 11
    aliases = {}
    if re_alias is not None:
        aliases[n_fixed + re_alias] = 1
        aliases[n_fixed + len(re_in) + im_alias] = 2
    nbt = SUBLANES * tl
    return pl.pallas_call(
        functools.partial(_s5_kernel, tl=tl, unroll=unroll, has_init=init_re is not None,
                          has_prev=prev_re is not None),
        grid=(bsz // SUBLANES, seq // tl),
        in_specs=[pl.BlockSpec((SUBLANES, tl, LANES), lambda i, j: (i, j, COL_S5 // LANES)),
                  pl.BlockSpec((SUBLANES, tl, LANES), lambda i, j: (i, j, COL_S5 // LANES + 1)),
                  _layer_spec(l, (1, S5_N), 2), _layer_spec(l, (1, S5_N), 2), _layer_spec(l, (1, S5_N), 2),
                  _layer_spec(l, (GROUP_W, S5_N), 2), _layer_spec(l, (GROUP_W, S5_N), 2),
                  _layer_spec(l, (2 * S5_N, GROUP_W), 2),
                  _layer_spec(l, (1, GROUP_W), 2),
                  _layer_spec(l, (GROUP_W, GROUP_W), 2),
                  _layer_spec(l, (1, GROUP_W), 2)] + re_in + im_in,
        out_specs=[pl.BlockSpec((SUBLANES, tl, GROUP_W), lambda i, j: (i, j, 0)), re_out, im_out],
        out_shape=[jax.ShapeDtypeStruct((bsz, seq, GROUP_W), F32), re_shape, im_shape],
        scratch_shapes=[pltpu.VMEM((GROUP_W, 2 * S5_N), BF16),
                        pltpu.VMEM((SUBLANES, S5_N), F32),
                        pltpu.VMEM((nbt, LANES), F32), pltpu.VMEM((nbt, LANES), F32),
                        pltpu.VMEM((nbt, GROUP_W), F32),
                        pltpu.VMEM((nparts, nbt // nparts, 2 * S5_N), F32),
                        pltpu.VMEM((nbt, GROUP_W), F32),
                        pltpu.VMEM((nbt, LANES), F32), pltpu.VMEM((nbt, LANES), F32)],
        input_output_aliases=aliases,
        compiler_params=_cparams(("parallel", "arbitrary")),
    )(proj, proj, prm["a_re"], prm["a_im"], prm["log_dt"], prm["b_re"], prm["b_im"], prm["c_blk"],
      prm["s5_d"], prm["glu_w"], prm["glu_b"], *re_ops, *im_ops)


def _block_diag(blocks):
    d, g, r, c = blocks.shape
    eye = jnp.eye(g, dtype=blocks.dtype)
    return (eye[None, :, None, :, None] * blocks[:, :, :, None, :]).reshape(d, g * r, g * c)


def _prep(ssd_conv_w, ssd_conv_b, ssd_dt_bias, ssd_a_log, ssd_d, ssd_norm, s5_a_re, s5_a_im, s5_log_dt,
          s5_b_re, s5_b_im, s5_c_re, s5_c_im, s5_d, s5_glu_w, s5_glu_b):
    per_head = lambda v: jnp.repeat(v, HEAD_W, axis=1)[:, None, :]
    c_blk = jnp.concatenate([_block_diag(jnp.swapaxes(s5_c_re, 2, 3)),
                             -_block_diag(jnp.swapaxes(s5_c_im, 2, 3))], axis=1)
    return dict(
        conv_w=ssd_conv_w, conv_b=ssd_conv_b[:, None, :],
        dt_bias=per_head(ssd_dt_bias), a_log=per_head(ssd_a_log), ssd_d=per_head(ssd_d),
        ssd_norm=ssd_norm[:, None, :],
        a_re=s5_a_re.reshape(DEPTH, 1, S5_N), a_im=s5_a_im.reshape(DEPTH, 1, S5_N),
        log_dt=jnp.repeat(s5_log_dt, S5_STATE, axis=1)[:, None, :],
        b_re=_block_diag(jnp.swapaxes(s5_b_re, 2, 3)), b_im=_block_diag(jnp.swapaxes(s5_b_im, 2, 3)),
        c_blk=c_blk.astype(BF16),
        s5_d=s5_d[:, None, :], glu_w=s5_glu_w.astype(BF16), glu_b=s5_glu_b[:, None, :],
    )


def _trunk(x, pos0, init, prm, w_in, w_out, w_up, w_down, norm_mix, norm_mlp, norm_final, lb_logits,
           angle_row, log_gamma, cfg):
    bsz, seq, _ = x.shape
    c_ret, nseq_ret, _ = cfg["ret"]
    cos, sin = _rope_tables(angle_row, pos0, seq if nseq_ret == 1 else nseq_ret * c_ret, seq, cfg["rope_tl"])
    x2d = x.reshape(bsz * seq, D_MODEL)
    i_ret, i_hg, i_ssd, i_cv, i_re, i_im = init if init is not None else (None,) * 6
    if init is not None:
        i_ssd = _ssd_state_in(i_ssd)
        i_re = i_re.reshape(DEPTH, bsz, S5_N)
        i_im = i_im.reshape(DEPTH, bsz, S5_N)
    s_ret = s_hg = s_ssd = s_cv = s_re = s_im = None
    for l in range(DEPTH):
        proj = _inproj(x2d, norm_mix, w_in, l, cfg["tm_in"]).reshape(bsz, seq, IN_COLS_PAD)
        ya, s_ret = _retention(proj, cos, sin, log_gamma, i_ret, s_ret, l, *cfg["ret"])
        yb, s_hg = _hgrn2(proj, lb_logits, i_hg, s_hg, l, *cfg["hg"])
        yc, s_ssd, s_cv = _ssd(proj, prm, i_cv, i_ssd, s_cv, s_ssd, l, *cfg["ssd"])
        yd, s_re, s_im = _s5(proj, prm, i_re, i_im, s_re, s_im, l, *cfg["s5"])
        ys = [y.reshape(bsz * seq, GROUP_W) for y in (ya, yb, yc, yd)]
        x2d = _outmlp(x2d, ys, w_out, norm_mlp, w_up, w_down, norm_final, l, cfg["tm_out"])
    return (x2d.reshape(bsz, seq, D_MODEL), s_ret, s_hg, _ssd_state_out(s_ssd), s_cv,
            s_re.reshape(DEPTH, bsz, S5_GROUPS, S5_STATE), s_im.reshape(DEPTH, bsz, S5_GROUPS, S5_STATE))


PROMPT_CFG = dict(rope_tl=256, tm_in=512, ret=(256, 1, 4), hg=(64, 1, 8), ssd=(128, 1, 8), s5=(128, 8, 2),
                  tm_out=512)
SAMPLE_CFG = dict(rope_tl=128, tm_in=512, ret=(8, 16, 2), hg=(8, 16, 2), ssd=(8, 16, 2), s5=(8, 8, 1),
                  tm_out=512)


def kernel(x_prompt, x_sample, state_ret, state_hgrn, state_ssd, state_ssd_conv, state_s5_re, state_s5_im, norm_mix, w_in, w_out, hg_lb_logits, ssd_conv_w, ssd_conv_b, ssd_dt_bias, ssd_a_log, ssd_d, ssd_norm, s5_a_re, s5_a_im, s5_log_dt, s5_b_re, s5_b_im, s5_c_re, s5_c_im, s5_d, s5_glu_w, s5_glu_b, norm_mlp, w_up, w_down, norm_final):
    prm = _prep(ssd_conv_w, ssd_conv_b, ssd_dt_bias, ssd_a_log, ssd_d, ssd_norm, s5_a_re, s5_a_im, s5_log_dt,
                s5_b_re, s5_b_im, s5_c_re, s5_c_im, s5_d, s5_glu_w, s5_glu_b)
    w_out_b = _cast_bf16(w_out, 512)
    w_up_b = _cast_bf16(w_up, 256)
    w_down_b = _cast_bf16(w_down, 1024)
    angle = 1.0 / (ROPE_BASE ** jnp.linspace(0.0, 1.0, HEAD_W // 2, dtype=F32))
    angle_row = jnp.tile(jnp.repeat(angle, 2), 2)[None, :]
    log_gamma = jnp.log(1.0 - jnp.exp2(-5.0 - jnp.arange(HEADS, dtype=F32)))
    shared = (prm, w_in, w_out_b, w_up_b, w_down_b, norm_mix[:, None, :], norm_mlp[:, None, :],
              norm_final[None, :], hg_lb_logits, angle_row, log_gamma)
    states = (state_ret, state_hgrn, state_ssd, state_ssd_conv, state_s5_re, state_s5_im)
    out_p = _trunk(x_prompt, 0, None, *shared, PROMPT_CFG)
    out_s = _trunk(x_sample, PAST_LEN, states, *shared, SAMPLE_CFG)
    return (out_p[0], out_s[0]) + out_p[1:] + out_s[1:]
```

```python
import functools
import itertools

import jax
import jax.numpy as jnp
from jax import lax
from jax.experimental import pallas as pl
from jax.experimental.pallas import tpu as pltpu

F32 = jnp.float32
BF16 = jnp.bfloat16

D_MODEL = 1024
DEPTH = 2
PAST_LEN = 16384
GROUP_W = 256
HEADS = 4
HEAD_W = 64
PAIR_HEADS = 2
ROPE_BASE = 10000.0
HG_MIN_FORGET = 1e-30
HG_SAFE_SPREAD = 80.0
SSD_STATE = 128
SSD_GROUPS = 2
SSD_CONV = 4
SSD_CONV_DIM = 768
CONV_PAD = 8
S5_GROUPS = 16
S5_STATE = 64
S5_N = S5_GROUPS * S5_STATE
D_FF = 4096
IN_COLS = 3332
EPS = 1e-6

COL_RET = 0
COL_HG = 1024
COL_SSD = 2048
COL_S5 = 3072
COL_DT = 3328
IN_COLS_PAD = 3584

LANES = 128
SUBLANES = 8
VMEM_LIMIT = 56 * 1024 * 1024


def _cparams(sem):
    return pltpu.CompilerParams(dimension_semantics=sem, vmem_limit_bytes=VMEM_LIMIT)


def _mm(a, b):
    return jnp.dot(a.astype(BF16), b.astype(BF16), preferred_element_type=F32)


def _mm_nt(a, b):
    return lax.dot_general(a.astype(BF16), b.astype(BF16), (((1,), (1,)), ((), ())),
                           preferred_element_type=F32)


def _mm_tn(a, b):
    return lax.dot_general(a.astype(BF16), b.astype(BF16), (((0,), (0,)), ((), ())),
                           preferred_element_type=F32)


def _mm_f32(a, b):
    return jnp.dot(a, b, preferred_element_type=F32, precision=lax.Precision.HIGHEST)


def _silu(x):
    return x * jax.nn.sigmoid(x)


def _rms(x, w):
    return x * lax.rsqrt(jnp.mean(x * x, axis=-1, keepdims=True) + EPS) * w


def _tri(c):
    t = lax.broadcasted_iota(jnp.int32, (c, c), 0)
    s = lax.broadcasted_iota(jnp.int32, (c, c), 1)
    return t >= s


def _head_of_lane(rows, width=GROUP_W):
    return lax.broadcasted_iota(jnp.int32, (rows, width), 1) // HEAD_W


def _expand_heads(x, head):
    nheads = x.shape[1] // HEAD_W
    return jnp.concatenate([jnp.where(head == h, x, 0.0) for h in range(nheads)], axis=0).astype(BF16)


def _block_mask(seg, width=GROUP_W):
    r = lax.broadcasted_iota(jnp.int32, (width, width), 0) // seg
    c = lax.broadcasted_iota(jnp.int32, (width, width), 1) // seg
    return r == c


def _seg_mean(x, m):
    hi = x.astype(BF16)
    lo = (x - hi.astype(F32)).astype(BF16)
    return jnp.dot(hi, m, preferred_element_type=F32) + jnp.dot(lo, m, preferred_element_type=F32)


def _row_blocks(nseq, c_seq, nblk):
    if nseq == 1:
        return [(slice(0, 1), slice(k * c_seq, (k + 1) * c_seq), [0]) for k in range(nblk)]
    return [(slice(k * nseq, (k + 1) * nseq), slice(0, c_seq), list(range(k * nseq, (k + 1) * nseq)))
            for k in range(nblk)]


def _seq_grid(rows, c_seq, copies):
    r = lax.broadcasted_iota(jnp.int32, (rows, copies * rows), 0)
    j = lax.broadcasted_iota(jnp.int32, (rows, copies * rows), 1)
    h = j // rows
    r2 = j - h * rows
    causal = r >= r2
    if rows > c_seq:
        causal = jnp.logical_and(r // c_seq == r2 // c_seq, causal)
    return r, r2, h, causal


def _stack_rows(pieces):
    return pieces[0] if len(pieces) == 1 else jnp.concatenate(pieces, axis=0)


def _mixer_grid(bsz, seq, c_seq, nseq, nblk):
    nb = 1 if nseq == 1 else nseq * nblk
    tb = c_seq * nblk if nseq == 1 else c_seq
    assert nseq == 1 or seq == c_seq
    return nb, tb, (bsz // nb, seq // tb)


def _layer_spec(l, tail, n_grid, single=False):
    zeros = (0,) * len(tail)
    kw = dict(pipeline_mode=pl.Buffered(1)) if single else {}
    if n_grid == 1:
        return pl.BlockSpec((None,) + tail, lambda i: (l,) + zeros, **kw)
    return pl.BlockSpec((None,) + tail, lambda i, j: (l,) + zeros, **kw)


def _state_spec(l, nb, tail):
    zeros = (0,) * len(tail)
    return pl.BlockSpec((None, nb) + tail, lambda i, j: (l, i) + zeros)


def _state_io(l, nb, tail, bsz, init, prev):
    spec = _state_spec(l, nb, tail)
    in_specs, operands = [], []
    if init is not None:
        in_specs.append(spec)
        operands.append(init)
    alias_pos = None
    if prev is not None:
        alias_pos = len(in_specs)
        in_specs.append(pl.BlockSpec(memory_space=pl.ANY))
        operands.append(prev)
    return in_specs, operands, spec, jax.ShapeDtypeStruct((DEPTH, bsz) + tail, F32), alias_pos


def _cast_kernel(w_ref, o_ref):
    o_ref[...] = w_ref[...].astype(BF16)


def _cast_bf16(w, tr):
    d, r, c = w.shape
    spec = pl.BlockSpec((None, tr, c), lambda l, i: (l, i, 0))
    return pl.pallas_call(
        _cast_kernel, grid=(d, r // tr), in_specs=[spec], out_specs=spec,
        out_shape=jax.ShapeDtypeStruct(w.shape, BF16),
        compiler_params=_cparams(("parallel", "parallel")),
    )(w)


def _stream_steps(rows, tm):
    bounds, start = [], 0
    for r in rows:
        bounds.append((start, r // tm))
        start += r // tm
    return bounds, start


def _stream_spec(tm, width, start, steps, single=False):
    kw = dict(pipeline_mode=pl.Buffered(1)) if single else {}
    return pl.BlockSpec((tm, width), lambda i: (jnp.clip(i - start, 0, steps - 1), 0), **kw)


def _when_stream(start, steps):
    i = pl.program_id(0)
    return pl.when(jnp.logical_and(i >= start, i < start + steps))


def _inproj_kernel(*refs, bounds):
    n = len(bounds)
    x_refs, (nw_ref, w_ref), o_refs, wb_ref = refs[:n], refs[n:n + 2], refs[n + 2:2 * n + 2], refs[2 * n + 2]

    @pl.when(pl.program_id(0) == 0)
    def _():
        rows = 256
        head = _head_of_lane(rows)
        for r in range(0, D_MODEL, rows):
            rs = slice(r, r + rows)
            wb_ref[rs, 0:COL_S5] = w_ref[rs, 0:COL_S5].astype(BF16)
            tail = w_ref[rs, COL_S5:IN_COLS]
            wb_ref[rs, COL_S5:COL_DT] = tail[:, HEADS:HEADS + GROUP_W].astype(BF16)
            dtw = tail[:, HEADS - 1:HEADS]
            for h in range(HEADS - 2, -1, -1):
                dtw = jnp.where(head == h, tail[:, h:h + 1], dtw)
            wb_ref[rs, COL_DT:IN_COLS_PAD] = dtw.astype(BF16)

    for x_ref, o_ref, (start, steps) in zip(x_refs, o_refs, bounds):
        @_when_stream(start, steps)
        def _():
            h = _rms(x_ref[...], nw_ref[...])
            o_ref[...] = jnp.dot(h.astype(BF16), wb_ref[...], preferred_element_type=F32)


def _inproj(xs, norm_w, w_in, l, tm):
    bounds, n_steps = _stream_steps([x.shape[0] for x in xs], tm)
    longest = max(steps for _, steps in bounds)
    return pl.pallas_call(
        functools.partial(_inproj_kernel, bounds=bounds),
        grid=(n_steps,),
        in_specs=[_stream_spec(tm, D_MODEL, *b, single=b[1] < longest) for b in bounds]
        + [_layer_spec(l, (1, D_MODEL), 1),
           _layer_spec(l, (D_MODEL, IN_COLS), 1, single=True)],
        out_specs=[_stream_spec(tm, IN_COLS_PAD, *b, single=b[1] < longest) for b in bounds],
        out_shape=[jax.ShapeDtypeStruct((x.shape[0], IN_COLS_PAD), F32) for x in xs],
        scratch_shapes=[pltpu.VMEM((D_MODEL, IN_COLS_PAD), BF16)],
        compiler_params=_cparams(("arbitrary",)),
    )(*xs, norm_w, w_in)


def _outmlp_kernel(*refs, bounds, final_norm, tn_up, tn_down):
    n = len(bounds)
    row_refs = refs[:5 * n]
    wo_ref, nw_ref, wu_ref, wd_ref, nf_ref = refs[5 * n:5 * n + 5]
    o_refs = refs[5 * n + 5:6 * n + 5]
    h_ref, u_ref = refs[6 * n + 5:]

    for k, (start, steps) in enumerate(bounds):
        x_ref, y_refs, o_ref = row_refs[5 * k], row_refs[5 * k + 1:5 * k + 5], o_refs[k]

        @_when_stream(start, steps)
        def _():
            x1 = x_ref[...]
            for g, y_ref in enumerate(y_refs):
                x1 = x1 + _mm(y_ref[...], wo_ref[g * GROUP_W:(g + 1) * GROUP_W, :])
            o_ref[...] = x1
            h_ref[...] = _rms(x1, nw_ref[...]).astype(BF16)
            for c0 in range(0, D_FF, tn_up):
                u = jnp.dot(h_ref[...], wu_ref[:, c0:c0 + tn_up], preferred_element_type=F32)
                u_ref[:, c0:c0 + tn_up] = jnp.square(jnp.maximum(u, 0.0)).astype(BF16)
            for c0 in range(0, D_MODEL, tn_down):
                o_ref[:, c0:c0 + tn_down] += jnp.dot(u_ref[...], wd_ref[:, c0:c0 + tn_down],
                                                     preferred_element_type=F32)
            if final_norm:
                o_ref[...] = _rms(o_ref[...], nf_ref[...])


def _outmlp(xs, ys, w_out, norm_w, w_up, w_down, norm_f, l, tm):
    bounds, n_steps = _stream_steps([x.shape[0] for x in xs], tm)
    row_specs, row_ops = [], []
    for x, y4, b in zip(xs, ys, bounds):
        row_specs += [_stream_spec(tm, D_MODEL, *b)] + [_stream_spec(tm, GROUP_W, *b)] * 4
        row_ops += [x, *y4]
    return pl.pallas_call(
        functools.partial(_outmlp_kernel, bounds=bounds, final_norm=(l == DEPTH - 1), tn_up=512, tn_down=256),
        grid=(n_steps,),
        in_specs=row_specs
        + [_layer_spec(l, (D_MODEL, D_MODEL), 1, single=True),
           _layer_spec(l, (1, D_MODEL), 1),
           _layer_spec(l, (D_MODEL, D_FF), 1, single=True),
           _layer_spec(l, (D_FF, D_MODEL), 1, single=True),
           pl.BlockSpec((1, D_MODEL), lambda i: (0, 0))],
        out_specs=[_stream_spec(tm, D_MODEL, *b) for b in bounds],
        out_shape=[jax.ShapeDtypeStruct((x.shape[0], D_MODEL), F32) for x in xs],
        scratch_shapes=[pltpu.VMEM((tm, D_MODEL), BF16),
                        pltpu.VMEM((tm, D_FF), BF16)],
        compiler_params=_cparams(("arbitrary",)),
    )(*row_ops, w_out, norm_w, w_up, w_down, norm_f)


def _rope_kernel(angle_ref, cos_ref, sin_ref, *, pos0, tl, period):
    i = pl.program_id(0)
    row = i * tl + lax.broadcasted_iota(jnp.int32, (tl, 2 * HEAD_W), 0)
    pos = (pos0 + row % period).astype(F32)
    theta = pos * angle_ref[...]
    lane = lax.broadcasted_iota(jnp.int32, (tl, 2 * HEAD_W), 1)
    cos_ref[...] = jnp.cos(theta)
    s = jnp.sin(theta)
    sin_ref[...] = jnp.where(lane % 2 == 0, -s, s)


def _rope_tables(angle_row, pos0, rows, period, tl):
    seq = rows
    return pl.pallas_call(
        functools.partial(_rope_kernel, pos0=pos0, tl=tl, period=period),
        grid=(seq // tl,),
        in_specs=[pl.BlockSpec((1, 2 * HEAD_W), lambda i: (0, 0))],
        out_specs=[pl.BlockSpec((tl, 2 * HEAD_W), lambda i: (i, 0))] * 2,
        out_shape=[jax.ShapeDtypeStruct((seq, 2 * HEAD_W), F32)] * 2,
        compiler_params=_cparams(("parallel",)),
    )(angle_row)


def _ret_kernel(*refs, c_seq, nseq, nblk, has_init, has_prev):
    lg_ref, p_ref, cos_ref, sin_ref = refs[:4]
    s0_ref = refs[4] if has_init else None
    y_ref, s_ref, sb_ref = refs[4 + has_init + has_prev:]
    nb = s_ref.shape[0]
    c = nseq * c_seq
    j = pl.program_id(1)

    @pl.when(j == 0)
    def _():
        sb_ref[...] = jnp.zeros_like(sb_ref)
        if has_init:
            for b in range(nb):
                for h in range(HEADS):
                    hs = slice(h * HEAD_W, (h + 1) * HEAD_W)
                    sb_ref[b, hs, hs] = s0_ref[b, h]

    def per_head(idx, shape):
        out = jnp.full(shape, lg_ref[HEADS - 1], F32)
        for h in range(HEADS - 2, -1, -1):
            out = jnp.where(idx == h, lg_ref[h], out)
        return out

    head = _head_of_lane(c)
    bd = _block_mask(HEAD_W)
    m_seg = jnp.where(bd, 1.0 / HEAD_W, 0.0).astype(BF16)
    even = lax.broadcasted_iota(jnp.int32, (c, GROUP_W), 1) % 2 == 0
    lg_lane = per_head(head[0:1, :], (1, GROUP_W))
    t_loc = (lax.broadcasted_iota(jnp.int32, (c, 1), 0) % c_seq).astype(F32)
    e_q = jnp.exp((t_loc + 1.0) * lg_lane)
    e_k = jnp.exp((c_seq - 1.0 - t_loc) * lg_lane)
    a_c = jnp.exp(c_seq * lg_lane)
    r_i, r2_i, h_i, causal = _seq_grid(c, c_seq, HEADS)
    dist = jnp.where(causal, (r_i - r2_i).astype(F32), 0.0)
    decay = jnp.where(causal, jnp.exp(dist * per_head(h_i, (c, HEADS * c))), 0.0)

    def rot(x, cos, sin):
        swapped = jnp.where(even, pltpu.roll(x, GROUP_W - 1, 1), pltpu.roll(x, 1, 1))
        return x * cos + swapped * sin

    for bs, ts, seqs in _row_blocks(nseq, c_seq, nblk):
        load = lambda lo: p_ref[bs, ts, lo:lo + GROUP_W].reshape(c, GROUP_W)
        crow = ts if nseq == 1 else slice(0, c)
        cos = jnp.concatenate([cos_ref[crow, :]] * 2, axis=1)
        sin = jnp.concatenate([sin_ref[crow, :]] * 2, axis=1)
        q = rot(load(0), cos, sin)
        k = rot(load(GROUP_W), cos, sin) * (HEAD_W ** -0.5)
        v = load(2 * GROUP_W)
        g = load(3 * GROUP_W)
        scores = _mm_nt(q, _expand_heads(k, head)) * decay
        o = _mm(scores, _expand_heads(v, head))
        qe = q * e_q
        ke = k * e_k
        cross = []
        for i, b in enumerate(seqs):
            sl = slice(i * c_seq, (i + 1) * c_seq)
            s_prev = sb_ref[b]
            cross.append(_mm(qe[sl], s_prev))
            sb_ref[b] = a_c * s_prev + jnp.where(bd, _mm_tn(ke[sl], v[sl]), 0.0)
        o = o + _stack_rows(cross)
        mu = _seg_mean(o, m_seg)
        d = o - mu
        var = _seg_mean(d * d, m_seg)
        y_ref[bs, ts, :] = (d * lax.rsqrt(var + EPS) * _silu(g)).reshape(len(seqs), c_seq, GROUP_W)

    @pl.when(j == pl.num_programs(1) - 1)
    def _():
        for b in range(nb):
            for h in range(HEADS):
                hs = slice(h * HEAD_W, (h + 1) * HEAD_W)
                s_ref[b, h] = sb_ref[b, hs, hs]


def _retention(proj, cos, sin, log_gamma, init, prev, l, c_seq, nseq, nblk):
    bsz, seq, _ = proj.shape
    nb, tb, grid = _mixer_grid(bsz, seq, c_seq, nseq, nblk)
    st_in, st_ops, st_out, st_shape, alias = _state_io(l, nb, (HEADS, HEAD_W, HEAD_W), bsz, init, prev)
    n_fixed = 4
    rope_spec = (pl.BlockSpec((tb, 2 * HEAD_W), lambda i, j: (j, 0)) if nseq == 1 else
                 pl.BlockSpec((nseq * c_seq, 2 * HEAD_W), lambda i, j: (0, 0)))
    return pl.pallas_call(
        functools.partial(_ret_kernel, c_seq=c_seq, nseq=nseq, nblk=nblk, has_init=init is not None,
                          has_prev=prev is not None),
        grid=grid,
        in_specs=[pl.BlockSpec(memory_space=pltpu.SMEM),
                  pl.BlockSpec((nb, tb, 4 * GROUP_W), lambda i, j: (i, j, COL_RET // (4 * GROUP_W))),
                  rope_spec, rope_spec] + st_in,
        out_specs=[pl.BlockSpec((nb, tb, GROUP_W), lambda i, j: (i, j, 0)), st_out],
        out_shape=[jax.ShapeDtypeStruct((bsz, seq, GROUP_W), F32), st_shape],
        scratch_shapes=[pltpu.VMEM((nb, GROUP_W, GROUP_W), F32)],
        input_output_aliases={} if alias is None else {n_fixed + alias: 1},
        compiler_params=_cparams(("parallel", "arbitrary")),
    )(log_gamma, proj, cos, sin, *st_ops)


def _hg_kernel(*refs, c_seq, nseq, nblk, layer, has_init, has_prev):
    p_ref, lb_ref = refs[:2]
    s0_ref = refs[2] if has_init else None
    y_ref, s_ref, st_ref, keep_ref = refs[2 + has_init + has_prev:]
    nb = s_ref.shape[0]
    c = nseq * c_seq
    j = pl.program_id(1)

    @pl.when(j == 0)
    def _():
        st_ref[...] = jnp.zeros_like(st_ref)
        if has_init:
            for b in range(nb):
                for h in range(HEADS):
                    hs = slice(h * HEAD_W, (h + 1) * HEAD_W)
                    st_ref[b, hs, hs] = s0_ref[b, h].T

    logits = lb_ref[...]
    e = jnp.exp(logits - jnp.max(logits, axis=0, keepdims=True))
    sm = e / jnp.sum(e, axis=0, keepdims=True)
    csum = sm[0:1, :]
    for i in range(1, layer + 1):
        csum = csum + sm[i:i + 1, :]
    lower = csum - sm[0:1, :]

    head = _head_of_lane(c)
    bd = _block_mask(HEAD_W)
    m_seg = jnp.where(bd, 1.0 / HEAD_W, 0.0).astype(BF16)
    r_i, r2_i, _, tril = _seq_grid(c, c_seq, 1)
    trif = tril.astype(F32)
    later = r2_i > r_i
    if nseq > 1:
        later = jnp.logical_and(r_i // c_seq == r2_i // c_seq, later)
    laterf = later.astype(F32)
    causal = _seq_grid(c, c_seq, HEADS)[3]
    mid = c // 2 - 1

    staged = []
    spreads = []
    for bs, ts, seqs in _row_blocks(nseq, c_seq, nblk):
        load = lambda lo: p_ref[bs, ts, lo:lo + GROUP_W].reshape(c, GROUP_W)
        qq = _silu(load(0))
        ff = load(GROUP_W)
        v = load(2 * GROUP_W)
        g = load(3 * GROUP_W)
        sig = jax.nn.sigmoid(ff)
        forget = lower + (1.0 - lower) * sig
        log_f = jnp.log(jnp.maximum(forget, HG_MIN_FORGET))
        key = (1.0 - lower) * (1.0 - sig)
        bcum = _mm_f32(trif, log_f)
        if nseq == 1:
            b_ref = bcum[mid:mid + 1, :]
            spread = bcum - b_ref
            q_in = qq * jnp.exp(spread)
            k_in = key * jnp.exp(-spread)
            q_x = q_in * jnp.exp(b_ref)
            k_x = k_in * jnp.exp(bcum[c - 1:c, :] - b_ref)
        else:
            spread = bcum
            q_in = q_x = qq * jnp.exp(bcum)
            k_in = key * jnp.exp(-bcum)
            k_x = key * jnp.exp(_mm_f32(laterf, log_f))
        spreads.append(jnp.max(jnp.abs(spread)))
        scores = jnp.where(causal, _mm_nt(q_in, _expand_heads(k_in, head)), 0.0)
        o = _mm(scores, _expand_heads(v, head))
        upd = []
        for i in range(len(seqs)):
            sl = slice(i * c_seq, (i + 1) * c_seq)
            a_seq = jnp.exp(bcum[(i + 1) * c_seq - 1:(i + 1) * c_seq, :])
            upd.append((q_x[sl], a_seq, jnp.where(bd, _mm_tn(v[sl], k_x[sl]), 0.0)))
        staged.append((bs, ts, seqs, o, upd, _silu(g)))

    unsafe = functools.reduce(jnp.maximum, spreads) > HG_SAFE_SPREAD
    if nb == 1:
        keep_ref[...] = st_ref[0]

    for bs, ts, seqs, o, upd, gate in staged:
        cross = []
        for b, (q_b, a_seq, ds) in zip(seqs, upd):
            st_prev = st_ref[b]
            cross.append(_mm_nt(q_b, st_prev))
            st_ref[b] = st_prev * a_seq + ds
        o = o + _stack_rows(cross)
        y = o * lax.rsqrt(_seg_mean(o * o, m_seg) + EPS) * gate
        y_ref[bs, ts, :] = y.reshape(bs.stop - bs.start, ts.stop - ts.start, GROUP_W)

    @pl.when(unsafe)
    def _():
        tb = p_ref.shape[1]
        assert nb == 1 or (tb == SUBLANES and pl.num_programs(1) == 1)
        rowid = lax.broadcasted_iota(jnp.int32, (SUBLANES, GROUP_W), 0)
        if nb == 1:
            st_ref[0] = keep_ref[...]

        def slab(s, carry):
            b, t0 = (0, pl.multiple_of(s * SUBLANES, SUBLANES)) if nb == 1 else (s, 0)
            rows = pl.ds(t0, SUBLANES)
            if nb > 1:
                st_ref[b] = jnp.zeros((GROUP_W, GROUP_W), F32)
                if has_init:
                    for h in range(HEADS):
                        hs = slice(h * HEAD_W, (h + 1) * HEAD_W)
                        st_ref[b, hs, hs] = s0_ref[b, h].T
            q8 = _silu(p_ref[b, rows, 0:GROUP_W])
            sig = jax.nn.sigmoid(p_ref[b, rows, GROUP_W:2 * GROUP_W])
            v8 = p_ref[b, rows, 2 * GROUP_W:3 * GROUP_W]
            g8 = p_ref[b, rows, 3 * GROUP_W:4 * GROUP_W]
            f8 = jnp.maximum(lower + (1.0 - lower) * sig, HG_MIN_FORGET)
            k8 = (1.0 - lower) * (1.0 - sig)
            o8 = jnp.zeros((SUBLANES, GROUP_W), F32)
            for i in range(SUBLANES):
                outer = _mm_tn(jnp.where(rowid == i, v8, 0.0), k8)
                st = st_ref[b] * f8[i:i + 1, :] + jnp.where(bd, outer, 0.0)
                st_ref[b] = st
                o8 = jnp.where(rowid == i, _mm_nt(q8, st), o8)
            y_ref[b, rows, :] = o8 * lax.rsqrt(_seg_mean(o8 * o8, m_seg) + EPS) * _silu(g8)
            return carry

        lax.fori_loop(0, nb * tb // SUBLANES, slab, 0)

    @pl.when(j == pl.num_programs(1) - 1)
    def _():
        for b in range(nb):
            for h in range(HEADS):
                hs = slice(h * HEAD_W, (h + 1) * HEAD_W)
                s_ref[b, h] = st_ref[b, hs, hs].T


def _hgrn2(proj, lb_logits, init, prev, l, c_seq, nseq, nblk):
    bsz, seq, _ = proj.shape
    nb, tb, grid = _mixer_grid(bsz, seq, c_seq, nseq, nblk)
    st_in, st_ops, st_out, st_shape, alias = _state_io(l, nb, (HEADS, HEAD_W, HEAD_W), bsz, init, prev)
    n_fixed = 2
    return pl.pallas_call(
        functools.partial(_hg_kernel, c_seq=c_seq, nseq=nseq, nblk=nblk, layer=l, has_init=init is not None,
                          has_prev=prev is not None),
        grid=grid,
        in_specs=[pl.BlockSpec((nb, tb, 4 * GROUP_W), lambda i, j: (i, j, COL_HG // (4 * GROUP_W))),
                  pl.BlockSpec((DEPTH, GROUP_W), lambda i, j: (0, 0))] + st_in,
        out_specs=[pl.BlockSpec((nb, tb, GROUP_W), lambda i, j: (i, j, 0)), st_out],
        out_shape=[jax.ShapeDtypeStruct((bsz, seq, GROUP_W), F32), st_shape],
        scratch_shapes=[pltpu.VMEM((nb, GROUP_W, GROUP_W), F32), pltpu.VMEM((GROUP_W, GROUP_W), F32)],
        input_output_aliases={} if alias is None else {n_fixed + alias: 1},
        compiler_params=_cparams(("parallel", "arbitrary")),
    )(proj, lb_logits, *st_ops)


def _ssd_kernel(*refs, c_seq, nseq, nblk, has_init, has_prev):
    p_ref, dt_ref, cw_ref, cb_ref, dtb_ref, alog_ref, dsk_ref, nw_ref = refs[:8]
    n_in = 8
    cv0_ref = s0_ref = None
    if has_init:
        cv0_ref, s0_ref = refs[n_in], refs[n_in + 1 + has_prev]
    y_ref, s_ref, cv_ref, xpad_ref = refs[n_in + 2 * (has_init + has_prev):]
    nb = s_ref.shape[0]
    tb = p_ref.shape[1]
    c = nseq * c_seq
    n_tail = SSD_CONV - 1
    lo = CONV_PAD - n_tail

    @pl.when(pl.program_id(1) == 0)
    def _():
        xpad_ref[:, 0:CONV_PAD, :] = jnp.zeros((nb, CONV_PAD, SSD_CONV_DIM), F32)
        if has_init:
            s_ref[...] = s0_ref[...]
            xpad_ref[:, lo:CONV_PAD, :] = cv0_ref[...]
        else:
            s_ref[...] = jnp.zeros_like(s_ref)

    r_i, r2_i, _, tri = _seq_grid(c, c_seq, 1)
    trif = tri.astype(F32)
    later = r2_i > r_i
    if nseq > 1:
        later = jnp.logical_and(r_i // c_seq == r2_i // c_seq, later)
    laterf = later.astype(F32)
    head = _head_of_lane(c)
    m_grp = jnp.where(_block_mask(LANES), 1.0 / LANES, 0.0).astype(BF16)
    cw = cw_ref[...]
    groups = [slice(g * LANES, (g + 1) * LANES) for g in range(SSD_GROUPS)]

    xpad_ref[:, CONV_PAD:CONV_PAD + tb, :] = p_ref[:, :, GROUP_W:GROUP_W + SSD_CONV_DIM]
    for bs, ts, seqs in _row_blocks(nseq, c_seq, nblk):
        r0 = ts.start
        load = lambda ref, col, w: ref[bs, ts, col:col + w].reshape(c, w)
        z = load(p_ref, 0, GROUP_W)
        conv = cb_ref[...] + cw[n_tail:n_tail + 1, :] * load(p_ref, GROUP_W, SSD_CONV_DIM)
        for w in range(n_tail):
            win = xpad_ref[bs, r0 + lo + w:r0 + lo + w + c_seq, :]
            conv = conv + cw[w:w + 1, :] * win.reshape(c, SSD_CONV_DIM)
        xc = _silu(conv)
        xs = xc[:, 0:GROUP_W]
        bm = xc[:, GROUP_W:2 * GROUP_W]
        cm = xc[:, 2 * GROUP_W:3 * GROUP_W]

        delta = jax.nn.softplus(load(dt_ref, 0, GROUP_W) + dtb_ref[...])
        log_a = -delta * jnp.exp(alog_ref[...])
        bcum = _mm_f32(trif, log_a)
        brem = bcum[c - 1:c, :] - bcum if nseq == 1 else _mm_f32(laterf, log_a)
        xd = xs * delta
        xr = xd * jnp.exp(brem)
        brow = [bcum[:, gl].T for gl in groups]
        gram = [_mm_nt(cm[:, gl], bm[:, gl]) for gl in groups]
        w_parts = []
        for h in range(HEADS):
            g, jh = divmod(h, HEADS // SSD_GROUPS)
            b_t = bcum[:, h * HEAD_W:h * HEAD_W + 1]
            b_s = brow[g][jh * HEAD_W:jh * HEAD_W + 1, :]
            w_parts.append(jnp.where(tri, gram[g] * jnp.exp(b_t - b_s), 0.0))
        o = _mm(jnp.concatenate(w_parts, axis=1), _expand_heads(xd, head))
        cross = []
        for i, b in enumerate(seqs):
            sl = slice(i * c_seq, (i + 1) * c_seq)
            last = (i + 1) * c_seq - 1
            parts = []
            for g, gl in enumerate(groups):
                s_prev = s_ref[b, g]
                parts.append(_mm_nt(cm[sl, gl], s_prev))
                a_col = jnp.exp(brow[g][:, last:last + 1])
                s_ref[b, g] = s_prev * a_col + _mm_tn(xr[sl, gl], bm[sl, gl])
            cross.append(jnp.concatenate(parts, axis=1))
        o = o + jnp.exp(bcum) * _stack_rows(cross)
        y = (o + dsk_ref[...] * xs) * _silu(z)
        y = y * lax.rsqrt(_seg_mean(y * y, m_grp) + EPS) * nw_ref[...]
        y_ref[bs, ts, :] = y.reshape(len(seqs), c_seq, GROUP_W)
    tail = xpad_ref[:, tb + lo:tb + CONV_PAD, :]
    cv_ref[...] = tail
    xpad_ref[:, lo:CONV_PAD, :] = tail


def _ssd(proj, prm, init_cv, init_s, prev_cv, prev_s, l, c_seq, nseq, nblk):
    bsz, seq, _ = proj.shape
    nb, tb, grid = _mixer_grid(bsz, seq, c_seq, nseq, nblk)
    cv_in, cv_ops, cv_out, cv_shape, cv_alias = _state_io(l, nb, (SSD_CONV - 1, SSD_CONV_DIM), bsz,
                                                          init_cv, prev_cv)
    s_in, s_ops, s_out, s_shape, s_alias = _state_io(l, nb, (SSD_GROUPS, LANES, SSD_STATE), bsz,
                                                     init_s, prev_s)
    n_fixed = 8
    aliases = {}
    if cv_alias is not None:
        aliases[n_fixed + cv_alias] = 2
        aliases[n_fixed + len(cv_in) + s_alias] = 1
    return pl.pallas_call(
        functools.partial(_ssd_kernel, c_seq=c_seq, nseq=nseq, nblk=nblk, has_init=init_s is not None,
                          has_prev=prev_s is not None),
        grid=grid,
        in_specs=[pl.BlockSpec((nb, tb, 4 * GROUP_W), lambda i, j: (i, j, COL_SSD // (4 * GROUP_W))),
                  pl.BlockSpec((nb, tb, GROUP_W), lambda i, j: (i, j, COL_DT // GROUP_W)),
                  _layer_spec(l, (SSD_CONV, SSD_CONV_DIM), 2),
                  _layer_spec(l, (1, SSD_CONV_DIM), 2),
                  _layer_spec(l, (1, GROUP_W), 2),
                  _layer_spec(l, (1, GROUP_W), 2),
                  _layer_spec(l, (1, GROUP_W), 2),
                  _layer_spec(l, (1, GROUP_W), 2)] + cv_in + s_in,
        out_specs=[pl.BlockSpec((nb, tb, GROUP_W), lambda i, j: (i, j, 0)), s_out, cv_out],
        out_shape=[jax.ShapeDtypeStruct((bsz, seq, GROUP_W), F32), s_shape, cv_shape],
        scratch_shapes=[pltpu.VMEM((nb, tb + CONV_PAD, SSD_CONV_DIM), F32)],
        input_output_aliases=aliases,
        compiler_params=_cparams(("parallel", "arbitrary")),
    )(proj, proj, prm["conv_w"], prm["conv_b"], prm["dt_bias"], prm["a_log"], prm["ssd_d"], prm["ssd_norm"],
      *cv_ops, *s_ops)


def _ssd_state_in(s):
    d, bsz = s.shape[:2]
    return jnp.swapaxes(s, 3, 4).reshape(d, bsz, SSD_GROUPS, LANES, SSD_STATE)


def _ssd_state_out(s):
    d, bsz = s.shape[:2]
    return jnp.swapaxes(s.reshape(d, bsz, HEADS, HEAD_W, SSD_STATE), 3, 4)


def _s5_kernel(*refs, tl, unroll, has_init, has_prev):
    (u0_ref, u1_ref, are_ref, aim_ref, ldt_ref, bre_ref, bim_ref, cw_ref, dsk_ref, gw_ref,
     gb_ref) = refs[:11]
    n_in = 11
    s0r_ref = s0i_ref = None
    if has_init:
        s0r_ref, s0i_ref = refs[n_in], refs[n_in + 1 + has_prev]
    (y_ref, sr_ref, si_ref, wb_ref, ab_ref, us0_ref, us1_ref, ut_ref, bu_ref, yt_ref, ys0_ref,
     ys1_ref) = refs[n_in + 2 * (has_init + has_prev):]
    nbt = SUBLANES * tl

    @pl.when(pl.program_id(1) == 0)
    def _():
        if has_init:
            sr_ref[...] = s0r_ref[...]
            si_ref[...] = s0i_ref[...]
        else:
            sr_ref[...] = jnp.zeros_like(sr_ref)
            si_ref[...] = jnp.zeros_like(si_ref)
        lr, li, st = are_ref[...], aim_ref[...], jnp.exp(ldt_ref[...])
        mag = jnp.exp(lr * st)
        abr = mag * jnp.cos(li * st)
        abi = mag * jnp.sin(li * st)
        ab_ref[0:1, :] = abr
        ab_ref[1:2, :] = abi
        den = lr * lr + li * li
        cr = ((abr - 1.0) * lr + abi * li) / den
        ci = (abi * lr - (abr - 1.0) * li) / den
        wb_ref[:, 0:S5_N] = (cr * bre_ref[...] - ci * bim_ref[...]).astype(BF16)
        wb_ref[:, S5_N:2 * S5_N] = (cr * bim_ref[...] + ci * bre_ref[...]).astype(BF16)

    us0_ref[...] = u0_ref[...].reshape(nbt, LANES)
    us1_ref[...] = u1_ref[...].reshape(nbt, LANES)

    nparts = bu_ref.shape[0]
    prow = nbt // nparts
    psteps = tl // nparts
    for h in range(nparts):
        for t in range(h * psteps, (h + 1) * psteps):
            dst = slice(t * SUBLANES, (t + 1) * SUBLANES)
            src = pl.ds(t, SUBLANES, stride=tl)
            ut_ref[dst, 0:LANES] = us0_ref[src, :]
            ut_ref[dst, LANES:2 * LANES] = us1_ref[src, :]
        u_h = ut_ref[h * prow:(h + 1) * prow, :]
        bu_ref[h] = jnp.dot(u_h.astype(BF16), wb_ref[...], preferred_element_type=F32)
    ar = jnp.broadcast_to(ab_ref[0:1, :], (SUBLANES, S5_N))
    ai = jnp.broadcast_to(ab_ref[1:2, :], (SUBLANES, S5_N))
    xr, xi = sr_ref[...], si_ref[...]
    for h in range(nparts):
        for t in range(tl // nparts):
            rows = slice(t * SUBLANES, (t + 1) * SUBLANES)
            nr = ar * xr - ai * xi + bu_ref[h, rows, 0:S5_N]
            ni = ar * xi + ai * xr + bu_ref[h, rows, S5_N:2 * S5_N]
            bu_ref[h, rows, 0:S5_N] = nr
            bu_ref[h, rows, S5_N:2 * S5_N] = ni
            xr, xi = nr, ni
        y = _mm(bu_ref[h], cw_ref[...]) + dsk_ref[...] * ut_ref[h * prow:(h + 1) * prow, :]
        y = jax.nn.gelu(y)
        yt_ref[h * prow:(h + 1) * prow, :] = y * jax.nn.sigmoid(_mm(y, gw_ref[...]) + gb_ref[...])
        for t in range(h * psteps, (h + 1) * psteps):
            src = slice(t * SUBLANES, (t + 1) * SUBLANES)
            dst = pl.ds(t, SUBLANES, stride=tl)
            ys0_ref[dst, :] = yt_ref[src, 0:LANES]
            ys1_ref[dst, :] = yt_ref[src, LANES:2 * LANES]
    sr_ref[...] = xr
    si_ref[...] = xi
    y_ref[:, :, 0:LANES] = ys0_ref[...].reshape(SUBLANES, tl, LANES)
    y_ref[:, :, LANES:2 * LANES] = ys1_ref[...].reshape(SUBLANES, tl, LANES)


def _s5(proj, prm, init_re, init_im, prev_re, prev_im, l, tl, unroll, nparts):
    bsz, seq, _ = proj.shape
    re_in, re_ops, re_out, re_shape, re_alias = _state_io(l, SUBLANES, (S5_N,), bsz, init_re, prev_re)
    im_in, im_ops, im_out, im_shape, im_alias = _state_io(l, SUBLANES, (S5_N,), bsz, init_im, prev_im)
    n_fixed = 11
    aliases = {}
    if re_alias is not None:
        aliases[n_fixed + re_alias] = 1
        aliases[n_fixed + len(re_in) + im_alias] = 2
    nbt = SUBLANES * tl
    return pl.pallas_call(
        functools.partial(_s5_kernel, tl=tl, unroll=unroll, has_init=init_re is not None,
                          has_prev=prev_re is not None),
        grid=(bsz // SUBLANES, seq // tl),
        in_specs=[pl.BlockSpec((SUBLANES, tl, LANES), lambda i, j: (i, j, COL_S5 // LANES)),
                  pl.BlockSpec((SUBLANES, tl, LANES), lambda i, j: (i, j, COL_S5 // LANES + 1)),
                  _layer_spec(l, (1, S5_N), 2), _layer_spec(l, (1, S5_N), 2), _layer_spec(l, (1, S5_N), 2),
                  _layer_spec(l, (GROUP_W, S5_N), 2), _layer_spec(l, (GROUP_W, S5_N), 2),
                  _layer_spec(l, (2 * S5_N, GROUP_W), 2),
                  _layer_spec(l, (1, GROUP_W), 2),
                  _layer_spec(l, (GROUP_W, GROUP_W), 2),
                  _layer_spec(l, (1, GROUP_W), 2)] + re_in + im_in,
        out_specs=[pl.BlockSpec((SUBLANES, tl, GROUP_W), lambda i, j: (i, j, 0)), re_out, im_out],
        out_shape=[jax.ShapeDtypeStruct((bsz, seq, GROUP_W), F32), re_shape, im_shape],
        scratch_shapes=[pltpu.VMEM((GROUP_W, 2 * S5_N), BF16),
                        pltpu.VMEM((SUBLANES, S5_N), F32),
                        pltpu.VMEM((nbt, LANES), F32), pltpu.VMEM((nbt, LANES), F32),
                        pltpu.VMEM((nbt, GROUP_W), F32),
                        pltpu.VMEM((nparts, nbt // nparts, 2 * S5_N), F32),
                        pltpu.VMEM((nbt, GROUP_W), F32),
                        pltpu.VMEM((nbt, LANES), F32), pltpu.VMEM((nbt, LANES), F32)],
        input_output_aliases=aliases,
        compiler_params=_cparams(("parallel", "arbitrary")),
    )(proj, proj, prm["a_re"], prm["a_im"], prm["log_dt"], prm["b_re"], prm["b_im"], prm["c_blk"],
      prm["s5_d"], prm["glu_w"], prm["glu_b"], *re_ops, *im_ops)


def _block_diag(blocks):
    d, g, r, c = blocks.shape
    eye = jnp.eye(g, dtype=blocks.dtype)
    return (eye[None, :, None, :, None] * blocks[:, :, :, None, :]).reshape(d, g * r, g * c)


def _prep(ssd_conv_w, ssd_conv_b, ssd_dt_bias, ssd_a_log, ssd_d, ssd_norm, s5_a_re, s5_a_im, s5_log_dt,
          s5_b_re, s5_b_im, s5_c_re, s5_c_im, s5_d, s5_glu_w, s5_glu_b):
    per_head = lambda v: jnp.repeat(v, HEAD_W, axis=1)[:, None, :]
    c_blk = jnp.concatenate([_block_diag(jnp.swapaxes(s5_c_re, 2, 3)),
                             -_block_diag(jnp.swapaxes(s5_c_im, 2, 3))], axis=1)
    return dict(
        conv_w=ssd_conv_w, conv_b=ssd_conv_b[:, None, :],
        dt_bias=per_head(ssd_dt_bias), a_log=per_head(ssd_a_log), ssd_d=per_head(ssd_d),
        ssd_norm=ssd_norm[:, None, :],
        a_re=s5_a_re.reshape(DEPTH, 1, S5_N), a_im=s5_a_im.reshape(DEPTH, 1, S5_N),
        log_dt=jnp.repeat(s5_log_dt, S5_STATE, axis=1)[:, None, :],
        b_re=_block_diag(jnp.swapaxes(s5_b_re, 2, 3)), b_im=_block_diag(jnp.swapaxes(s5_b_im, 2, 3)),
        c_blk=c_blk.astype(BF16),
        s5_d=s5_d[:, None, :], glu_w=s5_glu_w.astype(BF16), glu_b=s5_glu_b[:, None, :],
    )


class _Stream:
    def __init__(self, x, pos0, init, cfg, angle_row):
        self.bsz, self.seq, _ = x.shape
        self.cfg = cfg
        self.x2d = x.reshape(self.bsz * self.seq, D_MODEL)
        c_ret, nseq_ret, _ = cfg["ret"]
        rope_rows = self.seq if nseq_ret == 1 else nseq_ret * c_ret
        self.cos, self.sin = _rope_tables(angle_row, pos0, rope_rows, self.seq, cfg["rope_tl"])
        self.init = (None,) * 6
        if init is not None:
            i_ret, i_hg, i_ssd, i_cv, i_re, i_im = init
            self.init = (i_ret, i_hg, _ssd_state_in(i_ssd), i_cv,
                         i_re.reshape(DEPTH, self.bsz, S5_N), i_im.reshape(DEPTH, self.bsz, S5_N))
        self.new = (None,) * 6

    def mix(self, proj2d, l, prm, lb_logits, log_gamma):
        proj = proj2d.reshape(self.bsz, self.seq, IN_COLS_PAD)
        i_ret, i_hg, i_ssd, i_cv, i_re, i_im = self.init
        s_ret, s_hg, s_ssd, s_cv, s_re, s_im = self.new
        cfg = self.cfg
        ya, s_ret = _retention(proj, self.cos, self.sin, log_gamma, i_ret, s_ret, l, *cfg["ret"])
        yb, s_hg = _hgrn2(proj, lb_logits, i_hg, s_hg, l, *cfg["hg"])
        yc, s_ssd, s_cv = _ssd(proj, prm, i_cv, i_ssd, s_cv, s_ssd, l, *cfg["ssd"])
        yd, s_re, s_im = _s5(proj, prm, i_re, i_im, s_re, s_im, l, *cfg["s5"])
        self.new = (s_ret, s_hg, s_ssd, s_cv, s_re, s_im)
        return [y.reshape(self.bsz * self.seq, GROUP_W) for y in (ya, yb, yc, yd)]

    def outputs(self):
        s_ret, s_hg, s_ssd, s_cv, s_re, s_im = self.new
        tail = (DEPTH, self.bsz, S5_GROUPS, S5_STATE)
        return (self.x2d.reshape(self.bsz, self.seq, D_MODEL), s_ret, s_hg, _ssd_state_out(s_ssd), s_cv,
                s_re.reshape(tail), s_im.reshape(tail))


def _trunk(streams, prm, w_in, w_out, w_up, w_down, norm_mix, norm_mlp, norm_final, lb_logits, log_gamma,
           tm_in, tm_out):
    for l in range(DEPTH):
        projs = _inproj([s.x2d for s in streams], norm_mix, w_in, l, tm_in)
        ys = [s.mix(p, l, prm, lb_logits, log_gamma) for s, p in zip(streams, projs)]
        outs = _outmlp([s.x2d for s in streams], ys, w_out, norm_mlp, w_up, w_down, norm_final, l, tm_out)
        for s, x2d in zip(streams, outs):
            s.x2d = x2d
    return [s.outputs() for s in streams]


PROMPT_CFG = dict(rope_tl=256, ret=(256, 1, 4), hg=(64, 1, 8), ssd=(128, 1, 8), s5=(128, 8, 2))
SAMPLE_CFG = dict(rope_tl=128, ret=(8, 16, 2), hg=(8, 16, 2), ssd=(8, 16, 2), s5=(8, 8, 1))
TM_IN = 512
TM_OUT = 512


def kernel(x_prompt, x_sample, state_ret, state_hgrn, state_ssd, state_ssd_conv, state_s5_re, state_s5_im, norm_mix, w_in, w_out, hg_lb_logits, ssd_conv_w, ssd_conv_b, ssd_dt_bias, ssd_a_log, ssd_d, ssd_norm, s5_a_re, s5_a_im, s5_log_dt, s5_b_re, s5_b_im, s5_c_re, s5_c_im, s5_d, s5_glu_w, s5_glu_b, norm_mlp, w_up, w_down, norm_final):
    prm = _prep(ssd_conv_w, ssd_conv_b, ssd_dt_bias, ssd_a_log, ssd_d, ssd_norm, s5_a_re, s5_a_im, s5_log_dt,
                s5_b_re, s5_b_im, s5_c_re, s5_c_im, s5_d, s5_glu_w, s5_glu_b)
    w_out_b = _cast_bf16(w_out, 512)
    w_up_b = _cast_bf16(w_up, 256)
    w_down_b = _cast_bf16(w_down, 1024)
    angle = 1.0 / (ROPE_BASE ** jnp.linspace(0.0, 1.0, HEAD_W // 2, dtype=F32))
    angle_row = jnp.tile(jnp.repeat(angle, 2), 2)[None, :]
    log_gamma = jnp.log(1.0 - jnp.exp2(-5.0 - jnp.arange(HEADS, dtype=F32)))
    states = (state_ret, state_hgrn, state_ssd, state_ssd_conv, state_s5_re, state_s5_im)
    streams = [_Stream(x_prompt, 0, None, PROMPT_CFG, angle_row),
               _Stream(x_sample, PAST_LEN, states, SAMPLE_CFG, angle_row)]
    out_p, out_s = _trunk(streams, prm, w_in, w_out_b, w_up_b, w_down_b, norm_mix[:, None, :],
                          norm_mlp[:, None, :], norm_final[None, :], hg_lb_logits, log_gamma, TM_IN, TM_OUT)
    return (out_p[0], out_s[0]) + out_p[1:] + out_s[1:]
```

```python
import functools
import itertools

import jax
import jax.numpy as jnp
from jax import lax
from jax.experimental import pallas as pl
from jax.experimental.pallas import tpu as pltpu

F32 = jnp.float32
BF16 = jnp.bfloat16

D_MODEL = 1024
DEPTH = 2
PAST_LEN = 16384
GROUP_W = 256
HEADS = 4
HEAD_W = 64
PAIR_HEADS = 2
ROPE_BASE = 10000.0
HG_MIN_FORGET = 1e-30
HG_SAFE_SPREAD = 80.0
SSD_STATE = 128
SSD_GROUPS = 2
SSD_CONV = 4
SSD_CONV_DIM = 768
CONV_PAD = 8
S5_GROUPS = 16
S5_STATE = 64
S5_N = S5_GROUPS * S5_STATE
D_FF = 4096
IN_COLS = 3332
EPS = 1e-6

COL_RET = 0
COL_HG = 1024
COL_SSD = 2048
COL_S5 = 3072
COL_DT = 3328
IN_COLS_PAD = 3584

LANES = 128
SUBLANES = 8
VMEM_LIMIT = 56 * 1024 * 1024


def _cparams(sem):
    return pltpu.CompilerParams(dimension_semantics=sem, vmem_limit_bytes=VMEM_LIMIT)


def _mm(a, b):
    return jnp.dot(a.astype(BF16), b.astype(BF16), preferred_element_type=F32)


def _mm_nt(a, b):
    return lax.dot_general(a.astype(BF16), b.astype(BF16), (((1,), (1,)), ((), ())),
                           preferred_element_type=F32)


def _mm_tn(a, b):
    return lax.dot_general(a.astype(BF16), b.astype(BF16), (((0,), (0,)), ((), ())),
                           preferred_element_type=F32)


def _mm_f32(a, b):
    return jnp.dot(a, b, preferred_element_type=F32, precision=lax.Precision.HIGHEST)


def _silu(x):
    return x * jax.nn.sigmoid(x)


def _rms(x, w):
    return x * lax.rsqrt(jnp.mean(x * x, axis=-1, keepdims=True) + EPS) * w


def _tri(c):
    t = lax.broadcasted_iota(jnp.int32, (c, c), 0)
    s = lax.broadcasted_iota(jnp.int32, (c, c), 1)
    return t >= s


def _head_of_lane(rows, width=GROUP_W):
    return lax.broadcasted_iota(jnp.int32, (rows, width), 1) // HEAD_W


def _expand_heads(x, head):
    nheads = x.shape[1] // HEAD_W
    return jnp.concatenate([jnp.where(head == h, x, 0.0) for h in range(nheads)], axis=0).astype(BF16)


def _block_mask(seg, width=GROUP_W):
    r = lax.broadcasted_iota(jnp.int32, (width, width), 0) // seg
    c = lax.broadcasted_iota(jnp.int32, (width, width), 1) // seg
    return r == c


def _seg_mean(x, m):
    hi = x.astype(BF16)
    lo = (x - hi.astype(F32)).astype(BF16)
    return jnp.dot(hi, m, preferred_element_type=F32) + jnp.dot(lo, m, preferred_element_type=F32)


def _row_blocks(nseq, c_seq, nblk):
    if nseq == 1:
        return [(slice(0, 1), slice(k * c_seq, (k + 1) * c_seq), [0]) for k in range(nblk)]
    return [(slice(k * nseq, (k + 1) * nseq), slice(0, c_seq), list(range(k * nseq, (k + 1) * nseq)))
            for k in range(nblk)]


def _seq_grid(rows, c_seq, copies):
    r = lax.broadcasted_iota(jnp.int32, (rows, copies * rows), 0)
    j = lax.broadcasted_iota(jnp.int32, (rows, copies * rows), 1)
    h = j // rows
    r2 = j - h * rows
    causal = r >= r2
    if rows > c_seq:
        causal = jnp.logical_and(r // c_seq == r2 // c_seq, causal)
    return r, r2, h, causal


def _stack_rows(pieces):
    return pieces[0] if len(pieces) == 1 else jnp.concatenate(pieces, axis=0)


def _mixer_grid(bsz, seq, c_seq, nseq, nblk):
    nb = 1 if nseq == 1 else nseq * nblk
    tb = c_seq * nblk if nseq == 1 else c_seq
    assert nseq == 1 or seq == c_seq
    return nb, tb, (bsz // nb, seq // tb)


def _layer_spec(l, tail, n_grid, single=False):
    zeros = (0,) * len(tail)
    kw = dict(pipeline_mode=pl.Buffered(1)) if single else {}
    if n_grid == 1:
        return pl.BlockSpec((None,) + tail, lambda i: (l,) + zeros, **kw)
    return pl.BlockSpec((None,) + tail, lambda i, j: (l,) + zeros, **kw)


def _state_spec(l, nb, tail):
    zeros = (0,) * len(tail)
    return pl.BlockSpec((None, nb) + tail, lambda i, j: (l, i) + zeros)


def _state_io(l, nb, tail, bsz, init, prev):
    spec = _state_spec(l, nb, tail)
    in_specs, operands = [], []
    if init is not None:
        in_specs.append(spec)
        operands.append(init)
    alias_pos = None
    if prev is not None:
        alias_pos = len(in_specs)
        in_specs.append(pl.BlockSpec(memory_space=pl.ANY))
        operands.append(prev)
    return in_specs, operands, spec, jax.ShapeDtypeStruct((DEPTH, bsz) + tail, F32), alias_pos


def _cast_kernel(w_ref, o_ref):
    o_ref[...] = w_ref[...].astype(BF16)


def _cast_bf16(w, tr):
    d, r, c = w.shape
    spec = pl.BlockSpec((None, tr, c), lambda l, i: (l, i, 0))
    return pl.pallas_call(
        _cast_kernel, grid=(d, r // tr), in_specs=[spec], out_specs=spec,
        out_shape=jax.ShapeDtypeStruct(w.shape, BF16),
        compiler_params=_cparams(("parallel", "parallel")),
    )(w)


def _stream_steps(rows, tm):
    bounds, start = [], 0
    for r in rows:
        bounds.append((start, r // tm))
        start += r // tm
    return bounds, start


def _stream_spec(tm, width, start, steps, single=False):
    kw = dict(pipeline_mode=pl.Buffered(1)) if single else {}
    return pl.BlockSpec((tm, width), lambda i: (jnp.clip(i - start, 0, steps - 1), 0), **kw)


def _when_stream(start, steps):
    i = pl.program_id(0)
    return pl.when(jnp.logical_and(i >= start, i < start + steps))


def _inproj_kernel(*refs, bounds):
    n = len(bounds)
    x_refs, (nw_ref, w_ref), o_refs, wb_ref = refs[:n], refs[n:n + 2], refs[n + 2:2 * n + 2], refs[2 * n + 2]

    @pl.when(pl.program_id(0) == 0)
    def _():
        rows = 256
        head = _head_of_lane(rows)
        for r in range(0, D_MODEL, rows):
            rs = slice(r, r + rows)
            wb_ref[rs, 0:COL_S5] = w_ref[rs, 0:COL_S5].astype(BF16)
            tail = w_ref[rs, COL_S5:IN_COLS]
            wb_ref[rs, COL_S5:COL_DT] = tail[:, HEADS:HEADS + GROUP_W].astype(BF16)
            dtw = tail[:, HEADS - 1:HEADS]
            for h in range(HEADS - 2, -1, -1):
                dtw = jnp.where(head == h, tail[:, h:h + 1], dtw)
            wb_ref[rs, COL_DT:IN_COLS_PAD] = dtw.astype(BF16)

    for x_ref, o_ref, (start, steps) in zip(x_refs, o_refs, bounds):
        @_when_stream(start, steps)
        def _():
            h = _rms(x_ref[...], nw_ref[...])
            o_ref[...] = jnp.dot(h.astype(BF16), wb_ref[...], preferred_element_type=F32)


def _inproj(xs, norm_w, w_in, l, tm):
    bounds, n_steps = _stream_steps([x.shape[0] for x in xs], tm)
    longest = max(steps for _, steps in bounds)
    return pl.pallas_call(
        functools.partial(_inproj_kernel, bounds=bounds),
        grid=(n_steps,),
        in_specs=[_stream_spec(tm, D_MODEL, *b, single=b[1] < longest) for b in bounds]
        + [_layer_spec(l, (1, D_MODEL), 1),
           _layer_spec(l, (D_MODEL, IN_COLS), 1, single=True)],
        out_specs=[_stream_spec(tm, IN_COLS_PAD, *b, single=b[1] < longest) for b in bounds],
        out_shape=[jax.ShapeDtypeStruct((x.shape[0], IN_COLS_PAD), F32) for x in xs],
        scratch_shapes=[pltpu.VMEM((D_MODEL, IN_COLS_PAD), BF16)],
        compiler_params=_cparams(("arbitrary",)),
    )(*xs, norm_w, w_in)


def _outmlp_kernel(*refs, bounds, final_norm, tn_up, tn_down):
    n = len(bounds)
    row_refs = refs[:5 * n]
    wo_ref, nw_ref, wu_ref, wd_ref, nf_ref = refs[5 * n:5 * n + 5]
    o_refs = refs[5 * n + 5:6 * n + 5]
    h_ref, u_ref = refs[6 * n + 5:]

    for k, (start, steps) in enumerate(bounds):
        x_ref, y_refs, o_ref = row_refs[5 * k], row_refs[5 * k + 1:5 * k + 5], o_refs[k]

        @_when_stream(start, steps)
        def _():
            x1 = x_ref[...]
            for g, y_ref in enumerate(y_refs):
                x1 = x1 + _mm(y_ref[...], wo_ref[g * GROUP_W:(g + 1) * GROUP_W, :])
            o_ref[...] = x1
            h_ref[...] = _rms(x1, nw_ref[...]).astype(BF16)
            for c0 in range(0, D_FF, tn_up):
                u = jnp.dot(h_ref[...], wu_ref[:, c0:c0 + tn_up], preferred_element_type=F32)
                u_ref[:, c0:c0 + tn_up] = jnp.square(jnp.maximum(u, 0.0)).astype(BF16)
            for c0 in range(0, D_MODEL, tn_down):
                o_ref[:, c0:c0 + tn_down] += jnp.dot(u_ref[...], wd_ref[:, c0:c0 + tn_down],
                                                     preferred_element_type=F32)
            if final_norm:
                o_ref[...] = _rms(o_ref[...], nf_ref[...])


def _outmlp(xs, ys, w_out, norm_w, w_up, w_down, norm_f, l, tm):
    bounds, n_steps = _stream_steps([x.shape[0] for x in xs], tm)
    row_specs, row_ops = [], []
    for x, y4, b in zip(xs, ys, bounds):
        row_specs += [_stream_spec(tm, D_MODEL, *b)] + [_stream_spec(tm, GROUP_W, *b)] * 4
        row_ops += [x, *y4]
    return pl.pallas_call(
        functools.partial(_outmlp_kernel, bounds=bounds, final_norm=(l == DEPTH - 1), tn_up=512, tn_down=256),
        grid=(n_steps,),
        in_specs=row_specs
        + [_layer_spec(l, (D_MODEL, D_MODEL), 1, single=True),
           _layer_spec(l, (1, D_MODEL), 1),
           _layer_spec(l, (D_MODEL, D_FF), 1, single=True),
           _layer_spec(l, (D_FF, D_MODEL), 1, single=True),
           pl.BlockSpec((1, D_MODEL), lambda i: (0, 0))],
        out_specs=[_stream_spec(tm, D_MODEL, *b) for b in bounds],
        out_shape=[jax.ShapeDtypeStruct((x.shape[0], D_MODEL), F32) for x in xs],
        scratch_shapes=[pltpu.VMEM((tm, D_MODEL), BF16),
                        pltpu.VMEM((tm, D_FF), BF16)],
        compiler_params=_cparams(("arbitrary",)),
    )(*row_ops, w_out, norm_w, w_up, w_down, norm_f)


def _rope_kernel(angle_ref, cos_ref, sin_ref, *, pos0, tl, period):
    i = pl.program_id(0)
    row = i * tl + lax.broadcasted_iota(jnp.int32, (tl, 2 * HEAD_W), 0)
    pos = (pos0 + row % period).astype(F32)
    theta = pos * angle_ref[...]
    lane = lax.broadcasted_iota(jnp.int32, (tl, 2 * HEAD_W), 1)
    cos_ref[...] = jnp.cos(theta)
    s = jnp.sin(theta)
    sin_ref[...] = jnp.where(lane % 2 == 0, -s, s)


def _rope_tables(angle_row, pos0, rows, period, tl):
    seq = rows
    return pl.pallas_call(
        functools.partial(_rope_kernel, pos0=pos0, tl=tl, period=period),
        grid=(seq // tl,),
        in_specs=[pl.BlockSpec((1, 2 * HEAD_W), lambda i: (0, 0))],
        out_specs=[pl.BlockSpec((tl, 2 * HEAD_W), lambda i: (i, 0))] * 2,
        out_shape=[jax.ShapeDtypeStruct((seq, 2 * HEAD_W), F32)] * 2,
        compiler_params=_cparams(("parallel",)),
    )(angle_row)


def _ret_kernel(*refs, c_seq, nseq, nblk, has_init, has_prev):
    lg_ref, p_ref, cos_ref, sin_ref = refs[:4]
    s0_ref = refs[4] if has_init else None
    y_ref, s_ref, sb_ref = refs[4 + has_init + has_prev:]
    nb = s_ref.shape[0]
    c = nseq * c_seq
    j = pl.program_id(1)

    @pl.when(j == 0)
    def _():
        sb_ref[...] = jnp.zeros_like(sb_ref)
        if has_init:
            for b in range(nb):
                for h in range(HEADS):
                    hs = slice(h * HEAD_W, (h + 1) * HEAD_W)
                    sb_ref[b, hs, hs] = s0_ref[b, h]

    def per_head(idx, shape):
        out = jnp.full(shape, lg_ref[HEADS - 1], F32)
        for h in range(HEADS - 2, -1, -1):
            out = jnp.where(idx == h, lg_ref[h], out)
        return out

    head = _head_of_lane(c)
    bd = _block_mask(HEAD_W)
    m_seg = jnp.where(bd, 1.0 / HEAD_W, 0.0).astype(BF16)
    even = lax.broadcasted_iota(jnp.int32, (c, GROUP_W), 1) % 2 == 0
    lg_lane = per_head(head[0:1, :], (1, GROUP_W))
    t_loc = (lax.broadcasted_iota(jnp.int32, (c, 1), 0) % c_seq).astype(F32)
    e_q = jnp.exp((t_loc + 1.0) * lg_lane)
    e_k = jnp.exp((c_seq - 1.0 - t_loc) * lg_lane)
    a_c = jnp.exp(c_seq * lg_lane)
    r_i, r2_i, h_i, causal = _seq_grid(c, c_seq, HEADS)
    dist = jnp.where(causal, (r_i - r2_i).astype(F32), 0.0)
    decay = jnp.where(causal, jnp.exp(dist * per_head(h_i, (c, HEADS * c))), 0.0)

    def rot(x, cos, sin):
        swapped = jnp.where(even, pltpu.roll(x, GROUP_W - 1, 1), pltpu.roll(x, 1, 1))
        return x * cos + swapped * sin

    for bs, ts, seqs in _row_blocks(nseq, c_seq, nblk):
        load = lambda lo: p_ref[bs, ts, lo:lo + GROUP_W].reshape(c, GROUP_W)
        crow = ts if nseq == 1 else slice(0, c)
        cos = jnp.concatenate([cos_ref[crow, :]] * 2, axis=1)
        sin = jnp.concatenate([sin_ref[crow, :]] * 2, axis=1)
        q = rot(load(0), cos, sin)
        k = rot(load(GROUP_W), cos, sin) * (HEAD_W ** -0.5)
        v = load(2 * GROUP_W)
        g = load(3 * GROUP_W)
        scores = _mm_nt(q, _expand_heads(k, head)) * decay
        o = _mm(scores, _expand_heads(v, head))
        qe = q * e_q
        ke = k * e_k
        cross = []
        for i, b in enumerate(seqs):
            sl = slice(i * c_seq, (i + 1) * c_seq)
            s_prev = sb_ref[b]
            cross.append(_mm(qe[sl], s_prev))
            sb_ref[b] = a_c * s_prev + jnp.where(bd, _mm_tn(ke[sl], v[sl]), 0.0)
        o = o + _stack_rows(cross)
        mu = _seg_mean(o, m_seg)
        d = o - mu
        var = _seg_mean(d * d, m_seg)
        y_ref[bs, ts, :] = (d * lax.rsqrt(var + EPS) * _silu(g)).reshape(len(seqs), c_seq, GROUP_W)

    @pl.when(j == pl.num_programs(1) - 1)
    def _():
        for b in range(nb):
            for h in range(HEADS):
                hs = slice(h * HEAD_W, (h + 1) * HEAD_W)
                s_ref[b, h] = sb_ref[b, hs, hs]


def _retention(proj, cos, sin, log_gamma, init, prev, l, c_seq, nseq, nblk):
    bsz, seq, _ = proj.shape
    nb, tb, grid = _mixer_grid(bsz, seq, c_seq, nseq, nblk)
    st_in, st_ops, st_out, st_shape, alias = _state_io(l, nb, (HEADS, HEAD_W, HEAD_W), bsz, init, prev)
    n_fixed = 4
    rope_spec = (pl.BlockSpec((tb, 2 * HEAD_W), lambda i, j: (j, 0)) if nseq == 1 else
                 pl.BlockSpec((nseq * c_seq, 2 * HEAD_W), lambda i, j: (0, 0)))
    return pl.pallas_call(
        functools.partial(_ret_kernel, c_seq=c_seq, nseq=nseq, nblk=nblk, has_init=init is not None,
                          has_prev=prev is not None),
        grid=grid,
        in_specs=[pl.BlockSpec(memory_space=pltpu.SMEM),
                  pl.BlockSpec((nb, tb, 4 * GROUP_W), lambda i, j: (i, j, COL_RET // (4 * GROUP_W))),
                  rope_spec, rope_spec] + st_in,
        out_specs=[pl.BlockSpec((nb, tb, GROUP_W), lambda i, j: (i, j, 0)), st_out],
        out_shape=[jax.ShapeDtypeStruct((bsz, seq, GROUP_W), F32), st_shape],
        scratch_shapes=[pltpu.VMEM((nb, GROUP_W, GROUP_W), F32)],
        input_output_aliases={} if alias is None else {n_fixed + alias: 1},
        compiler_params=_cparams(("parallel", "arbitrary")),
    )(log_gamma, proj, cos, sin, *st_ops)


def _hg_kernel(*refs, c_seq, nseq, nblk, layer, has_init, has_prev):
    p_ref, lb_ref = refs[:2]
    s0_ref = refs[2] if has_init else None
    y_ref, s_ref, st_ref, keep_ref = refs[2 + has_init + has_prev:]
    nb = s_ref.shape[0]
    c = nseq * c_seq
    j = pl.program_id(1)

    @pl.when(j == 0)
    def _():
        st_ref[...] = jnp.zeros_like(st_ref)
        if has_init:
            for b in range(nb):
                for h in range(HEADS):
                    hs = slice(h * HEAD_W, (h + 1) * HEAD_W)
                    st_ref[b, hs, hs] = s0_ref[b, h].T

    logits = lb_ref[...]
    e = jnp.exp(logits - jnp.max(logits, axis=0, keepdims=True))
    sm = e / jnp.sum(e, axis=0, keepdims=True)
    csum = sm[0:1, :]
    for i in range(1, layer + 1):
        csum = csum + sm[i:i + 1, :]
    lower = csum - sm[0:1, :]

    head = _head_of_lane(c)
    bd = _block_mask(HEAD_W)
    m_seg = jnp.where(bd, 1.0 / HEAD_W, 0.0).astype(BF16)
    r_i, r2_i, _, tril = _seq_grid(c, c_seq, 1)
    trif = tril.astype(F32)
    later = r2_i > r_i
    if nseq > 1:
        later = jnp.logical_and(r_i // c_seq == r2_i // c_seq, later)
    laterf = later.astype(F32)
    causal = _seq_grid(c, c_seq, HEADS)[3]
    mid = c // 2 - 1

    staged = []
    spreads = []
    for bs, ts, seqs in _row_blocks(nseq, c_seq, nblk):
        load = lambda lo: p_ref[bs, ts, lo:lo + GROUP_W].reshape(c, GROUP_W)
        qq = _silu(load(0))
        ff = load(GROUP_W)
        v = load(2 * GROUP_W)
        g = load(3 * GROUP_W)
        sig = jax.nn.sigmoid(ff)
        forget = lower + (1.0 - lower) * sig
        log_f = jnp.log(jnp.maximum(forget, HG_MIN_FORGET))
        key = (1.0 - lower) * (1.0 - sig)
        bcum = _mm_f32(trif, log_f)
        if nseq == 1:
            b_ref = bcum[mid:mid + 1, :]
            spread = bcum - b_ref
            q_in = qq * jnp.exp(spread)
            k_in = key * jnp.exp(-spread)
            q_x = q_in * jnp.exp(b_ref)
            k_x = k_in * jnp.exp(bcum[c - 1:c, :] - b_ref)
        else:
            spread = bcum
            q_in = q_x = qq * jnp.exp(bcum)
            k_in = key * jnp.exp(-bcum)
            k_x = key * jnp.exp(_mm_f32(laterf, log_f))
        spreads.append(jnp.max(jnp.abs(spread)))
        scores = jnp.where(causal, _mm_nt(q_in, _expand_heads(k_in, head)), 0.0)
        o = _mm(scores, _expand_heads(v, head))
        upd = []
        for i in range(len(seqs)):
            sl = slice(i * c_seq, (i + 1) * c_seq)
            a_seq = jnp.exp(bcum[(i + 1) * c_seq - 1:(i + 1) * c_seq, :])
            upd.append((q_x[sl], a_seq, jnp.where(bd, _mm_tn(v[sl], k_x[sl]), 0.0)))
        staged.append((bs, ts, seqs, o, upd, _silu(g)))

    unsafe = functools.reduce(jnp.maximum, spreads) > HG_SAFE_SPREAD
    if nb == 1:
        keep_ref[...] = st_ref[0]

    for bs, ts, seqs, o, upd, gate in staged:
        cross = []
        for b, (q_b, a_seq, ds) in zip(seqs, upd):
            st_prev = st_ref[b]
            cross.append(_mm_nt(q_b, st_prev))
            st_ref[b] = st_prev * a_seq + ds
        o = o + _stack_rows(cross)
        y = o * lax.rsqrt(_seg_mean(o * o, m_seg) + EPS) * gate
        y_ref[bs, ts, :] = y.reshape(bs.stop - bs.start, ts.stop - ts.start, GROUP_W)

    @pl.when(unsafe)
    def _():
        tb = p_ref.shape[1]
        assert nb == 1 or (tb == SUBLANES and pl.num_programs(1) == 1)
        rowid = lax.broadcasted_iota(jnp.int32, (SUBLANES, GROUP_W), 0)
        if nb == 1:
            st_ref[0] = keep_ref[...]

        def slab(s, carry):
            b, t0 = (0, pl.multiple_of(s * SUBLANES, SUBLANES)) if nb == 1 else (s, 0)
            rows = pl.ds(t0, SUBLANES)
            if nb > 1:
                st_ref[b] = jnp.zeros((GROUP_W, GROUP_W), F32)
                if has_init:
                    for h in range(HEADS):
                        hs = slice(h * HEAD_W, (h + 1) * HEAD_W)
                        st_ref[b, hs, hs] = s0_ref[b, h].T
            q8 = _silu(p_ref[b, rows, 0:GROUP_W])
            sig = jax.nn.sigmoid(p_ref[b, rows, GROUP_W:2 * GROUP_W])
            v8 = p_ref[b, rows, 2 * GROUP_W:3 * GROUP_W]
            g8 = p_ref[b, rows, 3 * GROUP_W:4 * GROUP_W]
            f8 = jnp.maximum(lower + (1.0 - lower) * sig, HG_MIN_FORGET)
            k8 = (1.0 - lower) * (1.0 - sig)
            o8 = jnp.zeros((SUBLANES, GROUP_W), F32)
            for i in range(SUBLANES):
                outer = _mm_tn(jnp.where(rowid == i, v8, 0.0), k8)
                st = st_ref[b] * f8[i:i + 1, :] + jnp.where(bd, outer, 0.0)
                st_ref[b] = st
                o8 = jnp.where(rowid == i, _mm_nt(q8, st), o8)
            y_ref[b, rows, :] = o8 * lax.rsqrt(_seg_mean(o8 * o8, m_seg) + EPS) * _silu(g8)
            return carry

        lax.fori_loop(0, nb * tb // SUBLANES, slab, 0)

    @pl.when(j == pl.num_programs(1) - 1)
    def _():
        for b in range(nb):
            for h in range(HEADS):
                hs = slice(h * HEAD_W, (h + 1) * HEAD_W)
                s_ref[b, h] = st_ref[b, hs, hs].T


def _hgrn2(proj, lb_logits, init, prev, l, c_seq, nseq, nblk):
    bsz, seq, _ = proj.shape
    nb, tb, grid = _mixer_grid(bsz, seq, c_seq, nseq, nblk)
    st_in, st_ops, st_out, st_shape, alias = _state_io(l, nb, (HEADS, HEAD_W, HEAD_W), bsz, init, prev)
    n_fixed = 2
    return pl.pallas_call(
        functools.partial(_hg_kernel, c_seq=c_seq, nseq=nseq, nblk=nblk, layer=l, has_init=init is not None,
                          has_prev=prev is not None),
        grid=grid,
        in_specs=[pl.BlockSpec((nb, tb, 4 * GROUP_W), lambda i, j: (i, j, COL_HG // (4 * GROUP_W))),
                  pl.BlockSpec((DEPTH, GROUP_W), lambda i, j: (0, 0))] + st_in,
        out_specs=[pl.BlockSpec((nb, tb, GROUP_W), lambda i, j: (i, j, 0)), st_out],
        out_shape=[jax.ShapeDtypeStruct((bsz, seq, GROUP_W), F32), st_shape],
        scratch_shapes=[pltpu.VMEM((nb, GROUP_W, GROUP_W), F32), pltpu.VMEM((GROUP_W, GROUP_W), F32)],
        input_output_aliases={} if alias is None else {n_fixed + alias: 1},
        compiler_params=_cparams(("parallel", "arbitrary")),
    )(proj, lb_logits, *st_ops)


def _ssd_kernel(*refs, c_seq, nseq, nblk, has_init, has_prev):
    p_ref, dt_ref, cw_ref, cb_ref, dtb_ref, alog_ref, dsk_ref, nw_ref = refs[:8]
    n_in = 8
    cv0_ref = s0_ref = None
    if has_init:
        cv0_ref, s0_ref = refs[n_in], refs[n_in + 1 + has_prev]
    y_ref, s_ref, cv_ref, xpad_ref = refs[n_in + 2 * (has_init + has_prev):]
    nb = s_ref.shape[0]
    tb = p_ref.shape[1]
    c = nseq * c_seq
    n_tail = SSD_CONV - 1
    lo = CONV_PAD - n_tail

    @pl.when(pl.program_id(1) == 0)
    def _():
        xpad_ref[:, 0:CONV_PAD, :] = jnp.zeros((nb, CONV_PAD, SSD_CONV_DIM), F32)
        if has_init:
            s_ref[...] = s0_ref[...]
            xpad_ref[:, lo:CONV_PAD, :] = cv0_ref[...]
        else:
            s_ref[...] = jnp.zeros_like(s_ref)

    r_i, r2_i, _, tri = _seq_grid(c, c_seq, 1)
    trif = tri.astype(F32)
    later = r2_i > r_i
    if nseq > 1:
        later = jnp.logical_and(r_i // c_seq == r2_i // c_seq, later)
    laterf = later.astype(F32)
    head = _head_of_lane(c)
    m_grp = jnp.where(_block_mask(LANES), 1.0 / LANES, 0.0).astype(BF16)
    cw = cw_ref[...]
    groups = [slice(g * LANES, (g + 1) * LANES) for g in range(SSD_GROUPS)]

    xpad_ref[:, CONV_PAD:CONV_PAD + tb, :] = p_ref[:, :, GROUP_W:GROUP_W + SSD_CONV_DIM]
    for bs, ts, seqs in _row_blocks(nseq, c_seq, nblk):
        r0 = ts.start
        load = lambda ref, col, w: ref[bs, ts, col:col + w].reshape(c, w)
        z = load(p_ref, 0, GROUP_W)
        conv = cb_ref[...] + cw[n_tail:n_tail + 1, :] * load(p_ref, GROUP_W, SSD_CONV_DIM)
        for w in range(n_tail):
            win = xpad_ref[bs, r0 + lo + w:r0 + lo + w + c_seq, :]
            conv = conv + cw[w:w + 1, :] * win.reshape(c, SSD_CONV_DIM)
        xc = _silu(conv)
        xs = xc[:, 0:GROUP_W]
        bm = xc[:, GROUP_W:2 * GROUP_W]
        cm = xc[:, 2 * GROUP_W:3 * GROUP_W]

        delta = jax.nn.softplus(load(dt_ref, 0, GROUP_W) + dtb_ref[...])
        log_a = -delta * jnp.exp(alog_ref[...])
        bcum = _mm_f32(trif, log_a)
        brem = bcum[c - 1:c, :] - bcum if nseq == 1 else _mm_f32(laterf, log_a)
        xd = xs * delta
        xr = xd * jnp.exp(brem)
        brow = [bcum[:, gl].T for gl in groups]
        gram = [_mm_nt(cm[:, gl], bm[:, gl]) for gl in groups]
        w_parts = []
        for h in range(HEADS):
            g, jh = divmod(h, HEADS // SSD_GROUPS)
            b_t = bcum[:, h * HEAD_W:h * HEAD_W + 1]
            b_s = brow[g][jh * HEAD_W:jh * HEAD_W + 1, :]
            w_parts.append(jnp.where(tri, gram[g] * jnp.exp(b_t - b_s), 0.0))
        o = _mm(jnp.concatenate(w_parts, axis=1), _expand_heads(xd, head))
        cross = []
        for i, b in enumerate(seqs):
            sl = slice(i * c_seq, (i + 1) * c_seq)
            last = (i + 1) * c_seq - 1
            parts = []
            for g, gl in enumerate(groups):
                s_prev = s_ref[b, g]
                parts.append(_mm_nt(cm[sl, gl], s_prev))
                a_col = jnp.exp(brow[g][:, last:last + 1])
                s_ref[b, g] = s_prev * a_col + _mm_tn(xr[sl, gl], bm[sl, gl])
            cross.append(jnp.concatenate(parts, axis=1))
        o = o + jnp.exp(bcum) * _stack_rows(cross)
        y = (o + dsk_ref[...] * xs) * _silu(z)
        y = y * lax.rsqrt(_seg_mean(y * y, m_grp) + EPS) * nw_ref[...]
        y_ref[bs, ts, :] = y.reshape(len(seqs), c_seq, GROUP_W)
    tail = xpad_ref[:, tb + lo:tb + CONV_PAD, :]
    cv_ref[...] = tail
    xpad_ref[:, lo:CONV_PAD, :] = tail


def _ssd(proj, prm, init_cv, init_s, prev_cv, prev_s, l, c_seq, nseq, nblk):
    bsz, seq, _ = proj.shape
    nb, tb, grid = _mixer_grid(bsz, seq, c_seq, nseq, nblk)
    cv_in, cv_ops, cv_out, cv_shape, cv_alias = _state_io(l, nb, (SSD_CONV - 1, SSD_CONV_DIM), bsz,
                                                          init_cv, prev_cv)
    s_in, s_ops, s_out, s_shape, s_alias = _state_io(l, nb, (SSD_GROUPS, LANES, SSD_STATE), bsz,
                                                     init_s, prev_s)
    n_fixed = 8
    aliases = {}
    if cv_alias is not None:
        aliases[n_fixed + cv_alias] = 2
        aliases[n_fixed + len(cv_in) + s_alias] = 1
    return pl.pallas_call(
        functools.partial(_ssd_kernel, c_seq=c_seq, nseq=nseq, nblk=nblk, has_init=init_s is not None,
                          has_prev=prev_s is not None),
        grid=grid,
        in_specs=[pl.BlockSpec((nb, tb, 4 * GROUP_W), lambda i, j: (i, j, COL_SSD // (4 * GROUP_W))),
                  pl.BlockSpec((nb, tb, GROUP_W), lambda i, j: (i, j, COL_DT // GROUP_W)),
                  _layer_spec(l, (SSD_CONV, SSD_CONV_DIM), 2),
                  _layer_spec(l, (1, SSD_CONV_DIM), 2),
                  _layer_spec(l, (1, GROUP_W), 2),
                  _layer_spec(l, (1, GROUP_W), 2),
                  _layer_spec(l, (1, GROUP_W), 2),
                  _layer_spec(l, (1, GROUP_W), 2)] + cv_in + s_in,
        out_specs=[pl.BlockSpec((nb, tb, GROUP_W), lambda i, j: (i, j, 0)), s_out, cv_out],
        out_shape=[jax.ShapeDtypeStruct((bsz, seq, GROUP_W), F32), s_shape, cv_shape],
        scratch_shapes=[pltpu.VMEM((nb, tb + CONV_PAD, SSD_CONV_DIM), F32)],
        input_output_aliases=aliases,
        compiler_params=_cparams(("parallel", "arbitrary")),
    )(proj, proj, prm["conv_w"], prm["conv_b"], prm["dt_bias"], prm["a_log"], prm["ssd_d"], prm["ssd_norm"],
      *cv_ops, *s_ops)


def _ssd_state_in(s):
    d, bsz = s.shape[:2]
    return jnp.swapaxes(s, 3, 4).reshape(d, bsz, SSD_GROUPS, LANES, SSD_STATE)


def _ssd_state_out(s):
    d, bsz = s.shape[:2]
    return jnp.swapaxes(s.reshape(d, bsz, HEADS, HEAD_W, SSD_STATE), 3, 4)


def _s5_kernel(*refs, tl, unroll, has_init, has_prev):
    (u0_ref, u1_ref, are_ref, aim_ref, ldt_ref, bre_ref, bim_ref, cw_ref, dsk_ref, gw_ref,
     gb_ref) = refs[:11]
    n_in = 11
    s0r_ref = s0i_ref = None
    if has_init:
        s0r_ref, s0i_ref = refs[n_in], refs[n_in + 1 + has_prev]
    (y_ref, sr_ref, si_ref, wb_ref, ab_ref, us0_ref, us1_ref, ut_ref, bu_ref, yt_ref, ys0_ref,
     ys1_ref) = refs[n_in + 2 * (has_init + has_prev):]
    nbt = SUBLANES * tl

    @pl.when(pl.program_id(1) == 0)
    def _():
        if has_init:
            sr_ref[...] = s0r_ref[...]
            si_ref[...] = s0i_ref[...]
        else:
            sr_ref[...] = jnp.zeros_like(sr_ref)
            si_ref[...] = jnp.zeros_like(si_ref)
        lr, li, st = are_ref[...], aim_ref[...], jnp.exp(ldt_ref[...])
        mag = jnp.exp(lr * st)
        abr = mag * jnp.cos(li * st)
        abi = mag * jnp.sin(li * st)
        ab_ref[0:1, :] = abr
        ab_ref[1:2, :] = abi
        den = lr * lr + li * li
        cr = ((abr - 1.0) * lr + abi * li) / den
        ci = (abi * lr - (abr - 1.0) * li) / den
        wb_ref[:, 0:S5_N] = (cr * bre_ref[...] - ci * bim_ref[...]).astype(BF16)
        wb_ref[:, S5_N:2 * S5_N] = (cr * bim_ref[...] + ci * bre_ref[...]).astype(BF16)

    us0_ref[...] = u0_ref[...].reshape(nbt, LANES)
    us1_ref[...] = u1_ref[...].reshape(nbt, LANES)

    nparts = bu_ref.shape[0]
    prow = nbt // nparts
    psteps = tl // nparts
    for h in range(nparts):
        for t in range(h * psteps, (h + 1) * psteps):
            dst = slice(t * SUBLANES, (t + 1) * SUBLANES)
            src = pl.ds(t, SUBLANES, stride=tl)
            ut_ref[dst, 0:LANES] = us0_ref[src, :]
            ut_ref[dst, LANES:2 * LANES] = us1_ref[src, :]
        u_h = ut_ref[h * prow:(h + 1) * prow, :]
        bu_ref[h] = jnp.dot(u_h.astype(BF16), wb_ref[...], preferred_element_type=F32)
    ar = jnp.broadcast_to(ab_ref[0:1, :], (SUBLANES, S5_N))
    ai = jnp.broadcast_to(ab_ref[1:2, :], (SUBLANES, S5_N))
    xr, xi = sr_ref[...], si_ref[...]
    for h in range(nparts):
        for t in range(tl // nparts):
            rows = slice(t * SUBLANES, (t + 1) * SUBLANES)
            nr = ar * xr - ai * xi + bu_ref[h, rows, 0:S5_N]
            ni = ar * xi + ai * xr + bu_ref[h, rows, S5_N:2 * S5_N]
            bu_ref[h, rows, 0:S5_N] = nr
            bu_ref[h, rows, S5_N:2 * S5_N] = ni
            xr, xi = nr, ni
        y = _mm(bu_ref[h], cw_ref[...]) + dsk_ref[...] * ut_ref[h * prow:(h + 1) * prow, :]
        y = jax.nn.gelu(y)
        yt_ref[h * prow:(h + 1) * prow, :] = y * jax.nn.sigmoid(_mm(y, gw_ref[...]) + gb_ref[...])
        for t in range(h * psteps, (h + 1) * psteps):
            src = slice(t * SUBLANES, (t + 1) * SUBLANES)
            dst = pl.ds(t, SUBLANES, stride=tl)
            ys0_ref[dst, :] = yt_ref[src, 0:LANES]
            ys1_ref[dst, :] = yt_ref[src, LANES:2 * LANES]
    sr_ref[...] = xr
    si_ref[...] = xi
    y_ref[:, :, 0:LANES] = ys0_ref[...].reshape(SUBLANES, tl, LANES)
    y_ref[:, :, LANES:2 * LANES] = ys1_ref[...].reshape(SUBLANES, tl, LANES)


def _s5(proj, prm, init_re, init_im, prev_re, prev_im, l, tl, unroll, nparts):
    bsz, seq, _ = proj.shape
    re_in, re_ops, re_out, re_shape, re_alias = _state_io(l, SUBLANES, (S5_N,), bsz, init_re, prev_re)
    im_in, im_ops, im_out, im_shape, im_alias = _state_io(l, SUBLANES, (S5_N,), bsz, init_im, prev_im)
    n_fixed = 11
    aliases = {}
    if re_alias is not None:
        aliases[n_fixed + re_alias] = 1
        aliases[n_fixed + len(re_in) + im_alias] = 2
    nbt = SUBLANES * tl
    return pl.pallas_call(
        functools.partial(_s5_kernel, tl=tl, unroll=unroll, has_init=init_re is not None,
                          has_prev=prev_re is not None),
        grid=(bsz // SUBLANES, seq // tl),
        in_specs=[pl.BlockSpec((SUBLANES, tl, LANES), lambda i, j: (i, j, COL_S5 // LANES)),
                  pl.BlockSpec((SUBLANES, tl, LANES), lambda i, j: (i, j, COL_S5 // LANES + 1)),
                  _layer_spec(l, (1, S5_N), 2), _layer_spec(l, (1, S5_N), 2), _layer_spec(l, (1, S5_N), 2),
                  _layer_spec(l, (GROUP_W, S5_N), 2), _layer_spec(l, (GROUP_W, S5_N), 2),
                  _layer_spec(l, (2 * S5_N, GROUP_W), 2),
                  _layer_spec(l, (1, GROUP_W), 2),
                  _layer_spec(l, (GROUP_W, GROUP_W), 2),
                  _layer_spec(l, (1, GROUP_W), 2)] + re_in + im_in,
        out_specs=[pl.BlockSpec((SUBLANES, tl, GROUP_W), lambda i, j: (i, j, 0)), re_out, im_out],
        out_shape=[jax.ShapeDtypeStruct((bsz, seq, GROUP_W), F32), re_shape, im_shape],
        scratch_shapes=[pltpu.VMEM((GROUP_W, 2 * S5_N), BF16),
                        pltpu.VMEM((SUBLANES, S5_N), F32),
                        pltpu.VMEM((nbt, LANES), F32), pltpu.VMEM((nbt, LANES), F32),
                        pltpu.VMEM((nbt, GROUP_W), F32),
                        pltpu.VMEM((nparts, nbt // nparts, 2 * S5_N), F32),
                        pltpu.VMEM((nbt, GROUP_W), F32),
                        pltpu.VMEM((nbt, LANES), F32), pltpu.VMEM((nbt, LANES), F32)],
        input_output_aliases=aliases,
        compiler_params=_cparams(("parallel", "arbitrary")),
    )(proj, proj, prm["a_re"], prm["a_im"], prm["log_dt"], prm["b_re"], prm["b_im"], prm["c_blk"],
      prm["s5_d"], prm["glu_w"], prm["glu_b"], *re_ops, *im_ops)


def _block_diag(blocks):
    d, g, r, c = blocks.shape
    eye = jnp.eye(g, dtype=blocks.dtype)
    return (eye[None, :, None, :, None] * blocks[:, :, :, None, :]).reshape(d, g * r, g * c)


def _prep(ssd_conv_w, ssd_conv_b, ssd_dt_bias, ssd_a_log, ssd_d, ssd_norm, s5_a_re, s5_a_im, s5_log_dt,
          s5_b_re, s5_b_im, s5_c_re, s5_c_im, s5_d, s5_glu_w, s5_glu_b):
    per_head = lambda v: jnp.repeat(v, HEAD_W, axis=1)[:, None, :]
    c_blk = jnp.concatenate([_block_diag(jnp.swapaxes(s5_c_re, 2, 3)),
                             -_block_diag(jnp.swapaxes(s5_c_im, 2, 3))], axis=1)
    return dict(
        conv_w=ssd_conv_w, conv_b=ssd_conv_b[:, None, :],
        dt_bias=per_head(ssd_dt_bias), a_log=per_head(ssd_a_log), ssd_d=per_head(ssd_d),
        ssd_norm=ssd_norm[:, None, :],
        a_re=s5_a_re.reshape(DEPTH, 1, S5_N), a_im=s5_a_im.reshape(DEPTH, 1, S5_N),
        log_dt=jnp.repeat(s5_log_dt, S5_STATE, axis=1)[:, None, :],
        b_re=_block_diag(jnp.swapaxes(s5_b_re, 2, 3)), b_im=_block_diag(jnp.swapaxes(s5_b_im, 2, 3)),
        c_blk=c_blk.astype(BF16),
        s5_d=s5_d[:, None, :], glu_w=s5_glu_w.astype(BF16), glu_b=s5_glu_b[:, None, :],
    )


class _Stream:
    def __init__(self, x, pos0, init, cfg, angle_row):
        self.bsz, self.seq, _ = x.shape
        self.cfg = cfg
        self.x2d = x.reshape(self.bsz * self.seq, D_MODEL)
        c_ret, nseq_ret, _ = cfg["ret"]
        rope_rows = self.seq if nseq_ret == 1 else nseq_ret * c_ret
        self.cos, self.sin = _rope_tables(angle_row, pos0, rope_rows, self.seq, cfg["rope_tl"])
        self.init = (None,) * 6
        if init is not None:
            i_ret, i_hg, i_ssd, i_cv, i_re, i_im = init
            self.init = (i_ret, i_hg, _ssd_state_in(i_ssd), i_cv,
                         i_re.reshape(DEPTH, self.bsz, S5_N), i_im.reshape(DEPTH, self.bsz, S5_N))
        self.new = (None,) * 6

    def mix(self, proj2d, l, prm, lb_logits, log_gamma):
        proj = proj2d.reshape(self.bsz, self.seq, IN_COLS_PAD)
        i_ret, i_hg, i_ssd, i_cv, i_re, i_im = self.init
        s_ret, s_hg, s_ssd, s_cv, s_re, s_im = self.new
        cfg = self.cfg
        ya, s_ret = _retention(proj, self.cos, self.sin, log_gamma, i_ret, s_ret, l, *cfg["ret"])
        yb, s_hg = _hgrn2(proj, lb_logits, i_hg, s_hg, l, *cfg["hg"])
        yc, s_ssd, s_cv = _ssd(proj, prm, i_cv, i_ssd, s_cv, s_ssd, l, *cfg["ssd"])
        yd, s_re, s_im = _s5(proj, prm, i_re, i_im, s_re, s_im, l, *cfg["s5"])
        self.new = (s_ret, s_hg, s_ssd, s_cv, s_re, s_im)
        return [y.reshape(self.bsz * self.seq, GROUP_W) for y in (ya, yb, yc, yd)]

    def outputs(self):
        s_ret, s_hg, s_ssd, s_cv, s_re, s_im = self.new
        tail = (DEPTH, self.bsz, S5_GROUPS, S5_STATE)
        return (self.x2d.reshape(self.bsz, self.seq, D_MODEL), s_ret, s_hg, _ssd_state_out(s_ssd), s_cv,
                s_re.reshape(tail), s_im.reshape(tail))


def _trunk(streams, prm, w_in, w_out, w_up, w_down, norm_mix, norm_mlp, norm_final, lb_logits, log_gamma,
           tm_in, tm_out):
    for l in range(DEPTH):
        projs = _inproj([s.x2d for s in streams], norm_mix, w_in, l, tm_in)
        ys = [s.mix(p, l, prm, lb_logits, log_gamma) for s, p in zip(streams, projs)]
        outs = _outmlp([s.x2d for s in streams], ys, w_out, norm_mlp, w_up, w_down, norm_final, l, tm_out)
        for s, x2d in zip(streams, outs):
            s.x2d = x2d
    return [s.outputs() for s in streams]


PROMPT_CFG = dict(rope_tl=256, ret=(256, 1, 8), hg=(64, 1, 16), ssd=(128, 1, 16), s5=(256, 8, 4))
SAMPLE_CFG = dict(rope_tl=128, ret=(8, 16, 2), hg=(8, 16, 2), ssd=(8, 16, 2), s5=(8, 8, 1))
TM_IN = 512
TM_OUT = 512


def kernel(x_prompt, x_sample, state_ret, state_hgrn, state_ssd, state_ssd_conv, state_s5_re, state_s5_im, norm_mix, w_in, w_out, hg_lb_logits, ssd_conv_w, ssd_conv_b, ssd_dt_bias, ssd_a_log, ssd_d, ssd_norm, s5_a_re, s5_a_im, s5_log_dt, s5_b_re, s5_b_im, s5_c_re, s5_c_im, s5_d, s5_glu_w, s5_glu_b, norm_mlp, w_up, w_down, norm_final):
    prm = _prep(ssd_conv_w, ssd_conv_b, ssd_dt_bias, ssd_a_log, ssd_d, ssd_norm, s5_a_re, s5_a_im, s5_log_dt,
                s5_b_re, s5_b_im, s5_c_re, s5_c_im, s5_d, s5_glu_w, s5_glu_b)
    w_out_b = _cast_bf16(w_out, 512)
    w_up_b = _cast_bf16(w_up, 256)
    w_down_b = _cast_bf16(w_down, 1024)
    angle = 1.0 / (ROPE_BASE ** jnp.linspace(0.0, 1.0, HEAD_W // 2, dtype=F32))
    angle_row = jnp.tile(jnp.repeat(angle, 2), 2)[None, :]
    log_gamma = jnp.log(1.0 - jnp.exp2(-5.0 - jnp.arange(HEADS, dtype=F32)))
    states = (state_ret, state_hgrn, state_ssd, state_ssd_conv, state_s5_re, state_s5_im)
    streams = [_Stream(x_prompt, 0, None, PROMPT_CFG, angle_row),
               _Stream(x_sample, PAST_LEN, states, SAMPLE_CFG, angle_row)]
    out_p, out_s = _trunk(streams, prm, w_in, w_out_b, w_up_b, w_down_b, norm_mix[:, None, :],
                          norm_mlp[:, None, :], norm_final[None, :], hg_lb_logits, log_gamma, TM_IN, TM_OUT)
    return (out_p[0], out_s[0]) + out_p[1:] + out_s[1:]
```

```python
import functools
import itertools

import jax
import jax.numpy as jnp
from jax import lax
from jax.experimental import pallas as pl
from jax.experimental.pallas import tpu as pltpu

F32 = jnp.float32
BF16 = jnp.bfloat16

D_MODEL = 1024
DEPTH = 2
PAST_LEN = 16384
GROUP_W = 256
HEADS = 4
HEAD_W = 64
PAIR_HEADS = 2
ROPE_BASE = 10000.0
HG_MIN_FORGET = 1e-30
HG_SAFE_SPREAD = 80.0
SSD_STATE = 128
SSD_GROUPS = 2
SSD_CONV = 4
SSD_CONV_DIM = 768
CONV_PAD = 8
S5_GROUPS = 16
S5_STATE = 64
S5_N = S5_GROUPS * S5_STATE
D_FF = 4096
IN_COLS = 3332
EPS = 1e-6

COL_RET = 0
COL_HG = 1024
COL_SSD = 2048
COL_S5 = 3072
COL_DT = 3328
IN_COLS_PAD = 3584

LANES = 128
SUBLANES = 8
VMEM_LIMIT = 56 * 1024 * 1024


def _cparams(sem):
    return pltpu.CompilerParams(dimension_semantics=sem, vmem_limit_bytes=VMEM_LIMIT)


def _mm(a, b):
    return jnp.dot(a.astype(BF16), b.astype(BF16), preferred_element_type=F32)


def _mm_nt(a, b):
    return lax.dot_general(a.astype(BF16), b.astype(BF16), (((1,), (1,)), ((), ())),
                           preferred_element_type=F32)


def _mm_tn(a, b):
    return lax.dot_general(a.astype(BF16), b.astype(BF16), (((0,), (0,)), ((), ())),
                           preferred_element_type=F32)


def _mm_f32(a, b):
    return jnp.dot(a, b, preferred_element_type=F32, precision=lax.Precision.HIGHEST)


def _mm_mask_f32(mask01, x):
    m = mask01.astype(BF16)
    hi = x.astype(BF16)
    r1 = x - hi.astype(F32)
    mid = r1.astype(BF16)
    lo = (r1 - mid.astype(F32)).astype(BF16)
    return (jnp.dot(m, hi, preferred_element_type=F32) + jnp.dot(m, mid, preferred_element_type=F32)
            + jnp.dot(m, lo, preferred_element_type=F32))


def _silu(x):
    return x * jax.nn.sigmoid(x)


def _rms(x, w):
    return x * lax.rsqrt(jnp.mean(x * x, axis=-1, keepdims=True) + EPS) * w


def _tri(c):
    t = lax.broadcasted_iota(jnp.int32, (c, c), 0)
    s = lax.broadcasted_iota(jnp.int32, (c, c), 1)
    return t >= s


def _head_of_lane(rows, width=GROUP_W):
    return lax.broadcasted_iota(jnp.int32, (rows, width), 1) // HEAD_W


def _expand_heads(x, head):
    nheads = x.shape[1] // HEAD_W
    return jnp.concatenate([jnp.where(head == h, x, 0.0) for h in range(nheads)], axis=0).astype(BF16)


def _block_mask(seg, width=GROUP_W):
    r = lax.broadcasted_iota(jnp.int32, (width, width), 0) // seg
    c = lax.broadcasted_iota(jnp.int32, (width, width), 1) // seg
    return r == c


def _seg_mean(x, m):
    hi = x.astype(BF16)
    lo = (x - hi.astype(F32)).astype(BF16)
    return jnp.dot(hi, m, preferred_element_type=F32) + jnp.dot(lo, m, preferred_element_type=F32)


def _row_blocks(nseq, c_seq, nblk):
    if nseq == 1:
        return [(slice(0, 1), slice(k * c_seq, (k + 1) * c_seq), [0]) for k in range(nblk)]
    return [(slice(k * nseq, (k + 1) * nseq), slice(0, c_seq), list(range(k * nseq, (k + 1) * nseq)))
            for k in range(nblk)]


def _seq_grid(rows, c_seq, copies):
    r = lax.broadcasted_iota(jnp.int32, (rows, copies * rows), 0)
    j = lax.broadcasted_iota(jnp.int32, (rows, copies * rows), 1)
    h = j // rows
    r2 = j - h * rows
    causal = r >= r2
    if rows > c_seq:
        causal = jnp.logical_and(r // c_seq == r2 // c_seq, causal)
    return r, r2, h, causal


def _stack_rows(pieces):
    return pieces[0] if len(pieces) == 1 else jnp.concatenate(pieces, axis=0)


def _mixer_grid(bsz, seq, c_seq, nseq, nblk):
    nb = 1 if nseq == 1 else nseq * nblk
    tb = c_seq * nblk if nseq == 1 else c_seq
    assert nseq == 1 or seq == c_seq
    return nb, tb, (bsz // nb, seq // tb)


def _layer_spec(l, tail, n_grid, single=False):
    zeros = (0,) * len(tail)
    kw = dict(pipeline_mode=pl.Buffered(1)) if single else {}
    if n_grid == 1:
        return pl.BlockSpec((None,) + tail, lambda i: (l,) + zeros, **kw)
    return pl.BlockSpec((None,) + tail, lambda i, j: (l,) + zeros, **kw)


def _state_spec(l, nb, tail):
    zeros = (0,) * len(tail)
    return pl.BlockSpec((None, nb) + tail, lambda i, j: (l, i) + zeros)


def _state_io(l, nb, tail, bsz, init, prev):
    spec = _state_spec(l, nb, tail)
    in_specs, operands = [], []
    if init is not None:
        in_specs.append(spec)
        operands.append(init)
    alias_pos = None
    if prev is not None:
        alias_pos = len(in_specs)
        in_specs.append(pl.BlockSpec(memory_space=pl.ANY))
        operands.append(prev)
    return in_specs, operands, spec, jax.ShapeDtypeStruct((DEPTH, bsz) + tail, F32), alias_pos


def _cast_kernel(w_ref, o_ref):
    o_ref[...] = w_ref[...].astype(BF16)


def _cast_bf16(w, tr):
    d, r, c = w.shape
    spec = pl.BlockSpec((None, tr, c), lambda l, i: (l, i, 0))
    return pl.pallas_call(
        _cast_kernel, grid=(d, r // tr), in_specs=[spec], out_specs=spec,
        out_shape=jax.ShapeDtypeStruct(w.shape, BF16),
        compiler_params=_cparams(("parallel", "parallel")),
    )(w)


def _stream_steps(rows, tm):
    bounds, start = [], 0
    for r in rows:
        bounds.append((start, r // tm))
        start += r // tm
    return bounds, start


def _stream_spec(tm, width, start, steps, single=False):
    kw = dict(pipeline_mode=pl.Buffered(1)) if single else {}
    return pl.BlockSpec((tm, width), lambda i: (jnp.clip(i - start, 0, steps - 1), 0), **kw)


def _when_stream(start, steps):
    i = pl.program_id(0)
    return pl.when(jnp.logical_and(i >= start, i < start + steps))


def _inproj_kernel(*refs, bounds):
    n = len(bounds)
    x_refs, (nw_ref, w_ref), o_refs, wb_ref = refs[:n], refs[n:n + 2], refs[n + 2:2 * n + 2], refs[2 * n + 2]

    @pl.when(pl.program_id(0) == 0)
    def _():
        rows = 256
        head = _head_of_lane(rows)
        for r in range(0, D_MODEL, rows):
            rs = slice(r, r + rows)
            wb_ref[rs, 0:COL_S5] = w_ref[rs, 0:COL_S5].astype(BF16)
            tail = w_ref[rs, COL_S5:IN_COLS]
            wb_ref[rs, COL_S5:COL_DT] = tail[:, HEADS:HEADS + GROUP_W].astype(BF16)
            dtw = tail[:, HEADS - 1:HEADS]
            for h in range(HEADS - 2, -1, -1):
                dtw = jnp.where(head == h, tail[:, h:h + 1], dtw)
            wb_ref[rs, COL_DT:IN_COLS_PAD] = dtw.astype(BF16)

    for x_ref, o_ref, (start, steps) in zip(x_refs, o_refs, bounds):
        @_when_stream(start, steps)
        def _():
            h = _rms(x_ref[...], nw_ref[...])
            o_ref[...] = jnp.dot(h.astype(BF16), wb_ref[...], preferred_element_type=F32)


def _inproj(xs, norm_w, w_in, l, tm):
    bounds, n_steps = _stream_steps([x.shape[0] for x in xs], tm)
    longest = max(steps for _, steps in bounds)
    return pl.pallas_call(
        functools.partial(_inproj_kernel, bounds=bounds),
        grid=(n_steps,),
        in_specs=[_stream_spec(tm, D_MODEL, *b, single=b[1] < longest) for b in bounds]
        + [_layer_spec(l, (1, D_MODEL), 1),
           _layer_spec(l, (D_MODEL, IN_COLS), 1, single=True)],
        out_specs=[_stream_spec(tm, IN_COLS_PAD, *b, single=b[1] < longest) for b in bounds],
        out_shape=[jax.ShapeDtypeStruct((x.shape[0], IN_COLS_PAD), F32) for x in xs],
        scratch_shapes=[pltpu.VMEM((D_MODEL, IN_COLS_PAD), BF16)],
        compiler_params=_cparams(("arbitrary",)),
    )(*xs, norm_w, w_in)


def _outmlp_kernel(*refs, bounds, final_norm, tn_up, tn_down):
    n = len(bounds)
    row_refs = refs[:5 * n]
    wo_ref, nw_ref, wu_ref, wd_ref, nf_ref = refs[5 * n:5 * n + 5]
    o_refs = refs[5 * n + 5:6 * n + 5]
    h_ref, u_ref = refs[6 * n + 5:]

    for k, (start, steps) in enumerate(bounds):
        x_ref, y_refs, o_ref = row_refs[5 * k], row_refs[5 * k + 1:5 * k + 5], o_refs[k]

        @_when_stream(start, steps)
        def _():
            x1 = x_ref[...]
            for g, y_ref in enumerate(y_refs):
                x1 = x1 + _mm(y_ref[...], wo_ref[g * GROUP_W:(g + 1) * GROUP_W, :])
            o_ref[...] = x1
            h_ref[...] = _rms(x1, nw_ref[...]).astype(BF16)
            for c0 in range(0, D_FF, tn_up):
                u = jnp.dot(h_ref[...], wu_ref[:, c0:c0 + tn_up], preferred_element_type=F32)
                u_ref[:, c0:c0 + tn_up] = jnp.square(jnp.maximum(u, 0.0)).astype(BF16)
            for c0 in range(0, D_MODEL, tn_down):
                o_ref[:, c0:c0 + tn_down] += jnp.dot(u_ref[...], wd_ref[:, c0:c0 + tn_down],
                                                     preferred_element_type=F32)
            if final_norm:
                o_ref[...] = _rms(o_ref[...], nf_ref[...])


def _outmlp(xs, ys, w_out, norm_w, w_up, w_down, norm_f, l, tm):
    bounds, n_steps = _stream_steps([x.shape[0] for x in xs], tm)
    row_specs, row_ops = [], []
    for x, y4, b in zip(xs, ys, bounds):
        row_specs += [_stream_spec(tm, D_MODEL, *b)] + [_stream_spec(tm, GROUP_W, *b)] * 4
        row_ops += [x, *y4]
    return pl.pallas_call(
        functools.partial(_outmlp_kernel, bounds=bounds, final_norm=(l == DEPTH - 1), tn_up=512, tn_down=256),
        grid=(n_steps,),
        in_specs=row_specs
        + [_layer_spec(l, (D_MODEL, D_MODEL), 1, single=True),
           _layer_spec(l, (1, D_MODEL), 1),
           _layer_spec(l, (D_MODEL, D_FF), 1, single=True),
           _layer_spec(l, (D_FF, D_MODEL), 1, single=True),
           pl.BlockSpec((1, D_MODEL), lambda i: (0, 0))],
        out_specs=[_stream_spec(tm, D_MODEL, *b) for b in bounds],
        out_shape=[jax.ShapeDtypeStruct((x.shape[0], D_MODEL), F32) for x in xs],
        scratch_shapes=[pltpu.VMEM((tm, D_MODEL), BF16),
                        pltpu.VMEM((tm, D_FF), BF16)],
        compiler_params=_cparams(("arbitrary",)),
    )(*row_ops, w_out, norm_w, w_up, w_down, norm_f)


def _rope_kernel(angle_ref, cos_ref, sin_ref, *, pos0, tl, period):
    i = pl.program_id(0)
    row = i * tl + lax.broadcasted_iota(jnp.int32, (tl, 2 * HEAD_W), 0)
    pos = (pos0 + row % period).astype(F32)
    theta = pos * angle_ref[...]
    lane = lax.broadcasted_iota(jnp.int32, (tl, 2 * HEAD_W), 1)
    cos_ref[...] = jnp.cos(theta)
    s = jnp.sin(theta)
    sin_ref[...] = jnp.where(lane % 2 == 0, -s, s)


def _rope_tables(angle_row, pos0, rows, period, tl):
    seq = rows
    return pl.pallas_call(
        functools.partial(_rope_kernel, pos0=pos0, tl=tl, period=period),
        grid=(seq // tl,),
        in_specs=[pl.BlockSpec((1, 2 * HEAD_W), lambda i: (0, 0))],
        out_specs=[pl.BlockSpec((tl, 2 * HEAD_W), lambda i: (i, 0))] * 2,
        out_shape=[jax.ShapeDtypeStruct((seq, 2 * HEAD_W), F32)] * 2,
        compiler_params=_cparams(("parallel",)),
    )(angle_row)


def _ret_kernel(*refs, c_seq, nseq, nblk, has_init, has_prev):
    lg_ref, p_ref, cos_ref, sin_ref = refs[:4]
    s0_ref = refs[4] if has_init else None
    y_ref, s_ref, sb_ref = refs[4 + has_init + has_prev:]
    nb = s_ref.shape[0]
    c = nseq * c_seq
    j = pl.program_id(1)

    @pl.when(j == 0)
    def _():
        sb_ref[...] = jnp.zeros_like(sb_ref)
        if has_init:
            for b in range(nb):
                for h in range(HEADS):
                    hs = slice(h * HEAD_W, (h + 1) * HEAD_W)
                    sb_ref[b, hs, hs] = s0_ref[b, h]

    def per_head(idx, shape):
        out = jnp.full(shape, lg_ref[HEADS - 1], F32)
        for h in range(HEADS - 2, -1, -1):
            out = jnp.where(idx == h, lg_ref[h], out)
        return out

    head = _head_of_lane(c)
    bd = _block_mask(HEAD_W)
    m_seg = jnp.where(bd, 1.0 / HEAD_W, 0.0).astype(BF16)
    even = lax.broadcasted_iota(jnp.int32, (c, GROUP_W), 1) % 2 == 0
    lg_lane = per_head(head[0:1, :], (1, GROUP_W))
    t_loc = (lax.broadcasted_iota(jnp.int32, (c, 1), 0) % c_seq).astype(F32)
    e_q = jnp.exp((t_loc + 1.0) * lg_lane)
    e_k = jnp.exp((c_seq - 1.0 - t_loc) * lg_lane)
    a_c = jnp.exp(c_seq * lg_lane)
    r_i, r2_i, h_i, causal = _seq_grid(c, c_seq, HEADS)
    dist = jnp.where(causal, (r_i - r2_i).astype(F32), 0.0)
    decay = jnp.where(causal, jnp.exp(dist * per_head(h_i, (c, HEADS * c))), 0.0)

    def rot(x, cos, sin):
        swapped = jnp.where(even, pltpu.roll(x, GROUP_W - 1, 1), pltpu.roll(x, 1, 1))
        return x * cos + swapped * sin

    for bs, ts, seqs in _row_blocks(nseq, c_seq, nblk):
        load = lambda lo: p_ref[bs, ts, lo:lo + GROUP_W].reshape(c, GROUP_W)
        crow = ts if nseq == 1 else slice(0, c)
        cos = jnp.concatenate([cos_ref[crow, :]] * 2, axis=1)
        sin = jnp.concatenate([sin_ref[crow, :]] * 2, axis=1)
        q = rot(load(0), cos, sin)
        k = rot(load(GROUP_W), cos, sin) * (HEAD_W ** -0.5)
        v = load(2 * GROUP_W)
        g = load(3 * GROUP_W)
        scores = _mm_nt(q, _expand_heads(k, head)) * decay
        o = _mm(scores, _expand_heads(v, head))
        qe = q * e_q
        ke = k * e_k
        cross = []
        for i, b in enumerate(seqs):
            sl = slice(i * c_seq, (i + 1) * c_seq)
            s_prev = sb_ref[b]
            cross.append(_mm(qe[sl], s_prev))
            sb_ref[b] = a_c * s_prev + jnp.where(bd, _mm_tn(ke[sl], v[sl]), 0.0)
        o = o + _stack_rows(cross)
        mu = _seg_mean(o, m_seg)
        d = o - mu
        var = _seg_mean(d * d, m_seg)
        y_ref[bs, ts, :] = (d * lax.rsqrt(var + EPS) * _silu(g)).reshape(len(seqs), c_seq, GROUP_W)

    @pl.when(j == pl.num_programs(1) - 1)
    def _():
        for b in range(nb):
            for h in range(HEADS):
                hs = slice(h * HEAD_W, (h + 1) * HEAD_W)
                s_ref[b, h] = sb_ref[b, hs, hs]


def _retention(proj, cos, sin, log_gamma, init, prev, l, c_seq, nseq, nblk):
    bsz, seq, _ = proj.shape
    nb, tb, grid = _mixer_grid(bsz, seq, c_seq, nseq, nblk)
    st_in, st_ops, st_out, st_shape, alias = _state_io(l, nb, (HEADS, HEAD_W, HEAD_W), bsz, init, prev)
    n_fixed = 4
    rope_spec = (pl.BlockSpec((tb, 2 * HEAD_W), lambda i, j: (j, 0)) if nseq == 1 else
                 pl.BlockSpec((nseq * c_seq, 2 * HEAD_W), lambda i, j: (0, 0)))
    return pl.pallas_call(
        functools.partial(_ret_kernel, c_seq=c_seq, nseq=nseq, nblk=nblk, has_init=init is not None,
                          has_prev=prev is not None),
        grid=grid,
        in_specs=[pl.BlockSpec(memory_space=pltpu.SMEM),
                  pl.BlockSpec((nb, tb, 4 * GROUP_W), lambda i, j: (i, j, COL_RET // (4 * GROUP_W))),
                  rope_spec, rope_spec] + st_in,
        out_specs=[pl.BlockSpec((nb, tb, GROUP_W), lambda i, j: (i, j, 0)), st_out],
        out_shape=[jax.ShapeDtypeStruct((bsz, seq, GROUP_W), F32), st_shape],
        scratch_shapes=[pltpu.VMEM((nb, GROUP_W, GROUP_W), F32)],
        input_output_aliases={} if alias is None else {n_fixed + alias: 1},
        compiler_params=_cparams(("parallel", "arbitrary")),
    )(log_gamma, proj, cos, sin, *st_ops)


def _hg_kernel(*refs, c_seq, nseq, nblk, layer, has_init, has_prev):
    p_ref, lb_ref = refs[:2]
    s0_ref = refs[2] if has_init else None
    y_ref, s_ref, st_ref, keep_ref = refs[2 + has_init + has_prev:]
    nb = s_ref.shape[0]
    c = nseq * c_seq
    j = pl.program_id(1)

    pairs = [slice(pr * LANES, (pr + 1) * LANES) for pr in range(HEADS // PAIR_HEADS)]

    def pair_block(h):
        pr, jh = divmod(h, PAIR_HEADS)
        return pr, slice(jh * HEAD_W, (jh + 1) * HEAD_W)

    def load_state(b):
        st_ref[b] = jnp.zeros(st_ref.shape[1:], F32)
        if has_init:
            for h in range(HEADS):
                pr, hs = pair_block(h)
                st_ref[b, pr, hs, hs] = s0_ref[b, h].T

    @pl.when(j == 0)
    def _():
        for b in range(nb):
            load_state(b)

    logits = lb_ref[...]
    e = jnp.exp(logits - jnp.max(logits, axis=0, keepdims=True))
    sm = e / jnp.sum(e, axis=0, keepdims=True)
    csum = sm[0:1, :]
    for i in range(1, layer + 1):
        csum = csum + sm[i:i + 1, :]
    lower = csum - sm[0:1, :]

    head = _head_of_lane(c)
    m_seg = jnp.where(_block_mask(HEAD_W), 1.0 / HEAD_W, 0.0).astype(BF16)
    bd = _block_mask(HEAD_W, LANES)
    r_i, r2_i, _, tril = _seq_grid(c, c_seq, 1)
    trif = tril.astype(F32)
    later = r2_i > r_i
    if nseq > 1:
        later = jnp.logical_and(r_i // c_seq == r2_i // c_seq, later)
    laterf = later.astype(F32)
    causal = _seq_grid(c, c_seq, HEADS)[3]
    mid = c // 2 - 1
    mask_sum = _mm_mask_f32 if c >= LANES else _mm_f32

    staged = []
    spreads = []
    for bs, ts, seqs in _row_blocks(nseq, c_seq, nblk):
        load = lambda lo: p_ref[bs, ts, lo:lo + GROUP_W].reshape(c, GROUP_W)
        qq = _silu(load(0))
        ff = load(GROUP_W)
        v = load(2 * GROUP_W)
        g = load(3 * GROUP_W)
        sig = jax.nn.sigmoid(ff)
        forget = lower + (1.0 - lower) * sig
        log_f = jnp.log(jnp.maximum(forget, HG_MIN_FORGET))
        key = (1.0 - lower) * (1.0 - sig)
        bcum = mask_sum(trif, log_f)
        if nseq == 1:
            b_ref = bcum[mid:mid + 1, :]
            spread = bcum - b_ref
            q_in = qq * jnp.exp(spread)
            k_in = key * jnp.exp(-spread)
            q_x = q_in * jnp.exp(b_ref)
            k_x = k_in * jnp.exp(bcum[c - 1:c, :] - b_ref)
        else:
            spread = bcum
            q_in = q_x = qq * jnp.exp(bcum)
            k_in = key * jnp.exp(-bcum)
            k_x = key * jnp.exp(mask_sum(laterf, log_f))
        spreads.append(jnp.max(jnp.abs(spread)))
        scores = jnp.where(causal, _mm_nt(q_in, _expand_heads(k_in, head)), 0.0)
        o = _mm(scores, _expand_heads(v, head))
        upd = []
        for i in range(len(seqs)):
            sl = slice(i * c_seq, (i + 1) * c_seq)
            a_seq = jnp.exp(bcum[(i + 1) * c_seq - 1:(i + 1) * c_seq, :])
            ds = [jnp.where(bd, _mm_tn(v[sl, pl_], k_x[sl, pl_]), 0.0) for pl_ in pairs]
            upd.append((q_x[sl], a_seq, ds))
        staged.append((bs, ts, seqs, o, upd, _silu(g)))

    unsafe = functools.reduce(jnp.maximum, spreads) > HG_SAFE_SPREAD
    if nb == 1:
        keep_ref[...] = st_ref[0]

    for bs, ts, seqs, o, upd, gate in staged:
        cross = []
        for b, (q_b, a_seq, ds) in zip(seqs, upd):
            parts = []
            for pr, pl_ in enumerate(pairs):
                st_prev = st_ref[b, pr]
                parts.append(_mm_nt(q_b[:, pl_], st_prev))
                st_ref[b, pr] = st_prev * a_seq[:, pl_] + ds[pr]
            cross.append(jnp.concatenate(parts, axis=1))
        o = o + _stack_rows(cross)
        y = o * lax.rsqrt(_seg_mean(o * o, m_seg) + EPS) * gate
        y_ref[bs, ts, :] = y.reshape(bs.stop - bs.start, ts.stop - ts.start, GROUP_W)

    @pl.when(unsafe)
    def _():
        tb = p_ref.shape[1]
        assert nb == 1 or (tb == SUBLANES and pl.num_programs(1) == 1)
        rowid = lax.broadcasted_iota(jnp.int32, (SUBLANES, GROUP_W), 0)
        if nb == 1:
            st_ref[0] = keep_ref[...]

        def slab(s, carry):
            b, t0 = (0, pl.multiple_of(s * SUBLANES, SUBLANES)) if nb == 1 else (s, 0)
            rows = pl.ds(t0, SUBLANES)
            if nb > 1:
                load_state(b)
            q8 = _silu(p_ref[b, rows, 0:GROUP_W])
            sig = jax.nn.sigmoid(p_ref[b, rows, GROUP_W:2 * GROUP_W])
            v8 = p_ref[b, rows, 2 * GROUP_W:3 * GROUP_W]
            g8 = p_ref[b, rows, 3 * GROUP_W:4 * GROUP_W]
            f8 = jnp.maximum(lower + (1.0 - lower) * sig, HG_MIN_FORGET)
            k8 = (1.0 - lower) * (1.0 - sig)
            o8 = jnp.zeros((SUBLANES, GROUP_W), F32)
            for i in range(SUBLANES):
                v_i = jnp.where(rowid == i, v8, 0.0)
                parts = []
                for pr, pl_ in enumerate(pairs):
                    outer = _mm_tn(v_i[:, pl_], k8[:, pl_])
                    st = st_ref[b, pr] * f8[i:i + 1, pl_] + jnp.where(bd, outer, 0.0)
                    st_ref[b, pr] = st
                    parts.append(_mm_nt(q8[:, pl_], st))
                o8 = jnp.where(rowid == i, jnp.concatenate(parts, axis=1), o8)
            y_ref[b, rows, :] = o8 * lax.rsqrt(_seg_mean(o8 * o8, m_seg) + EPS) * _silu(g8)
            return carry

        lax.fori_loop(0, nb * tb // SUBLANES, slab, 0)

    @pl.when(j == pl.num_programs(1) - 1)
    def _():
        for b in range(nb):
            for h in range(HEADS):
                pr, hs = pair_block(h)
                s_ref[b, h] = st_ref[b, pr, hs, hs].T


def _hgrn2(proj, lb_logits, init, prev, l, c_seq, nseq, nblk):
    bsz, seq, _ = proj.shape
    nb, tb, grid = _mixer_grid(bsz, seq, c_seq, nseq, nblk)
    st_in, st_ops, st_out, st_shape, alias = _state_io(l, nb, (HEADS, HEAD_W, HEAD_W), bsz, init, prev)
    n_fixed = 2
    return pl.pallas_call(
        functools.partial(_hg_kernel, c_seq=c_seq, nseq=nseq, nblk=nblk, layer=l, has_init=init is not None,
                          has_prev=prev is not None),
        grid=grid,
        in_specs=[pl.BlockSpec((nb, tb, 4 * GROUP_W), lambda i, j: (i, j, COL_HG // (4 * GROUP_W))),
                  pl.BlockSpec((DEPTH, GROUP_W), lambda i, j: (0, 0))] + st_in,
        out_specs=[pl.BlockSpec((nb, tb, GROUP_W), lambda i, j: (i, j, 0)), st_out],
        out_shape=[jax.ShapeDtypeStruct((bsz, seq, GROUP_W), F32), st_shape],
        scratch_shapes=[pltpu.VMEM((nb, HEADS // PAIR_HEADS, LANES, LANES), F32),
                        pltpu.VMEM((HEADS // PAIR_HEADS, LANES, LANES), F32)],
        input_output_aliases={} if alias is None else {n_fixed + alias: 1},
        compiler_params=_cparams(("parallel", "arbitrary")),
    )(proj, lb_logits, *st_ops)


def _ssd_kernel(*refs, c_seq, nseq, nblk, has_init, has_prev):
    p_ref, dt_ref, cw_ref, cb_ref, dtb_ref, alog_ref, dsk_ref, nw_ref = refs[:8]
    n_in = 8
    cv0_ref = s0_ref = None
    if has_init:
        cv0_ref, s0_ref = refs[n_in], refs[n_in + 1 + has_prev]
    y_ref, s_ref, cv_ref, xpad_ref = refs[n_in + 2 * (has_init + has_prev):]
    nb = s_ref.shape[0]
    tb = p_ref.shape[1]
    c = nseq * c_seq
    n_tail = SSD_CONV - 1
    lo = CONV_PAD - n_tail

    @pl.when(pl.program_id(1) == 0)
    def _():
        xpad_ref[:, 0:CONV_PAD, :] = jnp.zeros((nb, CONV_PAD, SSD_CONV_DIM), F32)
        if has_init:
            s_ref[...] = s0_ref[...]
            xpad_ref[:, lo:CONV_PAD, :] = cv0_ref[...]
        else:
            s_ref[...] = jnp.zeros_like(s_ref)

    r_i, r2_i, _, tri = _seq_grid(c, c_seq, 1)
    trif = tri.astype(F32)
    later = r2_i > r_i
    if nseq > 1:
        later = jnp.logical_and(r_i // c_seq == r2_i // c_seq, later)
    laterf = later.astype(F32)
    head = _head_of_lane(c)
    m_grp = jnp.where(_block_mask(LANES), 1.0 / LANES, 0.0).astype(BF16)
    cw = cw_ref[...]
    groups = [slice(g * LANES, (g + 1) * LANES) for g in range(SSD_GROUPS)]

    xpad_ref[:, CONV_PAD:CONV_PAD + tb, :] = p_ref[:, :, GROUP_W:GROUP_W + SSD_CONV_DIM]
    for bs, ts, seqs in _row_blocks(nseq, c_seq, nblk):
        r0 = ts.start
        load = lambda ref, col, w: ref[bs, ts, col:col + w].reshape(c, w)
        z = load(p_ref, 0, GROUP_W)
        conv = cb_ref[...] + cw[n_tail:n_tail + 1, :] * load(p_ref, GROUP_W, SSD_CONV_DIM)
        for w in range(n_tail):
            win = xpad_ref[bs, r0 + lo + w:r0 + lo + w + c_seq, :]
            conv = conv + cw[w:w + 1, :] * win.reshape(c, SSD_CONV_DIM)
        xc = _silu(conv)
        xs = xc[:, 0:GROUP_W]
        bm = xc[:, GROUP_W:2 * GROUP_W]
        cm = xc[:, 2 * GROUP_W:3 * GROUP_W]

        delta = jax.nn.softplus(load(dt_ref, 0, GROUP_W) + dtb_ref[...])
        log_a = -delta * jnp.exp(alog_ref[...])
        bcum = _mm_mask_f32(trif, log_a)
        brem = bcum[c - 1:c, :] - bcum if nseq == 1 else _mm_mask_f32(laterf, log_a)
        xd = xs * delta
        xr = xd * jnp.exp(brem)
        brow = [bcum[:, gl].T for gl in groups]
        gram = [_mm_nt(cm[:, gl], bm[:, gl]) for gl in groups]
        w_parts = []
        for h in range(HEADS):
            g, jh = divmod(h, HEADS // SSD_GROUPS)
            b_t = bcum[:, h * HEAD_W:h * HEAD_W + 1]
            b_s = brow[g][jh * HEAD_W:jh * HEAD_W + 1, :]
            w_parts.append(jnp.where(tri, gram[g] * jnp.exp(b_t - b_s), 0.0))
        o = _mm(jnp.concatenate(w_parts, axis=1), _expand_heads(xd, head))
        cross = []
        for i, b in enumerate(seqs):
            sl = slice(i * c_seq, (i + 1) * c_seq)
            last = (i + 1) * c_seq - 1
            parts = []
            for g, gl in enumerate(groups):
                s_prev = s_ref[b, g]
                parts.append(_mm_nt(cm[sl, gl], s_prev))
                a_col = jnp.exp(brow[g][:, last:last + 1])
                s_ref[b, g] = s_prev * a_col + _mm_tn(xr[sl, gl], bm[sl, gl])
            cross.append(jnp.concatenate(parts, axis=1))
        o = o + jnp.exp(bcum) * _stack_rows(cross)
        y = (o + dsk_ref[...] * xs) * _silu(z)
        y = y * lax.rsqrt(_seg_mean(y * y, m_grp) + EPS) * nw_ref[...]
        y_ref[bs, ts, :] = y.reshape(len(seqs), c_seq, GROUP_W)
    tail = xpad_ref[:, tb + lo:tb + CONV_PAD, :]
    cv_ref[...] = tail
    xpad_ref[:, lo:CONV_PAD, :] = tail


def _ssd(proj, prm, init_cv, init_s, prev_cv, prev_s, l, c_seq, nseq, nblk):
    bsz, seq, _ = proj.shape
    nb, tb, grid = _mixer_grid(bsz, seq, c_seq, nseq, nblk)
    cv_in, cv_ops, cv_out, cv_shape, cv_alias = _state_io(l, nb, (SSD_CONV - 1, SSD_CONV_DIM), bsz,
                                                          init_cv, prev_cv)
    s_in, s_ops, s_out, s_shape, s_alias = _state_io(l, nb, (SSD_GROUPS, LANES, SSD_STATE), bsz,
                                                     init_s, prev_s)
    n_fixed = 8
    aliases = {}
    if cv_alias is not None:
        aliases[n_fixed + cv_alias] = 2
        aliases[n_fixed + len(cv_in) + s_alias] = 1
    return pl.pallas_call(
        functools.partial(_ssd_kernel, c_seq=c_seq, nseq=nseq, nblk=nblk, has_init=init_s is not None,
                          has_prev=prev_s is not None),
        grid=grid,
        in_specs=[pl.BlockSpec((nb, tb, 4 * GROUP_W), lambda i, j: (i, j, COL_SSD // (4 * GROUP_W))),
                  pl.BlockSpec((nb, tb, GROUP_W), lambda i, j: (i, j, COL_DT // GROUP_W)),
                  _layer_spec(l, (SSD_CONV, SSD_CONV_DIM), 2),
                  _layer_spec(l, (1, SSD_CONV_DIM), 2),
                  _layer_spec(l, (1, GROUP_W), 2),
                  _layer_spec(l, (1, GROUP_W), 2),
                  _layer_spec(l, (1, GROUP_W), 2),
                  _layer_spec(l, (1, GROUP_W), 2)] + cv_in + s_in,
        out_specs=[pl.BlockSpec((nb, tb, GROUP_W), lambda i, j: (i, j, 0)), s_out, cv_out],
        out_shape=[jax.ShapeDtypeStruct((bsz, seq, GROUP_W), F32), s_shape, cv_shape],
        scratch_shapes=[pltpu.VMEM((nb, tb + CONV_PAD, SSD_CONV_DIM), F32)],
        input_output_aliases=aliases,
        compiler_params=_cparams(("parallel", "arbitrary")),
    )(proj, proj, prm["conv_w"], prm["conv_b"], prm["dt_bias"], prm["a_log"], prm["ssd_d"], prm["ssd_norm"],
      *cv_ops, *s_ops)


def _ssd_state_in(s):
    d, bsz = s.shape[:2]
    return jnp.swapaxes(s, 3, 4).reshape(d, bsz, SSD_GROUPS, LANES, SSD_STATE)


def _ssd_state_out(s):
    d, bsz = s.shape[:2]
    return jnp.swapaxes(s.reshape(d, bsz, HEADS, HEAD_W, SSD_STATE), 3, 4)


def _s5_kernel(*refs, tl, unroll, has_init, has_prev):
    (u0_ref, u1_ref, are_ref, aim_ref, ldt_ref, bre_ref, bim_ref, cw_ref, dsk_ref, gw_ref,
     gb_ref) = refs[:11]
    n_in = 11
    s0r_ref = s0i_ref = None
    if has_init:
        s0r_ref, s0i_ref = refs[n_in], refs[n_in + 1 + has_prev]
    (y_ref, sr_ref, si_ref, wb_ref, ab_ref, us0_ref, us1_ref, ut_ref, bu_ref, yt_ref, ys0_ref,
     ys1_ref) = refs[n_in + 2 * (has_init + has_prev):]
    nbt = SUBLANES * tl

    @pl.when(pl.program_id(1) == 0)
    def _():
        if has_init:
            sr_ref[...] = s0r_ref[...]
            si_ref[...] = s0i_ref[...]
        else:
            sr_ref[...] = jnp.zeros_like(sr_ref)
            si_ref[...] = jnp.zeros_like(si_ref)
        lr, li, st = are_ref[...], aim_ref[...], jnp.exp(ldt_ref[...])
        mag = jnp.exp(lr * st)
        abr = mag * jnp.cos(li * st)
        abi = mag * jnp.sin(li * st)
        ab_ref[0:1, :] = abr
        ab_ref[1:2, :] = abi
        den = lr * lr + li * li
        cr = ((abr - 1.0) * lr + abi * li) / den
        ci = (abi * lr - (abr - 1.0) * li) / den
        wb_ref[:, 0:S5_N] = (cr * bre_ref[...] - ci * bim_ref[...]).astype(BF16)
        wb_ref[:, S5_N:2 * S5_N] = (cr * bim_ref[...] + ci * bre_ref[...]).astype(BF16)

    us0_ref[...] = u0_ref[...].reshape(nbt, LANES)
    us1_ref[...] = u1_ref[...].reshape(nbt, LANES)

    nparts = bu_ref.shape[0]
    prow = nbt // nparts
    psteps = tl // nparts
    for h in range(nparts):
        for t in range(h * psteps, (h + 1) * psteps):
            dst = slice(t * SUBLANES, (t + 1) * SUBLANES)
            src = pl.ds(t, SUBLANES, stride=tl)
            ut_ref[dst, 0:LANES] = us0_ref[src, :]
            ut_ref[dst, LANES:2 * LANES] = us1_ref[src, :]
        u_h = ut_ref[h * prow:(h + 1) * prow, :]
        bu_ref[h] = jnp.dot(u_h.astype(BF16), wb_ref[...], preferred_element_type=F32)
    ar = jnp.broadcast_to(ab_ref[0:1, :], (SUBLANES, S5_N))
    ai = jnp.broadcast_to(ab_ref[1:2, :], (SUBLANES, S5_N))
    xr, xi = sr_ref[...], si_ref[...]
    for h in range(nparts):
        for t in range(tl // nparts):
            rows = slice(t * SUBLANES, (t + 1) * SUBLANES)
            nr = ar * xr - ai * xi + bu_ref[h, rows, 0:S5_N]
            ni = ar * xi + ai * xr + bu_ref[h, rows, S5_N:2 * S5_N]
            bu_ref[h, rows, 0:S5_N] = nr
            bu_ref[h, rows, S5_N:2 * S5_N] = ni
            xr, xi = nr, ni
        y = _mm(bu_ref[h], cw_ref[...]) + dsk_ref[...] * ut_ref[h * prow:(h + 1) * prow, :]
        y = jax.nn.gelu(y)
        yt_ref[h * prow:(h + 1) * prow, :] = y * jax.nn.sigmoid(_mm(y, gw_ref[...]) + gb_ref[...])
        for t in range(h * psteps, (h + 1) * psteps):
            src = slice(t * SUBLANES, (t + 1) * SUBLANES)
            dst = pl.ds(t, SUBLANES, stride=tl)
            ys0_ref[dst, :] = yt_ref[src, 0:LANES]
            ys1_ref[dst, :] = yt_ref[src, LANES:2 * LANES]
    sr_ref[...] = xr
    si_ref[...] = xi
    y_ref[:, :, 0:LANES] = ys0_ref[...].reshape(SUBLANES, tl, LANES)
    y_ref[:, :, LANES:2 * LANES] = ys1_ref[...].reshape(SUBLANES, tl, LANES)


def _s5(proj, prm, init_re, init_im, prev_re, prev_im, l, tl, unroll, nparts):
    bsz, seq, _ = proj.shape
    re_in, re_ops, re_out, re_shape, re_alias = _state_io(l, SUBLANES, (S5_N,), bsz, init_re, prev_re)
    im_in, im_ops, im_out, im_shape, im_alias = _state_io(l, SUBLANES, (S5_N,), bsz, init_im, prev_im)
    n_fixed = 11
    aliases = {}
    if re_alias is not None:
        aliases[n_fixed + re_alias] = 1
        aliases[n_fixed + len(re_in) + im_alias] = 2
    nbt = SUBLANES * tl
    return pl.pallas_call(
        functools.partial(_s5_kernel, tl=tl, unroll=unroll, has_init=init_re is not None,
                          has_prev=prev_re is not None),
        grid=(bsz // SUBLANES, seq // tl),
        in_specs=[pl.BlockSpec((SUBLANES, tl, LANES), lambda i, j: (i, j, COL_S5 // LANES)),
                  pl.BlockSpec((SUBLANES, tl, LANES), lambda i, j: (i, j, COL_S5 // LANES + 1)),
                  _layer_spec(l, (1, S5_N), 2), _layer_spec(l, (1, S5_N), 2), _layer_spec(l, (1, S5_N), 2),
                  _layer_spec(l, (GROUP_W, S5_N), 2), _layer_spec(l, (GROUP_W, S5_N), 2),
                  _layer_spec(l, (2 * S5_N, GROUP_W), 2),
                  _layer_spec(l, (1, GROUP_W), 2),
                  _layer_spec(l, (GROUP_W, GROUP_W), 2),
                  _layer_spec(l, (1, GROUP_W), 2)] + re_in + im_in,
        out_specs=[pl.BlockSpec((SUBLANES, tl, GROUP_W), lambda i, j: (i, j, 0)), re_out, im_out],
        out_shape=[jax.ShapeDtypeStruct((bsz, seq, GROUP_W), F32), re_shape, im_shape],
        scratch_shapes=[pltpu.VMEM((GROUP_W, 2 * S5_N), BF16),
                        pltpu.VMEM((SUBLANES, S5_N), F32),
                        pltpu.VMEM((nbt, LANES), F32), pltpu.VMEM((nbt, LANES), F32),
                        pltpu.VMEM((nbt, GROUP_W), F32),
                        pltpu.VMEM((nparts, nbt // nparts, 2 * S5_N), F32),
                        pltpu.VMEM((nbt, GROUP_W), F32),
                        pltpu.VMEM((nbt, LANES), F32), pltpu.VMEM((nbt, LANES), F32)],
        input_output_aliases=aliases,
        compiler_params=_cparams(("parallel", "arbitrary")),
    )(proj, proj, prm["a_re"], prm["a_im"], prm["log_dt"], prm["b_re"], prm["b_im"], prm["c_blk"],
      prm["s5_d"], prm["glu_w"], prm["glu_b"], *re_ops, *im_ops)


def _block_diag(blocks):
    d, g, r, c = blocks.shape
    eye = jnp.eye(g, dtype=blocks.dtype)
    return (eye[None, :, None, :, None] * blocks[:, :, :, None, :]).reshape(d, g * r, g * c)


def _prep(ssd_conv_w, ssd_conv_b, ssd_dt_bias, ssd_a_log, ssd_d, ssd_norm, s5_a_re, s5_a_im, s5_log_dt,
          s5_b_re, s5_b_im, s5_c_re, s5_c_im, s5_d, s5_glu_w, s5_glu_b):
    per_head = lambda v: jnp.repeat(v, HEAD_W, axis=1)[:, None, :]
    c_blk = jnp.concatenate([_block_diag(jnp.swapaxes(s5_c_re, 2, 3)),
                             -_block_diag(jnp.swapaxes(s5_c_im, 2, 3))], axis=1)
    return dict(
        conv_w=ssd_conv_w, conv_b=ssd_conv_b[:, None, :],
        dt_bias=per_head(ssd_dt_bias), a_log=per_head(ssd_a_log), ssd_d=per_head(ssd_d),
        ssd_norm=ssd_norm[:, None, :],
        a_re=s5_a_re.reshape(DEPTH, 1, S5_N), a_im=s5_a_im.reshape(DEPTH, 1, S5_N),
        log_dt=jnp.repeat(s5_log_dt, S5_STATE, axis=1)[:, None, :],
        b_re=_block_diag(jnp.swapaxes(s5_b_re, 2, 3)), b_im=_block_diag(jnp.swapaxes(s5_b_im, 2, 3)),
        c_blk=c_blk.astype(BF16),
        s5_d=s5_d[:, None, :], glu_w=s5_glu_w.astype(BF16), glu_b=s5_glu_b[:, None, :],
    )


class _Stream:
    def __init__(self, x, pos0, init, cfg, angle_row):
        self.bsz, self.seq, _ = x.shape
        self.cfg = cfg
        self.x2d = x.reshape(self.bsz * self.seq, D_MODEL)
        c_ret, nseq_ret, _ = cfg["ret"]
        rope_rows = self.seq if nseq_ret == 1 else nseq_ret * c_ret
        self.cos, self.sin = _rope_tables(angle_row, pos0, rope_rows, self.seq, cfg["rope_tl"])
        self.init = (None,) * 6
        if init is not None:
            i_ret, i_hg, i_ssd, i_cv, i_re, i_im = init
            self.init = (i_ret, i_hg, _ssd_state_in(i_ssd), i_cv,
                         i_re.reshape(DEPTH, self.bsz, S5_N), i_im.reshape(DEPTH, self.bsz, S5_N))
        self.new = (None,) * 6

    def mix(self, proj2d, l, prm, lb_logits, log_gamma):
        proj = proj2d.reshape(self.bsz, self.seq, IN_COLS_PAD)
        i_ret, i_hg, i_ssd, i_cv, i_re, i_im = self.init
        s_ret, s_hg, s_ssd, s_cv, s_re, s_im = self.new
        cfg = self.cfg
        ya, s_ret = _retention(proj, self.cos, self.sin, log_gamma, i_ret, s_ret, l, *cfg["ret"])
        yb, s_hg = _hgrn2(proj, lb_logits, i_hg, s_hg, l, *cfg["hg"])
        yc, s_ssd, s_cv = _ssd(proj, prm, i_cv, i_ssd, s_cv, s_ssd, l, *cfg["ssd"])
        yd, s_re, s_im = _s5(proj, prm, i_re, i_im, s_re, s_im, l, *cfg["s5"])
        self.new = (s_ret, s_hg, s_ssd, s_cv, s_re, s_im)
        return [y.reshape(self.bsz * self.seq, GROUP_W) for y in (ya, yb, yc, yd)]

    def outputs(self):
        s_ret, s_hg, s_ssd, s_cv, s_re, s_im = self.new
        tail = (DEPTH, self.bsz, S5_GROUPS, S5_STATE)
        return (self.x2d.reshape(self.bsz, self.seq, D_MODEL), s_ret, s_hg, _ssd_state_out(s_ssd), s_cv,
                s_re.reshape(tail), s_im.reshape(tail))


def _trunk(streams, prm, w_in, w_out, w_up, w_down, norm_mix, norm_mlp, norm_final, lb_logits, log_gamma,
           tm_in, tm_out):
    for l in range(DEPTH):
        projs = _inproj([s.x2d for s in streams], norm_mix, w_in, l, tm_in)
        ys = [s.mix(p, l, prm, lb_logits, log_gamma) for s, p in zip(streams, projs)]
        outs = _outmlp([s.x2d for s in streams], ys, w_out, norm_mlp, w_up, w_down, norm_final, l, tm_out)
        for s, x2d in zip(streams, outs):
            s.x2d = x2d
    return [s.outputs() for s in streams]


PROMPT_CFG = dict(rope_tl=256, ret=(256, 1, 8), hg=(64, 1, 16), ssd=(128, 1, 16), s5=(256, 8, 4))
SAMPLE_CFG = dict(rope_tl=128, ret=(8, 16, 2), hg=(8, 16, 2), ssd=(8, 16, 2), s5=(8, 8, 1))
TM_IN = 512
TM_OUT = 512


def kernel(x_prompt, x_sample, state_ret, state_hgrn, state_ssd, state_ssd_conv, state_s5_re, state_s5_im, norm_mix, w_in, w_out, hg_lb_logits, ssd_conv_w, ssd_conv_b, ssd_dt_bias, ssd_a_log, ssd_d, ssd_norm, s5_a_re, s5_a_im, s5_log_dt, s5_b_re, s5_b_im, s5_c_re, s5_c_im, s5_d, s5_glu_w, s5_glu_b, norm_mlp, w_up, w_down, norm_final):
    prm = _prep(ssd_conv_w, ssd_conv_b, ssd_dt_bias, ssd_a_log, ssd_d, ssd_norm, s5_a_re, s5_a_im, s5_log_dt,
                s5_b_re, s5_b_im, s5_c_re, s5_c_im, s5_d, s5_glu_w, s5_glu_b)
    w_out_b = _cast_bf16(w_out, 512)
    w_up_b = _cast_bf16(w_up, 256)
    w_down_b = _cast_bf16(w_down, 1024)
    angle = 1.0 / (ROPE_BASE ** jnp.linspace(0.0, 1.0, HEAD_W // 2, dtype=F32))
    angle_row = jnp.tile(jnp.repeat(angle, 2), 2)[None, :]
    log_gamma = jnp.log(1.0 - jnp.exp2(-5.0 - jnp.arange(HEADS, dtype=F32)))
    states = (state_ret, state_hgrn, state_ssd, state_ssd_conv, state_s5_re, state_s5_im)
    streams = [_Stream(x_prompt, 0, None, PROMPT_CFG, angle_row),
               _Stream(x_sample, PAST_LEN, states, SAMPLE_CFG, angle_row)]
    out_p, out_s = _trunk(streams, prm, w_in, w_out_b, w_up_b, w_down_b, norm_mix[:, None, :],
                          norm_mlp[:, None, :], norm_final[None, :], hg_lb_logits, log_gamma, TM_IN, TM_OUT)
    return (out_p[0], out_s[0]) + out_p[1:] + out_s[1:]
```

```python
import functools

import jax
import jax.numpy as jnp
from jax import lax
from jax.experimental import pallas as pl
from jax.experimental.pallas import tpu as pltpu

F32 = jnp.float32
BF16 = jnp.bfloat16

D_MODEL = 1024
DEPTH = 2
PAST_LEN = 16384
GROUP_W = 256
HEADS = 4
HEAD_W = 64
PAIR_HEADS = 2
ROPE_BASE = 10000.0
HG_MIN_FORGET = 1e-30
HG_SAFE_SPREAD = 80.0
SSD_STATE = 128
SSD_GROUPS = 2
SSD_CONV = 4
SSD_CONV_DIM = 768
CONV_PAD = 8
S5_GROUPS = 16
S5_STATE = 64
S5_N = S5_GROUPS * S5_STATE
D_FF = 4096
IN_COLS = 3332
EPS = 1e-6

COL_RET = 0
COL_HG = 1024
COL_SSD = 2048
COL_S5 = 3072
COL_DT = 3328
IN_COLS_PAD = 3584

LANES = 128
SUBLANES = 8
VMEM_LIMIT = 56 * 1024 * 1024


def _cparams(sem):
    return pltpu.CompilerParams(dimension_semantics=sem, vmem_limit_bytes=VMEM_LIMIT)


def _mm(a, b):
    return jnp.dot(a.astype(BF16), b.astype(BF16), preferred_element_type=F32)


def _mm_nt(a, b):
    return lax.dot_general(a.astype(BF16), b.astype(BF16), (((1,), (1,)), ((), ())),
                           preferred_element_type=F32)


def _mm_tn(a, b):
    return lax.dot_general(a.astype(BF16), b.astype(BF16), (((0,), (0,)), ((), ())),
                           preferred_element_type=F32)


def _mm_f32(a, b):
    return jnp.dot(a, b, preferred_element_type=F32, precision=lax.Precision.HIGHEST)


def _mm_mask_f32(mask01, x):
    m = mask01.astype(BF16)
    hi = x.astype(BF16)
    r1 = x - hi.astype(F32)
    mid = r1.astype(BF16)
    lo = (r1 - mid.astype(F32)).astype(BF16)
    return (jnp.dot(m, hi, preferred_element_type=F32) + jnp.dot(m, mid, preferred_element_type=F32)
            + jnp.dot(m, lo, preferred_element_type=F32))


def _silu(x):
    return x * jax.nn.sigmoid(x)


def _rms(x, w):
    return x * lax.rsqrt(jnp.mean(x * x, axis=-1, keepdims=True) + EPS) * w


def _head_of_lane(rows):
    return lax.broadcasted_iota(jnp.int32, (rows, GROUP_W), 1) // HEAD_W


def _expand_heads(x, head):
    nheads = x.shape[1] // HEAD_W
    return jnp.concatenate([jnp.where(head == h, x, 0.0) for h in range(nheads)], axis=0).astype(BF16)


def _block_mask(seg, width=GROUP_W):
    r = lax.broadcasted_iota(jnp.int32, (width, width), 0) // seg
    c = lax.broadcasted_iota(jnp.int32, (width, width), 1) // seg
    return r == c


def _seg_mean(x, m):
    hi = x.astype(BF16)
    lo = (x - hi.astype(F32)).astype(BF16)
    return jnp.dot(hi, m, preferred_element_type=F32) + jnp.dot(lo, m, preferred_element_type=F32)


def _row_blocks(nseq, c_seq, nblk):
    if nseq == 1:
        return [(slice(0, 1), slice(k * c_seq, (k + 1) * c_seq), [0]) for k in range(nblk)]
    return [(slice(k * nseq, (k + 1) * nseq), slice(0, c_seq), list(range(k * nseq, (k + 1) * nseq)))
            for k in range(nblk)]


def _seq_grid(rows, c_seq, copies):
    r = lax.broadcasted_iota(jnp.int32, (rows, copies * rows), 0)
    j = lax.broadcasted_iota(jnp.int32, (rows, copies * rows), 1)
    h = j // rows
    r2 = j - h * rows
    causal = r >= r2
    if rows > c_seq:
        causal = jnp.logical_and(r // c_seq == r2 // c_seq, causal)
    return r, r2, h, causal


def _stack_rows(pieces):
    return pieces[0] if len(pieces) == 1 else jnp.concatenate(pieces, axis=0)


def _mixer_grid(bsz, seq, c_seq, nseq, nblk):
    nb = 1 if nseq == 1 else nseq * nblk
    tb = c_seq * nblk if nseq == 1 else c_seq
    assert nseq == 1 or seq == c_seq
    return nb, tb, (bsz // nb, seq // tb)


def _layer_spec(l, tail, n_grid, single=False):
    zeros = (0,) * len(tail)
    kw = dict(pipeline_mode=pl.Buffered(1)) if single else {}
    if n_grid == 1:
        return pl.BlockSpec((None,) + tail, lambda i: (l,) + zeros, **kw)
    return pl.BlockSpec((None,) + tail, lambda i, j: (l,) + zeros, **kw)


def _state_spec(l, nb, tail):
    zeros = (0,) * len(tail)
    return pl.BlockSpec((None, nb) + tail, lambda i, j: (l, i) + zeros)


def _state_io(l, nb, tail, bsz, init, prev):
    spec = _state_spec(l, nb, tail)
    in_specs, operands = [], []
    if init is not None:
        in_specs.append(spec)
        operands.append(init)
    alias_pos = None
    if prev is not None:
        alias_pos = len(in_specs)
        in_specs.append(pl.BlockSpec(memory_space=pl.ANY))
        operands.append(prev)
    return in_specs, operands, spec, jax.ShapeDtypeStruct((DEPTH, bsz) + tail, F32), alias_pos


def _cast_kernel(w_ref, o_ref):
    o_ref[...] = w_ref[...].astype(BF16)


def _cast_bf16(w, tr):
    d, r, c = w.shape
    spec = pl.BlockSpec((None, tr, c), lambda l, i: (l, i, 0))
    return pl.pallas_call(
        _cast_kernel, grid=(d, r // tr), in_specs=[spec], out_specs=spec,
        out_shape=jax.ShapeDtypeStruct(w.shape, BF16),
        compiler_params=_cparams(("parallel", "parallel")),
    )(w)


def _stream_steps(rows, tm):
    bounds, start = [], 0
    for r in rows:
        bounds.append((start, r // tm))
        start += r // tm
    return bounds, start


def _stream_spec(tm, width, start, steps, single=False):
    kw = dict(pipeline_mode=pl.Buffered(1)) if single else {}
    return pl.BlockSpec((tm, width), lambda i: (jnp.clip(i - start, 0, steps - 1), 0), **kw)


def _when_stream(start, steps):
    i = pl.program_id(0)
    return pl.when(jnp.logical_and(i >= start, i < start + steps))


def _inproj_kernel(*refs, bounds):
    n = len(bounds)
    x_refs, (nw_ref, w_ref), o_refs, wb_ref = refs[:n], refs[n:n + 2], refs[n + 2:2 * n + 2], refs[2 * n + 2]

    @pl.when(pl.program_id(0) == 0)
    def _():
        rows = 256
        for r in range(0, COL_S5, rows):
            wb_ref[r:r + rows, :] = w_ref[r:r + rows, :].astype(BF16)
        wb_ref[COL_S5:COL_DT, :] = w_ref[COL_S5 + HEADS:IN_COLS, :].astype(BF16)
        for h in range(HEADS):
            row = w_ref[COL_S5 + h:COL_S5 + h + 1, :]
            wb_ref[COL_DT + h * HEAD_W:COL_DT + (h + 1) * HEAD_W, :] = jnp.broadcast_to(
                row, (HEAD_W, D_MODEL)).astype(BF16)

    for x_ref, o_ref, (start, steps) in zip(x_refs, o_refs, bounds):
        @_when_stream(start, steps)
        def _():
            h = _rms(x_ref[...], nw_ref[...])
            o_ref[...] = _mm_nt(h, wb_ref[...])


def _inproj(xs, norm_w, w_in_t, l, tm):
    bounds, n_steps = _stream_steps([x.shape[0] for x in xs], tm)
    longest = max(steps for _, steps in bounds)
    return pl.pallas_call(
        functools.partial(_inproj_kernel, bounds=bounds),
        grid=(n_steps,),
        in_specs=[_stream_spec(tm, D_MODEL, *b, single=b[1] < longest) for b in bounds]
        + [_layer_spec(l, (1, D_MODEL), 1),
           _layer_spec(l, (IN_COLS, D_MODEL), 1, single=True)],
        out_specs=[_stream_spec(tm, IN_COLS_PAD, *b, single=b[1] < longest) for b in bounds],
        out_shape=[jax.ShapeDtypeStruct((x.shape[0], IN_COLS_PAD), F32) for x in xs],
        scratch_shapes=[pltpu.VMEM((IN_COLS_PAD, D_MODEL), BF16)],
        compiler_params=_cparams(("arbitrary",)),
    )(*xs, norm_w, w_in_t)


def _outmlp_kernel(*refs, bounds, final_norm, tn_up, tn_down):
    n = len(bounds)
    row_refs = refs[:5 * n]
    wo_ref, nw_ref, wu_ref, wd_ref, nf_ref = refs[5 * n:5 * n + 5]
    o_refs = refs[5 * n + 5:6 * n + 5]
    h_ref, u_ref = refs[6 * n + 5:]

    for k, (start, steps) in enumerate(bounds):
        x_ref, y_refs, o_ref = row_refs[5 * k], row_refs[5 * k + 1:5 * k + 5], o_refs[k]

        @_when_stream(start, steps)
        def _():
            x1 = x_ref[...]
            for g, y_ref in enumerate(y_refs):
                x1 = x1 + _mm(y_ref[...], wo_ref[g * GROUP_W:(g + 1) * GROUP_W, :])
            o_ref[...] = x1
            h_ref[...] = _rms(x1, nw_ref[...]).astype(BF16)
            for c0 in range(0, D_FF, tn_up):
                u = jnp.dot(h_ref[...], wu_ref[:, c0:c0 + tn_up], preferred_element_type=F32)
                u_ref[:, c0:c0 + tn_up] = jnp.square(jnp.maximum(u, 0.0)).astype(BF16)
            for c0 in range(0, D_MODEL, tn_down):
                o_ref[:, c0:c0 + tn_down] += jnp.dot(u_ref[...], wd_ref[:, c0:c0 + tn_down],
                                                     preferred_element_type=F32)
            if final_norm:
                o_ref[...] = _rms(o_ref[...], nf_ref[...])


def _outmlp(xs, ys, w_out, norm_w, w_up, w_down, norm_f, l, tm):
    bounds, n_steps = _stream_steps([x.shape[0] for x in xs], tm)
    row_specs, row_ops = [], []
    for x, y4, b in zip(xs, ys, bounds):
        row_specs += [_stream_spec(tm, D_MODEL, *b)] + [_stream_spec(tm, GROUP_W, *b)] * 4
        row_ops += [x, *y4]
    return pl.pallas_call(
        functools.partial(_outmlp_kernel, bounds=bounds, final_norm=(l == DEPTH - 1), tn_up=512, tn_down=256),
        grid=(n_steps,),
        in_specs=row_specs
        + [_layer_spec(l, (D_MODEL, D_MODEL), 1, single=True),
           _layer_spec(l, (1, D_MODEL), 1),
           _layer_spec(l, (D_MODEL, D_FF), 1, single=True),
           _layer_spec(l, (D_FF, D_MODEL), 1, single=True),
           pl.BlockSpec((1, D_MODEL), lambda i: (0, 0))],
        out_specs=[_stream_spec(tm, D_MODEL, *b) for b in bounds],
        out_shape=[jax.ShapeDtypeStruct((x.shape[0], D_MODEL), F32) for x in xs],
        scratch_shapes=[pltpu.VMEM((tm, D_MODEL), BF16),
                        pltpu.VMEM((tm, D_FF), BF16)],
        compiler_params=_cparams(("arbitrary",)),
    )(*row_ops, w_out, norm_w, w_up, w_down, norm_f)


def _rope_kernel(angle_ref, cos_ref, sin_ref, *, pos0, tl, period):
    i = pl.program_id(0)
    row = i * tl + lax.broadcasted_iota(jnp.int32, (tl, 2 * HEAD_W), 0)
    pos = (pos0 + row % period).astype(F32)
    theta = pos * angle_ref[...]
    lane = lax.broadcasted_iota(jnp.int32, (tl, 2 * HEAD_W), 1)
    cos_ref[...] = jnp.cos(theta)
    s = jnp.sin(theta)
    sin_ref[...] = jnp.where(lane % 2 == 0, -s, s)


def _rope_tables(angle_row, pos0, rows, period, tl):
    seq = rows
    return pl.pallas_call(
        functools.partial(_rope_kernel, pos0=pos0, tl=tl, period=period),
        grid=(seq // tl,),
        in_specs=[pl.BlockSpec((1, 2 * HEAD_W), lambda i: (0, 0))],
        out_specs=[pl.BlockSpec((tl, 2 * HEAD_W), lambda i: (i, 0))] * 2,
        out_shape=[jax.ShapeDtypeStruct((seq, 2 * HEAD_W), F32)] * 2,
        compiler_params=_cparams(("parallel",)),
    )(angle_row)


def _ret_kernel(*refs, c_seq, nseq, nblk, has_init, has_prev):
    lg_ref, p_ref, cos_ref, sin_ref = refs[:4]
    s0_ref = refs[4] if has_init else None
    y_ref, s_ref, sb_ref = refs[4 + has_init + has_prev:]
    nb = s_ref.shape[0]
    c = nseq * c_seq
    j = pl.program_id(1)

    @pl.when(j == 0)
    def _():
        sb_ref[...] = jnp.zeros_like(sb_ref)
        if has_init:
            for b in range(nb):
                for h in range(HEADS):
                    hs = slice(h * HEAD_W, (h + 1) * HEAD_W)
                    sb_ref[b, hs, hs] = s0_ref[b, h]

    def per_head(idx, shape):
        out = jnp.full(shape, lg_ref[HEADS - 1], F32)
        for h in range(HEADS - 2, -1, -1):
            out = jnp.where(idx == h, lg_ref[h], out)
        return out

    head = _head_of_lane(c)
    bd = _block_mask(HEAD_W)
    m_seg = jnp.where(bd, 1.0 / HEAD_W, 0.0).astype(BF16)
    even = lax.broadcasted_iota(jnp.int32, (c, GROUP_W), 1) % 2 == 0
    lg_lane = per_head(head[0:1, :], (1, GROUP_W))
    t_loc = (lax.broadcasted_iota(jnp.int32, (c, 1), 0) % c_seq).astype(F32)
    e_q = jnp.exp((t_loc + 1.0) * lg_lane)
    e_k = jnp.exp((c_seq - 1.0 - t_loc) * lg_lane)
    a_c = jnp.exp(c_seq * lg_lane)
    r_i, r2_i, h_i, causal = _seq_grid(c, c_seq, HEADS)
    dist = jnp.where(causal, (r_i - r2_i).astype(F32), 0.0)
    decay = jnp.where(causal, jnp.exp(dist * per_head(h_i, (c, HEADS * c))), 0.0)

    def rot(x, cos, sin):
        swapped = jnp.where(even, pltpu.roll(x, GROUP_W - 1, 1), pltpu.roll(x, 1, 1))
        return x * cos + swapped * sin

    for bs, ts, seqs in _row_blocks(nseq, c_seq, nblk):
        load = lambda lo: p_ref[bs, ts, lo:lo + GROUP_W].reshape(c, GROUP_W)
        crow = ts if nseq == 1 else slice(0, c)
        cos = jnp.concatenate([cos_ref[crow, :]] * 2, axis=1)
        sin = jnp.concatenate([sin_ref[crow, :]] * 2, axis=1)
        q = rot(load(0), cos, sin)
        k = rot(load(GROUP_W), cos, sin) * (HEAD_W ** -0.5)
        v = load(2 * GROUP_W)
        g = load(3 * GROUP_W)
        scores = _mm_nt(q, _expand_heads(k, head)) * decay
        o = _mm(scores, _expand_heads(v, head))
        qe = q * e_q
        ke = k * e_k
        cross = []
        for i, b in enumerate(seqs):
            sl = slice(i * c_seq, (i + 1) * c_seq)
            s_prev = sb_ref[b]
            cross.append(_mm_nt(qe[sl], s_prev))
            sb_ref[b] = a_c * s_prev + jnp.where(bd, _mm_tn(v[sl], ke[sl]), 0.0)
        o = o + _stack_rows(cross)
        mu = _seg_mean(o, m_seg)
        d = o - mu
        var = _seg_mean(d * d, m_seg)
        y_ref[bs, ts, :] = (d * lax.rsqrt(var + EPS) * _silu(g)).reshape(len(seqs), c_seq, GROUP_W)

    @pl.when(j == pl.num_programs(1) - 1)
    def _():
        for b in range(nb):
            for h in range(HEADS):
                hs = slice(h * HEAD_W, (h + 1) * HEAD_W)
                s_ref[b, h] = sb_ref[b, hs, hs]


def _retention(proj, cos, sin, log_gamma, init, prev, l, c_seq, nseq, nblk):
    bsz, seq, _ = proj.shape
    nb, tb, grid = _mixer_grid(bsz, seq, c_seq, nseq, nblk)
    st_in, st_ops, st_out, st_shape, alias = _state_io(l, nb, (HEADS, HEAD_W, HEAD_W), bsz, init, prev)
    n_fixed = 4
    rope_spec = (pl.BlockSpec((tb, 2 * HEAD_W), lambda i, j: (j, 0)) if nseq == 1 else
                 pl.BlockSpec((nseq * c_seq, 2 * HEAD_W), lambda i, j: (0, 0)))
    return pl.pallas_call(
        functools.partial(_ret_kernel, c_seq=c_seq, nseq=nseq, nblk=nblk, has_init=init is not None,
                          has_prev=prev is not None),
        grid=grid,
        in_specs=[pl.BlockSpec(memory_space=pltpu.SMEM),
                  pl.BlockSpec((nb, tb, 4 * GROUP_W), lambda i, j: (i, j, COL_RET // (4 * GROUP_W))),
                  rope_spec, rope_spec] + st_in,
        out_specs=[pl.BlockSpec((nb, tb, GROUP_W), lambda i, j: (i, j, 0)), st_out],
        out_shape=[jax.ShapeDtypeStruct((bsz, seq, GROUP_W), F32), st_shape],
        scratch_shapes=[pltpu.VMEM((nb, GROUP_W, GROUP_W), F32)],
        input_output_aliases={} if alias is None else {n_fixed + alias: 1},
        compiler_params=_cparams(("parallel", "arbitrary")),
    )(log_gamma, proj, cos, sin, *st_ops)


def _hg_kernel(*refs, c_seq, nseq, nblk, layer, has_init, has_prev):
    p_ref, lb_ref = refs[:2]
    s0_ref = refs[2] if has_init else None
    y_ref, s_ref, st_ref, keep_ref = refs[2 + has_init + has_prev:]
    nb = s_ref.shape[0]
    c = nseq * c_seq
    j = pl.program_id(1)

    pairs = [slice(pr * LANES, (pr + 1) * LANES) for pr in range(HEADS // PAIR_HEADS)]

    def pair_block(h):
        pr, jh = divmod(h, PAIR_HEADS)
        return pr, slice(jh * HEAD_W, (jh + 1) * HEAD_W)

    def load_state(b):
        st_ref[b] = jnp.zeros(st_ref.shape[1:], F32)
        if has_init:
            for h in range(HEADS):
                pr, hs = pair_block(h)
                st_ref[b, pr, hs, hs] = s0_ref[b, h]

    @pl.when(j == 0)
    def _():
        for b in range(nb):
            load_state(b)

    logits = lb_ref[...]
    e = jnp.exp(logits - jnp.max(logits, axis=0, keepdims=True))
    sm = e / jnp.sum(e, axis=0, keepdims=True)
    csum = sm[0:1, :]
    for i in range(1, layer + 1):
        csum = csum + sm[i:i + 1, :]
    lower = csum - sm[0:1, :]

    head = _head_of_lane(c)
    m_seg = jnp.where(_block_mask(HEAD_W), 1.0 / HEAD_W, 0.0).astype(BF16)
    bd = _block_mask(HEAD_W, LANES)
    r_i, r2_i, _, tril = _seq_grid(c, c_seq, 1)
    trif = tril.astype(F32)
    later = r2_i > r_i
    if nseq > 1:
        later = jnp.logical_and(r_i // c_seq == r2_i // c_seq, later)
    laterf = later.astype(F32)
    causal = _seq_grid(c, c_seq, HEADS)[3]
    mid = c // 2 - 1
    mask_sum = _mm_mask_f32 if c >= LANES else _mm_f32

    staged = []
    spreads = []
    for bs, ts, seqs in _row_blocks(nseq, c_seq, nblk):
        load = lambda lo: p_ref[bs, ts, lo:lo + GROUP_W].reshape(c, GROUP_W)
        qq = _silu(load(0))
        ff = load(GROUP_W)
        v = load(2 * GROUP_W)
        sig = jax.nn.sigmoid(ff)
        forget = lower + (1.0 - lower) * sig
        log_f = jnp.log(jnp.maximum(forget, HG_MIN_FORGET))
        key = (1.0 - lower) * (1.0 - sig)
        bcum = mask_sum(trif, log_f)
        if nseq == 1:
            b_ref = bcum[mid:mid + 1, :]
            spread = bcum - b_ref
            q_in = qq * jnp.exp(spread)
            k_in = key * jnp.exp(-spread)
            q_x = q_in * jnp.exp(b_ref)
            k_x = k_in * jnp.exp(bcum[c - 1:c, :] - b_ref)
        else:
            spread = bcum
            q_in = q_x = qq * jnp.exp(bcum)
            k_in = key * jnp.exp(-bcum)
            k_x = key * jnp.exp(mask_sum(laterf, log_f))
        spreads.append(jnp.max(jnp.abs(spread)))
        scores = jnp.where(causal, _mm_nt(q_in, _expand_heads(k_in, head)), 0.0)
        o = _mm(scores, _expand_heads(v, head))
        upd = []
        for i in range(len(seqs)):
            sl = slice(i * c_seq, (i + 1) * c_seq)
            a_seq = jnp.exp(bcum[(i + 1) * c_seq - 1:(i + 1) * c_seq, :])
            ds = [jnp.where(bd, _mm_tn(v[sl, pl_], k_x[sl, pl_]), 0.0) for pl_ in pairs]
            upd.append((q_x[sl], a_seq, ds))
        staged.append((bs, ts, seqs, o, upd))

    unsafe = functools.reduce(jnp.maximum, spreads) > HG_SAFE_SPREAD
    if nb == 1:
        keep_ref[...] = st_ref[0]

    for bs, ts, seqs, o, upd in staged:
        cross = []
        for b, (q_b, a_seq, ds) in zip(seqs, upd):
            parts = []
            for pr, pl_ in enumerate(pairs):
                st_prev = st_ref[b, pr]
                parts.append(_mm_nt(q_b[:, pl_], st_prev))
                st_ref[b, pr] = st_prev * a_seq[:, pl_] + ds[pr]
            cross.append(jnp.concatenate(parts, axis=1))
        o = o + _stack_rows(cross)
        gate = _silu(p_ref[bs, ts, 3 * GROUP_W:4 * GROUP_W].reshape(c, GROUP_W))
        y = o * lax.rsqrt(_seg_mean(o * o, m_seg) + EPS) * gate
        y_ref[bs, ts, :] = y.reshape(bs.stop - bs.start, ts.stop - ts.start, GROUP_W)

    @pl.when(unsafe)
    def _():
        tb = p_ref.shape[1]
        assert nb == 1 or (tb == SUBLANES and pl.num_programs(1) == 1)
        rowid = lax.broadcasted_iota(jnp.int32, (SUBLANES, GROUP_W), 0)
        if nb == 1:
            st_ref[0] = keep_ref[...]

        def slab(s, carry):
            b, t0 = (0, pl.multiple_of(s * SUBLANES, SUBLANES)) if nb == 1 else (s, 0)
            rows = pl.ds(t0, SUBLANES)
            if nb > 1:
                load_state(b)
            q8 = _silu(p_ref[b, rows, 0:GROUP_W])
            sig = jax.nn.sigmoid(p_ref[b, rows, GROUP_W:2 * GROUP_W])
            v8 = p_ref[b, rows, 2 * GROUP_W:3 * GROUP_W]
            g8 = p_ref[b, rows, 3 * GROUP_W:4 * GROUP_W]
            f8 = jnp.maximum(lower + (1.0 - lower) * sig, HG_MIN_FORGET)
            k8 = (1.0 - lower) * (1.0 - sig)
            o8 = jnp.zeros((SUBLANES, GROUP_W), F32)
            for i in range(SUBLANES):
                v_i = jnp.where(rowid == i, v8, 0.0)
                parts = []
                for pr, pl_ in enumerate(pairs):
                    outer = _mm_tn(v_i[:, pl_], k8[:, pl_])
                    st = st_ref[b, pr] * f8[i:i + 1, pl_] + jnp.where(bd, outer, 0.0)
                    st_ref[b, pr] = st
                    parts.append(_mm_nt(q8[:, pl_], st))
                o8 = jnp.where(rowid == i, jnp.concatenate(parts, axis=1), o8)
            y_ref[b, rows, :] = o8 * lax.rsqrt(_seg_mean(o8 * o8, m_seg) + EPS) * _silu(g8)
            return carry

        lax.fori_loop(0, nb * tb // SUBLANES, slab, 0)

    @pl.when(j == pl.num_programs(1) - 1)
    def _():
        for b in range(nb):
            for h in range(HEADS):
                pr, hs = pair_block(h)
                s_ref[b, h] = st_ref[b, pr, hs, hs]


def _hgrn2(proj, lb_logits, init, prev, l, c_seq, nseq, nblk):
    bsz, seq, _ = proj.shape
    nb, tb, grid = _mixer_grid(bsz, seq, c_seq, nseq, nblk)
    st_in, st_ops, st_out, st_shape, alias = _state_io(l, nb, (HEADS, HEAD_W, HEAD_W), bsz, init, prev)
    n_fixed = 2
    return pl.pallas_call(
        functools.partial(_hg_kernel, c_seq=c_seq, nseq=nseq, nblk=nblk, layer=l, has_init=init is not None,
                          has_prev=prev is not None),
        grid=grid,
        in_specs=[pl.BlockSpec((nb, tb, 4 * GROUP_W), lambda i, j: (i, j, COL_HG // (4 * GROUP_W))),
                  pl.BlockSpec((DEPTH, GROUP_W), lambda i, j: (0, 0))] + st_in,
        out_specs=[pl.BlockSpec((nb, tb, GROUP_W), lambda i, j: (i, j, 0)), st_out],
        out_shape=[jax.ShapeDtypeStruct((bsz, seq, GROUP_W), F32), st_shape],
        scratch_shapes=[pltpu.VMEM((nb, HEADS // PAIR_HEADS, LANES, LANES), F32),
                        pltpu.VMEM((HEADS // PAIR_HEADS, LANES, LANES), F32)],
        input_output_aliases={} if alias is None else {n_fixed + alias: 1},
        compiler_params=_cparams(("parallel", "arbitrary")),
    )(proj, lb_logits, *st_ops)


def _ssd_kernel(*refs, c_seq, nseq, nblk, has_init, has_prev):
    p_ref, dt_ref, cw_ref, cb_ref, dtb_ref, alog_ref, dsk_ref, nw_ref = refs[:8]
    n_in = 8
    cv0_ref = s0_ref = None
    if has_init:
        cv0_ref, s0_ref = refs[n_in], refs[n_in + 1 + has_prev]
    y_ref, s_ref, cv_ref, xpad_ref = refs[n_in + 2 * (has_init + has_prev):]
    nb = s_ref.shape[0]
    tb = p_ref.shape[1]
    c = nseq * c_seq
    n_tail = SSD_CONV - 1
    lo = CONV_PAD - n_tail

    @pl.when(pl.program_id(1) == 0)
    def _():
        xpad_ref[:, 0:CONV_PAD, :] = jnp.zeros((nb, CONV_PAD, SSD_CONV_DIM), F32)
        if has_init:
            s_ref[...] = s0_ref[...]
            xpad_ref[:, lo:CONV_PAD, :] = cv0_ref[...]
        else:
            s_ref[...] = jnp.zeros_like(s_ref)

    r_i, r2_i, _, tri = _seq_grid(c, c_seq, 1)
    trif = tri.astype(F32)
    later = r2_i > r_i
    if nseq > 1:
        later = jnp.logical_and(r_i // c_seq == r2_i // c_seq, later)
    laterf = later.astype(F32)
    head = _head_of_lane(c)
    m_grp = jnp.where(_block_mask(LANES), 1.0 / LANES, 0.0).astype(BF16)
    cw = cw_ref[...]
    groups = [slice(g * LANES, (g + 1) * LANES) for g in range(SSD_GROUPS)]

    xpad_ref[:, CONV_PAD:CONV_PAD + tb, :] = p_ref[:, :, GROUP_W:GROUP_W + SSD_CONV_DIM]
    for bs, ts, seqs in _row_blocks(nseq, c_seq, nblk):
        r0 = ts.start
        load = lambda ref, col, w: ref[bs, ts, col:col + w].reshape(c, w)
        z = load(p_ref, 0, GROUP_W)
        conv = cb_ref[...] + cw[n_tail:n_tail + 1, :] * load(p_ref, GROUP_W, SSD_CONV_DIM)
        for w in range(n_tail):
            win = xpad_ref[bs, r0 + lo + w:r0 + lo + w + c_seq, :]
            conv = conv + cw[w:w + 1, :] * win.reshape(c, SSD_CONV_DIM)
        xc = _silu(conv)
        xs = xc[:, 0:GROUP_W]
        bm = xc[:, GROUP_W:2 * GROUP_W]
        cm = xc[:, 2 * GROUP_W:3 * GROUP_W]

        delta = jax.nn.softplus(load(dt_ref, 0, GROUP_W) + dtb_ref[...])
        log_a = -delta * jnp.exp(alog_ref[...])
        bcum = _mm_mask_f32(trif, log_a)
        brem = bcum[c - 1:c, :] - bcum if nseq == 1 else _mm_mask_f32(laterf, log_a)
        xd = xs * delta
        xr = xd * jnp.exp(brem)
        brow = [bcum[:, gl].T for gl in groups]
        gram = [_mm_nt(cm[:, gl], bm[:, gl]) for gl in groups]
        w_parts = []
        for h in range(HEADS):
            g, jh = divmod(h, HEADS // SSD_GROUPS)
            b_t = bcum[:, h * HEAD_W:h * HEAD_W + 1]
            b_s = brow[g][jh * HEAD_W:jh * HEAD_W + 1, :]
            w_parts.append(jnp.where(tri, gram[g] * jnp.exp(b_t - b_s), 0.0))
        o = _mm(jnp.concatenate(w_parts, axis=1), _expand_heads(xd, head))
        cross = []
        for i, b in enumerate(seqs):
            sl = slice(i * c_seq, (i + 1) * c_seq)
            last = (i + 1) * c_seq - 1
            parts = []
            for g, gl in enumerate(groups):
                s_prev = s_ref[b, g]
                parts.append(_mm_nt(cm[sl, gl], s_prev))
                a_col = jnp.exp(brow[g][:, last:last + 1])
                s_ref[b, g] = s_prev * a_col + _mm_tn(xr[sl, gl], bm[sl, gl])
            cross.append(jnp.concatenate(parts, axis=1))
        o = o + jnp.exp(bcum) * _stack_rows(cross)
        y = (o + dsk_ref[...] * xs) * _silu(z)
        y = y * lax.rsqrt(_seg_mean(y * y, m_grp) + EPS) * nw_ref[...]
        y_ref[bs, ts, :] = y.reshape(len(seqs), c_seq, GROUP_W)
    tail = xpad_ref[:, tb + lo:tb + CONV_PAD, :]
    cv_ref[...] = tail
    xpad_ref[:, lo:CONV_PAD, :] = tail


def _ssd(proj, prm, init_cv, init_s, prev_cv, prev_s, l, c_seq, nseq, nblk):
    bsz, seq, _ = proj.shape
    nb, tb, grid = _mixer_grid(bsz, seq, c_seq, nseq, nblk)
    cv_in, cv_ops, cv_out, cv_shape, cv_alias = _state_io(l, nb, (SSD_CONV - 1, SSD_CONV_DIM), bsz,
                                                          init_cv, prev_cv)
    s_in, s_ops, s_out, s_shape, s_alias = _state_io(l, nb, (SSD_GROUPS, LANES, SSD_STATE), bsz,
                                                     init_s, prev_s)
    n_fixed = 8
    aliases = {}
    if cv_alias is not None:
        aliases[n_fixed + cv_alias] = 2
        aliases[n_fixed + len(cv_in) + s_alias] = 1
    return pl.pallas_call(
        functools.partial(_ssd_kernel, c_seq=c_seq, nseq=nseq, nblk=nblk, has_init=init_s is not None,
                          has_prev=prev_s is not None),
        grid=grid,
        in_specs=[pl.BlockSpec((nb, tb, 4 * GROUP_W), lambda i, j: (i, j, COL_SSD // (4 * GROUP_W))),
                  pl.BlockSpec((nb, tb, GROUP_W), lambda i, j: (i, j, COL_DT // GROUP_W)),
                  _layer_spec(l, (SSD_CONV, SSD_CONV_DIM), 2),
                  _layer_spec(l, (1, SSD_CONV_DIM), 2),
                  _layer_spec(l, (1, GROUP_W), 2),
                  _layer_spec(l, (1, GROUP_W), 2),
                  _layer_spec(l, (1, GROUP_W), 2),
                  _layer_spec(l, (1, GROUP_W), 2)] + cv_in + s_in,
        out_specs=[pl.BlockSpec((nb, tb, GROUP_W), lambda i, j: (i, j, 0)), s_out, cv_out],
        out_shape=[jax.ShapeDtypeStruct((bsz, seq, GROUP_W), F32), s_shape, cv_shape],
        scratch_shapes=[pltpu.VMEM((nb, tb + CONV_PAD, SSD_CONV_DIM), F32)],
        input_output_aliases=aliases,
        compiler_params=_cparams(("parallel", "arbitrary")),
    )(proj, proj, prm["conv_w"], prm["conv_b"], prm["dt_bias"], prm["a_log"], prm["ssd_d"], prm["ssd_norm"],
      *cv_ops, *s_ops)


def _ssd_state_in(s):
    d, bsz = s.shape[:2]
    return jnp.swapaxes(s, 3, 4).reshape(d, bsz, SSD_GROUPS, LANES, SSD_STATE)


def _ssd_state_out(s):
    d, bsz = s.shape[:2]
    return jnp.swapaxes(s.reshape(d, bsz, HEADS, HEAD_W, SSD_STATE), 3, 4)


def _s5_kernel(*refs, tl, has_init, has_prev):
    (u0_ref, u1_ref, are_ref, aim_ref, ldt_ref, bre_ref, bim_ref, cw_ref, dsk_ref, gw_ref,
     gb_ref) = refs[:11]
    n_in = 11
    s0r_ref = s0i_ref = None
    if has_init:
        s0r_ref, s0i_ref = refs[n_in], refs[n_in + 1 + has_prev]
    (y_ref, sr_ref, si_ref, wb_ref, ab_ref, us0_ref, us1_ref, ut_ref, bu_ref, yt_ref, ys0_ref,
     ys1_ref) = refs[n_in + 2 * (has_init + has_prev):]
    nbt = SUBLANES * tl

    @pl.when(pl.program_id(1) == 0)
    def _():
        if has_init:
            sr_ref[...] = s0r_ref[...]
            si_ref[...] = s0i_ref[...]
        else:
            sr_ref[...] = jnp.zeros_like(sr_ref)
            si_ref[...] = jnp.zeros_like(si_ref)
        lr, li, st = are_ref[...], aim_ref[...], jnp.exp(ldt_ref[...])
        mag = jnp.exp(lr * st)
        abr = mag * jnp.cos(li * st)
        abi = mag * jnp.sin(li * st)
        ab_ref[0:1, :] = abr
        ab_ref[1:2, :] = abi
        den = lr * lr + li * li
        cr = ((abr - 1.0) * lr + abi * li) / den
        ci = (abi * lr - (abr - 1.0) * li) / den
        wb_ref[:, 0:S5_N] = (cr * bre_ref[...] - ci * bim_ref[...]).astype(BF16)
        wb_ref[:, S5_N:2 * S5_N] = (cr * bim_ref[...] + ci * bre_ref[...]).astype(BF16)

    us0_ref[...] = u0_ref[...].reshape(nbt, LANES)
    us1_ref[...] = u1_ref[...].reshape(nbt, LANES)

    nparts = bu_ref.shape[0]
    prow = nbt // nparts
    psteps = tl // nparts
    for h in range(nparts):
        for t in range(h * psteps, (h + 1) * psteps):
            dst = slice(t * SUBLANES, (t + 1) * SUBLANES)
            src = pl.ds(t, SUBLANES, stride=tl)
            ut_ref[dst, 0:LANES] = us0_ref[src, :]
            ut_ref[dst, LANES:2 * LANES] = us1_ref[src, :]
        u_h = ut_ref[h * prow:(h + 1) * prow, :]
        bu_ref[h] = jnp.dot(u_h.astype(BF16), wb_ref[...], preferred_element_type=F32)
    ar = jnp.broadcast_to(ab_ref[0:1, :], (SUBLANES, S5_N))
    ai = jnp.broadcast_to(ab_ref[1:2, :], (SUBLANES, S5_N))
    xr, xi = sr_ref[...], si_ref[...]
    for h in range(nparts):
        for t in range(tl // nparts):
            rows = slice(t * SUBLANES, (t + 1) * SUBLANES)
            nr = ar * xr - ai * xi + bu_ref[h, rows, 0:S5_N]
            ni = ar * xi + ai * xr + bu_ref[h, rows, S5_N:2 * S5_N]
            bu_ref[h, rows, 0:S5_N] = nr
            bu_ref[h, rows, S5_N:2 * S5_N] = ni
            xr, xi = nr, ni
        y = _mm(bu_ref[h], cw_ref[...]) + dsk_ref[...] * ut_ref[h * prow:(h + 1) * prow, :]
        y = jax.nn.gelu(y)
        yt_ref[h * prow:(h + 1) * prow, :] = y * jax.nn.sigmoid(_mm(y, gw_ref[...]) + gb_ref[...])
        for t in range(h * psteps, (h + 1) * psteps):
            src = slice(t * SUBLANES, (t + 1) * SUBLANES)
            dst = pl.ds(t, SUBLANES, stride=tl)
            ys0_ref[dst, :] = yt_ref[src, 0:LANES]
            ys1_ref[dst, :] = yt_ref[src, LANES:2 * LANES]
    sr_ref[...] = xr
    si_ref[...] = xi
    y_ref[:, :, 0:LANES] = ys0_ref[...].reshape(SUBLANES, tl, LANES)
    y_ref[:, :, LANES:2 * LANES] = ys1_ref[...].reshape(SUBLANES, tl, LANES)


def _s5(proj, prm, init_re, init_im, prev_re, prev_im, l, tl, nparts):
    bsz, seq, _ = proj.shape
    re_in, re_ops, re_out, re_shape, re_alias = _state_io(l, SUBLANES, (S5_N,), bsz, init_re, prev_re)
    im_in, im_ops, im_out, im_shape, im_alias = _state_io(l, SUBLANES, (S5_N,), bsz, init_im, prev_im)
    n_fixed = 11
    aliases = {}
    if re_alias is not None:
        aliases[n_fixed + re_alias] = 1
        aliases[n_fixed + len(re_in) + im_alias] = 2
    nbt = SUBLANES * tl
    return pl.pallas_call(
        functools.partial(_s5_kernel, tl=tl, has_init=init_re is not None,
                          has_prev=prev_re is not None),
        grid=(bsz // SUBLANES, seq // tl),
        in_specs=[pl.BlockSpec((SUBLANES, tl, LANES), lambda i, j: (i, j, COL_S5 // LANES)),
                  pl.BlockSpec((SUBLANES, tl, LANES), lambda i, j: (i, j, COL_S5 // LANES + 1)),
                  _layer_spec(l, (1, S5_N), 2), _layer_spec(l, (1, S5_N), 2), _layer_spec(l, (1, S5_N), 2),
                  _layer_spec(l, (GROUP_W, S5_N), 2), _layer_spec(l, (GROUP_W, S5_N), 2),
                  _layer_spec(l, (2 * S5_N, GROUP_W), 2),
                  _layer_spec(l, (1, GROUP_W), 2),
                  _layer_spec(l, (GROUP_W, GROUP_W), 2),
                  _layer_spec(l, (1, GROUP_W), 2)] + re_in + im_in,
        out_specs=[pl.BlockSpec((SUBLANES, tl, GROUP_W), lambda i, j: (i, j, 0)), re_out, im_out],
        out_shape=[jax.ShapeDtypeStruct((bsz, seq, GROUP_W), F32), re_shape, im_shape],
        scratch_shapes=[pltpu.VMEM((GROUP_W, 2 * S5_N), BF16),
                        pltpu.VMEM((SUBLANES, S5_N), F32),
                        pltpu.VMEM((nbt, LANES), F32), pltpu.VMEM((nbt, LANES), F32),
                        pltpu.VMEM((nbt, GROUP_W), F32),
                        pltpu.VMEM((nparts, nbt // nparts, 2 * S5_N), F32),
                        pltpu.VMEM((nbt, GROUP_W), F32),
                        pltpu.VMEM((nbt, LANES), F32), pltpu.VMEM((nbt, LANES), F32)],
        input_output_aliases=aliases,
        compiler_params=_cparams(("parallel", "arbitrary")),
    )(proj, proj, prm["a_re"], prm["a_im"], prm["log_dt"], prm["b_re"], prm["b_im"], prm["c_blk"],
      prm["s5_d"], prm["glu_w"], prm["glu_b"], *re_ops, *im_ops)


def _block_diag(blocks):
    d, g, r, c = blocks.shape
    eye = jnp.eye(g, dtype=blocks.dtype)
    return (eye[None, :, None, :, None] * blocks[:, :, :, None, :]).reshape(d, g * r, g * c)


def _prep(ssd_conv_w, ssd_conv_b, ssd_dt_bias, ssd_a_log, ssd_d, ssd_norm, s5_a_re, s5_a_im, s5_log_dt,
          s5_b_re, s5_b_im, s5_c_re, s5_c_im, s5_d, s5_glu_w, s5_glu_b):
    per_head = lambda v: jnp.repeat(v, HEAD_W, axis=1)[:, None, :]
    c_blk = jnp.concatenate([_block_diag(jnp.swapaxes(s5_c_re, 2, 3)),
                             -_block_diag(jnp.swapaxes(s5_c_im, 2, 3))], axis=1)
    return dict(
        conv_w=ssd_conv_w, conv_b=ssd_conv_b[:, None, :],
        dt_bias=per_head(ssd_dt_bias), a_log=per_head(ssd_a_log), ssd_d=per_head(ssd_d),
        ssd_norm=ssd_norm[:, None, :],
        a_re=s5_a_re.reshape(DEPTH, 1, S5_N), a_im=s5_a_im.reshape(DEPTH, 1, S5_N),
        log_dt=jnp.repeat(s5_log_dt, S5_STATE, axis=1)[:, None, :],
        b_re=_block_diag(jnp.swapaxes(s5_b_re, 2, 3)), b_im=_block_diag(jnp.swapaxes(s5_b_im, 2, 3)),
        c_blk=c_blk.astype(BF16),
        s5_d=s5_d[:, None, :], glu_w=s5_glu_w.astype(BF16), glu_b=s5_glu_b[:, None, :],
    )


class _Stream:
    def __init__(self, x, pos0, init, cfg, angle_row):
        self.bsz, self.seq, _ = x.shape
        self.cfg = cfg
        self.x2d = x.reshape(self.bsz * self.seq, D_MODEL)
        c_ret, nseq_ret, _ = cfg["ret"]
        rope_rows = self.seq if nseq_ret == 1 else nseq_ret * c_ret
        self.cos, self.sin = _rope_tables(angle_row, pos0, rope_rows, self.seq, cfg["rope_tl"])
        self.init = (None,) * 6
        if init is not None:
            i_ret, i_hg, i_ssd, i_cv, i_re, i_im = init
            self.init = (jnp.swapaxes(i_ret, 3, 4), jnp.swapaxes(i_hg, 3, 4), _ssd_state_in(i_ssd), i_cv,
                         i_re.reshape(DEPTH, self.bsz, S5_N), i_im.reshape(DEPTH, self.bsz, S5_N))
        self.new = (None,) * 6

    def mix(self, proj2d, l, prm, lb_logits, log_gamma):
        proj = proj2d.reshape(self.bsz, self.seq, IN_COLS_PAD)
        i_ret, i_hg, i_ssd, i_cv, i_re, i_im = self.init
        s_ret, s_hg, s_ssd, s_cv, s_re, s_im = self.new
        cfg = self.cfg
        ya, s_ret = _retention(proj, self.cos, self.sin, log_gamma, i_ret, s_ret, l, *cfg["ret"])
        yb, s_hg = _hgrn2(proj, lb_logits, i_hg, s_hg, l, *cfg["hg"])
        yc, s_ssd, s_cv = _ssd(proj, prm, i_cv, i_ssd, s_cv, s_ssd, l, *cfg["ssd"])
        yd, s_re, s_im = _s5(proj, prm, i_re, i_im, s_re, s_im, l, *cfg["s5"])
        self.new = (s_ret, s_hg, s_ssd, s_cv, s_re, s_im)
        return [y.reshape(self.bsz * self.seq, GROUP_W) for y in (ya, yb, yc, yd)]

    def outputs(self):
        s_ret, s_hg, s_ssd, s_cv, s_re, s_im = self.new
        tail = (DEPTH, self.bsz, S5_GROUPS, S5_STATE)
        return (self.x2d.reshape(self.bsz, self.seq, D_MODEL), jnp.swapaxes(s_ret, 3, 4), jnp.swapaxes(s_hg, 3, 4),
                _ssd_state_out(s_ssd), s_cv,
                s_re.reshape(tail), s_im.reshape(tail))


def _trunk(streams, prm, w_in, w_out, w_up, w_down, norm_mix, norm_mlp, norm_final, lb_logits, log_gamma,
           tm_in, tm_out):
    for l in range(DEPTH):
        projs = _inproj([s.x2d for s in streams], norm_mix, w_in, l, tm_in)
        ys = [s.mix(p, l, prm, lb_logits, log_gamma) for s, p in zip(streams, projs)]
        outs = _outmlp([s.x2d for s in streams], ys, w_out, norm_mlp, w_up, w_down, norm_final, l, tm_out)
        for s, x2d in zip(streams, outs):
            s.x2d = x2d
    return [s.outputs() for s in streams]


PROMPT_CFG = dict(rope_tl=256, ret=(256, 1, 8), hg=(64, 1, 16), ssd=(128, 1, 16), s5=(256, 4))
SAMPLE_CFG = dict(rope_tl=128, ret=(8, 16, 2), hg=(8, 16, 2), ssd=(8, 16, 2), s5=(8, 1))
TM_IN = 512
TM_OUT = 512


def kernel(x_prompt, x_sample, state_ret, state_hgrn, state_ssd, state_ssd_conv, state_s5_re, state_s5_im, norm_mix, w_in, w_out, hg_lb_logits, ssd_conv_w, ssd_conv_b, ssd_dt_bias, ssd_a_log, ssd_d, ssd_norm, s5_a_re, s5_a_im, s5_log_dt, s5_b_re, s5_b_im, s5_c_re, s5_c_im, s5_d, s5_glu_w, s5_glu_b, norm_mlp, w_up, w_down, norm_final):
    prm = _prep(ssd_conv_w, ssd_conv_b, ssd_dt_bias, ssd_a_log, ssd_d, ssd_norm, s5_a_re, s5_a_im, s5_log_dt,
                s5_b_re, s5_b_im, s5_c_re, s5_c_im, s5_d, s5_glu_w, s5_glu_b)
    w_out_b = _cast_bf16(w_out, 512)
    w_up_b = _cast_bf16(w_up, 256)
    w_down_b = _cast_bf16(w_down, 1024)
    angle = 1.0 / (ROPE_BASE ** jnp.linspace(0.0, 1.0, HEAD_W // 2, dtype=F32))
    angle_row = jnp.tile(jnp.repeat(angle, 2), 2)[None, :]
    log_gamma = jnp.log(1.0 - jnp.exp2(-5.0 - jnp.arange(HEADS, dtype=F32)))
    states = (state_ret, state_hgrn, state_ssd, state_ssd_conv, state_s5_re, state_s5_im)
    streams = [_Stream(x_prompt, 0, None, PROMPT_CFG, angle_row),
               _Stream(x_sample, PAST_LEN, states, SAMPLE_CFG, angle_row)]
    out_p, out_s = _trunk(streams, prm, jnp.swapaxes(w_in, 1, 2), w_out_b, w_up_b, w_down_b, norm_mix[:, None, :],
                          norm_mlp[:, None, :], norm_final[None, :], hg_lb_logits, log_gamma, TM_IN, TM_OUT)
    return (out_p[0], out_s[0]) + out_p[1:] + out_s[1:]
```

```python
import functools

import jax
import jax.numpy as jnp
from jax import lax
from jax.experimental import pallas as pl
from jax.experimental.pallas import tpu as pltpu

F32 = jnp.float32
BF16 = jnp.bfloat16

D_MODEL = 1024
DEPTH = 2
PAST_LEN = 16384
GROUP_W = 256
HEADS = 4
HEAD_W = 64
PAIR_HEADS = 2
ROPE_BASE = 10000.0
HG_MIN_FORGET = 1e-30
HG_SAFE_SPREAD = 80.0
SSD_STATE = 128
SSD_GROUPS = 2
SSD_CONV = 4
SSD_CONV_DIM = 768
CONV_PAD = 8
S5_GROUPS = 16
S5_STATE = 64
S5_N = S5_GROUPS * S5_STATE
D_FF = 4096
IN_COLS = 3332
EPS = 1e-6

COL_RET = 0
COL_HG = 1024
COL_SSD = 2048
COL_S5 = 3072
COL_DT = 3328
IN_COLS_PAD = 3584

LANES = 128
SUBLANES = 8
VMEM_LIMIT = 56 * 1024 * 1024


def _cparams(sem):
    return pltpu.CompilerParams(dimension_semantics=sem, vmem_limit_bytes=VMEM_LIMIT)


def _mm(a, b):
    return jnp.dot(a.astype(BF16), b.astype(BF16), preferred_element_type=F32)


def _mm_nt(a, b):
    return lax.dot_general(a.astype(BF16), b.astype(BF16), (((1,), (1,)), ((), ())),
                           preferred_element_type=F32)


def _mm_tn(a, b):
    return lax.dot_general(a.astype(BF16), b.astype(BF16), (((0,), (0,)), ((), ())),
                           preferred_element_type=F32)


def _mm_f32(a, b):
    return jnp.dot(a, b, preferred_element_type=F32, precision=lax.Precision.HIGHEST)


def _mm_mask_f32(mask01, x):
    m = mask01.astype(BF16)
    hi = x.astype(BF16)
    r1 = x - hi.astype(F32)
    mid = r1.astype(BF16)
    lo = (r1 - mid.astype(F32)).astype(BF16)
    return (jnp.dot(m, hi, preferred_element_type=F32) + jnp.dot(m, mid, preferred_element_type=F32)
            + jnp.dot(m, lo, preferred_element_type=F32))


def _silu(x):
    return x * jax.nn.sigmoid(x)


def _rms(x, w):
    return x * lax.rsqrt(jnp.mean(x * x, axis=-1, keepdims=True) + EPS) * w


def _head_of_lane(rows):
    return lax.broadcasted_iota(jnp.int32, (rows, GROUP_W), 1) // HEAD_W


def _expand_heads(x, head):
    nheads = x.shape[1] // HEAD_W
    return jnp.concatenate([jnp.where(head == h, x, 0.0) for h in range(nheads)], axis=0).astype(BF16)


def _block_mask(seg, width=GROUP_W):
    r = lax.broadcasted_iota(jnp.int32, (width, width), 0) // seg
    c = lax.broadcasted_iota(jnp.int32, (width, width), 1) // seg
    return r == c


def _seg_mean(x, m):
    hi = x.astype(BF16)
    lo = (x - hi.astype(F32)).astype(BF16)
    return jnp.dot(hi, m, preferred_element_type=F32) + jnp.dot(lo, m, preferred_element_type=F32)


def _row_blocks(nseq, c_seq, nblk):
    if nseq == 1:
        return [(slice(0, 1), slice(k * c_seq, (k + 1) * c_seq), [0]) for k in range(nblk)]
    return [(slice(k * nseq, (k + 1) * nseq), slice(0, c_seq), list(range(k * nseq, (k + 1) * nseq)))
            for k in range(nblk)]


def _seq_grid(rows, c_seq, copies):
    r = lax.broadcasted_iota(jnp.int32, (rows, copies * rows), 0)
    j = lax.broadcasted_iota(jnp.int32, (rows, copies * rows), 1)
    h = j // rows
    r2 = j - h * rows
    causal = r >= r2
    if rows > c_seq:
        causal = jnp.logical_and(r // c_seq == r2 // c_seq, causal)
    return r, r2, h, causal


def _stack_rows(pieces):
    return pieces[0] if len(pieces) == 1 else jnp.concatenate(pieces, axis=0)


def _mixer_grid(bsz, seq, c_seq, nseq, nblk):
    nb = 1 if nseq == 1 else nseq * nblk
    tb = c_seq * nblk if nseq == 1 else c_seq
    assert nseq == 1 or seq == c_seq
    return nb, tb, (bsz // nb, seq // tb)


def _layer_spec(l, tail, n_grid, single=False):
    zeros = (0,) * len(tail)
    kw = dict(pipeline_mode=pl.Buffered(1)) if single else {}
    if n_grid == 1:
        return pl.BlockSpec((None,) + tail, lambda i: (l,) + zeros, **kw)
    return pl.BlockSpec((None,) + tail, lambda i, j: (l,) + zeros, **kw)


def _state_spec(l, nb, tail):
    zeros = (0,) * len(tail)
    return pl.BlockSpec((None, nb) + tail, lambda i, j: (l, i) + zeros)


def _state_io(l, nb, tail, bsz, init, prev):
    spec = _state_spec(l, nb, tail)
    in_specs, operands = [], []
    if init is not None:
        in_specs.append(spec)
        operands.append(init)
    alias_pos = None
    if prev is not None:
        alias_pos = len(in_specs)
        in_specs.append(pl.BlockSpec(memory_space=pl.ANY))
        operands.append(prev)
    return in_specs, operands, spec, jax.ShapeDtypeStruct((DEPTH, bsz) + tail, F32), alias_pos


def _cast_kernel(w_ref, o_ref):
    o_ref[...] = w_ref[...].astype(BF16)


def _cast_bf16(w, tr):
    d, r, c = w.shape
    spec = pl.BlockSpec((None, tr, c), lambda l, i: (l, i, 0))
    return pl.pallas_call(
        _cast_kernel, grid=(d, r // tr), in_specs=[spec], out_specs=spec,
        out_shape=jax.ShapeDtypeStruct(w.shape, BF16),
        compiler_params=_cparams(("parallel", "parallel")),
    )(w)


def _stream_steps(rows, tm):
    bounds, start = [], 0
    for r in rows:
        bounds.append((start, r // tm))
        start += r // tm
    return bounds, start


def _stream_spec(tm, width, start, steps, single=False):
    kw = dict(pipeline_mode=pl.Buffered(1)) if single else {}
    return pl.BlockSpec((tm, width), lambda i: (jnp.clip(i - start, 0, steps - 1), 0), **kw)


def _when_stream(start, steps):
    i = pl.program_id(0)
    return pl.when(jnp.logical_and(i >= start, i < start + steps))


def _inproj_kernel(*refs, bounds):
    n = len(bounds)
    x_refs, (nw_ref, w_ref), o_refs, wb_ref = refs[:n], refs[n:n + 2], refs[n + 2:2 * n + 2], refs[2 * n + 2]

    @pl.when(pl.program_id(0) == 0)
    def _():
        rows = 256
        for r in range(0, COL_S5, rows):
            wb_ref[r:r + rows, :] = w_ref[r:r + rows, :].astype(BF16)
        wb_ref[COL_S5:COL_DT, :] = w_ref[COL_S5 + HEADS:IN_COLS, :].astype(BF16)
        for h in range(HEADS):
            row = w_ref[COL_S5 + h:COL_S5 + h + 1, :]
            wb_ref[COL_DT + h * HEAD_W:COL_DT + (h + 1) * HEAD_W, :] = jnp.broadcast_to(
                row, (HEAD_W, D_MODEL)).astype(BF16)

    for x_ref, o_ref, (start, steps) in zip(x_refs, o_refs, bounds):
        @_when_stream(start, steps)
        def _():
            h = _rms(x_ref[...], nw_ref[...])
            o_ref[...] = _mm_nt(h, wb_ref[...])


def _inproj(xs, norm_w, w_in_t, l, tm):
    bounds, n_steps = _stream_steps([x.shape[0] for x in xs], tm)
    longest = max(steps for _, steps in bounds)
    return pl.pallas_call(
        functools.partial(_inproj_kernel, bounds=bounds),
        grid=(n_steps,),
        in_specs=[_stream_spec(tm, D_MODEL, *b, single=b[1] < longest) for b in bounds]
        + [_layer_spec(l, (1, D_MODEL), 1),
           _layer_spec(l, (IN_COLS, D_MODEL), 1, single=True)],
        out_specs=[_stream_spec(tm, IN_COLS_PAD, *b, single=b[1] < longest) for b in bounds],
        out_shape=[jax.ShapeDtypeStruct((x.shape[0], IN_COLS_PAD), F32) for x in xs],
        scratch_shapes=[pltpu.VMEM((IN_COLS_PAD, D_MODEL), BF16)],
        compiler_params=_cparams(("arbitrary",)),
    )(*xs, norm_w, w_in_t)


def _outmlp_kernel(*refs, bounds, final_norm, tn_up, tn_down):
    n = len(bounds)
    row_refs = refs[:5 * n]
    wo_ref, nw_ref, wu_ref, wd_ref, nf_ref = refs[5 * n:5 * n + 5]
    o_refs = refs[5 * n + 5:6 * n + 5]
    h_ref, u_ref = refs[6 * n + 5:]

    for k, (start, steps) in enumerate(bounds):
        x_ref, y_refs, o_ref = row_refs[5 * k], row_refs[5 * k + 1:5 * k + 5], o_refs[k]

        @_when_stream(start, steps)
        def _():
            x1 = x_ref[...]
            for g, y_ref in enumerate(y_refs):
                x1 = x1 + _mm(y_ref[...], wo_ref[g * GROUP_W:(g + 1) * GROUP_W, :])
            o_ref[...] = x1
            h_ref[...] = _rms(x1, nw_ref[...]).astype(BF16)
            for c0 in range(0, D_FF, tn_up):
                u = jnp.dot(h_ref[...], wu_ref[:, c0:c0 + tn_up], preferred_element_type=F32)
                u_ref[:, c0:c0 + tn_up] = jnp.square(jnp.maximum(u, 0.0)).astype(BF16)
            for c0 in range(0, D_MODEL, tn_down):
                o_ref[:, c0:c0 + tn_down] += jnp.dot(u_ref[...], wd_ref[:, c0:c0 + tn_down],
                                                     preferred_element_type=F32)
            if final_norm:
                o_ref[...] = _rms(o_ref[...], nf_ref[...])


def _outmlp(xs, ys, w_out, norm_w, w_up, w_down, norm_f, l, tm):
    bounds, n_steps = _stream_steps([x.shape[0] for x in xs], tm)
    row_specs, row_ops = [], []
    for x, y4, b in zip(xs, ys, bounds):
        row_specs += [_stream_spec(tm, D_MODEL, *b)] + [_stream_spec(tm, GROUP_W, *b)] * 4
        row_ops += [x, *y4]
    return pl.pallas_call(
        functools.partial(_outmlp_kernel, bounds=bounds, final_norm=(l == DEPTH - 1), tn_up=512, tn_down=256),
        grid=(n_steps,),
        in_specs=row_specs
        + [_layer_spec(l, (D_MODEL, D_MODEL), 1, single=True),
           _layer_spec(l, (1, D_MODEL), 1),
           _layer_spec(l, (D_MODEL, D_FF), 1, single=True),
           _layer_spec(l, (D_FF, D_MODEL), 1, single=True),
           pl.BlockSpec((1, D_MODEL), lambda i: (0, 0))],
        out_specs=[_stream_spec(tm, D_MODEL, *b) for b in bounds],
        out_shape=[jax.ShapeDtypeStruct((x.shape[0], D_MODEL), F32) for x in xs],
        scratch_shapes=[pltpu.VMEM((tm, D_MODEL), BF16),
                        pltpu.VMEM((tm, D_FF), BF16)],
        compiler_params=_cparams(("arbitrary",)),
    )(*row_ops, w_out, norm_w, w_up, w_down, norm_f)


def _rope_kernel(angle_ref, cos_ref, sin_ref, *, pos0, tl, period):
    i = pl.program_id(0)
    row = i * tl + lax.broadcasted_iota(jnp.int32, (tl, 2 * HEAD_W), 0)
    pos = (pos0 + row % period).astype(F32)
    theta = pos * angle_ref[...]
    lane = lax.broadcasted_iota(jnp.int32, (tl, 2 * HEAD_W), 1)
    cos_ref[...] = jnp.cos(theta)
    s = jnp.sin(theta)
    sin_ref[...] = jnp.where(lane % 2 == 0, -s, s)


def _rope_tables(angle_row, pos0, rows, period, tl):
    seq = rows
    return pl.pallas_call(
        functools.partial(_rope_kernel, pos0=pos0, tl=tl, period=period),
        grid=(seq // tl,),
        in_specs=[pl.BlockSpec((1, 2 * HEAD_W), lambda i: (0, 0))],
        out_specs=[pl.BlockSpec((tl, 2 * HEAD_W), lambda i: (i, 0))] * 2,
        out_shape=[jax.ShapeDtypeStruct((seq, 2 * HEAD_W), F32)] * 2,
        compiler_params=_cparams(("parallel",)),
    )(angle_row)


def _ret_kernel(*refs, c_seq, nseq, nblk, has_init, has_prev):
    lg_ref, p_ref, cos_ref, sin_ref = refs[:4]
    s0_ref = refs[4] if has_init else None
    y_ref, s_ref, sb_ref = refs[4 + has_init + has_prev:]
    nb = s_ref.shape[0]
    c = nseq * c_seq
    j = pl.program_id(1)

    @pl.when(j == 0)
    def _():
        sb_ref[...] = jnp.zeros_like(sb_ref)
        if has_init:
            for b in range(nb):
                for h in range(HEADS):
                    hs = slice(h * HEAD_W, (h + 1) * HEAD_W)
                    sb_ref[b, hs, hs] = s0_ref[b, h]

    def per_head(idx, shape):
        out = jnp.full(shape, lg_ref[HEADS - 1], F32)
        for h in range(HEADS - 2, -1, -1):
            out = jnp.where(idx == h, lg_ref[h], out)
        return out

    head = _head_of_lane(c)
    bd = _block_mask(HEAD_W)
    m_seg = jnp.where(bd, 1.0 / HEAD_W, 0.0).astype(BF16)
    even = lax.broadcasted_iota(jnp.int32, (c, GROUP_W), 1) % 2 == 0
    lg_lane = per_head(head[0:1, :], (1, GROUP_W))
    t_loc = (lax.broadcasted_iota(jnp.int32, (c, 1), 0) % c_seq).astype(F32)
    e_q = jnp.exp((t_loc + 1.0) * lg_lane)
    e_k = jnp.exp((c_seq - 1.0 - t_loc) * lg_lane)
    a_c = jnp.exp(c_seq * lg_lane)
    r_i, r2_i, h_i, causal = _seq_grid(c, c_seq, HEADS)
    dist = jnp.where(causal, (r_i - r2_i).astype(F32), 0.0)
    decay = jnp.where(causal, jnp.exp(dist * per_head(h_i, (c, HEADS * c))), 0.0)

    def rot(x, cos, sin):
        swapped = jnp.where(even, pltpu.roll(x, GROUP_W - 1, 1), pltpu.roll(x, 1, 1))
        return x * cos + swapped * sin

    for bs, ts, seqs in _row_blocks(nseq, c_seq, nblk):
        load = lambda lo: p_ref[bs, ts, lo:lo + GROUP_W].reshape(c, GROUP_W)
        crow = ts if nseq == 1 else slice(0, c)
        cos = jnp.concatenate([cos_ref[crow, :]] * 2, axis=1)
        sin = jnp.concatenate([sin_ref[crow, :]] * 2, axis=1)
        q = rot(load(0), cos, sin)
        k = rot(load(GROUP_W), cos, sin) * (HEAD_W ** -0.5)
        v = load(2 * GROUP_W)
        g = load(3 * GROUP_W)
        scores = _mm_nt(q, _expand_heads(k, head)) * decay
        o = _mm(scores, _expand_heads(v, head))
        qe = q * e_q
        ke = k * e_k
        cross = []
        for i, b in enumerate(seqs):
            sl = slice(i * c_seq, (i + 1) * c_seq)
            s_prev = sb_ref[b]
            cross.append(_mm_nt(qe[sl], s_prev))
            sb_ref[b] = a_c * s_prev + jnp.where(bd, _mm_tn(v[sl], ke[sl]), 0.0)
        o = o + _stack_rows(cross)
        mu = _seg_mean(o, m_seg)
        d = o - mu
        var = _seg_mean(d * d, m_seg)
        y_ref[bs, ts, :] = (d * lax.rsqrt(var + EPS) * _silu(g)).reshape(len(seqs), c_seq, GROUP_W)

    @pl.when(j == pl.num_programs(1) - 1)
    def _():
        for b in range(nb):
            for h in range(HEADS):
                hs = slice(h * HEAD_W, (h + 1) * HEAD_W)
                s_ref[b, h] = sb_ref[b, hs, hs].T


def _retention(proj, cos, sin, log_gamma, init, prev, l, c_seq, nseq, nblk):
    bsz, seq, _ = proj.shape
    nb, tb, grid = _mixer_grid(bsz, seq, c_seq, nseq, nblk)
    st_in, st_ops, st_out, st_shape, alias = _state_io(l, nb, (HEADS, HEAD_W, HEAD_W), bsz, init, prev)
    n_fixed = 4
    rope_spec = (pl.BlockSpec((tb, 2 * HEAD_W), lambda i, j: (j, 0)) if nseq == 1 else
                 pl.BlockSpec((nseq * c_seq, 2 * HEAD_W), lambda i, j: (0, 0)))
    return pl.pallas_call(
        functools.partial(_ret_kernel, c_seq=c_seq, nseq=nseq, nblk=nblk, has_init=init is not None,
                          has_prev=prev is not None),
        grid=grid,
        in_specs=[pl.BlockSpec(memory_space=pltpu.SMEM),
                  pl.BlockSpec((nb, tb, 4 * GROUP_W), lambda i, j: (i, j, COL_RET // (4 * GROUP_W))),
                  rope_spec, rope_spec] + st_in,
        out_specs=[pl.BlockSpec((nb, tb, GROUP_W), lambda i, j: (i, j, 0)), st_out],
        out_shape=[jax.ShapeDtypeStruct((bsz, seq, GROUP_W), F32), st_shape],
        scratch_shapes=[pltpu.VMEM((nb, GROUP_W, GROUP_W), F32)],
        input_output_aliases={} if alias is None else {n_fixed + alias: 1},
        compiler_params=_cparams(("parallel", "arbitrary")),
    )(log_gamma, proj, cos, sin, *st_ops)


def _hg_kernel(*refs, c_seq, nseq, nblk, layer, has_init, has_prev):
    p_ref, lb_ref = refs[:2]
    s0_ref = refs[2] if has_init else None
    y_ref, s_ref, st_ref, keep_ref = refs[2 + has_init + has_prev:]
    nb = s_ref.shape[0]
    c = nseq * c_seq
    j = pl.program_id(1)

    pairs = [slice(pr * LANES, (pr + 1) * LANES) for pr in range(HEADS // PAIR_HEADS)]

    def pair_block(h):
        pr, jh = divmod(h, PAIR_HEADS)
        return pr, slice(jh * HEAD_W, (jh + 1) * HEAD_W)

    def load_state(b):
        st_ref[b] = jnp.zeros(st_ref.shape[1:], F32)
        if has_init:
            for h in range(HEADS):
                pr, hs = pair_block(h)
                st_ref[b, pr, hs, hs] = s0_ref[b, h]

    @pl.when(j == 0)
    def _():
        for b in range(nb):
            load_state(b)

    logits = lb_ref[...]
    e = jnp.exp(logits - jnp.max(logits, axis=0, keepdims=True))
    sm = e / jnp.sum(e, axis=0, keepdims=True)
    csum = sm[0:1, :]
    for i in range(1, layer + 1):
        csum = csum + sm[i:i + 1, :]
    lower = csum - sm[0:1, :]

    head = _head_of_lane(c)
    m_seg = jnp.where(_block_mask(HEAD_W), 1.0 / HEAD_W, 0.0).astype(BF16)
    bd = _block_mask(HEAD_W, LANES)
    r_i, r2_i, _, tril = _seq_grid(c, c_seq, 1)
    trif = tril.astype(F32)
    later = r2_i > r_i
    if nseq > 1:
        later = jnp.logical_and(r_i // c_seq == r2_i // c_seq, later)
    laterf = later.astype(F32)
    causal = _seq_grid(c, c_seq, HEADS)[3]
    mid = c // 2 - 1
    mask_sum = _mm_mask_f32 if c >= LANES else _mm_f32

    staged = []
    spreads = []
    for bs, ts, seqs in _row_blocks(nseq, c_seq, nblk):
        load = lambda lo: p_ref[bs, ts, lo:lo + GROUP_W].reshape(c, GROUP_W)
        qq = _silu(load(0))
        ff = load(GROUP_W)
        v = load(2 * GROUP_W)
        sig = jax.nn.sigmoid(ff)
        forget = lower + (1.0 - lower) * sig
        log_f = jnp.log(jnp.maximum(forget, HG_MIN_FORGET))
        key = (1.0 - lower) * (1.0 - sig)
        bcum = mask_sum(trif, log_f)
        if nseq == 1:
            b_ref = bcum[mid:mid + 1, :]
            spread = bcum - b_ref
            q_in = qq * jnp.exp(spread)
            k_in = key * jnp.exp(-spread)
            q_x = q_in * jnp.exp(b_ref)
            k_x = k_in * jnp.exp(bcum[c - 1:c, :] - b_ref)
        else:
            spread = bcum
            q_in = q_x = qq * jnp.exp(bcum)
            k_in = key * jnp.exp(-bcum)
            k_x = key * jnp.exp(mask_sum(laterf, log_f))
        spreads.append(jnp.max(jnp.abs(spread)))
        scores = jnp.where(causal, _mm_nt(q_in, _expand_heads(k_in, head)), 0.0)
        o = _mm(scores, _expand_heads(v, head))
        upd = []
        for i in range(len(seqs)):
            sl = slice(i * c_seq, (i + 1) * c_seq)
            a_seq = jnp.exp(bcum[(i + 1) * c_seq - 1:(i + 1) * c_seq, :])
            ds = [jnp.where(bd, _mm_tn(v[sl, pl_], k_x[sl, pl_]), 0.0) for pl_ in pairs]
            upd.append((q_x[sl], a_seq, ds))
        staged.append((bs, ts, seqs, o, upd))

    unsafe = functools.reduce(jnp.maximum, spreads) > HG_SAFE_SPREAD
    if nb == 1:
        keep_ref[...] = st_ref[0]

    for bs, ts, seqs, o, upd in staged:
        cross = []
        for b, (q_b, a_seq, ds) in zip(seqs, upd):
            parts = []
            for pr, pl_ in enumerate(pairs):
                st_prev = st_ref[b, pr]
                parts.append(_mm_nt(q_b[:, pl_], st_prev))
                st_ref[b, pr] = st_prev * a_seq[:, pl_] + ds[pr]
            cross.append(jnp.concatenate(parts, axis=1))
        o = o + _stack_rows(cross)
        gate = _silu(p_ref[bs, ts, 3 * GROUP_W:4 * GROUP_W].reshape(c, GROUP_W))
        y = o * lax.rsqrt(_seg_mean(o * o, m_seg) + EPS) * gate
        y_ref[bs, ts, :] = y.reshape(bs.stop - bs.start, ts.stop - ts.start, GROUP_W)

    @pl.when(unsafe)
    def _():
        tb = p_ref.shape[1]
        assert nb == 1 or (tb == SUBLANES and pl.num_programs(1) == 1)
        rowid = lax.broadcasted_iota(jnp.int32, (SUBLANES, GROUP_W), 0)
        if nb == 1:
            st_ref[0] = keep_ref[...]

        def slab(s, carry):
            b, t0 = (0, pl.multiple_of(s * SUBLANES, SUBLANES)) if nb == 1 else (s, 0)
            rows = pl.ds(t0, SUBLANES)
            if nb > 1:
                load_state(b)
            q8 = _silu(p_ref[b, rows, 0:GROUP_W])
            sig = jax.nn.sigmoid(p_ref[b, rows, GROUP_W:2 * GROUP_W])
            v8 = p_ref[b, rows, 2 * GROUP_W:3 * GROUP_W]
            g8 = p_ref[b, rows, 3 * GROUP_W:4 * GROUP_W]
            f8 = jnp.maximum(lower + (1.0 - lower) * sig, HG_MIN_FORGET)
            k8 = (1.0 - lower) * (1.0 - sig)
            o8 = jnp.zeros((SUBLANES, GROUP_W), F32)
            for i in range(SUBLANES):
                v_i = jnp.where(rowid == i, v8, 0.0)
                parts = []
                for pr, pl_ in enumerate(pairs):
                    outer = _mm_tn(v_i[:, pl_], k8[:, pl_])
                    st = st_ref[b, pr] * f8[i:i + 1, pl_] + jnp.where(bd, outer, 0.0)
                    st_ref[b, pr] = st
                    parts.append(_mm_nt(q8[:, pl_], st))
                o8 = jnp.where(rowid == i, jnp.concatenate(parts, axis=1), o8)
            y_ref[b, rows, :] = o8 * lax.rsqrt(_seg_mean(o8 * o8, m_seg) + EPS) * _silu(g8)
            return carry

        lax.fori_loop(0, nb * tb // SUBLANES, slab, 0)

    @pl.when(j == pl.num_programs(1) - 1)
    def _():
        for b in range(nb):
            for h in range(HEADS):
                pr, hs = pair_block(h)
                s_ref[b, h] = st_ref[b, pr, hs, hs].T


def _hgrn2(proj, lb_logits, init, prev, l, c_seq, nseq, nblk):
    bsz, seq, _ = proj.shape
    nb, tb, grid = _mixer_grid(bsz, seq, c_seq, nseq, nblk)
    st_in, st_ops, st_out, st_shape, alias = _state_io(l, nb, (HEADS, HEAD_W, HEAD_W), bsz, init, prev)
    n_fixed = 2
    return pl.pallas_call(
        functools.partial(_hg_kernel, c_seq=c_seq, nseq=nseq, nblk=nblk, layer=l, has_init=init is not None,
                          has_prev=prev is not None),
        grid=grid,
        in_specs=[pl.BlockSpec((nb, tb, 4 * GROUP_W), lambda i, j: (i, j, COL_HG // (4 * GROUP_W))),
                  pl.BlockSpec((DEPTH, GROUP_W), lambda i, j: (0, 0))] + st_in,
        out_specs=[pl.BlockSpec((nb, tb, GROUP_W), lambda i, j: (i, j, 0)), st_out],
        out_shape=[jax.ShapeDtypeStruct((bsz, seq, GROUP_W), F32), st_shape],
        scratch_shapes=[pltpu.VMEM((nb, HEADS // PAIR_HEADS, LANES, LANES), F32),
                        pltpu.VMEM((HEADS // PAIR_HEADS, LANES, LANES), F32)],
        input_output_aliases={} if alias is None else {n_fixed + alias: 1},
        compiler_params=_cparams(("parallel", "arbitrary")),
    )(proj, lb_logits, *st_ops)


def _ssd_kernel(*refs, c_seq, nseq, nblk, has_init, has_prev):
    p_ref, dt_ref, cw_ref, cb_ref, dtb_ref, alog_ref, dsk_ref, nw_ref = refs[:8]
    n_in = 8
    cv0_ref = s0_ref = None
    if has_init:
        cv0_ref, s0_ref = refs[n_in], refs[n_in + 1 + has_prev]
    y_ref, s_ref, cv_ref, xpad_ref = refs[n_in + 2 * (has_init + has_prev):]
    nb = s_ref.shape[0]
    tb = p_ref.shape[1]
    c = nseq * c_seq
    n_tail = SSD_CONV - 1
    lo = CONV_PAD - n_tail

    @pl.when(pl.program_id(1) == 0)
    def _():
        xpad_ref[:, 0:CONV_PAD, :] = jnp.zeros((nb, CONV_PAD, SSD_CONV_DIM), F32)
        if has_init:
            s_ref[...] = s0_ref[...]
            xpad_ref[:, lo:CONV_PAD, :] = cv0_ref[...]
        else:
            s_ref[...] = jnp.zeros_like(s_ref)

    r_i, r2_i, _, tri = _seq_grid(c, c_seq, 1)
    trif = tri.astype(F32)
    later = r2_i > r_i
    if nseq > 1:
        later = jnp.logical_and(r_i // c_seq == r2_i // c_seq, later)
    laterf = later.astype(F32)
    head = _head_of_lane(c)
    m_grp = jnp.where(_block_mask(LANES), 1.0 / LANES, 0.0).astype(BF16)
    cw = cw_ref[...]
    groups = [slice(g * LANES, (g + 1) * LANES) for g in range(SSD_GROUPS)]

    xpad_ref[:, CONV_PAD:CONV_PAD + tb, :] = p_ref[:, :, GROUP_W:GROUP_W + SSD_CONV_DIM]
    for bs, ts, seqs in _row_blocks(nseq, c_seq, nblk):
        r0 = ts.start
        load = lambda ref, col, w: ref[bs, ts, col:col + w].reshape(c, w)
        z = load(p_ref, 0, GROUP_W)
        conv = cb_ref[...] + cw[n_tail:n_tail + 1, :] * load(p_ref, GROUP_W, SSD_CONV_DIM)
        for w in range(n_tail):
            win = xpad_ref[bs, r0 + lo + w:r0 + lo + w + c_seq, :]
            conv = conv + cw[w:w + 1, :] * win.reshape(c, SSD_CONV_DIM)
        xc = _silu(conv)
        xs = xc[:, 0:GROUP_W]
        bm = xc[:, GROUP_W:2 * GROUP_W]
        cm = xc[:, 2 * GROUP_W:3 * GROUP_W]

        delta = jax.nn.softplus(load(dt_ref, 0, GROUP_W) + dtb_ref[...])
        log_a = -delta * jnp.exp(alog_ref[...])
        bcum = _mm_mask_f32(trif, log_a)
        brem = bcum[c - 1:c, :] - bcum if nseq == 1 else _mm_mask_f32(laterf, log_a)
        xd = xs * delta
        xr = xd * jnp.exp(brem)
        brow = [bcum[:, gl].T for gl in groups]
        gram = [_mm_nt(cm[:, gl], bm[:, gl]) for gl in groups]
        w_parts = []
        for h in range(HEADS):
            g, jh = divmod(h, HEADS // SSD_GROUPS)
            b_t = bcum[:, h * HEAD_W:h * HEAD_W + 1]
            b_s = brow[g][jh * HEAD_W:jh * HEAD_W + 1, :]
            w_parts.append(jnp.where(tri, gram[g] * jnp.exp(b_t - b_s), 0.0))
        o = _mm(jnp.concatenate(w_parts, axis=1), _expand_heads(xd, head))
        cross = []
        for i, b in enumerate(seqs):
            sl = slice(i * c_seq, (i + 1) * c_seq)
            last = (i + 1) * c_seq - 1
            parts = []
            for g, gl in enumerate(groups):
                s_prev = s_ref[b, g]
                parts.append(_mm_nt(cm[sl, gl], s_prev))
                a_col = jnp.exp(brow[g][:, last:last + 1])
                s_ref[b, g] = s_prev * a_col + _mm_tn(xr[sl, gl], bm[sl, gl])
            cross.append(jnp.concatenate(parts, axis=1))
        o = o + jnp.exp(bcum) * _stack_rows(cross)
        y = (o + dsk_ref[...] * xs) * _silu(z)
        y = y * lax.rsqrt(_seg_mean(y * y, m_grp) + EPS) * nw_ref[...]
        y_ref[bs, ts, :] = y.reshape(len(seqs), c_seq, GROUP_W)
    tail = xpad_ref[:, tb + lo:tb + CONV_PAD, :]
    cv_ref[...] = tail
    xpad_ref[:, lo:CONV_PAD, :] = tail


def _ssd(proj, prm, init_cv, init_s, prev_cv, prev_s, l, c_seq, nseq, nblk):
    bsz, seq, _ = proj.shape
    nb, tb, grid = _mixer_grid(bsz, seq, c_seq, nseq, nblk)
    cv_in, cv_ops, cv_out, cv_shape, cv_alias = _state_io(l, nb, (SSD_CONV - 1, SSD_CONV_DIM), bsz,
                                                          init_cv, prev_cv)
    s_in, s_ops, s_out, s_shape, s_alias = _state_io(l, nb, (SSD_GROUPS, LANES, SSD_STATE), bsz,
                                                     init_s, prev_s)
    n_fixed = 8
    aliases = {}
    if cv_alias is not None:
        aliases[n_fixed + cv_alias] = 2
        aliases[n_fixed + len(cv_in) + s_alias] = 1
    return pl.pallas_call(
        functools.partial(_ssd_kernel, c_seq=c_seq, nseq=nseq, nblk=nblk, has_init=init_s is not None,
                          has_prev=prev_s is not None),
        grid=grid,
        in_specs=[pl.BlockSpec((nb, tb, 4 * GROUP_W), lambda i, j: (i, j, COL_SSD // (4 * GROUP_W))),
                  pl.BlockSpec((nb, tb, GROUP_W), lambda i, j: (i, j, COL_DT // GROUP_W)),
                  _layer_spec(l, (SSD_CONV, SSD_CONV_DIM), 2),
                  _layer_spec(l, (1, SSD_CONV_DIM), 2),
                  _layer_spec(l, (1, GROUP_W), 2),
                  _layer_spec(l, (1, GROUP_W), 2),
                  _layer_spec(l, (1, GROUP_W), 2),
                  _layer_spec(l, (1, GROUP_W), 2)] + cv_in + s_in,
        out_specs=[pl.BlockSpec((nb, tb, GROUP_W), lambda i, j: (i, j, 0)), s_out, cv_out],
        out_shape=[jax.ShapeDtypeStruct((bsz, seq, GROUP_W), F32), s_shape, cv_shape],
        scratch_shapes=[pltpu.VMEM((nb, tb + CONV_PAD, SSD_CONV_DIM), F32)],
        input_output_aliases=aliases,
        compiler_params=_cparams(("parallel", "arbitrary")),
    )(proj, proj, prm["conv_w"], prm["conv_b"], prm["dt_bias"], prm["a_log"], prm["ssd_d"], prm["ssd_norm"],
      *cv_ops, *s_ops)


def _ssd_state_in(s):
    d, bsz = s.shape[:2]
    return jnp.swapaxes(s, 3, 4).reshape(d, bsz, SSD_GROUPS, LANES, SSD_STATE)


def _ssd_state_out(s):
    d, bsz = s.shape[:2]
    return jnp.swapaxes(s.reshape(d, bsz, HEADS, HEAD_W, SSD_STATE), 3, 4)


def _s5_kernel(*refs, tl, has_init, has_prev):
    (u0_ref, u1_ref, are_ref, aim_ref, ldt_ref, bre_ref, bim_ref, cw_ref, dsk_ref, gw_ref,
     gb_ref) = refs[:11]
    n_in = 11
    s0r_ref = s0i_ref = None
    if has_init:
        s0r_ref, s0i_ref = refs[n_in], refs[n_in + 1 + has_prev]
    (y_ref, sr_ref, si_ref, wb_ref, ab_ref, us0_ref, us1_ref, ut_ref, bu_ref, yt_ref, ys0_ref,
     ys1_ref) = refs[n_in + 2 * (has_init + has_prev):]
    nbt = SUBLANES * tl

    @pl.when(pl.program_id(1) == 0)
    def _():
        if has_init:
            sr_ref[...] = s0r_ref[...]
            si_ref[...] = s0i_ref[...]
        else:
            sr_ref[...] = jnp.zeros_like(sr_ref)
            si_ref[...] = jnp.zeros_like(si_ref)
        lr, li, st = are_ref[...], aim_ref[...], jnp.exp(ldt_ref[...])
        mag = jnp.exp(lr * st)
        abr = mag * jnp.cos(li * st)
        abi = mag * jnp.sin(li * st)
        ab_ref[0:1, :] = abr
        ab_ref[1:2, :] = abi
        den = lr * lr + li * li
        cr = ((abr - 1.0) * lr + abi * li) / den
        ci = (abi * lr - (abr - 1.0) * li) / den
        wb_ref[:, 0:S5_N] = (cr * bre_ref[...] - ci * bim_ref[...]).astype(BF16)
        wb_ref[:, S5_N:2 * S5_N] = (cr * bim_ref[...] + ci * bre_ref[...]).astype(BF16)

    us0_ref[...] = u0_ref[...].reshape(nbt, LANES)
    us1_ref[...] = u1_ref[...].reshape(nbt, LANES)

    nparts = bu_ref.shape[0]
    prow = nbt // nparts
    psteps = tl // nparts
    for h in range(nparts):
        for t in range(h * psteps, (h + 1) * psteps):
            dst = slice(t * SUBLANES, (t + 1) * SUBLANES)
            src = pl.ds(t, SUBLANES, stride=tl)
            ut_ref[dst, 0:LANES] = us0_ref[src, :]
            ut_ref[dst, LANES:2 * LANES] = us1_ref[src, :]
        u_h = ut_ref[h * prow:(h + 1) * prow, :]
        bu_ref[h] = jnp.dot(u_h.astype(BF16), wb_ref[...], preferred_element_type=F32)
    ar = jnp.broadcast_to(ab_ref[0:1, :], (SUBLANES, S5_N))
    ai = jnp.broadcast_to(ab_ref[1:2, :], (SUBLANES, S5_N))
    xr, xi = sr_ref[...], si_ref[...]
    for h in range(nparts):
        for t in range(tl // nparts):
            rows = slice(t * SUBLANES, (t + 1) * SUBLANES)
            nr = ar * xr - ai * xi + bu_ref[h, rows, 0:S5_N]
            ni = ar * xi + ai * xr + bu_ref[h, rows, S5_N:2 * S5_N]
            bu_ref[h, rows, 0:S5_N] = nr
            bu_ref[h, rows, S5_N:2 * S5_N] = ni
            xr, xi = nr, ni
        y = _mm(bu_ref[h], cw_ref[...]) + dsk_ref[...] * ut_ref[h * prow:(h + 1) * prow, :]
        y = jax.nn.gelu(y)
        yt_ref[h * prow:(h + 1) * prow, :] = y * jax.nn.sigmoid(_mm(y, gw_ref[...]) + gb_ref[...])
        for t in range(h * psteps, (h + 1) * psteps):
            src = slice(t * SUBLANES, (t + 1) * SUBLANES)
            dst = pl.ds(t, SUBLANES, stride=tl)
            ys0_ref[dst, :] = yt_ref[src, 0:LANES]
            ys1_ref[dst, :] = yt_ref[src, LANES:2 * LANES]
    sr_ref[...] = xr
    si_ref[...] = xi
    y_ref[:, :, 0:LANES] = ys0_ref[...].reshape(SUBLANES, tl, LANES)
    y_ref[:, :, LANES:2 * LANES] = ys1_ref[...].reshape(SUBLANES, tl, LANES)


def _s5(proj, prm, init_re, init_im, prev_re, prev_im, l, tl, nparts):
    bsz, seq, _ = proj.shape
    re_in, re_ops, re_out, re_shape, re_alias = _state_io(l, SUBLANES, (S5_N,), bsz, init_re, prev_re)
    im_in, im_ops, im_out, im_shape, im_alias = _state_io(l, SUBLANES, (S5_N,), bsz, init_im, prev_im)
    n_fixed = 11
    aliases = {}
    if re_alias is not None:
        aliases[n_fixed + re_alias] = 1
        aliases[n_fixed + len(re_in) + im_alias] = 2
    nbt = SUBLANES * tl
    return pl.pallas_call(
        functools.partial(_s5_kernel, tl=tl, has_init=init_re is not None,
                          has_prev=prev_re is not None),
        grid=(bsz // SUBLANES, seq // tl),
        in_specs=[pl.BlockSpec((SUBLANES, tl, LANES), lambda i, j: (i, j, COL_S5 // LANES)),
                  pl.BlockSpec((SUBLANES, tl, LANES), lambda i, j: (i, j, COL_S5 // LANES + 1)),
                  _layer_spec(l, (1, S5_N), 2), _layer_spec(l, (1, S5_N), 2), _layer_spec(l, (1, S5_N), 2),
                  _layer_spec(l, (GROUP_W, S5_N), 2), _layer_spec(l, (GROUP_W, S5_N), 2),
                  _layer_spec(l, (2 * S5_N, GROUP_W), 2),
                  _layer_spec(l, (1, GROUP_W), 2),
                  _layer_spec(l, (GROUP_W, GROUP_W), 2),
                  _layer_spec(l, (1, GROUP_W), 2)] + re_in + im_in,
        out_specs=[pl.BlockSpec((SUBLANES, tl, GROUP_W), lambda i, j: (i, j, 0)), re_out, im_out],
        out_shape=[jax.ShapeDtypeStruct((bsz, seq, GROUP_W), F32), re_shape, im_shape],
        scratch_shapes=[pltpu.VMEM((GROUP_W, 2 * S5_N), BF16),
                        pltpu.VMEM((SUBLANES, S5_N), F32),
                        pltpu.VMEM((nbt, LANES), F32), pltpu.VMEM((nbt, LANES), F32),
                        pltpu.VMEM((nbt, GROUP_W), F32),
                        pltpu.VMEM((nparts, nbt // nparts, 2 * S5_N), F32),
                        pltpu.VMEM((nbt, GROUP_W), F32),
                        pltpu.VMEM((nbt, LANES), F32), pltpu.VMEM((nbt, LANES), F32)],
        input_output_aliases=aliases,
        compiler_params=_cparams(("parallel", "arbitrary")),
    )(proj, proj, prm["a_re"], prm["a_im"], prm["log_dt"], prm["b_re"], prm["b_im"], prm["c_blk"],
      prm["s5_d"], prm["glu_w"], prm["glu_b"], *re_ops, *im_ops)


def _block_diag(blocks):
    d, g, r, c = blocks.shape
    eye = jnp.eye(g, dtype=blocks.dtype)
    return (eye[None, :, None, :, None] * blocks[:, :, :, None, :]).reshape(d, g * r, g * c)


def _prep(ssd_conv_w, ssd_conv_b, ssd_dt_bias, ssd_a_log, ssd_d, ssd_norm, s5_a_re, s5_a_im, s5_log_dt,
          s5_b_re, s5_b_im, s5_c_re, s5_c_im, s5_d, s5_glu_w, s5_glu_b):
    per_head = lambda v: jnp.repeat(v, HEAD_W, axis=1)[:, None, :]
    c_blk = jnp.concatenate([_block_diag(jnp.swapaxes(s5_c_re, 2, 3)),
                             -_block_diag(jnp.swapaxes(s5_c_im, 2, 3))], axis=1)
    return dict(
        conv_w=ssd_conv_w, conv_b=ssd_conv_b[:, None, :],
        dt_bias=per_head(ssd_dt_bias), a_log=per_head(ssd_a_log), ssd_d=per_head(ssd_d),
        ssd_norm=ssd_norm[:, None, :],
        a_re=s5_a_re.reshape(DEPTH, 1, S5_N), a_im=s5_a_im.reshape(DEPTH, 1, S5_N),
        log_dt=jnp.repeat(s5_log_dt, S5_STATE, axis=1)[:, None, :],
        b_re=_block_diag(jnp.swapaxes(s5_b_re, 2, 3)), b_im=_block_diag(jnp.swapaxes(s5_b_im, 2, 3)),
        c_blk=c_blk.astype(BF16),
        s5_d=s5_d[:, None, :], glu_w=s5_glu_w.astype(BF16), glu_b=s5_glu_b[:, None, :],
    )


class _Stream:
    def __init__(self, x, pos0, init, cfg, angle_row):
        self.bsz, self.seq, _ = x.shape
        self.cfg = cfg
        self.x2d = x.reshape(self.bsz * self.seq, D_MODEL)
        c_ret, nseq_ret, _ = cfg["ret"]
        rope_rows = self.seq if nseq_ret == 1 else nseq_ret * c_ret
        self.cos, self.sin = _rope_tables(angle_row, pos0, rope_rows, self.seq, cfg["rope_tl"])
        self.init = (None,) * 6
        if init is not None:
            i_ret, i_hg, i_ssd, i_cv, i_re, i_im = init
            self.init = (jnp.swapaxes(i_ret, 3, 4), jnp.swapaxes(i_hg, 3, 4), _ssd_state_in(i_ssd), i_cv,
                         i_re.reshape(DEPTH, self.bsz, S5_N), i_im.reshape(DEPTH, self.bsz, S5_N))
        self.new = (None,) * 6

    def mix(self, proj2d, l, prm, lb_logits, log_gamma):
        proj = proj2d.reshape(self.bsz, self.seq, IN_COLS_PAD)
        i_ret, i_hg, i_ssd, i_cv, i_re, i_im = self.init
        s_ret, s_hg, s_ssd, s_cv, s_re, s_im = self.new
        cfg = self.cfg
        ya, s_ret = _retention(proj, self.cos, self.sin, log_gamma, i_ret, s_ret, l, *cfg["ret"])
        yb, s_hg = _hgrn2(proj, lb_logits, i_hg, s_hg, l, *cfg["hg"])
        yc, s_ssd, s_cv = _ssd(proj, prm, i_cv, i_ssd, s_cv, s_ssd, l, *cfg["ssd"])
        yd, s_re, s_im = _s5(proj, prm, i_re, i_im, s_re, s_im, l, *cfg["s5"])
        self.new = (s_ret, s_hg, s_ssd, s_cv, s_re, s_im)
        return [y.reshape(self.bsz * self.seq, GROUP_W) for y in (ya, yb, yc, yd)]

    def outputs(self):
        s_ret, s_hg, s_ssd, s_cv, s_re, s_im = self.new
        tail = (DEPTH, self.bsz, S5_GROUPS, S5_STATE)
        return (self.x2d.reshape(self.bsz, self.seq, D_MODEL), s_ret, s_hg,
                _ssd_state_out(s_ssd), s_cv,
                s_re.reshape(tail), s_im.reshape(tail))


def _trunk(streams, prm, w_in, w_out, w_up, w_down, norm_mix, norm_mlp, norm_final, lb_logits, log_gamma,
           tm_in, tm_out):
    for l in range(DEPTH):
        projs = _inproj([s.x2d for s in streams], norm_mix, w_in, l, tm_in)
        ys = [s.mix(p, l, prm, lb_logits, log_gamma) for s, p in zip(streams, projs)]
        outs = _outmlp([s.x2d for s in streams], ys, w_out, norm_mlp, w_up, w_down, norm_final, l, tm_out)
        for s, x2d in zip(streams, outs):
            s.x2d = x2d
    return [s.outputs() for s in streams]


PROMPT_CFG = dict(rope_tl=256, ret=(256, 1, 8), hg=(64, 1, 16), ssd=(128, 1, 16), s5=(256, 4))
SAMPLE_CFG = dict(rope_tl=128, ret=(8, 16, 2), hg=(8, 16, 2), ssd=(8, 16, 2), s5=(8, 1))
TM_IN = 512
TM_OUT = 512


def kernel(x_prompt, x_sample, state_ret, state_hgrn, state_ssd, state_ssd_conv, state_s5_re, state_s5_im, norm_mix, w_in, w_out, hg_lb_logits, ssd_conv_w, ssd_conv_b, ssd_dt_bias, ssd_a_log, ssd_d, ssd_norm, s5_a_re, s5_a_im, s5_log_dt, s5_b_re, s5_b_im, s5_c_re, s5_c_im, s5_d, s5_glu_w, s5_glu_b, norm_mlp, w_up, w_down, norm_final):
    prm = _prep(ssd_conv_w, ssd_conv_b, ssd_dt_bias, ssd_a_log, ssd_d, ssd_norm, s5_a_re, s5_a_im, s5_log_dt,
                s5_b_re, s5_b_im, s5_c_re, s5_c_im, s5_d, s5_glu_w, s5_glu_b)
    w_out_b = _cast_bf16(w_out, 512)
    w_up_b = _cast_bf16(w_up, 256)
    w_down_b = _cast_bf16(w_down, 1024)
    angle = 1.0 / (ROPE_BASE ** jnp.linspace(0.0, 1.0, HEAD_W // 2, dtype=F32))
    angle_row = jnp.tile(jnp.repeat(angle, 2), 2)[None, :]
    log_gamma = jnp.log(1.0 - jnp.exp2(-5.0 - jnp.arange(HEADS, dtype=F32)))
    states = (state_ret, state_hgrn, state_ssd, state_ssd_conv, state_s5_re, state_s5_im)
    streams = [_Stream(x_prompt, 0, None, PROMPT_CFG, angle_row),
               _Stream(x_sample, PAST_LEN, states, SAMPLE_CFG, angle_row)]
    out_p, out_s = _trunk(streams, prm, jnp.swapaxes(w_in, 1, 2), w_out_b, w_up_b, w_down_b, norm_mix[:, None, :],
                          norm_mlp[:, None, :], norm_final[None, :], hg_lb_logits, log_gamma, TM_IN, TM_OUT)
    return (out_p[0], out_s[0]) + out_p[1:] + out_s[1:]
```

```python
import functools

import jax
import jax.numpy as jnp
from jax import lax
from jax.experimental import pallas as pl
from jax.experimental.pallas import tpu as pltpu

F32 = jnp.float32
BF16 = jnp.bfloat16

D_MODEL = 1024
DEPTH = 2
PAST_LEN = 16384
GROUP_W = 256
HEADS = 4
HEAD_W = 64
PAIR_HEADS = 2
ROPE_BASE = 10000.0
HG_MIN_FORGET = 1e-30
HG_SAFE_SPREAD = 80.0
SSD_STATE = 128
SSD_GROUPS = 2
SSD_CONV = 4
SSD_CONV_DIM = 768
CONV_PAD = 8
S5_GROUPS = 16
S5_STATE = 64
S5_N = S5_GROUPS * S5_STATE
D_FF = 4096
IN_COLS = 3332
EPS = 1e-6

COL_RET = 0
COL_HG = 1024
COL_SSD = 2048
COL_S5 = 3072
COL_DT = 3328
IN_COLS_PAD = 3584

LANES = 128
SUBLANES = 8
VMEM_LIMIT = 56 * 1024 * 1024


def _cparams(sem):
    return pltpu.CompilerParams(dimension_semantics=sem, vmem_limit_bytes=VMEM_LIMIT)


def _mm(a, b):
    return jnp.dot(a.astype(BF16), b.astype(BF16), preferred_element_type=F32)


def _mm_nt(a, b):
    return lax.dot_general(a.astype(BF16), b.astype(BF16), (((1,), (1,)), ((), ())),
                           preferred_element_type=F32)


def _mm_tn(a, b):
    return lax.dot_general(a.astype(BF16), b.astype(BF16), (((0,), (0,)), ((), ())),
                           preferred_element_type=F32)


def _mm_f32(a, b):
    return jnp.dot(a, b, preferred_element_type=F32, precision=lax.Precision.HIGHEST)


def _mm_mask_f32(mask01, x):
    m = mask01.astype(BF16)
    hi = x.astype(BF16)
    r1 = x - hi.astype(F32)
    mid = r1.astype(BF16)
    lo = (r1 - mid.astype(F32)).astype(BF16)
    return (jnp.dot(m, hi, preferred_element_type=F32) + jnp.dot(m, mid, preferred_element_type=F32)
            + jnp.dot(m, lo, preferred_element_type=F32))


def _silu(x):
    return x * jax.nn.sigmoid(x)


def _rms(x, w):
    return x * lax.rsqrt(jnp.mean(x * x, axis=-1, keepdims=True) + EPS) * w


def _head_of_lane(rows):
    return lax.broadcasted_iota(jnp.int32, (rows, GROUP_W), 1) // HEAD_W


def _expand_heads(x, head):
    nheads = x.shape[1] // HEAD_W
    return jnp.concatenate([jnp.where(head == h, x, 0.0) for h in range(nheads)], axis=0).astype(BF16)


def _block_mask(seg, width=GROUP_W):
    r = lax.broadcasted_iota(jnp.int32, (width, width), 0) // seg
    c = lax.broadcasted_iota(jnp.int32, (width, width), 1) // seg
    return r == c


def _seg_mean(x, m):
    hi = x.astype(BF16)
    lo = (x - hi.astype(F32)).astype(BF16)
    return jnp.dot(hi, m, preferred_element_type=F32) + jnp.dot(lo, m, preferred_element_type=F32)


def _row_blocks(nseq, c_seq, nblk):
    if nseq == 1:
        return [(slice(0, 1), slice(k * c_seq, (k + 1) * c_seq), [0]) for k in range(nblk)]
    return [(slice(k * nseq, (k + 1) * nseq), slice(0, c_seq), list(range(k * nseq, (k + 1) * nseq)))
            for k in range(nblk)]


def _seq_grid(rows, c_seq, copies):
    r = lax.broadcasted_iota(jnp.int32, (rows, copies * rows), 0)
    j = lax.broadcasted_iota(jnp.int32, (rows, copies * rows), 1)
    h = j // rows
    r2 = j - h * rows
    causal = r >= r2
    if rows > c_seq:
        causal = jnp.logical_and(r // c_seq == r2 // c_seq, causal)
    return r, r2, h, causal


def _stack_rows(pieces):
    return pieces[0] if len(pieces) == 1 else jnp.concatenate(pieces, axis=0)


def _mixer_grid(bsz, seq, c_seq, nseq, nblk):
    nb = 1 if nseq == 1 else nseq * nblk
    tb = c_seq * nblk if nseq == 1 else c_seq
    assert nseq == 1 or seq == c_seq
    return nb, tb, (bsz // nb, seq // tb)


def _layer_spec(l, tail, n_grid, single=False):
    zeros = (0,) * len(tail)
    kw = dict(pipeline_mode=pl.Buffered(1)) if single else {}
    if n_grid == 1:
        return pl.BlockSpec((None,) + tail, lambda i: (l,) + zeros, **kw)
    return pl.BlockSpec((None,) + tail, lambda i, j: (l,) + zeros, **kw)


def _state_spec(l, nb, tail):
    zeros = (0,) * len(tail)
    return pl.BlockSpec((None, nb) + tail, lambda i, j: (l, i) + zeros)


def _state_io(l, nb, tail, bsz, init, prev):
    spec = _state_spec(l, nb, tail)
    in_specs, operands = [], []
    if init is not None:
        in_specs.append(spec)
        operands.append(init)
    alias_pos = None
    if prev is not None:
        alias_pos = len(in_specs)
        in_specs.append(pl.BlockSpec(memory_space=pl.ANY))
        operands.append(prev)
    return in_specs, operands, spec, jax.ShapeDtypeStruct((DEPTH, bsz) + tail, F32), alias_pos


def _cast_kernel(w_ref, o_ref):
    o_ref[...] = w_ref[...].astype(BF16)


def _cast_bf16(w, tr):
    d, r, c = w.shape
    spec = pl.BlockSpec((None, tr, c), lambda l, i: (l, i, 0))
    return pl.pallas_call(
        _cast_kernel, grid=(d, r // tr), in_specs=[spec], out_specs=spec,
        out_shape=jax.ShapeDtypeStruct(w.shape, BF16),
        compiler_params=_cparams(("parallel", "parallel")),
    )(w)


def _stream_steps(rows, tm):
    bounds, start = [], 0
    for r in rows:
        bounds.append((start, r // tm))
        start += r // tm
    return bounds, start


def _stream_spec(tm, width, start, steps, single=False):
    kw = dict(pipeline_mode=pl.Buffered(1)) if single else {}
    return pl.BlockSpec((tm, width), lambda i: (jnp.clip(i - start, 0, steps - 1), 0), **kw)


def _when_stream(start, steps):
    i = pl.program_id(0)
    return pl.when(jnp.logical_and(i >= start, i < start + steps))


def _inproj_kernel(*refs, bounds):
    n = len(bounds)
    x_refs, (nw_ref, w_ref), o_refs, wb_ref = refs[:n], refs[n:n + 2], refs[n + 2:2 * n + 2], refs[2 * n + 2]

    @pl.when(pl.program_id(0) == 0)
    def _():
        rows = 256
        for r in range(0, COL_S5, rows):
            wb_ref[r:r + rows, :] = w_ref[r:r + rows, :].astype(BF16)
        wb_ref[COL_S5:COL_DT, :] = w_ref[COL_S5 + HEADS:IN_COLS, :].astype(BF16)
        for h in range(HEADS):
            row = w_ref[COL_S5 + h:COL_S5 + h + 1, :]
            wb_ref[COL_DT + h * HEAD_W:COL_DT + (h + 1) * HEAD_W, :] = jnp.broadcast_to(
                row, (HEAD_W, D_MODEL)).astype(BF16)

    for x_ref, o_ref, (start, steps) in zip(x_refs, o_refs, bounds):
        @_when_stream(start, steps)
        def _():
            h = _rms(x_ref[...], nw_ref[...])
            o_ref[...] = _mm_nt(h, wb_ref[...])


def _inproj(xs, norm_w, w_in_t, l, tm):
    bounds, n_steps = _stream_steps([x.shape[0] for x in xs], tm)
    longest = max(steps for _, steps in bounds)
    return pl.pallas_call(
        functools.partial(_inproj_kernel, bounds=bounds),
        grid=(n_steps,),
        in_specs=[_stream_spec(tm, D_MODEL, *b, single=b[1] < longest) for b in bounds]
        + [_layer_spec(l, (1, D_MODEL), 1),
           _layer_spec(l, (IN_COLS, D_MODEL), 1, single=True)],
        out_specs=[_stream_spec(tm, IN_COLS_PAD, *b, single=b[1] < longest) for b in bounds],
        out_shape=[jax.ShapeDtypeStruct((x.shape[0], IN_COLS_PAD), F32) for x in xs],
        scratch_shapes=[pltpu.VMEM((IN_COLS_PAD, D_MODEL), BF16)],
        compiler_params=_cparams(("arbitrary",)),
    )(*xs, norm_w, w_in_t)


def _outmlp_kernel(*refs, bounds, final_norm, tn_up, tn_down):
    n = len(bounds)
    row_refs = refs[:5 * n]
    wo_ref, nw_ref, wu_ref, wd_ref, nf_ref = refs[5 * n:5 * n + 5]
    o_refs = refs[5 * n + 5:6 * n + 5]
    h_ref, u_ref = refs[6 * n + 5:]

    for k, (start, steps) in enumerate(bounds):
        x_ref, y_refs, o_ref = row_refs[5 * k], row_refs[5 * k + 1:5 * k + 5], o_refs[k]

        @_when_stream(start, steps)
        def _():
            x1 = x_ref[...]
            for g, y_ref in enumerate(y_refs):
                x1 = x1 + _mm(y_ref[...], wo_ref[g * GROUP_W:(g + 1) * GROUP_W, :])
            o_ref[...] = x1
            h_ref[...] = _rms(x1, nw_ref[...]).astype(BF16)
            for c0 in range(0, D_FF, tn_up):
                u = jnp.dot(h_ref[...], wu_ref[:, c0:c0 + tn_up], preferred_element_type=F32)
                u_ref[:, c0:c0 + tn_up] = jnp.square(jnp.maximum(u, 0.0)).astype(BF16)
            for c0 in range(0, D_MODEL, tn_down):
                o_ref[:, c0:c0 + tn_down] += jnp.dot(u_ref[...], wd_ref[:, c0:c0 + tn_down],
                                                     preferred_element_type=F32)
            if final_norm:
                o_ref[...] = _rms(o_ref[...], nf_ref[...])


def _outmlp(xs, ys, w_out, norm_w, w_up, w_down, norm_f, l, tm):
    bounds, n_steps = _stream_steps([x.shape[0] for x in xs], tm)
    row_specs, row_ops = [], []
    for x, y4, b in zip(xs, ys, bounds):
        row_specs += [_stream_spec(tm, D_MODEL, *b)] + [_stream_spec(tm, GROUP_W, *b)] * 4
        row_ops += [x, *y4]
    return pl.pallas_call(
        functools.partial(_outmlp_kernel, bounds=bounds, final_norm=(l == DEPTH - 1), tn_up=512, tn_down=256),
        grid=(n_steps,),
        in_specs=row_specs
        + [_layer_spec(l, (D_MODEL, D_MODEL), 1, single=True),
           _layer_spec(l, (1, D_MODEL), 1),
           _layer_spec(l, (D_MODEL, D_FF), 1, single=True),
           _layer_spec(l, (D_FF, D_MODEL), 1, single=True),
           pl.BlockSpec((1, D_MODEL), lambda i: (0, 0))],
        out_specs=[_stream_spec(tm, D_MODEL, *b) for b in bounds],
        out_shape=[jax.ShapeDtypeStruct((x.shape[0], D_MODEL), F32) for x in xs],
        scratch_shapes=[pltpu.VMEM((tm, D_MODEL), BF16),
                        pltpu.VMEM((tm, D_FF), BF16)],
        compiler_params=_cparams(("arbitrary",)),
    )(*row_ops, w_out, norm_w, w_up, w_down, norm_f)


def _rope_kernel(angle_ref, cos_ref, sin_ref, *, pos0, tl, period):
    i = pl.program_id(0)
    row = i * tl + lax.broadcasted_iota(jnp.int32, (tl, 2 * HEAD_W), 0)
    pos = (pos0 + row % period).astype(F32)
    theta = pos * angle_ref[...]
    lane = lax.broadcasted_iota(jnp.int32, (tl, 2 * HEAD_W), 1)
    cos_ref[...] = jnp.cos(theta)
    s = jnp.sin(theta)
    sin_ref[...] = jnp.where(lane % 2 == 0, -s, s)


def _rope_tables(angle_row, pos0, rows, period, tl):
    seq = rows
    return pl.pallas_call(
        functools.partial(_rope_kernel, pos0=pos0, tl=tl, period=period),
        grid=(seq // tl,),
        in_specs=[pl.BlockSpec((1, 2 * HEAD_W), lambda i: (0, 0))],
        out_specs=[pl.BlockSpec((tl, 2 * HEAD_W), lambda i: (i, 0))] * 2,
        out_shape=[jax.ShapeDtypeStruct((seq, 2 * HEAD_W), F32)] * 2,
        compiler_params=_cparams(("parallel",)),
    )(angle_row)


def _ret_kernel(*refs, c_seq, nseq, nblk, has_init, has_prev):
    lg_ref, p_ref, cos_ref, sin_ref = refs[:4]
    s0_ref = refs[4] if has_init else None
    y_ref, s_ref, sb_ref = refs[4 + has_init + has_prev:]
    nb = s_ref.shape[0]
    c = nseq * c_seq
    j = pl.program_id(1)

    @pl.when(j == 0)
    def _():
        sb_ref[...] = jnp.zeros_like(sb_ref)
        if has_init:
            for b in range(nb):
                for h in range(HEADS):
                    hs = slice(h * HEAD_W, (h + 1) * HEAD_W)
                    sb_ref[b, hs, hs] = s0_ref[b, h]

    def per_head(idx, shape):
        out = jnp.full(shape, lg_ref[HEADS - 1], F32)
        for h in range(HEADS - 2, -1, -1):
            out = jnp.where(idx == h, lg_ref[h], out)
        return out

    head = _head_of_lane(c)
    bd = _block_mask(HEAD_W)
    m_seg = jnp.where(bd, 1.0 / HEAD_W, 0.0).astype(BF16)
    even = lax.broadcasted_iota(jnp.int32, (c, GROUP_W), 1) % 2 == 0
    lg_lane = per_head(head[0:1, :], (1, GROUP_W))
    t_loc = (lax.broadcasted_iota(jnp.int32, (c, 1), 0) % c_seq).astype(F32)
    e_q = jnp.exp((t_loc + 1.0) * lg_lane)
    e_k = jnp.exp((c_seq - 1.0 - t_loc) * lg_lane)
    a_c = jnp.exp(c_seq * lg_lane)
    r_i, r2_i, h_i, causal = _seq_grid(c, c_seq, HEADS)
    dist = jnp.where(causal, (r_i - r2_i).astype(F32), 0.0)
    decay = jnp.where(causal, jnp.exp(dist * per_head(h_i, (c, HEADS * c))), 0.0)

    def rot(x, cos, sin):
        swapped = jnp.where(even, pltpu.roll(x, GROUP_W - 1, 1), pltpu.roll(x, 1, 1))
        return x * cos + swapped * sin

    for bs, ts, seqs in _row_blocks(nseq, c_seq, nblk):
        load = lambda lo: p_ref[bs, ts, lo:lo + GROUP_W].reshape(c, GROUP_W)
        crow = ts if nseq == 1 else slice(0, c)
        cos = jnp.concatenate([cos_ref[crow, :]] * 2, axis=1)
        sin = jnp.concatenate([sin_ref[crow, :]] * 2, axis=1)
        q = rot(load(0), cos, sin)
        k = rot(load(GROUP_W), cos, sin) * (HEAD_W ** -0.5)
        v = load(2 * GROUP_W)
        g = load(3 * GROUP_W)
        scores = _mm_nt(q, _expand_heads(k, head)) * decay
        o = _mm(scores, _expand_heads(v, head))
        qe = q * e_q
        ke = k * e_k
        cross = []
        for i, b in enumerate(seqs):
            sl = slice(i * c_seq, (i + 1) * c_seq)
            s_prev = sb_ref[b]
            cross.append(_mm_nt(qe[sl], s_prev))
            sb_ref[b] = a_c * s_prev + jnp.where(bd, _mm_tn(v[sl], ke[sl]), 0.0)
        o = o + _stack_rows(cross)
        mu = _seg_mean(o, m_seg)
        d = o - mu
        var = _seg_mean(d * d, m_seg)
        y_ref[bs, ts, :] = (d * lax.rsqrt(var + EPS) * _silu(g)).reshape(len(seqs), c_seq, GROUP_W)

    @pl.when(j == pl.num_programs(1) - 1)
    def _():
        for b in range(nb):
            for h in range(HEADS):
                hs = slice(h * HEAD_W, (h + 1) * HEAD_W)
                s_ref[b, h] = sb_ref[b, hs, hs]


def _retention(proj, cos, sin, log_gamma, init, prev, l, c_seq, nseq, nblk):
    bsz, seq, _ = proj.shape
    nb, tb, grid = _mixer_grid(bsz, seq, c_seq, nseq, nblk)
    st_in, st_ops, st_out, st_shape, alias = _state_io(l, nb, (HEADS, HEAD_W, HEAD_W), bsz, init, prev)
    n_fixed = 4
    rope_spec = (pl.BlockSpec((tb, 2 * HEAD_W), lambda i, j: (j, 0)) if nseq == 1 else
                 pl.BlockSpec((nseq * c_seq, 2 * HEAD_W), lambda i, j: (0, 0)))
    return pl.pallas_call(
        functools.partial(_ret_kernel, c_seq=c_seq, nseq=nseq, nblk=nblk, has_init=init is not None,
                          has_prev=prev is not None),
        grid=grid,
        in_specs=[pl.BlockSpec(memory_space=pltpu.SMEM),
                  pl.BlockSpec((nb, tb, 4 * GROUP_W), lambda i, j: (i, j, COL_RET // (4 * GROUP_W))),
                  rope_spec, rope_spec] + st_in,
        out_specs=[pl.BlockSpec((nb, tb, GROUP_W), lambda i, j: (i, j, 0)), st_out],
        out_shape=[jax.ShapeDtypeStruct((bsz, seq, GROUP_W), F32), st_shape],
        scratch_shapes=[pltpu.VMEM((nb, GROUP_W, GROUP_W), F32)],
        input_output_aliases={} if alias is None else {n_fixed + alias: 1},
        compiler_params=_cparams(("parallel", "arbitrary")),
    )(log_gamma, proj, cos, sin, *st_ops)


def _hg_kernel(*refs, c_seq, nseq, nblk, layer, has_init, has_prev):
    p_ref, lb_ref = refs[:2]
    s0_ref = refs[2] if has_init else None
    y_ref, s_ref, st_ref, keep_ref = refs[2 + has_init + has_prev:]
    nb = s_ref.shape[0]
    c = nseq * c_seq
    j = pl.program_id(1)

    pairs = [slice(pr * LANES, (pr + 1) * LANES) for pr in range(HEADS // PAIR_HEADS)]

    def pair_block(h):
        pr, jh = divmod(h, PAIR_HEADS)
        return pr, slice(jh * HEAD_W, (jh + 1) * HEAD_W)

    def load_state(b):
        st_ref[b] = jnp.zeros(st_ref.shape[1:], F32)
        if has_init:
            for h in range(HEADS):
                pr, hs = pair_block(h)
                st_ref[b, pr, hs, hs] = s0_ref[b, h]

    @pl.when(j == 0)
    def _():
        for b in range(nb):
            load_state(b)

    logits = lb_ref[...]
    e = jnp.exp(logits - jnp.max(logits, axis=0, keepdims=True))
    sm = e / jnp.sum(e, axis=0, keepdims=True)
    csum = sm[0:1, :]
    for i in range(1, layer + 1):
        csum = csum + sm[i:i + 1, :]
    lower = csum - sm[0:1, :]

    head = _head_of_lane(c)
    m_seg = jnp.where(_block_mask(HEAD_W), 1.0 / HEAD_W, 0.0).astype(BF16)
    bd = _block_mask(HEAD_W, LANES)
    r_i, r2_i, _, tril = _seq_grid(c, c_seq, 1)
    trif = tril.astype(F32)
    later = r2_i > r_i
    if nseq > 1:
        later = jnp.logical_and(r_i // c_seq == r2_i // c_seq, later)
    laterf = later.astype(F32)
    causal = _seq_grid(c, c_seq, HEADS)[3]
    mid = c // 2 - 1
    mask_sum = _mm_mask_f32 if c >= LANES else _mm_f32

    staged = []
    spreads = []
    for bs, ts, seqs in _row_blocks(nseq, c_seq, nblk):
        load = lambda lo: p_ref[bs, ts, lo:lo + GROUP_W].reshape(c, GROUP_W)
        qq = _silu(load(0))
        ff = load(GROUP_W)
        v = load(2 * GROUP_W)
        sig = jax.nn.sigmoid(ff)
        forget = lower + (1.0 - lower) * sig
        log_f = jnp.log(jnp.maximum(forget, HG_MIN_FORGET))
        key = (1.0 - lower) * (1.0 - sig)
        bcum = mask_sum(trif, log_f)
        if nseq == 1:
            b_ref = bcum[mid:mid + 1, :]
            spread = bcum - b_ref
            q_in = qq * jnp.exp(spread)
            k_in = key * jnp.exp(-spread)
            q_x = q_in * jnp.exp(b_ref)
            k_x = k_in * jnp.exp(bcum[c - 1:c, :] - b_ref)
        else:
            spread = bcum
            q_in = q_x = qq * jnp.exp(bcum)
            k_in = key * jnp.exp(-bcum)
            k_x = key * jnp.exp(mask_sum(laterf, log_f))
        spreads.append(jnp.max(jnp.abs(spread)))
        scores = jnp.where(causal, _mm_nt(q_in, _expand_heads(k_in, head)), 0.0)
        o = _mm(scores, _expand_heads(v, head))
        upd = []
        for i in range(len(seqs)):
            sl = slice(i * c_seq, (i + 1) * c_seq)
            a_seq = jnp.exp(bcum[(i + 1) * c_seq - 1:(i + 1) * c_seq, :])
            ds = [jnp.where(bd, _mm_tn(v[sl, pl_], k_x[sl, pl_]), 0.0) for pl_ in pairs]
            upd.append((q_x[sl], a_seq, ds))
        staged.append((bs, ts, seqs, o, upd))

    unsafe = functools.reduce(jnp.maximum, spreads) > HG_SAFE_SPREAD
    if nb == 1:
        keep_ref[...] = st_ref[0]

    for bs, ts, seqs, o, upd in staged:
        cross = []
        for b, (q_b, a_seq, ds) in zip(seqs, upd):
            parts = []
            for pr, pl_ in enumerate(pairs):
                st_prev = st_ref[b, pr]
                parts.append(_mm_nt(q_b[:, pl_], st_prev))
                st_ref[b, pr] = st_prev * a_seq[:, pl_] + ds[pr]
            cross.append(jnp.concatenate(parts, axis=1))
        o = o + _stack_rows(cross)
        gate = _silu(p_ref[bs, ts, 3 * GROUP_W:4 * GROUP_W].reshape(c, GROUP_W))
        y = o * lax.rsqrt(_seg_mean(o * o, m_seg) + EPS) * gate
        y_ref[bs, ts, :] = y.reshape(bs.stop - bs.start, ts.stop - ts.start, GROUP_W)

    @pl.when(unsafe)
    def _():
        tb = p_ref.shape[1]
        assert nb == 1 or (tb == SUBLANES and pl.num_programs(1) == 1)
        rowid = lax.broadcasted_iota(jnp.int32, (SUBLANES, GROUP_W), 0)
        if nb == 1:
            st_ref[0] = keep_ref[...]

        def slab(s, carry):
            b, t0 = (0, pl.multiple_of(s * SUBLANES, SUBLANES)) if nb == 1 else (s, 0)
            rows = pl.ds(t0, SUBLANES)
            if nb > 1:
                load_state(b)
            q8 = _silu(p_ref[b, rows, 0:GROUP_W])
            sig = jax.nn.sigmoid(p_ref[b, rows, GROUP_W:2 * GROUP_W])
            v8 = p_ref[b, rows, 2 * GROUP_W:3 * GROUP_W]
            g8 = p_ref[b, rows, 3 * GROUP_W:4 * GROUP_W]
            f8 = jnp.maximum(lower + (1.0 - lower) * sig, HG_MIN_FORGET)
            k8 = (1.0 - lower) * (1.0 - sig)
            o8 = jnp.zeros((SUBLANES, GROUP_W), F32)
            for i in range(SUBLANES):
                v_i = jnp.where(rowid == i, v8, 0.0)
                parts = []
                for pr, pl_ in enumerate(pairs):
                    outer = _mm_tn(v_i[:, pl_], k8[:, pl_])
                    st = st_ref[b, pr] * f8[i:i + 1, pl_] + jnp.where(bd, outer, 0.0)
                    st_ref[b, pr] = st
                    parts.append(_mm_nt(q8[:, pl_], st))
                o8 = jnp.where(rowid == i, jnp.concatenate(parts, axis=1), o8)
            y_ref[b, rows, :] = o8 * lax.rsqrt(_seg_mean(o8 * o8, m_seg) + EPS) * _silu(g8)
            return carry

        lax.fori_loop(0, nb * tb // SUBLANES, slab, 0)

    @pl.when(j == pl.num_programs(1) - 1)
    def _():
        for b in range(nb):
            for h in range(HEADS):
                pr, hs = pair_block(h)
                s_ref[b, h] = st_ref[b, pr, hs, hs]


def _hgrn2(proj, lb_logits, init, prev, l, c_seq, nseq, nblk):
    bsz, seq, _ = proj.shape
    nb, tb, grid = _mixer_grid(bsz, seq, c_seq, nseq, nblk)
    st_in, st_ops, st_out, st_shape, alias = _state_io(l, nb, (HEADS, HEAD_W, HEAD_W), bsz, init, prev)
    n_fixed = 2
    return pl.pallas_call(
        functools.partial(_hg_kernel, c_seq=c_seq, nseq=nseq, nblk=nblk, layer=l, has_init=init is not None,
                          has_prev=prev is not None),
        grid=grid,
        in_specs=[pl.BlockSpec((nb, tb, 4 * GROUP_W), lambda i, j: (i, j, COL_HG // (4 * GROUP_W))),
                  pl.BlockSpec((DEPTH, GROUP_W), lambda i, j: (0, 0))] + st_in,
        out_specs=[pl.BlockSpec((nb, tb, GROUP_W), lambda i, j: (i, j, 0)), st_out],
        out_shape=[jax.ShapeDtypeStruct((bsz, seq, GROUP_W), F32), st_shape],
        scratch_shapes=[pltpu.VMEM((nb, HEADS // PAIR_HEADS, LANES, LANES), F32),
                        pltpu.VMEM((HEADS // PAIR_HEADS, LANES, LANES), F32)],
        input_output_aliases={} if alias is None else {n_fixed + alias: 1},
        compiler_params=_cparams(("parallel", "arbitrary")),
    )(proj, lb_logits, *st_ops)


def _ssd_kernel(*refs, c_seq, nseq, nblk, has_init, has_prev):
    p_ref, dt_ref, cw_ref, cb_ref, dtb_ref, alog_ref, dsk_ref, nw_ref = refs[:8]
    n_in = 8
    cv0_ref = s0_ref = None
    if has_init:
        cv0_ref, s0_ref = refs[n_in], refs[n_in + 1 + has_prev]
    y_ref, s_ref, cv_ref, xpad_ref = refs[n_in + 2 * (has_init + has_prev):]
    nb = s_ref.shape[0]
    tb = p_ref.shape[1]
    c = nseq * c_seq
    n_tail = SSD_CONV - 1
    lo = CONV_PAD - n_tail

    @pl.when(pl.program_id(1) == 0)
    def _():
        xpad_ref[:, 0:CONV_PAD, :] = jnp.zeros((nb, CONV_PAD, SSD_CONV_DIM), F32)
        if has_init:
            s_ref[...] = s0_ref[...]
            xpad_ref[:, lo:CONV_PAD, :] = cv0_ref[...]
        else:
            s_ref[...] = jnp.zeros_like(s_ref)

    r_i, r2_i, _, tri = _seq_grid(c, c_seq, 1)
    trif = tri.astype(F32)
    later = r2_i > r_i
    if nseq > 1:
        later = jnp.logical_and(r_i // c_seq == r2_i // c_seq, later)
    laterf = later.astype(F32)
    head = _head_of_lane(c)
    m_grp = jnp.where(_block_mask(LANES), 1.0 / LANES, 0.0).astype(BF16)
    cw = cw_ref[...]
    groups = [slice(g * LANES, (g + 1) * LANES) for g in range(SSD_GROUPS)]

    xpad_ref[:, CONV_PAD:CONV_PAD + tb, :] = p_ref[:, :, GROUP_W:GROUP_W + SSD_CONV_DIM]
    for bs, ts, seqs in _row_blocks(nseq, c_seq, nblk):
        r0 = ts.start
        load = lambda ref, col, w: ref[bs, ts, col:col + w].reshape(c, w)
        z = load(p_ref, 0, GROUP_W)
        conv = cb_ref[...] + cw[n_tail:n_tail + 1, :] * load(p_ref, GROUP_W, SSD_CONV_DIM)
        for w in range(n_tail):
            win = xpad_ref[bs, r0 + lo + w:r0 + lo + w + c_seq, :]
            conv = conv + cw[w:w + 1, :] * win.reshape(c, SSD_CONV_DIM)
        xc = _silu(conv)
        xs = xc[:, 0:GROUP_W]
        bm = xc[:, GROUP_W:2 * GROUP_W]
        cm = xc[:, 2 * GROUP_W:3 * GROUP_W]

        delta = jax.nn.softplus(load(dt_ref, 0, GROUP_W) + dtb_ref[...])
        log_a = -delta * jnp.exp(alog_ref[...])
        bcum = _mm_mask_f32(trif, log_a)
        brem = bcum[c - 1:c, :] - bcum if nseq == 1 else _mm_mask_f32(laterf, log_a)
        xd = xs * delta
        xr = xd * jnp.exp(brem)
        brow = [bcum[:, gl].T for gl in groups]
        gram = [_mm_nt(cm[:, gl], bm[:, gl]) for gl in groups]
        w_parts = []
        for h in range(HEADS):
            g, jh = divmod(h, HEADS // SSD_GROUPS)
            b_t = bcum[:, h * HEAD_W:h * HEAD_W + 1]
            b_s = brow[g][jh * HEAD_W:jh * HEAD_W + 1, :]
            w_parts.append(jnp.where(tri, gram[g] * jnp.exp(b_t - b_s), 0.0))
        o = _mm(jnp.concatenate(w_parts, axis=1), _expand_heads(xd, head))
        cross = []
        for i, b in enumerate(seqs):
            sl = slice(i * c_seq, (i + 1) * c_seq)
            last = (i + 1) * c_seq - 1
            parts = []
            for g, gl in enumerate(groups):
                s_prev = s_ref[b, g]
                parts.append(_mm_nt(cm[sl, gl], s_prev))
                a_col = jnp.exp(brow[g][:, last:last + 1])
                s_ref[b, g] = s_prev * a_col + _mm_tn(xr[sl, gl], bm[sl, gl])
            cross.append(jnp.concatenate(parts, axis=1))
        o = o + jnp.exp(bcum) * _stack_rows(cross)
        y = (o + dsk_ref[...] * xs) * _silu(z)
        y = y * lax.rsqrt(_seg_mean(y * y, m_grp) + EPS) * nw_ref[...]
        y_ref[bs, ts, :] = y.reshape(len(seqs), c_seq, GROUP_W)
    tail = xpad_ref[:, tb + lo:tb + CONV_PAD, :]
    cv_ref[...] = tail
    xpad_ref[:, lo:CONV_PAD, :] = tail


def _ssd(proj, prm, init_cv, init_s, prev_cv, prev_s, l, c_seq, nseq, nblk):
    bsz, seq, _ = proj.shape
    nb, tb, grid = _mixer_grid(bsz, seq, c_seq, nseq, nblk)
    cv_in, cv_ops, cv_out, cv_shape, cv_alias = _state_io(l, nb, (SSD_CONV - 1, SSD_CONV_DIM), bsz,
                                                          init_cv, prev_cv)
    s_in, s_ops, s_out, s_shape, s_alias = _state_io(l, nb, (SSD_GROUPS, LANES, SSD_STATE), bsz,
                                                     init_s, prev_s)
    n_fixed = 8
    aliases = {}
    if cv_alias is not None:
        aliases[n_fixed + cv_alias] = 2
        aliases[n_fixed + len(cv_in) + s_alias] = 1
    return pl.pallas_call(
        functools.partial(_ssd_kernel, c_seq=c_seq, nseq=nseq, nblk=nblk, has_init=init_s is not None,
                          has_prev=prev_s is not None),
        grid=grid,
        in_specs=[pl.BlockSpec((nb, tb, 4 * GROUP_W), lambda i, j: (i, j, COL_SSD // (4 * GROUP_W))),
                  pl.BlockSpec((nb, tb, GROUP_W), lambda i, j: (i, j, COL_DT // GROUP_W)),
                  _layer_spec(l, (SSD_CONV, SSD_CONV_DIM), 2),
                  _layer_spec(l, (1, SSD_CONV_DIM), 2),
                  _layer_spec(l, (1, GROUP_W), 2),
                  _layer_spec(l, (1, GROUP_W), 2),
                  _layer_spec(l, (1, GROUP_W), 2),
                  _layer_spec(l, (1, GROUP_W), 2)] + cv_in + s_in,
        out_specs=[pl.BlockSpec((nb, tb, GROUP_W), lambda i, j: (i, j, 0)), s_out, cv_out],
        out_shape=[jax.ShapeDtypeStruct((bsz, seq, GROUP_W), F32), s_shape, cv_shape],
        scratch_shapes=[pltpu.VMEM((nb, tb + CONV_PAD, SSD_CONV_DIM), F32)],
        input_output_aliases=aliases,
        compiler_params=_cparams(("parallel", "arbitrary")),
    )(proj, proj, prm["conv_w"], prm["conv_b"], prm["dt_bias"], prm["a_log"], prm["ssd_d"], prm["ssd_norm"],
      *cv_ops, *s_ops)


def _ssd_state_in(s):
    d, bsz = s.shape[:2]
    return jnp.swapaxes(s, 3, 4).reshape(d, bsz, SSD_GROUPS, LANES, SSD_STATE)


def _ssd_state_out(s):
    d, bsz = s.shape[:2]
    return jnp.swapaxes(s.reshape(d, bsz, HEADS, HEAD_W, SSD_STATE), 3, 4)


def _s5_kernel(*refs, tl, ng, has_init, has_prev):
    (u0_ref, u1_ref, are_ref, aim_ref, ldt_ref, bre_ref, bim_ref, cw_ref, dsk_ref, gw_ref,
     gb_ref) = refs[:11]
    n_in = 11
    s0r_ref = s0i_ref = None
    if has_init:
        s0r_ref, s0i_ref = refs[n_in], refs[n_in + 1 + has_prev]
    (y_ref, sr_ref, si_ref, wb_ref, ab_ref, us0_ref, us1_ref, ut_ref, bu_ref, yt_ref, ys0_ref,
     ys1_ref) = refs[n_in + 2 * (has_init + has_prev):]
    nseq = SUBLANES * ng
    nbt = nseq * tl

    @pl.when(pl.program_id(1) == 0)
    def _():
        if has_init:
            sr_ref[...] = s0r_ref[...]
            si_ref[...] = s0i_ref[...]
        else:
            sr_ref[...] = jnp.zeros_like(sr_ref)
            si_ref[...] = jnp.zeros_like(si_ref)
        lr, li, st = are_ref[...], aim_ref[...], jnp.exp(ldt_ref[...])
        mag = jnp.exp(lr * st)
        abr = mag * jnp.cos(li * st)
        abi = mag * jnp.sin(li * st)
        ab_ref[0:1, :] = abr
        ab_ref[1:2, :] = abi
        den = lr * lr + li * li
        cr = ((abr - 1.0) * lr + abi * li) / den
        ci = (abi * lr - (abr - 1.0) * li) / den
        wb_ref[:, 0:S5_N] = (cr * bre_ref[...] - ci * bim_ref[...]).astype(BF16)
        wb_ref[:, S5_N:2 * S5_N] = (cr * bim_ref[...] + ci * bre_ref[...]).astype(BF16)

    us0_ref[...] = u0_ref[...].reshape(nbt, LANES)
    us1_ref[...] = u1_ref[...].reshape(nbt, LANES)

    nparts = bu_ref.shape[0]
    prow = nbt // nparts
    psteps = prow // SUBLANES
    assert ng == 1 or (nparts == ng and psteps == tl)

    def slabs(h):
        if ng == 1:
            return [(t, t) for t in range(h * psteps, (h + 1) * psteps)]
        return [(h * tl + t, h * SUBLANES * tl + t) for t in range(tl)]

    for h in range(nparts):
        for slab, row0 in slabs(h):
            dst = slice(slab * SUBLANES, (slab + 1) * SUBLANES)
            src = pl.ds(row0, SUBLANES, stride=tl)
            ut_ref[dst, 0:LANES] = us0_ref[src, :]
            ut_ref[dst, LANES:2 * LANES] = us1_ref[src, :]
        u_h = ut_ref[h * prow:(h + 1) * prow, :]
        bu_ref[h] = jnp.dot(u_h.astype(BF16), wb_ref[...], preferred_element_type=F32)
    ar = jnp.broadcast_to(ab_ref[0:1, :], (SUBLANES, S5_N))
    ai = jnp.broadcast_to(ab_ref[1:2, :], (SUBLANES, S5_N))
    if ng == 1:
        xr, xi = sr_ref[...], si_ref[...]
    for h in range(nparts):
        grp = slice(h * SUBLANES, (h + 1) * SUBLANES)
        if ng > 1:
            xr, xi = sr_ref[grp, :], si_ref[grp, :]
        for t in range(psteps):
            rows = slice(t * SUBLANES, (t + 1) * SUBLANES)
            nr = ar * xr - ai * xi + bu_ref[h, rows, 0:S5_N]
            ni = ar * xi + ai * xr + bu_ref[h, rows, S5_N:2 * S5_N]
            bu_ref[h, rows, 0:S5_N] = nr
            bu_ref[h, rows, S5_N:2 * S5_N] = ni
            xr, xi = nr, ni
        y = _mm(bu_ref[h], cw_ref[...]) + dsk_ref[...] * ut_ref[h * prow:(h + 1) * prow, :]
        y = jax.nn.gelu(y)
        yt_ref[h * prow:(h + 1) * prow, :] = y * jax.nn.sigmoid(_mm(y, gw_ref[...]) + gb_ref[...])
        if ng > 1:
            sr_ref[grp, :] = xr
            si_ref[grp, :] = xi
        for slab, row0 in slabs(h):
            src = slice(slab * SUBLANES, (slab + 1) * SUBLANES)
            dst = pl.ds(row0, SUBLANES, stride=tl)
            ys0_ref[dst, :] = yt_ref[src, 0:LANES]
            ys1_ref[dst, :] = yt_ref[src, LANES:2 * LANES]
    if ng == 1:
        sr_ref[...] = xr
        si_ref[...] = xi
    y_ref[:, :, 0:LANES] = ys0_ref[...].reshape(nseq, tl, LANES)
    y_ref[:, :, LANES:2 * LANES] = ys1_ref[...].reshape(nseq, tl, LANES)


def _s5(proj, prm, init_re, init_im, prev_re, prev_im, l, tl, nparts, ng):
    bsz, seq, _ = proj.shape
    nseq = SUBLANES * ng
    re_in, re_ops, re_out, re_shape, re_alias = _state_io(l, nseq, (S5_N,), bsz, init_re, prev_re)
    im_in, im_ops, im_out, im_shape, im_alias = _state_io(l, nseq, (S5_N,), bsz, init_im, prev_im)
    n_fixed = 11
    aliases = {}
    if re_alias is not None:
        aliases[n_fixed + re_alias] = 1
        aliases[n_fixed + len(re_in) + im_alias] = 2
    nbt = nseq * tl
    return pl.pallas_call(
        functools.partial(_s5_kernel, tl=tl, ng=ng, has_init=init_re is not None,
                          has_prev=prev_re is not None),
        grid=(bsz // nseq, seq // tl),
        in_specs=[pl.BlockSpec((nseq, tl, LANES), lambda i, j: (i, j, COL_S5 // LANES)),
                  pl.BlockSpec((nseq, tl, LANES), lambda i, j: (i, j, COL_S5 // LANES + 1)),
                  _layer_spec(l, (1, S5_N), 2), _layer_spec(l, (1, S5_N), 2), _layer_spec(l, (1, S5_N), 2),
                  _layer_spec(l, (GROUP_W, S5_N), 2), _layer_spec(l, (GROUP_W, S5_N), 2),
                  _layer_spec(l, (2 * S5_N, GROUP_W), 2),
                  _layer_spec(l, (1, GROUP_W), 2),
                  _layer_spec(l, (GROUP_W, GROUP_W), 2),
                  _layer_spec(l, (1, GROUP_W), 2)] + re_in + im_in,
        out_specs=[pl.BlockSpec((nseq, tl, GROUP_W), lambda i, j: (i, j, 0)), re_out, im_out],
        out_shape=[jax.ShapeDtypeStruct((bsz, seq, GROUP_W), F32), re_shape, im_shape],
        scratch_shapes=[pltpu.VMEM((GROUP_W, 2 * S5_N), BF16),
                        pltpu.VMEM((SUBLANES, S5_N), F32),
                        pltpu.VMEM((nbt, LANES), F32), pltpu.VMEM((nbt, LANES), F32),
                        pltpu.VMEM((nbt, GROUP_W), F32),
                        pltpu.VMEM((nparts, nbt // nparts, 2 * S5_N), F32),
                        pltpu.VMEM((nbt, GROUP_W), F32),
                        pltpu.VMEM((nbt, LANES), F32), pltpu.VMEM((nbt, LANES), F32)],
        input_output_aliases=aliases,
        compiler_params=_cparams(("parallel", "arbitrary")),
    )(proj, proj, prm["a_re"], prm["a_im"], prm["log_dt"], prm["b_re"], prm["b_im"], prm["c_blk"],
      prm["s5_d"], prm["glu_w"], prm["glu_b"], *re_ops, *im_ops)


def _block_diag(blocks):
    d, g, r, c = blocks.shape
    eye = jnp.eye(g, dtype=blocks.dtype)
    return (eye[None, :, None, :, None] * blocks[:, :, :, None, :]).reshape(d, g * r, g * c)


def _prep(ssd_conv_w, ssd_conv_b, ssd_dt_bias, ssd_a_log, ssd_d, ssd_norm, s5_a_re, s5_a_im, s5_log_dt,
          s5_b_re, s5_b_im, s5_c_re, s5_c_im, s5_d, s5_glu_w, s5_glu_b):
    per_head = lambda v: jnp.repeat(v, HEAD_W, axis=1)[:, None, :]
    c_blk = jnp.concatenate([_block_diag(jnp.swapaxes(s5_c_re, 2, 3)),
                             -_block_diag(jnp.swapaxes(s5_c_im, 2, 3))], axis=1)
    return dict(
        conv_w=ssd_conv_w, conv_b=ssd_conv_b[:, None, :],
        dt_bias=per_head(ssd_dt_bias), a_log=per_head(ssd_a_log), ssd_d=per_head(ssd_d),
        ssd_norm=ssd_norm[:, None, :],
        a_re=s5_a_re.reshape(DEPTH, 1, S5_N), a_im=s5_a_im.reshape(DEPTH, 1, S5_N),
        log_dt=jnp.repeat(s5_log_dt, S5_STATE, axis=1)[:, None, :],
        b_re=_block_diag(jnp.swapaxes(s5_b_re, 2, 3)), b_im=_block_diag(jnp.swapaxes(s5_b_im, 2, 3)),
        c_blk=c_blk.astype(BF16),
        s5_d=s5_d[:, None, :], glu_w=s5_glu_w.astype(BF16), glu_b=s5_glu_b[:, None, :],
    )


class _Stream:
    def __init__(self, x, pos0, init, cfg, angle_row):
        self.bsz, self.seq, _ = x.shape
        self.cfg = cfg
        self.x2d = x.reshape(self.bsz * self.seq, D_MODEL)
        c_ret, nseq_ret, _ = cfg["ret"]
        rope_rows = self.seq if nseq_ret == 1 else nseq_ret * c_ret
        self.cos, self.sin = _rope_tables(angle_row, pos0, rope_rows, self.seq, cfg["rope_tl"])
        self.init = (None,) * 6
        if init is not None:
            i_ret, i_hg, i_ssd, i_cv, i_re, i_im = init
            self.init = (jnp.swapaxes(i_ret, 3, 4), jnp.swapaxes(i_hg, 3, 4), _ssd_state_in(i_ssd), i_cv,
                         i_re.reshape(DEPTH, self.bsz, S5_N), i_im.reshape(DEPTH, self.bsz, S5_N))
        self.new = (None,) * 6

    def mix(self, proj2d, l, prm, lb_logits, log_gamma):
        proj = proj2d.reshape(self.bsz, self.seq, IN_COLS_PAD)
        i_ret, i_hg, i_ssd, i_cv, i_re, i_im = self.init
        s_ret, s_hg, s_ssd, s_cv, s_re, s_im = self.new
        cfg = self.cfg
        ya, s_ret = _retention(proj, self.cos, self.sin, log_gamma, i_ret, s_ret, l, *cfg["ret"])
        yb, s_hg = _hgrn2(proj, lb_logits, i_hg, s_hg, l, *cfg["hg"])
        yc, s_ssd, s_cv = _ssd(proj, prm, i_cv, i_ssd, s_cv, s_ssd, l, *cfg["ssd"])
        yd, s_re, s_im = _s5(proj, prm, i_re, i_im, s_re, s_im, l, *cfg["s5"])
        self.new = (s_ret, s_hg, s_ssd, s_cv, s_re, s_im)
        return [y.reshape(self.bsz * self.seq, GROUP_W) for y in (ya, yb, yc, yd)]

    def outputs(self):
        s_ret, s_hg, s_ssd, s_cv, s_re, s_im = self.new
        tail = (DEPTH, self.bsz, S5_GROUPS, S5_STATE)
        return (self.x2d.reshape(self.bsz, self.seq, D_MODEL), jnp.swapaxes(s_ret, 3, 4), jnp.swapaxes(s_hg, 3, 4),
                _ssd_state_out(s_ssd), s_cv,
                s_re.reshape(tail), s_im.reshape(tail))


def _trunk(streams, prm, w_in, w_out, w_up, w_down, norm_mix, norm_mlp, norm_final, lb_logits, log_gamma,
           tm_in, tm_out):
    for l in range(DEPTH):
        projs = _inproj([s.x2d for s in streams], norm_mix, w_in, l, tm_in)
        ys = [s.mix(p, l, prm, lb_logits, log_gamma) for s, p in zip(streams, projs)]
        outs = _outmlp([s.x2d for s in streams], ys, w_out, norm_mlp, w_up, w_down, norm_final, l, tm_out)
        for s, x2d in zip(streams, outs):
            s.x2d = x2d
    return [s.outputs() for s in streams]


PROMPT_CFG = dict(rope_tl=256, ret=(256, 1, 8), hg=(64, 1, 16), ssd=(128, 1, 16), s5=(256, 4, 1))
SAMPLE_CFG = dict(rope_tl=128, ret=(8, 16, 2), hg=(8, 16, 2), ssd=(8, 16, 2), s5=(8, 16, 16))
TM_IN = 512
TM_OUT = 512


def kernel(x_prompt, x_sample, state_ret, state_hgrn, state_ssd, state_ssd_conv, state_s5_re, state_s5_im, norm_mix, w_in, w_out, hg_lb_logits, ssd_conv_w, ssd_conv_b, ssd_dt_bias, ssd_a_log, ssd_d, ssd_norm, s5_a_re, s5_a_im, s5_log_dt, s5_b_re, s5_b_im, s5_c_re, s5_c_im, s5_d, s5_glu_w, s5_glu_b, norm_mlp, w_up, w_down, norm_final):
    prm = _prep(ssd_conv_w, ssd_conv_b, ssd_dt_bias, ssd_a_log, ssd_d, ssd_norm, s5_a_re, s5_a_im, s5_log_dt,
                s5_b_re, s5_b_im, s5_c_re, s5_c_im, s5_d, s5_glu_w, s5_glu_b)
    w_out_b = _cast_bf16(w_out, 512)
    w_up_b = _cast_bf16(w_up, 256)
    w_down_b = _cast_bf16(w_down, 1024)
    angle = 1.0 / (ROPE_BASE ** jnp.linspace(0.0, 1.0, HEAD_W // 2, dtype=F32))
    angle_row = jnp.tile(jnp.repeat(angle, 2), 2)[None, :]
    log_gamma = jnp.log(1.0 - jnp.exp2(-5.0 - jnp.arange(HEADS, dtype=F32)))
    states = (state_ret, state_hgrn, state_ssd, state_ssd_conv, state_s5_re, state_s5_im)
    streams = [_Stream(x_prompt, 0, None, PROMPT_CFG, angle_row),
               _Stream(x_sample, PAST_LEN, states, SAMPLE_CFG, angle_row)]
    out_p, out_s = _trunk(streams, prm, jnp.swapaxes(w_in, 1, 2), w_out_b, w_up_b, w_down_b, norm_mix[:, None, :],
                          norm_mlp[:, None, :], norm_final[None, :], hg_lb_logits, log_gamma, TM_IN, TM_OUT)
    return (out_p[0], out_s[0]) + out_p[1:] + out_s[1:]
```

```python
import functools

import jax
import jax.numpy as jnp
from jax import lax
from jax.experimental import pallas as pl
from jax.experimental.pallas import tpu as pltpu

F32 = jnp.float32
BF16 = jnp.bfloat16

D_MODEL = 1024
DEPTH = 2
PAST_LEN = 16384
GROUP_W = 256
HEADS = 4
HEAD_W = 64
PAIR_HEADS = 2
ROPE_BASE = 10000.0
HG_MIN_FORGET = 1e-30
HG_SAFE_SPREAD = 80.0
SSD_STATE = 128
SSD_GROUPS = 2
SSD_CONV = 4
SSD_CONV_DIM = 768
CONV_PAD = 8
S5_GROUPS = 16
S5_STATE = 64
S5_N = S5_GROUPS * S5_STATE
D_FF = 4096
IN_COLS = 3332
EPS = 1e-6

COL_RET = 0
COL_HG = 1024
COL_SSD = 2048
COL_S5 = 3072
COL_DT = 3328
IN_COLS_PAD = 3584

LANES = 128
SUBLANES = 8
VMEM_LIMIT = 56 * 1024 * 1024


def _cparams(sem):
    return pltpu.CompilerParams(dimension_semantics=sem, vmem_limit_bytes=VMEM_LIMIT)


def _mm(a, b):
    return jnp.dot(a.astype(BF16), b.astype(BF16), preferred_element_type=F32)


def _mm_nt(a, b):
    return lax.dot_general(a.astype(BF16), b.astype(BF16), (((1,), (1,)), ((), ())),
                           preferred_element_type=F32)


def _mm_tn(a, b):
    return lax.dot_general(a.astype(BF16), b.astype(BF16), (((0,), (0,)), ((), ())),
                           preferred_element_type=F32)


def _mm_f32(a, b):
    return jnp.dot(a, b, preferred_element_type=F32, precision=lax.Precision.HIGHEST)


def _mm_mask_f32(mask01, x):
    m = mask01.astype(BF16)
    hi = x.astype(BF16)
    r1 = x - hi.astype(F32)
    mid = r1.astype(BF16)
    lo = (r1 - mid.astype(F32)).astype(BF16)
    return (jnp.dot(m, hi, preferred_element_type=F32) + jnp.dot(m, mid, preferred_element_type=F32)
            + jnp.dot(m, lo, preferred_element_type=F32))


def _silu(x):
    return x * jax.nn.sigmoid(x)


def _rms(x, w):
    return x * lax.rsqrt(jnp.mean(x * x, axis=-1, keepdims=True) + EPS) * w


def _head_of_lane(rows):
    return lax.broadcasted_iota(jnp.int32, (rows, GROUP_W), 1) // HEAD_W


def _expand_heads(x, head):
    nheads = x.shape[1] // HEAD_W
    return jnp.concatenate([jnp.where(head == h, x, 0.0) for h in range(nheads)], axis=0).astype(BF16)


def _block_mask(seg, width=GROUP_W):
    r = lax.broadcasted_iota(jnp.int32, (width, width), 0) // seg
    c = lax.broadcasted_iota(jnp.int32, (width, width), 1) // seg
    return r == c


def _seg_mean(x, m):
    hi = x.astype(BF16)
    lo = (x - hi.astype(F32)).astype(BF16)
    return jnp.dot(hi, m, preferred_element_type=F32) + jnp.dot(lo, m, preferred_element_type=F32)


def _row_blocks(nseq, c_seq, nblk):
    if nseq == 1:
        return [(slice(0, 1), slice(k * c_seq, (k + 1) * c_seq), [0]) for k in range(nblk)]
    return [(slice(k * nseq, (k + 1) * nseq), slice(0, c_seq), list(range(k * nseq, (k + 1) * nseq)))
            for k in range(nblk)]


def _seq_grid(rows, c_seq, copies):
    r = lax.broadcasted_iota(jnp.int32, (rows, copies * rows), 0)
    j = lax.broadcasted_iota(jnp.int32, (rows, copies * rows), 1)
    h = j // rows
    r2 = j - h * rows
    causal = r >= r2
    if rows > c_seq:
        causal = jnp.logical_and(r // c_seq == r2 // c_seq, causal)
    return r, r2, h, causal


def _stack_rows(pieces):
    return pieces[0] if len(pieces) == 1 else jnp.concatenate(pieces, axis=0)


def _mixer_grid(bsz, seq, c_seq, nseq, nblk):
    nb = 1 if nseq == 1 else nseq * nblk
    tb = c_seq * nblk if nseq == 1 else c_seq
    assert nseq == 1 or seq == c_seq
    return nb, tb, (bsz // nb, seq // tb)


def _layer_spec(l, tail, n_grid, single=False):
    zeros = (0,) * len(tail)
    kw = dict(pipeline_mode=pl.Buffered(1)) if single else {}
    if n_grid == 1:
        return pl.BlockSpec((None,) + tail, lambda i: (l,) + zeros, **kw)
    return pl.BlockSpec((None,) + tail, lambda i, j: (l,) + zeros, **kw)


def _state_spec(l, nb, tail):
    zeros = (0,) * len(tail)
    return pl.BlockSpec((None, nb) + tail, lambda i, j: (l, i) + zeros)


def _state_io(l, nb, tail, bsz, init, prev):
    spec = _state_spec(l, nb, tail)
    in_specs, operands = [], []
    if init is not None:
        in_specs.append(spec)
        operands.append(init)
    alias_pos = None
    if prev is not None:
        alias_pos = len(in_specs)
        in_specs.append(pl.BlockSpec(memory_space=pl.ANY))
        operands.append(prev)
    return in_specs, operands, spec, jax.ShapeDtypeStruct((DEPTH, bsz) + tail, F32), alias_pos


def _cast_kernel(w_ref, o_ref):
    o_ref[...] = w_ref[...].astype(BF16)


def _cast_pipeline_kernel(w_hbm, o_hbm, *, steps, tr, cols):
    pltpu.emit_pipeline(
        _cast_kernel, grid=(steps,),
        in_specs=[pl.BlockSpec((tr, cols), lambda i: (i, 0), pipeline_mode=pl.Buffered(3))],
        out_specs=[pl.BlockSpec((tr, cols), lambda i: (i, 0))],
    )(w_hbm, o_hbm)


def _cast_bf16(w, tr):
    d, r, c = w.shape
    rows = d * r
    out = pl.pallas_call(
        functools.partial(_cast_pipeline_kernel, steps=rows // tr, tr=tr, cols=c),
        in_specs=[pl.BlockSpec(memory_space=pl.ANY)],
        out_specs=pl.BlockSpec(memory_space=pl.ANY),
        out_shape=jax.ShapeDtypeStruct((rows, c), BF16),
        compiler_params=pltpu.CompilerParams(vmem_limit_bytes=VMEM_LIMIT),
    )(w.reshape(rows, c))
    return out.reshape(d, r, c)


def _stream_steps(rows, tm):
    bounds, start = [], 0
    for r in rows:
        bounds.append((start, r // tm))
        start += r // tm
    return bounds, start


def _stream_spec(tm, width, start, steps, single=False):
    kw = dict(pipeline_mode=pl.Buffered(1)) if single else {}
    return pl.BlockSpec((tm, width), lambda i: (jnp.clip(i - start, 0, steps - 1), 0), **kw)


def _when_stream(start, steps):
    i = pl.program_id(0)
    return pl.when(jnp.logical_and(i >= start, i < start + steps))


def _inproj_kernel(*refs, bounds):
    n = len(bounds)
    x_refs, (nw_ref, w_ref), o_refs, wb_ref = refs[:n], refs[n:n + 2], refs[n + 2:2 * n + 2], refs[2 * n + 2]

    @pl.when(pl.program_id(0) == 0)
    def _():
        rows = 256
        for r in range(0, COL_S5, rows):
            wb_ref[r:r + rows, :] = w_ref[r:r + rows, :].astype(BF16)
        wb_ref[COL_S5:COL_DT, :] = w_ref[COL_S5 + HEADS:IN_COLS, :].astype(BF16)
        for h in range(HEADS):
            row = w_ref[COL_S5 + h:COL_S5 + h + 1, :]
            wb_ref[COL_DT + h * HEAD_W:COL_DT + (h + 1) * HEAD_W, :] = jnp.broadcast_to(
                row, (HEAD_W, D_MODEL)).astype(BF16)

    for x_ref, o_ref, (start, steps) in zip(x_refs, o_refs, bounds):
        @_when_stream(start, steps)
        def _():
            h = _rms(x_ref[...], nw_ref[...])
            o_ref[...] = _mm_nt(h, wb_ref[...])


def _inproj(xs, norm_w, w_in_t, l, tm):
    bounds, n_steps = _stream_steps([x.shape[0] for x in xs], tm)
    longest = max(steps for _, steps in bounds)
    return pl.pallas_call(
        functools.partial(_inproj_kernel, bounds=bounds),
        grid=(n_steps,),
        in_specs=[_stream_spec(tm, D_MODEL, *b, single=b[1] < longest) for b in bounds]
        + [_layer_spec(l, (1, D_MODEL), 1),
           _layer_spec(l, (IN_COLS, D_MODEL), 1, single=True)],
        out_specs=[_stream_spec(tm, IN_COLS_PAD, *b, single=b[1] < longest) for b in bounds],
        out_shape=[jax.ShapeDtypeStruct((x.shape[0], IN_COLS_PAD), F32) for x in xs],
        scratch_shapes=[pltpu.VMEM((IN_COLS_PAD, D_MODEL), BF16)],
        compiler_params=_cparams(("arbitrary",)),
    )(*xs, norm_w, w_in_t)


def _outmlp_kernel(*refs, bounds, final_norm, tn_up, tn_down):
    n = len(bounds)
    row_refs = refs[:5 * n]
    wo_ref, nw_ref, wu_ref, wd_ref, nf_ref = refs[5 * n:5 * n + 5]
    o_refs = refs[5 * n + 5:6 * n + 5]
    h_ref, u_ref = refs[6 * n + 5:]

    for k, (start, steps) in enumerate(bounds):
        x_ref, y_refs, o_ref = row_refs[5 * k], row_refs[5 * k + 1:5 * k + 5], o_refs[k]

        @_when_stream(start, steps)
        def _():
            x1 = x_ref[...]
            for g, y_ref in enumerate(y_refs):
                x1 = x1 + _mm(y_ref[...], wo_ref[g * GROUP_W:(g + 1) * GROUP_W, :])
            o_ref[...] = x1
            h_ref[...] = _rms(x1, nw_ref[...]).astype(BF16)
            for c0 in range(0, D_FF, tn_up):
                u = jnp.dot(h_ref[...], wu_ref[:, c0:c0 + tn_up], preferred_element_type=F32)
                u_ref[:, c0:c0 + tn_up] = jnp.square(jnp.maximum(u, 0.0)).astype(BF16)
            for c0 in range(0, D_MODEL, tn_down):
                o_ref[:, c0:c0 + tn_down] += jnp.dot(u_ref[...], wd_ref[:, c0:c0 + tn_down],
                                                     preferred_element_type=F32)
            if final_norm:
                o_ref[...] = _rms(o_ref[...], nf_ref[...])


def _outmlp(xs, ys, w_out, norm_w, w_up, w_down, norm_f, l, tm):
    bounds, n_steps = _stream_steps([x.shape[0] for x in xs], tm)
    row_specs, row_ops = [], []
    for x, y4, b in zip(xs, ys, bounds):
        row_specs += [_stream_spec(tm, D_MODEL, *b)] + [_stream_spec(tm, GROUP_W, *b)] * 4
        row_ops += [x, *y4]
    return pl.pallas_call(
        functools.partial(_outmlp_kernel, bounds=bounds, final_norm=(l == DEPTH - 1), tn_up=512, tn_down=256),
        grid=(n_steps,),
        in_specs=row_specs
        + [_layer_spec(l, (D_MODEL, D_MODEL), 1, single=True),
           _layer_spec(l, (1, D_MODEL), 1),
           _layer_spec(l, (D_MODEL, D_FF), 1, single=True),
           _layer_spec(l, (D_FF, D_MODEL), 1, single=True),
           pl.BlockSpec((1, D_MODEL), lambda i: (0, 0))],
        out_specs=[_stream_spec(tm, D_MODEL, *b) for b in bounds],
        out_shape=[jax.ShapeDtypeStruct((x.shape[0], D_MODEL), F32) for x in xs],
        scratch_shapes=[pltpu.VMEM((tm, D_MODEL), BF16),
                        pltpu.VMEM((tm, D_FF), BF16)],
        compiler_params=_cparams(("arbitrary",)),
    )(*row_ops, w_out, norm_w, w_up, w_down, norm_f)


def _rope_kernel(angle_ref, cos_ref, sin_ref, *, pos0, tl, period):
    i = pl.program_id(0)
    row = i * tl + lax.broadcasted_iota(jnp.int32, (tl, 2 * HEAD_W), 0)
    pos = (pos0 + row % period).astype(F32)
    theta = pos * angle_ref[...]
    lane = lax.broadcasted_iota(jnp.int32, (tl, 2 * HEAD_W), 1)
    cos_ref[...] = jnp.cos(theta)
    s = jnp.sin(theta)
    sin_ref[...] = jnp.where(lane % 2 == 0, -s, s)


def _rope_tables(angle_row, pos0, rows, period, tl):
    seq = rows
    return pl.pallas_call(
        functools.partial(_rope_kernel, pos0=pos0, tl=tl, period=period),
        grid=(seq // tl,),
        in_specs=[pl.BlockSpec((1, 2 * HEAD_W), lambda i: (0, 0))],
        out_specs=[pl.BlockSpec((tl, 2 * HEAD_W), lambda i: (i, 0))] * 2,
        out_shape=[jax.ShapeDtypeStruct((seq, 2 * HEAD_W), F32)] * 2,
        compiler_params=_cparams(("parallel",)),
    )(angle_row)


def _ret_kernel(*refs, c_seq, nseq, nblk, has_init, has_prev):
    lg_ref, p_ref, cos_ref, sin_ref = refs[:4]
    s0_ref = refs[4] if has_init else None
    y_ref, s_ref, sb_ref = refs[4 + has_init + has_prev:]
    nb = s_ref.shape[0]
    c = nseq * c_seq
    j = pl.program_id(1)

    @pl.when(j == 0)
    def _():
        sb_ref[...] = jnp.zeros_like(sb_ref)
        if has_init:
            for b in range(nb):
                for h in range(HEADS):
                    hs = slice(h * HEAD_W, (h + 1) * HEAD_W)
                    sb_ref[b, hs, hs] = s0_ref[b, h]

    def per_head(idx, shape):
        out = jnp.full(shape, lg_ref[HEADS - 1], F32)
        for h in range(HEADS - 2, -1, -1):
            out = jnp.where(idx == h, lg_ref[h], out)
        return out

    head = _head_of_lane(c)
    bd = _block_mask(HEAD_W)
    m_seg = jnp.where(bd, 1.0 / HEAD_W, 0.0).astype(BF16)
    even = lax.broadcasted_iota(jnp.int32, (c, GROUP_W), 1) % 2 == 0
    lg_lane = per_head(head[0:1, :], (1, GROUP_W))
    t_loc = (lax.broadcasted_iota(jnp.int32, (c, 1), 0) % c_seq).astype(F32)
    e_q = jnp.exp((t_loc + 1.0) * lg_lane)
    e_k = jnp.exp((c_seq - 1.0 - t_loc) * lg_lane)
    a_c = jnp.exp(c_seq * lg_lane)
    r_i, r2_i, h_i, causal = _seq_grid(c, c_seq, HEADS)
    dist = jnp.where(causal, (r_i - r2_i).astype(F32), 0.0)
    decay = jnp.where(causal, jnp.exp(dist * per_head(h_i, (c, HEADS * c))), 0.0)

    def rot(x, cos, sin):
        swapped = jnp.where(even, pltpu.roll(x, GROUP_W - 1, 1), pltpu.roll(x, 1, 1))
        return x * cos + swapped * sin

    for bs, ts, seqs in _row_blocks(nseq, c_seq, nblk):
        load = lambda lo: p_ref[bs, ts, lo:lo + GROUP_W].reshape(c, GROUP_W)
        crow = ts if nseq == 1 else slice(0, c)
        cos = jnp.concatenate([cos_ref[crow, :]] * 2, axis=1)
        sin = jnp.concatenate([sin_ref[crow, :]] * 2, axis=1)
        q = rot(load(0), cos, sin)
        k = rot(load(GROUP_W), cos, sin) * (HEAD_W ** -0.5)
        v = load(2 * GROUP_W)
        g = load(3 * GROUP_W)
        scores = _mm_nt(q, _expand_heads(k, head)) * decay
        o = _mm(scores, _expand_heads(v, head))
        qe = q * e_q
        ke = k * e_k
        cross = []
        for i, b in enumerate(seqs):
            sl = slice(i * c_seq, (i + 1) * c_seq)
            s_prev = sb_ref[b]
            cross.append(_mm_nt(qe[sl], s_prev))
            sb_ref[b] = a_c * s_prev + jnp.where(bd, _mm_tn(v[sl], ke[sl]), 0.0)
        o = o + _stack_rows(cross)
        mu = _seg_mean(o, m_seg)
        d = o - mu
        var = _seg_mean(d * d, m_seg)
        y_ref[bs, ts, :] = (d * lax.rsqrt(var + EPS) * _silu(g)).reshape(len(seqs), c_seq, GROUP_W)

    @pl.when(j == pl.num_programs(1) - 1)
    def _():
        for b in range(nb):
            for h in range(HEADS):
                hs = slice(h * HEAD_W, (h + 1) * HEAD_W)
                s_ref[b, h] = sb_ref[b, hs, hs]


def _retention(proj, cos, sin, log_gamma, init, prev, l, c_seq, nseq, nblk):
    bsz, seq, _ = proj.shape
    nb, tb, grid = _mixer_grid(bsz, seq, c_seq, nseq, nblk)
    st_in, st_ops, st_out, st_shape, alias = _state_io(l, nb, (HEADS, HEAD_W, HEAD_W), bsz, init, prev)
    n_fixed = 4
    rope_spec = (pl.BlockSpec((tb, 2 * HEAD_W), lambda i, j: (j, 0)) if nseq == 1 else
                 pl.BlockSpec((nseq * c_seq, 2 * HEAD_W), lambda i, j: (0, 0)))
    return pl.pallas_call(
        functools.partial(_ret_kernel, c_seq=c_seq, nseq=nseq, nblk=nblk, has_init=init is not None,
                          has_prev=prev is not None),
        grid=grid,
        in_specs=[pl.BlockSpec(memory_space=pltpu.SMEM),
                  pl.BlockSpec((nb, tb, 4 * GROUP_W), lambda i, j: (i, j, COL_RET // (4 * GROUP_W))),
                  rope_spec, rope_spec] + st_in,
        out_specs=[pl.BlockSpec((nb, tb, GROUP_W), lambda i, j: (i, j, 0)), st_out],
        out_shape=[jax.ShapeDtypeStruct((bsz, seq, GROUP_W), F32), st_shape],
        scratch_shapes=[pltpu.VMEM((nb, GROUP_W, GROUP_W), F32)],
        input_output_aliases={} if alias is None else {n_fixed + alias: 1},
        compiler_params=_cparams(("parallel", "arbitrary")),
    )(log_gamma, proj, cos, sin, *st_ops)


def _hg_kernel(*refs, c_seq, nseq, nblk, layer, has_init, has_prev):
    p_ref, lb_ref = refs[:2]
    s0_ref = refs[2] if has_init else None
    y_ref, s_ref, st_ref, keep_ref = refs[2 + has_init + has_prev:]
    nb = s_ref.shape[0]
    c = nseq * c_seq
    j = pl.program_id(1)

    pairs = [slice(pr * LANES, (pr + 1) * LANES) for pr in range(HEADS // PAIR_HEADS)]

    def pair_block(h):
        pr, jh = divmod(h, PAIR_HEADS)
        return pr, slice(jh * HEAD_W, (jh + 1) * HEAD_W)

    def load_state(b):
        st_ref[b] = jnp.zeros(st_ref.shape[1:], F32)
        if has_init:
            for h in range(HEADS):
                pr, hs = pair_block(h)
                st_ref[b, pr, hs, hs] = s0_ref[b, h]

    @pl.when(j == 0)
    def _():
        for b in range(nb):
            load_state(b)

    logits = lb_ref[...]
    e = jnp.exp(logits - jnp.max(logits, axis=0, keepdims=True))
    sm = e / jnp.sum(e, axis=0, keepdims=True)
    csum = sm[0:1, :]
    for i in range(1, layer + 1):
        csum = csum + sm[i:i + 1, :]
    lower = csum - sm[0:1, :]

    head = _head_of_lane(c)
    m_seg = jnp.where(_block_mask(HEAD_W), 1.0 / HEAD_W, 0.0).astype(BF16)
    bd = _block_mask(HEAD_W, LANES)
    r_i, r2_i, _, tril = _seq_grid(c, c_seq, 1)
    trif = tril.astype(F32)
    later = r2_i > r_i
    if nseq > 1:
        later = jnp.logical_and(r_i // c_seq == r2_i // c_seq, later)
    laterf = later.astype(F32)
    causal = _seq_grid(c, c_seq, HEADS)[3]
    mid = c // 2 - 1
    mask_sum = _mm_mask_f32 if c >= LANES else _mm_f32

    staged = []
    spreads = []
    for bs, ts, seqs in _row_blocks(nseq, c_seq, nblk):
        load = lambda lo: p_ref[bs, ts, lo:lo + GROUP_W].reshape(c, GROUP_W)
        qq = _silu(load(0))
        ff = load(GROUP_W)
        v = load(2 * GROUP_W)
        sig = jax.nn.sigmoid(ff)
        forget = lower + (1.0 - lower) * sig
        log_f = jnp.log(jnp.maximum(forget, HG_MIN_FORGET))
        key = (1.0 - lower) * (1.0 - sig)
        bcum = mask_sum(trif, log_f)
        if nseq == 1:
            b_ref = bcum[mid:mid + 1, :]
            spread = bcum - b_ref
            q_in = qq * jnp.exp(spread)
            k_in = key * jnp.exp(-spread)
            q_x = q_in * jnp.exp(b_ref)
            k_x = k_in * jnp.exp(bcum[c - 1:c, :] - b_ref)
        else:
            spread = bcum
            q_in = q_x = qq * jnp.exp(bcum)
            k_in = key * jnp.exp(-bcum)
            k_x = key * jnp.exp(mask_sum(laterf, log_f))
        spreads.append(jnp.max(jnp.abs(spread)))
        scores = jnp.where(causal, _mm_nt(q_in, _expand_heads(k_in, head)), 0.0)
        o = _mm(scores, _expand_heads(v, head))
        upd = []
        for i in range(len(seqs)):
            sl = slice(i * c_seq, (i + 1) * c_seq)
            a_seq = jnp.exp(bcum[(i + 1) * c_seq - 1:(i + 1) * c_seq, :])
            ds = [jnp.where(bd, _mm_tn(v[sl, pl_], k_x[sl, pl_]), 0.0) for pl_ in pairs]
            upd.append((q_x[sl], a_seq, ds))
        staged.append((bs, ts, seqs, o, upd))

    unsafe = functools.reduce(jnp.maximum, spreads) > HG_SAFE_SPREAD
    if nb == 1:
        keep_ref[...] = st_ref[0]

    for bs, ts, seqs, o, upd in staged:
        cross = []
        for b, (q_b, a_seq, ds) in zip(seqs, upd):
            parts = []
            for pr, pl_ in enumerate(pairs):
                st_prev = st_ref[b, pr]
                parts.append(_mm_nt(q_b[:, pl_], st_prev))
                st_ref[b, pr] = st_prev * a_seq[:, pl_] + ds[pr]
            cross.append(jnp.concatenate(parts, axis=1))
        o = o + _stack_rows(cross)
        gate = _silu(p_ref[bs, ts, 3 * GROUP_W:4 * GROUP_W].reshape(c, GROUP_W))
        y = o * lax.rsqrt(_seg_mean(o * o, m_seg) + EPS) * gate
        y_ref[bs, ts, :] = y.reshape(bs.stop - bs.start, ts.stop - ts.start, GROUP_W)

    @pl.when(unsafe)
    def _():
        tb = p_ref.shape[1]
        assert nb == 1 or (tb == SUBLANES and pl.num_programs(1) == 1)
        rowid = lax.broadcasted_iota(jnp.int32, (SUBLANES, GROUP_W), 0)
        if nb == 1:
            st_ref[0] = keep_ref[...]

        def slab(s, carry):
            b, t0 = (0, pl.multiple_of(s * SUBLANES, SUBLANES)) if nb == 1 else (s, 0)
            rows = pl.ds(t0, SUBLANES)
            if nb > 1:
                load_state(b)
            q8 = _silu(p_ref[b, rows, 0:GROUP_W])
            sig = jax.nn.sigmoid(p_ref[b, rows, GROUP_W:2 * GROUP_W])
            v8 = p_ref[b, rows, 2 * GROUP_W:3 * GROUP_W]
            g8 = p_ref[b, rows, 3 * GROUP_W:4 * GROUP_W]
            f8 = jnp.maximum(lower + (1.0 - lower) * sig, HG_MIN_FORGET)
            k8 = (1.0 - lower) * (1.0 - sig)
            o8 = jnp.zeros((SUBLANES, GROUP_W), F32)
            for i in range(SUBLANES):
                v_i = jnp.where(rowid == i, v8, 0.0)
                parts = []
                for pr, pl_ in enumerate(pairs):
                    outer = _mm_tn(v_i[:, pl_], k8[:, pl_])
                    st = st_ref[b, pr] * f8[i:i + 1, pl_] + jnp.where(bd, outer, 0.0)
                    st_ref[b, pr] = st
                    parts.append(_mm_nt(q8[:, pl_], st))
                o8 = jnp.where(rowid == i, jnp.concatenate(parts, axis=1), o8)
            y_ref[b, rows, :] = o8 * lax.rsqrt(_seg_mean(o8 * o8, m_seg) + EPS) * _silu(g8)
            return carry

        lax.fori_loop(0, nb * tb // SUBLANES, slab, 0)

    @pl.when(j == pl.num_programs(1) - 1)
    def _():
        for b in range(nb):
            for h in range(HEADS):
                pr, hs = pair_block(h)
                s_ref[b, h] = st_ref[b, pr, hs, hs]


def _hgrn2(proj, lb_logits, init, prev, l, c_seq, nseq, nblk):
    bsz, seq, _ = proj.shape
    nb, tb, grid = _mixer_grid(bsz, seq, c_seq, nseq, nblk)
    st_in, st_ops, st_out, st_shape, alias = _state_io(l, nb, (HEADS, HEAD_W, HEAD_W), bsz, init, prev)
    n_fixed = 2
    return pl.pallas_call(
        functools.partial(_hg_kernel, c_seq=c_seq, nseq=nseq, nblk=nblk, layer=l, has_init=init is not None,
                          has_prev=prev is not None),
        grid=grid,
        in_specs=[pl.BlockSpec((nb, tb, 4 * GROUP_W), lambda i, j: (i, j, COL_HG // (4 * GROUP_W))),
                  pl.BlockSpec((DEPTH, GROUP_W), lambda i, j: (0, 0))] + st_in,
        out_specs=[pl.BlockSpec((nb, tb, GROUP_W), lambda i, j: (i, j, 0)), st_out],
        out_shape=[jax.ShapeDtypeStruct((bsz, seq, GROUP_W), F32), st_shape],
        scratch_shapes=[pltpu.VMEM((nb, HEADS // PAIR_HEADS, LANES, LANES), F32),
                        pltpu.VMEM((HEADS // PAIR_HEADS, LANES, LANES), F32)],
        input_output_aliases={} if alias is None else {n_fixed + alias: 1},
        compiler_params=_cparams(("parallel", "arbitrary")),
    )(proj, lb_logits, *st_ops)


def _ssd_kernel(*refs, c_seq, nseq, nblk, has_init, has_prev):
    p_ref, dt_ref, cw_ref, cb_ref, dtb_ref, alog_ref, dsk_ref, nw_ref = refs[:8]
    n_in = 8
    cv0_ref = s0_ref = None
    if has_init:
        cv0_ref, s0_ref = refs[n_in], refs[n_in + 1 + has_prev]
    y_ref, s_ref, cv_ref, xpad_ref = refs[n_in + 2 * (has_init + has_prev):]
    nb = s_ref.shape[0]
    tb = p_ref.shape[1]
    c = nseq * c_seq
    n_tail = SSD_CONV - 1
    lo = CONV_PAD - n_tail

    @pl.when(pl.program_id(1) == 0)
    def _():
        xpad_ref[:, 0:CONV_PAD, :] = jnp.zeros((nb, CONV_PAD, SSD_CONV_DIM), F32)
        if has_init:
            s_ref[...] = s0_ref[...]
            xpad_ref[:, lo:CONV_PAD, :] = cv0_ref[...]
        else:
            s_ref[...] = jnp.zeros_like(s_ref)

    r_i, r2_i, _, tri = _seq_grid(c, c_seq, 1)
    trif = tri.astype(F32)
    later = r2_i > r_i
    if nseq > 1:
        later = jnp.logical_and(r_i // c_seq == r2_i // c_seq, later)
    laterf = later.astype(F32)
    head = _head_of_lane(c)
    m_grp = jnp.where(_block_mask(LANES), 1.0 / LANES, 0.0).astype(BF16)
    cw = cw_ref[...]
    groups = [slice(g * LANES, (g + 1) * LANES) for g in range(SSD_GROUPS)]

    xpad_ref[:, CONV_PAD:CONV_PAD + tb, :] = p_ref[:, :, GROUP_W:GROUP_W + SSD_CONV_DIM]
    for bs, ts, seqs in _row_blocks(nseq, c_seq, nblk):
        r0 = ts.start
        load = lambda ref, col, w: ref[bs, ts, col:col + w].reshape(c, w)
        z = load(p_ref, 0, GROUP_W)
        conv = cb_ref[...] + cw[n_tail:n_tail + 1, :] * load(p_ref, GROUP_W, SSD_CONV_DIM)
        for w in range(n_tail):
            win = xpad_ref[bs, r0 + lo + w:r0 + lo + w + c_seq, :]
            conv = conv + cw[w:w + 1, :] * win.reshape(c, SSD_CONV_DIM)
        xc = _silu(conv)
        xs = xc[:, 0:GROUP_W]
        bm = xc[:, GROUP_W:2 * GROUP_W]
        cm = xc[:, 2 * GROUP_W:3 * GROUP_W]

        delta = jax.nn.softplus(load(dt_ref, 0, GROUP_W) + dtb_ref[...])
        log_a = -delta * jnp.exp(alog_ref[...])
        bcum = _mm_mask_f32(trif, log_a)
        brem = bcum[c - 1:c, :] - bcum if nseq == 1 else _mm_mask_f32(laterf, log_a)
        xd = xs * delta
        xr = xd * jnp.exp(brem)
        brow = [bcum[:, gl].T for gl in groups]
        gram = [_mm_nt(cm[:, gl], bm[:, gl]) for gl in groups]
        w_parts = []
        for h in range(HEADS):
            g, jh = divmod(h, HEADS // SSD_GROUPS)
            b_t = bcum[:, h * HEAD_W:h * HEAD_W + 1]
            b_s = brow[g][jh * HEAD_W:jh * HEAD_W + 1, :]
            w_parts.append(jnp.where(tri, gram[g] * jnp.exp(b_t - b_s), 0.0))
        o = _mm(jnp.concatenate(w_parts, axis=1), _expand_heads(xd, head))
        cross = []
        for i, b in enumerate(seqs):
            sl = slice(i * c_seq, (i + 1) * c_seq)
            last = (i + 1) * c_seq - 1
            parts = []
            for g, gl in enumerate(groups):
                s_prev = s_ref[b, g]
                parts.append(_mm_nt(cm[sl, gl], s_prev))
                a_col = jnp.exp(brow[g][:, last:last + 1])
                s_ref[b, g] = s_prev * a_col + _mm_tn(xr[sl, gl], bm[sl, gl])
            cross.append(jnp.concatenate(parts, axis=1))
        o = o + jnp.exp(bcum) * _stack_rows(cross)
        y = (o + dsk_ref[...] * xs) * _silu(z)
        y = y * lax.rsqrt(_seg_mean(y * y, m_grp) + EPS) * nw_ref[...]
        y_ref[bs, ts, :] = y.reshape(len(seqs), c_seq, GROUP_W)
    tail = xpad_ref[:, tb + lo:tb + CONV_PAD, :]
    cv_ref[...] = tail
    xpad_ref[:, lo:CONV_PAD, :] = tail


def _ssd(proj, prm, init_cv, init_s, prev_cv, prev_s, l, c_seq, nseq, nblk):
    bsz, seq, _ = proj.shape
    nb, tb, grid = _mixer_grid(bsz, seq, c_seq, nseq, nblk)
    cv_in, cv_ops, cv_out, cv_shape, cv_alias = _state_io(l, nb, (SSD_CONV - 1, SSD_CONV_DIM), bsz,
                                                          init_cv, prev_cv)
    s_in, s_ops, s_out, s_shape, s_alias = _state_io(l, nb, (SSD_GROUPS, LANES, SSD_STATE), bsz,
                                                     init_s, prev_s)
    n_fixed = 8
    aliases = {}
    if cv_alias is not None:
        aliases[n_fixed + cv_alias] = 2
        aliases[n_fixed + len(cv_in) + s_alias] = 1
    return pl.pallas_call(
        functools.partial(_ssd_kernel, c_seq=c_seq, nseq=nseq, nblk=nblk, has_init=init_s is not None,
                          has_prev=prev_s is not None),
        grid=grid,
        in_specs=[pl.BlockSpec((nb, tb, 4 * GROUP_W), lambda i, j: (i, j, COL_SSD // (4 * GROUP_W))),
                  pl.BlockSpec((nb, tb, GROUP_W), lambda i, j: (i, j, COL_DT // GROUP_W)),
                  _layer_spec(l, (SSD_CONV, SSD_CONV_DIM), 2),
                  _layer_spec(l, (1, SSD_CONV_DIM), 2),
                  _layer_spec(l, (1, GROUP_W), 2),
                  _layer_spec(l, (1, GROUP_W), 2),
                  _layer_spec(l, (1, GROUP_W), 2),
                  _layer_spec(l, (1, GROUP_W), 2)] + cv_in + s_in,
        out_specs=[pl.BlockSpec((nb, tb, GROUP_W), lambda i, j: (i, j, 0)), s_out, cv_out],
        out_shape=[jax.ShapeDtypeStruct((bsz, seq, GROUP_W), F32), s_shape, cv_shape],
        scratch_shapes=[pltpu.VMEM((nb, tb + CONV_PAD, SSD_CONV_DIM), F32)],
        input_output_aliases=aliases,
        compiler_params=_cparams(("parallel", "arbitrary")),
    )(proj, proj, prm["conv_w"], prm["conv_b"], prm["dt_bias"], prm["a_log"], prm["ssd_d"], prm["ssd_norm"],
      *cv_ops, *s_ops)


def _ssd_state_in(s):
    d, bsz = s.shape[:2]
    return jnp.swapaxes(s, 3, 4).reshape(d, bsz, SSD_GROUPS, LANES, SSD_STATE)


def _ssd_state_out(s):
    d, bsz = s.shape[:2]
    return jnp.swapaxes(s.reshape(d, bsz, HEADS, HEAD_W, SSD_STATE), 3, 4)


def _s5_kernel(*refs, tl, ng, has_init, has_prev):
    (u0_ref, u1_ref, are_ref, aim_ref, ldt_ref, bre_ref, bim_ref, cw_ref, dsk_ref, gw_ref,
     gb_ref) = refs[:11]
    n_in = 11
    s0r_ref = s0i_ref = None
    if has_init:
        s0r_ref, s0i_ref = refs[n_in], refs[n_in + 1 + has_prev]
    (y_ref, sr_ref, si_ref, wb_ref, ab_ref, us0_ref, us1_ref, ut_ref, bu_ref, yt_ref, ys0_ref,
     ys1_ref) = refs[n_in + 2 * (has_init + has_prev):]
    nseq = SUBLANES * ng
    nbt = nseq * tl

    @pl.when(pl.program_id(1) == 0)
    def _():
        if has_init:
            sr_ref[...] = s0r_ref[...]
            si_ref[...] = s0i_ref[...]
        else:
            sr_ref[...] = jnp.zeros_like(sr_ref)
            si_ref[...] = jnp.zeros_like(si_ref)
        lr, li, st = are_ref[...], aim_ref[...], jnp.exp(ldt_ref[...])
        mag = jnp.exp(lr * st)
        abr = mag * jnp.cos(li * st)
        abi = mag * jnp.sin(li * st)
        ab_ref[0:1, :] = abr
        ab_ref[1:2, :] = abi
        den = lr * lr + li * li
        cr = ((abr - 1.0) * lr + abi * li) / den
        ci = (abi * lr - (abr - 1.0) * li) / den
        wb_ref[:, 0:S5_N] = (cr * bre_ref[...] - ci * bim_ref[...]).astype(BF16)
        wb_ref[:, S5_N:2 * S5_N] = (cr * bim_ref[...] + ci * bre_ref[...]).astype(BF16)

    us0_ref[...] = u0_ref[...].reshape(nbt, LANES)
    us1_ref[...] = u1_ref[...].reshape(nbt, LANES)

    nparts = bu_ref.shape[0]
    prow = nbt // nparts
    psteps = prow // SUBLANES
    assert ng == 1 or (nparts == ng and psteps == tl)

    def slabs(h):
        if ng == 1:
            return [(t, t) for t in range(h * psteps, (h + 1) * psteps)]
        return [(h * tl + t, h * SUBLANES * tl + t) for t in range(tl)]

    for h in range(nparts):
        for slab, row0 in slabs(h):
            dst = slice(slab * SUBLANES, (slab + 1) * SUBLANES)
            src = pl.ds(row0, SUBLANES, stride=tl)
            ut_ref[dst, 0:LANES] = us0_ref[src, :]
            ut_ref[dst, LANES:2 * LANES] = us1_ref[src, :]
        u_h = ut_ref[h * prow:(h + 1) * prow, :]
        bu_ref[h] = jnp.dot(u_h.astype(BF16), wb_ref[...], preferred_element_type=F32)
    ar = jnp.broadcast_to(ab_ref[0:1, :], (SUBLANES, S5_N))
    ai = jnp.broadcast_to(ab_ref[1:2, :], (SUBLANES, S5_N))
    if ng == 1:
        xr, xi = sr_ref[...], si_ref[...]
    for h in range(nparts):
        grp = slice(h * SUBLANES, (h + 1) * SUBLANES)
        if ng > 1:
            xr, xi = sr_ref[grp, :], si_ref[grp, :]
        for t in range(psteps):
            rows = slice(t * SUBLANES, (t + 1) * SUBLANES)
            nr = ar * xr - ai * xi + bu_ref[h, rows, 0:S5_N]
            ni = ar * xi + ai * xr + bu_ref[h, rows, S5_N:2 * S5_N]
            bu_ref[h, rows, 0:S5_N] = nr
            bu_ref[h, rows, S5_N:2 * S5_N] = ni
            xr, xi = nr, ni
        y = _mm(bu_ref[h], cw_ref[...]) + dsk_ref[...] * ut_ref[h * prow:(h + 1) * prow, :]
        y = jax.nn.gelu(y)
        yt_ref[h * prow:(h + 1) * prow, :] = y * jax.nn.sigmoid(_mm(y, gw_ref[...]) + gb_ref[...])
        if ng > 1:
            sr_ref[grp, :] = xr
            si_ref[grp, :] = xi
        for slab, row0 in slabs(h):
            src = slice(slab * SUBLANES, (slab + 1) * SUBLANES)
            dst = pl.ds(row0, SUBLANES, stride=tl)
            ys0_ref[dst, :] = yt_ref[src, 0:LANES]
            ys1_ref[dst, :] = yt_ref[src, LANES:2 * LANES]
    if ng == 1:
        sr_ref[...] = xr
        si_ref[...] = xi
    y_ref[:, :, 0:LANES] = ys0_ref[...].reshape(nseq, tl, LANES)
    y_ref[:, :, LANES:2 * LANES] = ys1_ref[...].reshape(nseq, tl, LANES)


def _s5(proj, prm, init_re, init_im, prev_re, prev_im, l, tl, nparts, ng):
    bsz, seq, _ = proj.shape
    nseq = SUBLANES * ng
    re_in, re_ops, re_out, re_shape, re_alias = _state_io(l, nseq, (S5_N,), bsz, init_re, prev_re)
    im_in, im_ops, im_out, im_shape, im_alias = _state_io(l, nseq, (S5_N,), bsz, init_im, prev_im)
    n_fixed = 11
    aliases = {}
    if re_alias is not None:
        aliases[n_fixed + re_alias] = 1
        aliases[n_fixed + len(re_in) + im_alias] = 2
    nbt = nseq * tl
    return pl.pallas_call(
        functools.partial(_s5_kernel, tl=tl, ng=ng, has_init=init_re is not None,
                          has_prev=prev_re is not None),
        grid=(bsz // nseq, seq // tl),
        in_specs=[pl.BlockSpec((nseq, tl, LANES), lambda i, j: (i, j, COL_S5 // LANES)),
                  pl.BlockSpec((nseq, tl, LANES), lambda i, j: (i, j, COL_S5 // LANES + 1)),
                  _layer_spec(l, (1, S5_N), 2), _layer_spec(l, (1, S5_N), 2), _layer_spec(l, (1, S5_N), 2),
                  _layer_spec(l, (GROUP_W, S5_N), 2), _layer_spec(l, (GROUP_W, S5_N), 2),
                  _layer_spec(l, (2 * S5_N, GROUP_W), 2),
                  _layer_spec(l, (1, GROUP_W), 2),
                  _layer_spec(l, (GROUP_W, GROUP_W), 2),
                  _layer_spec(l, (1, GROUP_W), 2)] + re_in + im_in,
        out_specs=[pl.BlockSpec((nseq, tl, GROUP_W), lambda i, j: (i, j, 0)), re_out, im_out],
        out_shape=[jax.ShapeDtypeStruct((bsz, seq, GROUP_W), F32), re_shape, im_shape],
        scratch_shapes=[pltpu.VMEM((GROUP_W, 2 * S5_N), BF16),
                        pltpu.VMEM((SUBLANES, S5_N), F32),
                        pltpu.VMEM((nbt, LANES), F32), pltpu.VMEM((nbt, LANES), F32),
                        pltpu.VMEM((nbt, GROUP_W), F32),
                        pltpu.VMEM((nparts, nbt // nparts, 2 * S5_N), F32),
                        pltpu.VMEM((nbt, GROUP_W), F32),
                        pltpu.VMEM((nbt, LANES), F32), pltpu.VMEM((nbt, LANES), F32)],
        input_output_aliases=aliases,
        compiler_params=_cparams(("parallel", "arbitrary")),
    )(proj, proj, prm["a_re"], prm["a_im"], prm["log_dt"], prm["b_re"], prm["b_im"], prm["c_blk"],
      prm["s5_d"], prm["glu_w"], prm["glu_b"], *re_ops, *im_ops)


def _block_diag(blocks):
    d, g, r, c = blocks.shape
    eye = jnp.eye(g, dtype=blocks.dtype)
    return (eye[None, :, None, :, None] * blocks[:, :, :, None, :]).reshape(d, g * r, g * c)


def _prep(ssd_conv_w, ssd_conv_b, ssd_dt_bias, ssd_a_log, ssd_d, ssd_norm, s5_a_re, s5_a_im, s5_log_dt,
          s5_b_re, s5_b_im, s5_c_re, s5_c_im, s5_d, s5_glu_w, s5_glu_b):
    per_head = lambda v: jnp.repeat(v, HEAD_W, axis=1)[:, None, :]
    c_blk = jnp.concatenate([_block_diag(jnp.swapaxes(s5_c_re, 2, 3)),
                             -_block_diag(jnp.swapaxes(s5_c_im, 2, 3))], axis=1)
    return dict(
        conv_w=ssd_conv_w, conv_b=ssd_conv_b[:, None, :],
        dt_bias=per_head(ssd_dt_bias), a_log=per_head(ssd_a_log), ssd_d=per_head(ssd_d),
        ssd_norm=ssd_norm[:, None, :],
        a_re=s5_a_re.reshape(DEPTH, 1, S5_N), a_im=s5_a_im.reshape(DEPTH, 1, S5_N),
        log_dt=jnp.repeat(s5_log_dt, S5_STATE, axis=1)[:, None, :],
        b_re=_block_diag(jnp.swapaxes(s5_b_re, 2, 3)), b_im=_block_diag(jnp.swapaxes(s5_b_im, 2, 3)),
        c_blk=c_blk.astype(BF16),
        s5_d=s5_d[:, None, :], glu_w=s5_glu_w.astype(BF16), glu_b=s5_glu_b[:, None, :],
    )


class _Stream:
    def __init__(self, x, pos0, init, cfg, angle_row):
        self.bsz, self.seq, _ = x.shape
        self.cfg = cfg
        self.x2d = x.reshape(self.bsz * self.seq, D_MODEL)
        c_ret, nseq_ret, _ = cfg["ret"]
        rope_rows = self.seq if nseq_ret == 1 else nseq_ret * c_ret
        self.cos, self.sin = _rope_tables(angle_row, pos0, rope_rows, self.seq, cfg["rope_tl"])
        self.init = (None,) * 6
        if init is not None:
            i_ret, i_hg, i_ssd, i_cv, i_re, i_im = init
            self.init = (jnp.swapaxes(i_ret, 3, 4), jnp.swapaxes(i_hg, 3, 4), _ssd_state_in(i_ssd), i_cv,
                         i_re.reshape(DEPTH, self.bsz, S5_N), i_im.reshape(DEPTH, self.bsz, S5_N))
        self.new = (None,) * 6

    def mix(self, proj2d, l, prm, lb_logits, log_gamma):
        proj = proj2d.reshape(self.bsz, self.seq, IN_COLS_PAD)
        i_ret, i_hg, i_ssd, i_cv, i_re, i_im = self.init
        s_ret, s_hg, s_ssd, s_cv, s_re, s_im = self.new
        cfg = self.cfg
        ya, s_ret = _retention(proj, self.cos, self.sin, log_gamma, i_ret, s_ret, l, *cfg["ret"])
        yb, s_hg = _hgrn2(proj, lb_logits, i_hg, s_hg, l, *cfg["hg"])
        yc, s_ssd, s_cv = _ssd(proj, prm, i_cv, i_ssd, s_cv, s_ssd, l, *cfg["ssd"])
        yd, s_re, s_im = _s5(proj, prm, i_re, i_im, s_re, s_im, l, *cfg["s5"])
        self.new = (s_ret, s_hg, s_ssd, s_cv, s_re, s_im)
        return [y.reshape(self.bsz * self.seq, GROUP_W) for y in (ya, yb, yc, yd)]

    def outputs(self):
        s_ret, s_hg, s_ssd, s_cv, s_re, s_im = self.new
        tail = (DEPTH, self.bsz, S5_GROUPS, S5_STATE)
        return (self.x2d.reshape(self.bsz, self.seq, D_MODEL), jnp.swapaxes(s_ret, 3, 4), jnp.swapaxes(s_hg, 3, 4),
                _ssd_state_out(s_ssd), s_cv,
                s_re.reshape(tail), s_im.reshape(tail))


def _trunk(streams, prm, w_in, w_out, w_up, w_down, norm_mix, norm_mlp, norm_final, lb_logits, log_gamma,
           tm_in, tm_out):
    for l in range(DEPTH):
        projs = _inproj([s.x2d for s in streams], norm_mix, w_in, l, tm_in)
        ys = [s.mix(p, l, prm, lb_logits, log_gamma) for s, p in zip(streams, projs)]
        outs = _outmlp([s.x2d for s in streams], ys, w_out, norm_mlp, w_up, w_down, norm_final, l, tm_out)
        for s, x2d in zip(streams, outs):
            s.x2d = x2d
    return [s.outputs() for s in streams]


PROMPT_CFG = dict(rope_tl=256, ret=(256, 1, 8), hg=(64, 1, 16), ssd=(128, 1, 16), s5=(256, 4, 1))
SAMPLE_CFG = dict(rope_tl=128, ret=(8, 16, 2), hg=(8, 16, 2), ssd=(8, 16, 2), s5=(8, 16, 16))
TM_IN = 512
TM_OUT = 512


def kernel(x_prompt, x_sample, state_ret, state_hgrn, state_ssd, state_ssd_conv, state_s5_re, state_s5_im, norm_mix, w_in, w_out, hg_lb_logits, ssd_conv_w, ssd_conv_b, ssd_dt_bias, ssd_a_log, ssd_d, ssd_norm, s5_a_re, s5_a_im, s5_log_dt, s5_b_re, s5_b_im, s5_c_re, s5_c_im, s5_d, s5_glu_w, s5_glu_b, norm_mlp, w_up, w_down, norm_final):
    prm = _prep(ssd_conv_w, ssd_conv_b, ssd_dt_bias, ssd_a_log, ssd_d, ssd_norm, s5_a_re, s5_a_im, s5_log_dt,
                s5_b_re, s5_b_im, s5_c_re, s5_c_im, s5_d, s5_glu_w, s5_glu_b)
    w_out_b = _cast_bf16(w_out, 512)
    w_up_b = _cast_bf16(w_up, 256)
    w_down_b = _cast_bf16(w_down, 1024)
    angle = 1.0 / (ROPE_BASE ** jnp.linspace(0.0, 1.0, HEAD_W // 2, dtype=F32))
    angle_row = jnp.tile(jnp.repeat(angle, 2), 2)[None, :]
    log_gamma = jnp.log(1.0 - jnp.exp2(-5.0 - jnp.arange(HEADS, dtype=F32)))
    states = (state_ret, state_hgrn, state_ssd, state_ssd_conv, state_s5_re, state_s5_im)
    streams = [_Stream(x_prompt, 0, None, PROMPT_CFG, angle_row),
               _Stream(x_sample, PAST_LEN, states, SAMPLE_CFG, angle_row)]
    out_p, out_s = _trunk(streams, prm, jnp.swapaxes(w_in, 1, 2), w_out_b, w_up_b, w_down_b, norm_mix[:, None, :],
                          norm_mlp[:, None, :], norm_final[None, :], hg_lb_logits, log_gamma, TM_IN, TM_OUT)
    return (out_p[0], out_s[0]) + out_p[1:] + out_s[1:]
```
